```python
import math
import jax
import jax.numpy as jnp
from jax import lax
import numpy as np

D_MODEL = 1024
BATCH = 4
SEQ = 4096
DEPTH = 2
DEC_BATCH = 32
DEC_SEQ = 4
PAST_LEN = 16384
PAGE_SIZE = 128

D_MIX = D_MODEL
POOL_WINDOWS = (2, 4, 8, 16)
POOL_GROUPS = len(POOL_WINDOWS)
POOL_WIDTH = D_MIX // 4
POOL_GW = POOL_WIDTH // POOL_GROUPS
POOL_BUF = max(POOL_WINDOWS) - 1
HEAD_DIM = 64
NSA_WIDTH = D_MIX // 2
NSA_HEADS = NSA_WIDTH // HEAD_DIM
NSA_KV = 2
NSA_HPG = NSA_HEADS // NSA_KV
L_CMP = 32
L_SLC = 64
N_SEL = 16
WINDOW = 512
Q_BLOCK = 64
GLA_WIDTH = D_MIX - POOL_WIDTH - NSA_WIDTH
GLA_HEADS = 4
GLA_DV = GLA_WIDTH // GLA_HEADS
GLA_DK = GLA_DV // 2
GLA_RANK = 16
GLA_TAU = 16.0
GLA_CHUNK = 16
D_FF = 2816
N_EXPERTS = 8
TOP_K = 2
D_FF_EXPERT = 2816
MOE_BLOCK = 128
N_DENSE = (DEPTH + 1) // 2
N_MOE = DEPTH // 2
EPS = 1e-6
IN_WIDTHS = (POOL_WIDTH, NSA_WIDTH, 6 * NSA_KV * HEAD_DIM, 3 * NSA_HEADS,
             GLA_HEADS * GLA_DK, GLA_HEADS * GLA_DK, GLA_WIDTH, GLA_RANK, GLA_WIDTH)
IN_SPLITS = tuple(int(s) for s in np.cumsum(IN_WIDTHS)[:-1])
N_IN = sum(IN_WIDTHS)

kernel_name = 'hybrid_pool_nsa_gla_step'


def rmsnorm(x, g):
    xf = x.astype(jnp.float32)
    y = xf * lax.rsqrt(jnp.mean(xf * xf, axis=-1, keepdims=True) + EPS)
    return (y * g.astype(jnp.float32)).astype(x.dtype)


def masked_softmax(s, mask):
    s = jnp.where(mask, s.astype(jnp.float32), -jnp.inf)
    m = jnp.max(s, axis=-1, keepdims=True)
    m = jnp.where(jnp.isfinite(m), m, 0.0)
    e = jnp.exp(s - m)
    den = jnp.sum(e, axis=-1, keepdims=True)
    return e / jnp.where(den > 0.0, den, 1.0)


def pool_mix(u_ext, pos0, w, scale):
    B, L, C = u_ext.shape
    T = L - POOL_BUF
    uf = u_ext.astype(jnp.float32)
    cs = jnp.concatenate([jnp.zeros((B, 1, C), jnp.float32), jnp.cumsum(uf, axis=1)], axis=1)
    r = POOL_BUF + jnp.arange(T)
    pos = pos0 + jnp.arange(T)
    means = []
    for gi, wsize in enumerate(POOL_WINDOWS):
        csg = cs[:, :, gi * POOL_GW:(gi + 1) * POOL_GW]
        cnt = jnp.minimum(wsize, pos + 1).astype(jnp.float32)
        means.append((csg[:, r + 1] - csg[:, r + 1 - wsize]) / cnt[None, :, None])
    d = jnp.concatenate(means, axis=-1) - uf[:, POOL_BUF:]
    d = d.reshape(B, T, POOL_GROUPS, POOL_GW)
    y = jnp.einsum('btgc,gce->btge', d, w.astype(jnp.float32)).reshape(B, T, POOL_WIDTH)
    return (y * scale.astype(jnp.float32)).astype(u_ext.dtype)


def compress(rows, w, pe):
    B, tp, G, D = rows.shape
    blk = rows.reshape(B, tp // L_CMP, L_CMP, G, D) + pe[:, None, :]
    return jnp.einsum('bnjgd,jde->bnge', blk, w)


def nsa_attend(q, rows, win_ext, gates, pos0, kc_gain, cmp_w, cmp_pe):
    B, T = q.shape[:2]
    t_pad = rows.shape[1]
    n_cmp = t_pad // L_CMP
    n_slc = t_pad // L_SLC
    kc = rmsnorm(compress(rows[:, :, 0], cmp_w[0], cmp_pe[0]), kc_gain)
    vc = compress(rows[:, :, 1], cmp_w[1], cmp_pe[1])
    ks_blk = rows[:, :, 2].reshape(B, n_slc, L_SLC, NSA_KV, HEAD_DIM).transpose(0, 3, 1, 2, 4)
    vs_blk = rows[:, :, 3].reshape(B, n_slc, L_SLC, NSA_KV, HEAD_DIM).transpose(0, 3, 1, 2, 4)
    kw_all = win_ext[:, :, 0]
    vw_all = win_ext[:, :, 1]
    k_sel = min(N_SEL - 1, n_slc)
    qb = math.gcd(T, Q_BLOCK)
    nb = T // qb
    cmp_end = (jnp.arange(n_cmp) + 1) * L_CMP - 1
    slc_idx = jnp.arange(n_slc)
    b_ix = jnp.arange(B)[:, None, None, None]
    g_ix = jnp.arange(NSA_KV)[None, :, None, None]

    def block(args):
        i, q_i, g_i = args
        t = pos0 + i * qb + jnp.arange(qb)
        s_c = jnp.einsum('bqgjd,bngd->bgjqn', q_i, kc)
        p_c = masked_softmax(s_c, (cmp_end[None, :] <= t[:, None])[None, None, None])
        o_c = jnp.einsum('bgjqn,bngd->bqgjd', p_c.astype(vc.dtype), vc)
        imp = p_c.sum(axis=2).reshape(B, NSA_KV, qb, n_slc, L_SLC // L_CMP).sum(-1)
        cur = t // L_SLC
        done = slc_idx[None, :] < cur[:, None]
        imp = jnp.where(done[None, None], imp, -jnp.inf)
        _, top = lax.top_k(imp, k_sel)
        cur_b = jnp.broadcast_to(cur[None, None, :, None], (B, NSA_KV, qb, 1))
        sel = jnp.concatenate([top, cur_b], axis=-1)
        sel_ok = jnp.concatenate([top < cur_b, jnp.ones_like(cur_b, dtype=bool)], axis=-1)
        k_g = ks_blk[b_ix, g_ix, sel]
        v_g = vs_blk[b_ix, g_ix, sel]
        n_keys = sel.shape[-1] * L_SLC
        key_pos = sel[..., None] * L_SLC + jnp.arange(L_SLC)
        m_s = (sel_ok[..., None] & (key_pos <= t[None, None, :, None, None])).reshape(B, NSA_KV, 1, qb, n_keys)
        s_s = jnp.einsum('bqgjd,bgqkld->bgjqkl', q_i, k_g)
        p_s = masked_softmax(s_s.reshape(B, NSA_KV, NSA_HPG, qb, n_keys), m_s).reshape(s_s.shape)
        o_s = jnp.einsum('bgjqkl,bgqkld->bqgjd', p_s.astype(v_g.dtype), v_g)
        kw = lax.dynamic_slice_in_dim(kw_all, i * qb, WINDOW + qb, axis=1)
        vw = lax.dynamic_slice_in_dim(vw_all, i * qb, WINDOW + qb, axis=1)
        w_pos = pos0 - WINDOW + i * qb + jnp.arange(WINDOW + qb)
        m_w = (w_pos[None, :] >= 0) & (w_pos[None, :] <= t[:, None]) & (w_pos[None, :] >= t[:, None] - WINDOW)
        s_w = jnp.einsum('bqgjd,bkgd->bgjqk', q_i, kw)
        p_w = masked_softmax(s_w, m_w[None, None, None])
        o_w = jnp.einsum('bgjqk,bkgd->bqgjd', p_w.astype(vw.dtype), vw)
        return g_i[..., 0:1] * o_c + g_i[..., 1:2] * o_s + g_i[..., 2:3] * o_w

    q_b = q.reshape(B, nb, qb, NSA_KV, NSA_HPG, HEAD_DIM).swapaxes(0, 1)
    g_b = gates.reshape(B, nb, qb, NSA_KV, NSA_HPG, 3).swapaxes(0, 1)
    o = lax.map(block, (jnp.arange(nb), q_b, g_b))
    return o.swapaxes(0, 1).reshape(B, T, NSA_WIDTH)


def gla_mix(q, k, v, a_lr, r, s0, wa2, ba, o_gain):
    B, T, H, DK = q.shape
    DV = v.shape[-1]
    f32 = jnp.float32
    log_a = jax.nn.log_sigmoid(jnp.dot(a_lr.astype(f32), wa2.astype(f32)) + ba.astype(f32)) / GLA_TAU
    log_a = log_a.reshape(B, T, H, DK)
    c = math.gcd(T, GLA_CHUNK)
    n = T // c

    def to_chunks(z):
        return z.astype(f32).reshape(B, n, c, H, z.shape[-1]).swapaxes(0, 1)

    causal = jnp.tril(jnp.ones((c, c), dtype=bool))

    def step(S, inp):
        qc, kc, vc, lc = inp
        b = jnp.cumsum(lc, axis=1)
        o_inter = jnp.einsum('bthk,bhkv->bthv', qc * jnp.exp(b), S)
        decay = jnp.exp(jnp.where(causal[None, :, :, None, None], b[:, :, None] - b[:, None, :], -jnp.inf))
        att = jnp.einsum('bthk,bshk,btshk->bths', qc, kc, decay)
        o_intra = jnp.einsum('bths,bshv->bthv', att, vc)
        b_last = b[:, -1]
        S = S * jnp.exp(b_last)[..., None] + jnp.einsum('bshk,bshv->bhkv', kc * jnp.exp(b_last[:, None] - b), vc)
        return S, o_inter + o_intra

    S, o = lax.scan(step, s0.astype(f32), (to_chunks(q * (GLA_DK ** -0.5)), to_chunks(k), to_chunks(v), to_chunks(log_a)))
    o = o.swapaxes(0, 1).reshape(B, T, H, DV)
    o = rmsnorm(o, o_gain) * jax.nn.silu(r.astype(f32)).reshape(B, T, H, DV)
    return o.reshape(B, T, H * DV).astype(q.dtype), S.astype(s0.dtype)


def swiglu(h, w1, w3, w2):
    return jnp.dot(jax.nn.silu(jnp.dot(h, w1)) * jnp.dot(h, w3), w2)


def moe_ffn(h, router, w1, w3, w2):
    shp = h.shape
    hf = h.reshape(-1, shp[-1])
    N = hf.shape[0]
    logits = jnp.dot(hf, router).astype(jnp.float32)
    top_v, top_i = lax.top_k(logits, TOP_K)
    gate = jax.nn.softmax(top_v, axis=-1)
    A = N * TOP_K
    e_a = top_i.reshape(-1)
    tok_a = jnp.repeat(jnp.arange(N), TOP_K)
    g_a = gate.reshape(-1)
    order = jnp.argsort(e_a)
    e_s, tok_s, g_s = e_a[order], tok_a[order], g_a[order]
    counts = jnp.bincount(e_a, length=N_EXPERTS)
    start = jnp.cumsum(counts) - counts
    padded = (counts + MOE_BLOCK - 1) // MOE_BLOCK * MOE_BLOCK
    p_end = jnp.cumsum(padded)
    p_start = p_end - padded
    dest = p_start[e_s] + (jnp.arange(A) - start[e_s])
    n_blocks = -(-A // MOE_BLOCK) + N_EXPERTS
    R = n_blocks * MOE_BLOCK
    row_tok = jnp.zeros((R,), jnp.int32).at[dest].set(tok_s)
    row_gate = jnp.zeros((R,), jnp.float32).at[dest].set(g_s)
    blk_expert = jnp.clip(jnp.searchsorted(p_end, jnp.arange(n_blocks) * MOE_BLOCK, side='right'), 0, N_EXPERTS - 1)

    def run_block(args):
        toks, g, e = args
        yb = swiglu(hf[toks], w1[e], w3[e], w2[e])
        return yb * g[:, None].astype(yb.dtype)

    yb = lax.map(run_block, (row_tok.reshape(n_blocks, MOE_BLOCK), row_gate.reshape(n_blocks, MOE_BLOCK), blk_expert))
    y = jax.ops.segment_sum(yb.reshape(R, shp[-1]), row_tok, num_segments=N)
    return y.reshape(shp)


def mixing_sublayer(x, pos0, past_kv, win_front, gla_s0, pool_buf, n_win_out,
                    g_mix, w_in_l, w_out_l, pool_w_l, pool_scale_l, q_gain, k_gain,
                    cmp_w, cmp_pe, wa2, ba, o_gain):
    B, T, _ = x.shape
    z = jnp.dot(rmsnorm(x, g_mix), w_in_l)
    u_pool, nsa_q, nsa_kv, nsa_g, gla_q, gla_k, gla_v, gla_a, gla_r = jnp.split(z, IN_SPLITS, axis=-1)
    pool_ext = jnp.concatenate([pool_buf.astype(z.dtype), u_pool], axis=1)
    y_pool = pool_mix(pool_ext, pos0, pool_w_l, pool_scale_l)
    q = rmsnorm(nsa_q.reshape(B, T, NSA_HEADS, HEAD_DIM), q_gain) * (HEAD_DIM ** -0.5)
    kv = nsa_kv.reshape(B, T, 6, NSA_KV, HEAD_DIM)
    new_rows = jnp.stack([kv[:, :, 0], kv[:, :, 1], rmsnorm(kv[:, :, 2], k_gain[1]), kv[:, :, 3]], axis=2)
    new_win = jnp.stack([rmsnorm(kv[:, :, 4], k_gain[2]), kv[:, :, 5]], axis=2)
    t_all = past_kv.shape[1] + T
    t_pad = -(-t_all // L_SLC) * L_SLC
    rows = jnp.concatenate([past_kv.astype(z.dtype), new_rows,
                            jnp.zeros((B, t_pad - t_all) + new_rows.shape[2:], z.dtype)], axis=1)
    win_ext = jnp.concatenate([win_front.astype(z.dtype), new_win], axis=1)
    gates = jax.nn.sigmoid(nsa_g.reshape(B, T, NSA_HEADS, 3))
    y_nsa = nsa_attend(q, rows, win_ext, gates, pos0, k_gain[0], cmp_w, cmp_pe)
    y_gla, s_new = gla_mix(gla_q.reshape(B, T, GLA_HEADS, GLA_DK), gla_k.reshape(B, T, GLA_HEADS, GLA_DK),
                           gla_v.reshape(B, T, GLA_HEADS, GLA_DV), gla_a, gla_r, gla_s0, wa2, ba, o_gain)
    y = x + jnp.dot(jnp.concatenate([y_pool, y_nsa, y_gla], axis=-1), w_out_l)
    return y, new_rows, win_ext[:, -n_win_out:], s_new, pool_ext[:, -POOL_BUF:]


def ffn_sublayer(h, l, g, ffn_w1, ffn_w3, ffn_w2, moe_router, moe_w1, moe_w3, moe_w2):
    hn = rmsnorm(h, g)
    i = l // 2
    if l % 2 == 0:
        f = swiglu(hn, ffn_w1[i], ffn_w3[i], ffn_w2[i])
    else:
        f = moe_ffn(hn, moe_router[i], moe_w1[i], moe_w3[i], moe_w2[i])
    return h + f


def setup_inputs(seed: int = 0) -> dict:
    key = jax.random.key(seed)
    ks = jax.random.split(key, 27)
    f32 = jnp.float32
    n_pages = PAST_LEN // PAGE_SIZE
    n_used = DEC_BATCH * n_pages
    n_pool = n_used + max(1, n_used // 4)
    w_buf = min(WINDOW, PAST_LEN)

    def nrm(k, shape, scale):
        return jax.random.normal(k, shape, f32) * scale

    def gain(k, shape):
        return 1.0 + 0.02 * jax.random.normal(k, shape, f32)

    page_table = jax.random.permutation(ks[3], n_pool)[:n_used].reshape(DEC_BATCH, n_pages).astype(jnp.int32)
    return {
        'x_prompt': nrm(ks[0], (BATCH, SEQ, D_MODEL), 1.0),
        'x_sample': nrm(ks[1], (DEC_BATCH, DEC_SEQ, D_MODEL), 1.0),
        'cache_nsa_kv': nrm(ks[2], (DEPTH, n_pool, PAGE_SIZE, 4, NSA_KV, HEAD_DIM), 1.0),
        'state_nsa_win': nrm(ks[4], (DEPTH, DEC_BATCH, w_buf, 2, NSA_KV, HEAD_DIM), 1.0),
        'state_gla': nrm(ks[5], (DEPTH, DEC_BATCH, GLA_HEADS, GLA_DK, GLA_DV), 0.5),
        'state_pool': nrm(ks[6], (DEPTH, DEC_BATCH, POOL_BUF, POOL_WIDTH), 1.0),
        'page_table': page_table,
        'norm_mix': gain(ks[7], (DEPTH, D_MODEL)),
        'norm_ffn': gain(ks[8], (DEPTH, D_MODEL)),
        'w_in': nrm(ks[9], (DEPTH, D_MODEL, N_IN), D_MODEL ** -0.5),
        'w_out': nrm(ks[10], (DEPTH, D_MIX, D_MODEL), D_MIX ** -0.5),
        'pool_w': nrm(ks[11], (DEPTH, POOL_GROUPS, POOL_GW, POOL_GW), POOL_GW ** -0.5),
        'pool_scale': 1.0 + 0.1 * jax.random.normal(ks[12], (DEPTH, POOL_WIDTH), f32),
        'nsa_q_norm': gain(ks[13], (DEPTH, HEAD_DIM)),
        'nsa_k_norm': gain(ks[14], (DEPTH, 3, HEAD_DIM)),
        'nsa_cmp_w': nrm(ks[15], (DEPTH, 2, L_CMP, HEAD_DIM, HEAD_DIM), (L_CMP * HEAD_DIM) ** -0.5),
        'nsa_cmp_pe': nrm(ks[16], (DEPTH, 2, L_CMP, HEAD_DIM), 0.1),
        'gla_wa2': nrm(ks[17], (DEPTH, GLA_RANK, GLA_HEADS * GLA_DK), GLA_RANK ** -0.5),
        'gla_ba': nrm(ks[18], (DEPTH, GLA_HEADS * GLA_DK), 0.1),
        'gla_norm': gain(ks[19], (DEPTH, GLA_DV)),
        'ffn_w1': nrm(ks[20], (N_DENSE, D_MODEL, D_FF), D_MODEL ** -0.5),
        'ffn_w3': nrm(ks[21], (N_DENSE, D_MODEL, D_FF), D_MODEL ** -0.5),
        'ffn_w2': nrm(ks[22], (N_DENSE, D_FF, D_MODEL), D_FF ** -0.5),
        'moe_router': nrm(ks[23], (N_MOE, D_MODEL, N_EXPERTS), D_MODEL ** -0.5),
        'moe_w1': nrm(ks[24], (N_MOE, N_EXPERTS, D_MODEL, D_FF_EXPERT), D_MODEL ** -0.5),
        'moe_w3': nrm(ks[25], (N_MOE, N_EXPERTS, D_MODEL, D_FF_EXPERT), D_MODEL ** -0.5),
        'moe_w2': nrm(ks[26], (N_MOE, N_EXPERTS, D_FF_EXPERT, D_MODEL), D_FF_EXPERT ** -0.5),
    }


def reference(x_prompt, x_sample, cache_nsa_kv, state_nsa_win, state_gla, state_pool, page_table,
              norm_mix, norm_ffn, w_in, w_out, pool_w, pool_scale, nsa_q_norm, nsa_k_norm,
              nsa_cmp_w, nsa_cmp_pe, gla_wa2, gla_ba, gla_norm, ffn_w1, ffn_w3, ffn_w2,
              moe_router, moe_w1, moe_w3, moe_w2):
    n_pages = PAST_LEN // PAGE_SIZE
    w_buf = state_nsa_win.shape[2]
    hp, hs = x_prompt, x_sample
    bp, bs = hp.shape[0], hs.shape[0]
    kv_p, kv_s, win_p, win_s, gla_p, gla_s, pool_p, pool_s = [], [], [], [], [], [], [], []
    for l in range(DEPTH):
        mix_w = (norm_mix[l], w_in[l], w_out[l], pool_w[l], pool_scale[l], nsa_q_norm[l], nsa_k_norm[l],
                 nsa_cmp_w[l], nsa_cmp_pe[l], gla_wa2[l], gla_ba[l], gla_norm[l])
        hp, a, b, c, d = mixing_sublayer(
            hp, 0,
            jnp.zeros((bp, 0, 4, NSA_KV, HEAD_DIM), hp.dtype),
            jnp.zeros((bp, WINDOW, 2, NSA_KV, HEAD_DIM), hp.dtype),
            jnp.zeros((bp, GLA_HEADS, GLA_DK, GLA_DV), state_gla.dtype),
            jnp.zeros((bp, POOL_BUF, POOL_WIDTH), hp.dtype),
            min(WINDOW, hp.shape[1]), *mix_w)
        kv_p.append(a); win_p.append(b); gla_p.append(c); pool_p.append(d)
        past = cache_nsa_kv[l][page_table].reshape(bs, n_pages * PAGE_SIZE, 4, NSA_KV, HEAD_DIM)
        front = jnp.pad(state_nsa_win[l], ((0, 0), (WINDOW - w_buf, 0), (0, 0), (0, 0), (0, 0)))
        hs, a, b, c, d = mixing_sublayer(hs, PAST_LEN, past, front, state_gla[l], state_pool[l], w_buf, *mix_w)
        kv_s.append(a); win_s.append(b); gla_s.append(c); pool_s.append(d)
        hp = ffn_sublayer(hp, l, norm_ffn[l], ffn_w1, ffn_w3, ffn_w2, moe_router, moe_w1, moe_w3, moe_w2)
        hs = ffn_sublayer(hs, l, norm_ffn[l], ffn_w1, ffn_w3, ffn_w2, moe_router, moe_w1, moe_w3, moe_w2)
    return (hp, hs, jnp.stack(kv_p), jnp.stack(kv_s), jnp.stack(win_p), jnp.stack(win_s),
            jnp.stack(gla_p), jnp.stack(gla_s), jnp.stack(pool_p), jnp.stack(pool_s))
```

```python
import functools
import math

import jax
import jax.numpy as jnp
from jax import lax
from jax.experimental import pallas as pl
from jax.experimental.pallas import tpu as pltpu

F32 = jnp.float32
BF16 = jnp.bfloat16

EPS = 1e-6
LANES = 128
HEAD_DIM = 64
PAGE_SIZE = 128
POOL_WINDOWS = (2, 4, 8, 16)
POOL_BUF = 15
POOL_WIDTH = 256
NSA_WIDTH = 512
NSA_HEADS = 8
NSA_KV = 2
NSA_HPG = 4
L_CMP = 32
L_SLC = 64
N_SEL = 16
WINDOW = 512
GLA_HEADS = 4
GLA_DK = 32
GLA_DV = 64
GLA_WIDTH = 256
GLA_RANK = 16
GLA_TAU = 16.0
GLA_SUB = 16
TOP_K = 2
N_IN_PAD = 2432
MASKED = -1e30
VMEM_LIMIT = 56 * 1024 * 1024


def _params(*sem):
    return pltpu.CompilerParams(dimension_semantics=sem, vmem_limit_bytes=VMEM_LIMIT)


def _tile(n, target):
    best = None
    for t in range(8, min(n, target) + 1, 8):
        if n % t == 0:
            best = t
    assert best is not None, (n, target)
    return best


def _dot(a, b):
    return jnp.dot(a, b, preferred_element_type=F32)


def _dot_nt(a, b):
    return lax.dot_general(a, b, (((1,), (1,)), ((), ())), preferred_element_type=F32)


def _dot_tn(a, b):
    return lax.dot_general(a, b, (((0,), (0,)), ((), ())), preferred_element_type=F32)


def _split2_dot(a, ones):
    hi = a.astype(BF16)
    lo = (a - hi.astype(F32)).astype(BF16)
    return _dot(hi, ones) + _dot(lo, ones)


def _split3_dot(ones, a):
    a1 = a.astype(BF16)
    r1 = a - a1.astype(F32)
    a2 = r1.astype(BF16)
    a3 = (r1 - a2.astype(F32)).astype(BF16)
    return _dot(ones, a1) + _dot(ones, a2) + _dot(ones, a3)


def _head_rmsnorm(a, gain_row, seg_ones):
    ms = _split2_dot(a * a, seg_ones) * (1.0 / HEAD_DIM)
    return a * lax.rsqrt(ms + EPS) * gain_row


def _masked_softmax(s, mask):
    sm = jnp.where(mask, s, MASKED)
    m = jnp.max(sm, axis=-1, keepdims=True)
    e = jnp.where(mask, jnp.exp(sm - m), 0.0)
    den = jnp.sum(e, axis=-1, keepdims=True)
    return e * (1.0 / jnp.where(den > 0.0, den, 1.0))


def _seg_ones():
    i = jnp.arange(LANES)
    return (i[:, None] // HEAD_DIM == i[None, :] // HEAD_DIM).astype(BF16)


def _inproj_kernel(x_ref, g_ref, w_ref, qg_ref, kg_ref, seg_ref,
                   pool_o, q_o, rows_o, win_o, misc_o, gla_o):
    x = x_ref[...]
    ms = jnp.mean(x * x, axis=-1, keepdims=True)
    xn = (x * lax.rsqrt(ms + EPS) * g_ref[...]).astype(BF16)
    seg = seg_ref[...]

    def mm(c0, c1):
        return _dot(xn, w_ref[:, c0:c1])

    pool_o[...] = mm(0, 256)
    for c in range(4):
        a = mm(256 + LANES * c, 256 + LANES * (c + 1))
        q_o[:, LANES * c:LANES * (c + 1)] = (
            _head_rmsnorm(a, qg_ref[...], seg) * (HEAD_DIM ** -0.5)).astype(BF16)
    kv0 = 768
    rows_o[:, 0:256] = mm(kv0, kv0 + 256)
    rows_o[:, 256:384] = _head_rmsnorm(mm(kv0 + 256, kv0 + 384), kg_ref[1:2, :], seg)
    rows_o[:, 384:512] = mm(kv0 + 384, kv0 + 512)
    win_o[:, 0:128] = _head_rmsnorm(mm(kv0 + 512, kv0 + 640), kg_ref[2:3, :], seg)
    win_o[:, 128:256] = mm(kv0 + 640, kv0 + 768)
    misc_o[...] = mm(1536, 1664)
    gla_o[...] = mm(1664, 2432)


def _pad_w_in(w):
    d = w.shape[0]
    return jnp.concatenate([
        w[:, 0:1560], w[:, 2072:2088], jnp.zeros((d, 88), w.dtype),
        w[:, 1560:2072], w[:, 2088:2344]], axis=1)


def _inproj(x, g_mix, w_in_l, q_gain, k_gain):
    n, d = x.shape
    tm = _tile(n, 384)
    w = _pad_w_in(w_in_l).astype(BF16)
    qg = jnp.tile(q_gain, 2)[None, :]
    kg = jnp.zeros((8, LANES), F32).at[0:3].set(jnp.tile(k_gain, (1, 2)))
    full = lambda shape: pl.BlockSpec(shape, lambda i: (0, 0))
    row = lambda c: pl.BlockSpec((tm, c), lambda i: (i, 0))
    return pl.pallas_call(
        _inproj_kernel,
        grid=(n // tm,),
        in_specs=[row(d), full((1, d)), full((d, N_IN_PAD)), full((1, LANES)),
                  full((8, LANES)), full((LANES, LANES))],
        out_specs=[row(256), row(512), row(512), row(256), row(128), row(768)],
        out_shape=[jax.ShapeDtypeStruct((n, 256), F32), jax.ShapeDtypeStruct((n, 512), BF16),
                   jax.ShapeDtypeStruct((n, 512), F32), jax.ShapeDtypeStruct((n, 256), F32),
                   jax.ShapeDtypeStruct((n, 128), F32), jax.ShapeDtypeStruct((n, 768), F32)],
        compiler_params=_params("arbitrary"),
        name="inproj",
    )(x, g_mix[None, :], w, qg, kg, _seg_ones())


def _pool_kernel(buf_ref, u_ref, w_ref, sc_ref, o_ref, ext, *, pos0, tp):
    i = pl.program_id(1)

    @pl.when(i == 0)
    def _():
        ext[0:16, :] = buf_ref[0]

    ext[16:16 + tp, :] = u_ref[0]
    u0 = ext[16:16 + tp, :]
    acc = u0
    sums = {}
    for k in range(1, 16):
        acc = acc + ext[16 - k:16 - k + tp, :]
        if k + 1 in POOL_WINDOWS:
            sums[k + 1] = acc
    lane = lax.broadcasted_iota(jnp.int32, (tp, POOL_WIDTH), 1)
    pos = pos0 + i * tp + lax.broadcasted_iota(jnp.int32, (tp, POOL_WIDTH), 0)
    grp = lane // (POOL_WIDTH // len(POOL_WINDOWS))
    total = sums[16]
    wsize = jnp.full((tp, POOL_WIDTH), 16, jnp.int32)
    for gi, wz in enumerate(POOL_WINDOWS[:-1]):
        total = jnp.where(grp == gi, sums[wz], total)
        wsize = jnp.where(grp == gi, wz, wsize)
    cnt = jnp.minimum(wsize, pos + 1).astype(F32)
    dlt = total / cnt - u0
    o_ref[0] = (_dot(dlt.astype(BF16), w_ref[...]) * sc_ref[...]).astype(o_ref.dtype)
    if tp >= 16:
        ext[0:16, :] = ext[tp:tp + 16, :]


def _pool(u, buf, pool_w_l, pool_scale_l, pos0):
    b, t, c = u.shape
    tp = _tile(t, 512) if t >= 8 else t
    buf16 = jnp.concatenate([jnp.zeros((b, 1, c), F32), buf.astype(F32)], axis=1)
    gw = c // len(POOL_WINDOWS)
    wbd = jnp.zeros((c, c), F32)
    for gi in range(len(POOL_WINDOWS)):
        wbd = wbd.at[gi * gw:(gi + 1) * gw, gi * gw:(gi + 1) * gw].set(pool_w_l[gi])
    return pl.pallas_call(
        functools.partial(_pool_kernel, pos0=pos0, tp=tp),
        grid=(b, t // tp),
        in_specs=[pl.BlockSpec((1, 16, c), lambda i, j: (i, 0, 0)),
                  pl.BlockSpec((1, tp, c), lambda i, j: (i, j, 0)),
                  pl.BlockSpec((c, c), lambda i, j: (0, 0)),
                  pl.BlockSpec((1, c), lambda i, j: (0, 0))],
        out_specs=pl.BlockSpec((1, tp, c), lambda i, j: (i, j, 0)),
        out_shape=jax.ShapeDtypeStruct((b, t, c), BF16),
        scratch_shapes=[pltpu.VMEM((16 + tp, c), F32)],
        compiler_params=_params("arbitrary", "arbitrary"),
        name="pool",
    )(buf16, u, wbd.astype(BF16), pool_scale_l[None, :])


def _gla_kernel(gla_ref, misc_ref, s0_ref, wa_ref, ba_ref, og_ref, seg_ref, eb_ref, mk_ref,
                o_ref, sT_ref, st, qs, ks, bs, qts, kts, vs, os_, *, tg, t_valid):
    i = pl.program_id(1)
    c = GLA_SUB
    nsub = tg // c

    @pl.when(i == 0)
    def _():
        st[...] = s0_ref[0]

    gl = gla_ref[...]
    q = gl[:, 0:128] * (GLA_DK ** -0.5)
    k = gl[:, 128:256]
    v = gl[:, 256:512]
    r = gl[:, 512:768]
    x = _dot(misc_ref[...].astype(BF16), wa_ref[...]) + ba_ref[...]
    la = (jnp.minimum(x, 0.0) - jnp.log1p(jnp.exp(-jnp.abs(x)))) * (1.0 / GLA_TAU)
    row = lax.broadcasted_iota(jnp.int32, (tg, LANES), 0)
    if t_valid is not None:
        la = jnp.where(i * tg + row < t_valid, la, 0.0)
    rr = lax.broadcasted_iota(jnp.int32, (tg, tg), 0)
    cc = lax.broadcasted_iota(jnp.int32, (tg, tg), 1)
    same = (rr // c) == (cc // c)
    tri = (same & (cc <= rr)).astype(BF16)
    allo = same.astype(BF16)
    b = _split3_dot(tri, la)
    blast = _split3_dot(allo, la)
    qs[...] = q
    ks[...] = k
    bs[...] = b
    qts[...] = (q * jnp.exp(b)).astype(BF16)
    kts[...] = (k * jnp.exp(blast - b)).astype(BF16)
    vs[...] = v
    eb = eb_ref[...]
    mk = mk_ref[...]
    tt = lax.broadcasted_iota(jnp.int32, (c, LANES), 0)

    def sub(j, carry):
        r0 = pl.multiple_of(j * c, c)
        qi = qs[pl.ds(r0, c), :]
        ki = ks[pl.ds(r0, c), :]
        bi = bs[pl.ds(r0, c), :]
        vi = vs[pl.ds(r0, c), :]
        s_t = st[...]
        o_inter = _dot_nt(qts[pl.ds(r0, c), :], s_t.astype(BF16))
        parts = []
        for s in range(c):
            dec = jnp.exp(jnp.minimum(bi - bi[s:s + 1, :], 0.0))
            parts.append(jnp.where(tt >= s, qi * ki[s:s + 1, :] * dec, 0.0))
        p_all = jnp.concatenate(parts, axis=0).astype(BF16)
        a_all = _dot(p_all, eb)
        o_diag = a_all[0:c, :] * vi[0:1, :]
        for s in range(1, c):
            o_diag = o_diag + a_all[s * c:(s + 1) * c, :] * vi[s:s + 1, :]
        os_[pl.ds(r0, c), :] = o_inter + o_diag
        u_t = _dot_tn(vi.astype(BF16), kts[pl.ds(r0, c), :])
        dl = jnp.exp(bi[c - 1:c, :])
        st[...] = s_t * dl + u_t * mk
        return carry

    lax.fori_loop(0, nsub, sub, 0)

    o = os_[...]
    seg = seg_ref[...]
    og = og_ref[...]
    sil = r * (1.0 / (1.0 + jnp.exp(-r)))
    for h in range(2):
        sl = slice(h * LANES, (h + 1) * LANES)
        o_ref[:, sl] = (_head_rmsnorm(o[:, sl], og[:, sl], seg) * sil[:, sl]).astype(o_ref.dtype)
    sT_ref[0] = st[...]


def _gla(gla_rows, misc_rows, s0, wa2, ba, o_gain, b, t):
    t_valid = None
    if t % GLA_SUB:
        t_valid = t
        tp = -(-t // GLA_SUB) * GLA_SUB
        pad = lambda z: jnp.pad(z.reshape(b, t, -1), ((0, 0), (0, tp - t), (0, 0))).reshape(b * tp, -1)
        gla_rows, misc_rows = pad(gla_rows), pad(misc_rows)
    else:
        tp = t
    tg = _tile(tp, 256)
    assert tg % GLA_SUB == 0
    nt = tp // tg
    kk = GLA_HEADS * GLA_DK
    vv = GLA_HEADS * GLA_DV
    ki = jnp.arange(kk)
    vi = jnp.arange(vv)
    head_eq = (vi[:, None] // GLA_DV == ki[None, :] // GLA_DK)
    mk = head_eq.astype(F32)
    eb = head_eq.T.astype(BF16)
    s0t = jnp.einsum('bhkv,hg->bhvgk', s0.astype(F32), jnp.eye(GLA_HEADS, dtype=F32)).reshape(b, vv, kk)
    wa = jnp.zeros((LANES, kk), F32).at[24:24 + GLA_RANK].set(wa2).astype(BF16)
    full = lambda shape: pl.BlockSpec(shape, lambda i, j: (0,) * len(shape))
    rows = lambda cdim: pl.BlockSpec((tg, cdim), lambda i, j: (i * nt + j, 0))
    o, s_t = pl.pallas_call(
        functools.partial(_gla_kernel, tg=tg, t_valid=t_valid),
        grid=(b, nt),
        in_specs=[rows(768), rows(128), pl.BlockSpec((1, vv, kk), lambda i, j: (i, 0, 0)),
                  full((LANES, kk)), full((1, kk)), full((1, vv)), full((LANES, LANES)),
                  full((kk, vv)), full((vv, kk))],
        out_specs=[rows(vv), pl.BlockSpec((1, vv, kk), lambda i, j: (i, 0, 0))],
        out_shape=[jax.ShapeDtypeStruct((b * tp, vv), BF16), jax.ShapeDtypeStruct((b, vv, kk), F32)],
        scratch_shapes=[pltpu.VMEM((vv, kk), F32), pltpu.VMEM((tg, kk), F32), pltpu.VMEM((tg, kk), F32),
                        pltpu.VMEM((tg, kk), F32), pltpu.VMEM((tg, kk), BF16), pltpu.VMEM((tg, kk), BF16),
                        pltpu.VMEM((tg, vv), F32), pltpu.VMEM((tg, vv), F32)],
        compiler_params=_params("arbitrary", "arbitrary"),
        name="gla",
    )(gla_rows, misc_rows, s0t, wa, ba[None, :], jnp.tile(o_gain, GLA_HEADS)[None, :], _seg_ones(), eb, mk)
    if tp != t:
        o = o.reshape(b, tp, vv)[:, :t].reshape(b * t, vv)
    s5 = s_t.reshape(b, GLA_HEADS, GLA_DV, GLA_HEADS, GLA_DK)
    s_new = jnp.einsum('bhvgk,hg->bhkv', s5, jnp.eye(GLA_HEADS, dtype=F32))
    return o, s_new


def _compress_rows(read, nb, pe_ref, wk_ref, wv_ref, kg_ref, seg_ref):
    acck = jnp.zeros((nb, LANES), F32)
    accv = jnp.zeros((nb, LANES), F32)
    for j in range(L_CMP):
        xk = (read(0, j) + pe_ref[0, j:j + 1, :]).astype(BF16)
        xv = (read(1, j) + pe_ref[1, j:j + 1, :]).astype(BF16)
        acck = acck + _dot(xk, wk_ref[j])
        accv = accv + _dot(xv, wv_ref[j])
    kc = _head_rmsnorm(acck, kg_ref[0:1, :], seg_ref[...])
    return kc, accv


def _compress_kernel(rows_ref, pe_ref, wk_ref, wv_ref, kg_ref, seg_ref, o_ref, *, nb):
    kc, vc = _compress_rows(lambda kind, j: rows_ref[0, kind, pl.ds(j, nb, stride=L_CMP), :], nb,
                            pe_ref, wk_ref, wv_ref, kg_ref, seg_ref)
    o_ref[0, :, 0:LANES] = kc
    o_ref[0, :, LANES:2 * LANES] = vc


def _cmp_weights(cmp_w, cmp_pe, k_gain):
    def bd(w):
        z = jnp.zeros_like(w)
        return jnp.concatenate([jnp.concatenate([w, z], axis=2), jnp.concatenate([z, w], axis=2)], axis=1)
    pe = jnp.tile(cmp_pe, (1, 1, 2))
    kg = jnp.zeros((8, LANES), F32).at[0].set(jnp.tile(k_gain[0], 2))
    return pe, bd(cmp_w[0]).astype(BF16), bd(cmp_w[1]).astype(BF16), kg


def _compress(rows, cw):
    b, _, tp, _ = rows.shape
    nb = tp // L_CMP
    pe, wk, wv, kg = cw
    full = lambda shape: pl.BlockSpec(shape, lambda i: (0,) * len(shape))
    return pl.pallas_call(
        functools.partial(_compress_kernel, nb=nb),
        grid=(b,),
        in_specs=[pl.BlockSpec((1, 2, tp, LANES), lambda i: (i, 0, 0, 0)), full((2, L_CMP, LANES)),
                  full((L_CMP, LANES, LANES)), full((L_CMP, LANES, LANES)), full((8, LANES)),
                  full((LANES, LANES))],
        out_specs=pl.BlockSpec((1, nb, 256), lambda i: (i, 0, 0)),
        out_shape=jax.ShapeDtypeStruct((b, nb, 256), F32),
        compiler_params=_params("arbitrary"),
        name="compress",
    )(rows, pe, wk, wv, kg, _seg_ones())


def _paged_compress_kernel(pt_ref, cache_ref, pe_ref, wk_ref, wv_ref, kg_ref, seg_ref, o_ref,
                           slab, sem, *, pages, n_slab, page0):
    bi = pl.program_id(0)
    si = pl.program_id(1)
    step = bi * n_slab + si
    nsteps = pl.num_programs(0) * n_slab
    slot = step % 2

    def copy(page, slot_, p, kind):
        return pltpu.make_async_copy(cache_ref.at[page, :, kind * LANES:(kind + 1) * LANES],
                                     slab.at[slot_, kind, pl.ds(p * PAGE_SIZE, PAGE_SIZE), :], sem.at[slot_])

    def issue(bb, ss, slot_):
        for p in range(pages):
            page = page0 + pt_ref[bb, ss * pages + p]
            for kind in range(2):
                copy(page, slot_, p, kind).start()

    @pl.when(step == 0)
    def _():
        issue(0, 0, 0)

    @pl.when(step + 1 < nsteps)
    def _():
        nxt = step + 1
        issue(nxt // n_slab, nxt % n_slab, 1 - slot)

    for p in range(pages):
        for kind in range(2):
            copy(page0, slot, p, kind).wait()

    nb = pages * (PAGE_SIZE // L_CMP)
    kc, vc = _compress_rows(lambda kind, j: slab[slot, kind, pl.ds(j, nb, stride=L_CMP), :], nb,
                            pe_ref, wk_ref, wv_ref, kg_ref, seg_ref)
    o_ref[0, :, 0:LANES] = kc
    o_ref[0, :, LANES:2 * LANES] = vc


def _paged_compress(cache2, page_table, page0, cw):
    b, n_pages = page_table.shape
    pages = math.gcd(n_pages, 32)
    n_slab = n_pages // pages
    nb = pages * (PAGE_SIZE // L_CMP)
    pe, wk, wv, kg = cw
    full = lambda shape: pl.BlockSpec(shape, lambda i, j, pt: (0,) * len(shape))
    grid_spec = pltpu.PrefetchScalarGridSpec(
        num_scalar_prefetch=1,
        grid=(b, n_slab),
        in_specs=[pl.BlockSpec(memory_space=pl.ANY), full((2, L_CMP, LANES)),
                  full((L_CMP, LANES, LANES)), full((L_CMP, LANES, LANES)), full((8, LANES)),
                  full((LANES, LANES))],
        out_specs=pl.BlockSpec((1, nb, 256), lambda i, j, pt: (i, j, 0)),
        scratch_shapes=[pltpu.VMEM((2, 2, pages * PAGE_SIZE, LANES), F32), pltpu.SemaphoreType.DMA((2,))],
    )
    return pl.pallas_call(
        functools.partial(_paged_compress_kernel, pages=pages, n_slab=n_slab, page0=page0),
        grid_spec=grid_spec,
        out_shape=jax.ShapeDtypeStruct((b, n_pages * (PAGE_SIZE // L_CMP), 256), F32),
        compiler_params=_params("arbitrary", "arbitrary"),
        name="paged_compress",
    )(page_table, cache2, pe, wk, wv, kg, _seg_ones())


def _nsa_prompt_kernel(q_ref, kc_ref, vc_ref, ks_ref, vs_ref, kw_ref, vw_ref, gate_ref, ex_ref,
                       o_ref, *, tq, ck):
    i = pl.program_id(1)
    t0 = i * tq
    n_slc = LANES // 2
    row_t = t0 + lax.broadcasted_iota(jnp.int32, (tq, 1), 0)
    lane = lax.broadcasted_iota(jnp.int32, (1, LANES), 1)
    blk = lane % n_slc
    nat = jnp.where(lane < n_slc, 2 * lane, 2 * (lane - n_slc) + 1)
    cmp_ok = ((nat + 1) * L_CMP - 1) <= row_t
    cur = row_t // L_SLC
    done = blk < cur
    gx = gate_ref[...]
    gates = 1.0 / (1.0 + jnp.exp(-gx))
    n_chunks = (t0 + tq + ck - 1) // ck
    kstart = pl.multiple_of(jnp.maximum(t0 - WINDOW, 0), tq)
    wlen = WINDOW + tq
    wpos = kstart + lax.broadcasted_iota(jnp.int32, (1, wlen), 1)
    win_ok = (wpos <= row_t) & (wpos >= row_t - WINDOW)

    for g in range(NSA_KV):
        qg = q_ref[0, g].reshape(NSA_HPG * tq, HEAD_DIM)
        s_c = _dot_nt(qg, kc_ref[0, g]).reshape(NSA_HPG, tq, LANES)
        p_c = _masked_softmax(s_c, cmp_ok[None])
        o_c = _dot(p_c.reshape(NSA_HPG * tq, LANES).astype(BF16), vc_ref[0, g])
        imp = p_c[0] + p_c[1] + p_c[2] + p_c[3]
        imp = imp + pltpu.roll(imp, n_slc, axis=1)
        imp = jnp.where(done, imp, -jnp.inf)
        rank = jnp.zeros((tq, LANES), F32)
        for r in range(1, n_slc):
            vm = pltpu.roll(imp, r, axis=1)
            ahead = (vm > imp) | ((blk >= r) & (vm == imp))
            rank = rank + jnp.where(ahead, 1.0, 0.0)
        sel = (done & (rank < N_SEL - 1)) | (blk == cur)
        selb = jnp.where(sel, 1.0, 0.0).astype(BF16)

        def chunk(c, carry):
            m, l, acc = carry
            k0 = pl.multiple_of(c * ck, ck)
            s = _dot_nt(qg, ks_ref[0, g, pl.ds(k0, ck), :]).reshape(NSA_HPG, tq, ck)
            hit = _dot(selb, ex_ref[c])
            kpos = k0 + lax.broadcasted_iota(jnp.int32, (1, ck), 1)
            ok = ((hit > 0.5) & (kpos <= row_t))[None]
            sm = jnp.where(ok, s, MASKED)
            m_new = jnp.maximum(m, jnp.max(sm, axis=-1, keepdims=True))
            p = jnp.where(ok, jnp.exp(sm - m_new), 0.0)
            alpha = jnp.exp(m - m_new)
            l = alpha * l + jnp.sum(p, axis=-1, keepdims=True)
            pv = _dot(p.reshape(NSA_HPG * tq, ck).astype(BF16), vs_ref[0, g, pl.ds(k0, ck), :])
            acc = alpha * acc + pv.reshape(NSA_HPG, tq, HEAD_DIM)
            return m_new, l, acc

        m0 = jnp.full((NSA_HPG, tq, 1), MASKED, F32)
        l0 = jnp.zeros((NSA_HPG, tq, 1), F32)
        a0 = jnp.zeros((NSA_HPG, tq, HEAD_DIM), F32)
        _, l_s, acc_s = lax.fori_loop(0, n_chunks, chunk, (m0, l0, a0))
        o_s = acc_s * (1.0 / jnp.where(l_s > 0.0, l_s, 1.0))
        s_w = _dot_nt(qg, kw_ref[0, g, pl.ds(kstart, wlen), :]).reshape(NSA_HPG, tq, wlen)
        p_w = _masked_softmax(s_w, win_ok[None])
        o_w = _dot(p_w.reshape(NSA_HPG * tq, wlen).astype(BF16), vw_ref[0, g, pl.ds(kstart, wlen), :])
        o_c = o_c.reshape(NSA_HPG, tq, HEAD_DIM)
        o_w = o_w.reshape(NSA_HPG, tq, HEAD_DIM)
        for j in range(NSA_HPG):
            h = g * NSA_HPG + j
            o = (gates[:, 3 * h:3 * h + 1] * o_c[j] + gates[:, 3 * h + 1:3 * h + 2] * o_s[j]
                 + gates[:, 3 * h + 2:3 * h + 3] * o_w[j])
            o_ref[0, g, j] = o.astype(o_ref.dtype)


def _nsa_prompt(q_rows, rows, win, misc, cw, b, t):
    assert t // L_CMP == LANES and t % L_SLC == 0, "prompt kernel is laid out for 128 compressed blocks"
    tq = 256
    ck = 512
    n_slc = t // L_SLC
    cmp = _compress(rows[:, 0:256].reshape(b, t, 2, LANES).transpose(0, 2, 1, 3), cw)
    order = jnp.concatenate([jnp.arange(0, LANES, 2), jnp.arange(1, LANES, 2)])
    cmp = cmp[:, order].reshape(b, LANES, 2, NSA_KV, HEAD_DIM).astype(BF16)
    kc = cmp[:, :, 0].transpose(0, 2, 1, 3)
    vc = cmp[:, :, 1].transpose(0, 2, 1, 3)
    split = lambda z: z.reshape(b, t, NSA_KV, HEAD_DIM).transpose(0, 2, 1, 3).astype(BF16)
    ks, vs = split(rows[:, 256:384]), split(rows[:, 384:512])
    kw, vw = split(win[:, 0:128]), split(win[:, 128:256])
    q5 = q_rows.reshape(b, t, NSA_KV, NSA_HPG, HEAD_DIM).transpose(0, 2, 3, 1, 4)
    key_blk = jnp.arange(t) // L_SLC
    ex = (jnp.arange(LANES)[:, None] == key_blk[None, :]) & (jnp.arange(LANES)[:, None] < n_slc)
    ex = ex.astype(BF16).reshape(LANES, t // ck, ck).transpose(1, 0, 2)
    nt = t // tq
    per_b = lambda shape: pl.BlockSpec(shape, lambda i, j: (i,) + (0,) * (len(shape) - 1))
    out = pl.pallas_call(
        functools.partial(_nsa_prompt_kernel, tq=tq, ck=ck),
        grid=(b, nt),
        in_specs=[pl.BlockSpec((1, NSA_KV, NSA_HPG, tq, HEAD_DIM), lambda i, j: (i, 0, 0, j, 0)),
                  per_b((1, NSA_KV, LANES, HEAD_DIM)), per_b((1, NSA_KV, LANES, HEAD_DIM)),
                  per_b((1, NSA_KV, t, HEAD_DIM)), per_b((1, NSA_KV, t, HEAD_DIM)),
                  per_b((1, NSA_KV, t, HEAD_DIM)), per_b((1, NSA_KV, t, HEAD_DIM)),
                  pl.BlockSpec((tq, LANES), lambda i, j: (i * nt + j, 0)),
                  pl.BlockSpec((t // ck, LANES, ck), lambda i, j: (0, 0, 0))],
        out_specs=pl.BlockSpec((1, NSA_KV, NSA_HPG, tq, HEAD_DIM), lambda i, j: (i, 0, 0, j, 0)),
        out_shape=jax.ShapeDtypeStruct((b, NSA_KV, NSA_HPG, t, HEAD_DIM), BF16),
        compiler_params=_params("arbitrary", "arbitrary"),
        name="nsa_prompt",
    )(q5, kc, vc, ks, vs, kw, vw, misc, ex)
    return out.transpose(0, 3, 1, 2, 4).reshape(b * t, NSA_WIDTH)


def _softmax_masked_jnp(s, mask):
    s = jnp.where(mask, s, -jnp.inf)
    m = jnp.max(s, axis=-1, keepdims=True)
    m = jnp.where(jnp.isfinite(m), m, 0.0)
    e = jnp.exp(s - m)
    den = jnp.sum(e, axis=-1, keepdims=True)
    return e / jnp.where(den > 0.0, den, 1.0)


def _bdot(eq, a, b):
    return jnp.einsum(eq, a.astype(BF16), b.astype(BF16), preferred_element_type=F32)


def _nsa_sample(q_rows, rows, win, misc, cw, cache2, cache6, page_table, page0, win_state, b, t, past_len):
    t_all = past_len + t
    t_pad = -(-t_all // L_SLC) * L_SLC
    n_cmp = t_pad // L_CMP
    n_slc = t_pad // L_SLC
    assert past_len % PAGE_SIZE == 0
    cmp_past = _paged_compress(cache2, page_table, page0, cw)
    tail = jnp.concatenate([rows[:, 0:256].reshape(b, t, 256),
                            jnp.zeros((b, t_pad - t_all, 256), F32)], axis=1)
    n_tail = (t_pad - past_len) // L_CMP
    cmp_tail = _compress(tail.reshape(1, b * (t_pad - past_len), 2, LANES).transpose(0, 2, 1, 3), cw)
    cmp_tail = cmp_tail.reshape(b, n_tail, 256)
    cmp = jnp.concatenate([cmp_past, cmp_tail], axis=1).reshape(b, n_cmp, 2, NSA_KV, HEAD_DIM)
    kc, vc = cmp[:, :, 0], cmp[:, :, 1]
    q = q_rows.reshape(b, t, NSA_KV, NSA_HPG, HEAD_DIM)
    gates = jax.nn.sigmoid(misc[:, 0:3 * NSA_HEADS]).reshape(b, t, NSA_KV, NSA_HPG, 3)
    pos = past_len + jnp.arange(t)
    cmp_end = (jnp.arange(n_cmp) + 1) * L_CMP - 1
    s_c = _bdot('bqgjd,bngd->bgjqn', q, kc)
    p_c = _softmax_masked_jnp(s_c, (cmp_end[None, :] <= pos[:, None])[None, None, None])
    o_c = _bdot('bgjqn,bngd->bqgjd', p_c, vc)
    imp = p_c.sum(axis=2).reshape(b, NSA_KV, t, n_slc, L_SLC // L_CMP).sum(-1)
    cur = pos // L_SLC
    done = jnp.arange(n_slc)[None, :] < cur[:, None]
    imp = jnp.where(done[None, None], imp, -jnp.inf)
    k_sel = min(N_SEL - 1, n_slc)
    _, top = lax.top_k(imp, k_sel)
    cur_b = jnp.broadcast_to(cur[None, None, :, None], (b, NSA_KV, t, 1))
    sel = jnp.concatenate([top, cur_b], axis=-1)
    sel_ok = jnp.concatenate([top < cur_b, jnp.ones_like(cur_b, dtype=bool)], axis=-1)
    n_past_blk = past_len // L_SLC
    per_page = PAGE_SIZE // L_SLC
    selp = jnp.minimum(sel, n_past_blk - 1)
    b_ix = jnp.arange(b)[:, None, None, None]
    g_ix = jnp.arange(NSA_KV)[None, :, None, None]
    half = (page0 + page_table[b_ix, selp // per_page]) * per_page + selp % per_page
    k_past = cache6[half, :, 2, g_ix, :]
    v_past = cache6[half, :, 3, g_ix, :]
    tail_s = jnp.concatenate([rows[:, 256:512].reshape(b, t, 256),
                              jnp.zeros((b, t_pad - t_all, 256), F32)], axis=1)
    tail_s = tail_s.reshape(b, (t_pad - past_len) // L_SLC, L_SLC, 2, NSA_KV, HEAD_DIM)
    selt = jnp.clip(sel - n_past_blk, 0, tail_s.shape[1] - 1)
    k_tail = tail_s[b_ix, selt, :, 0, g_ix, :]
    v_tail = tail_s[b_ix, selt, :, 1, g_ix, :]
    from_tail = (sel >= n_past_blk)[..., None, None]
    k_g = jnp.where(from_tail, k_tail, k_past)
    v_g = jnp.where(from_tail, v_tail, v_past)
    n_keys = sel.shape[-1] * L_SLC
    key_pos = sel[..., None] * L_SLC + jnp.arange(L_SLC)
    m_s = (sel_ok[..., None] & (key_pos <= pos[None, None, :, None, None])).reshape(b, NSA_KV, 1, t, n_keys)
    s_s = _bdot('bqgjd,bgqkld->bgjqkl', q, k_g)
    p_s = _softmax_masked_jnp(s_s.reshape(b, NSA_KV, NSA_HPG, t, n_keys), m_s).reshape(s_s.shape)
    o_s = _bdot('bgjqkl,bgqkld->bqgjd', p_s, v_g)
    w_buf = win_state.shape[1]
    win_ext = jnp.concatenate([win_state.reshape(b, w_buf, 2, NSA_KV, HEAD_DIM),
                               win.reshape(b, t, 2, NSA_KV, HEAD_DIM)], axis=1)
    w_pos = past_len - w_buf + jnp.arange(w_buf + t)
    m_w = (w_pos[None, :] >= 0) & (w_pos[None, :] <= pos[:, None]) & (w_pos[None, :] >= pos[:, None] - WINDOW)
    s_w = _bdot('bqgjd,bkgd->bgjqk', q, win_ext[:, :, 0])
    p_w = _softmax_masked_jnp(s_w, m_w[None, None, None])
    o_w = _bdot('bgjqk,bkgd->bqgjd', p_w, win_ext[:, :, 1])
    o = gates[..., 0:1] * o_c + gates[..., 1:2] * o_s + gates[..., 2:3] * o_w
    return o.reshape(b * t, NSA_WIDTH).astype(BF16), win_ext[:, -w_buf:]


def _outproj_kernel(yp_ref, yn_ref, yg_ref, x_ref, w_ref, g_ref, *rest, with_router):
    if with_router:
        r_ref, h_o, hn_o, lg_o = rest
    else:
        h_o, hn_o = rest
    h = (x_ref[...] + _dot(yp_ref[...], w_ref[0:256, :]) + _dot(yn_ref[...], w_ref[256:768, :])
         + _dot(yg_ref[...], w_ref[768:1024, :]))
    h_o[...] = h
    ms = jnp.mean(h * h, axis=-1, keepdims=True)
    hn = (h * lax.rsqrt(ms + EPS) * g_ref[...]).astype(BF16)
    hn_o[...] = hn
    if with_router:
        lg_o[...] = _dot(hn, r_ref[...])


def _outproj(y_pool, y_nsa, y_gla, x, w_out_l, g_ffn, router):
    n, d = x.shape
    tm = _tile(n, 384)
    with_router = router is not None
    row = lambda c: pl.BlockSpec((tm, c), lambda i: (i, 0))
    full = lambda shape: pl.BlockSpec(shape, lambda i: (0, 0))
    in_specs = [row(256), row(512), row(256), row(d), full((d, d)), full((1, d))]
    args = [y_pool, y_nsa, y_gla, x, w_out_l.astype(BF16), g_ffn[None, :]]
    out_specs = [row(d), row(d)]
    out_shape = [jax.ShapeDtypeStruct((n, d), F32), jax.ShapeDtypeStruct((n, d), BF16)]
    if with_router:
        ne = router.shape[1]
        in_specs.append(full((d, LANES)))
        args.append(jnp.zeros((d, LANES), F32).at[:, :ne].set(router).astype(BF16))
        out_specs.append(row(LANES))
        out_shape.append(jax.ShapeDtypeStruct((n, LANES), F32))
    return pl.pallas_call(
        functools.partial(_outproj_kernel, with_router=with_router),
        grid=(n // tm,), in_specs=in_specs, out_specs=out_specs, out_shape=out_shape,
        compiler_params=_params("arbitrary"), name="outproj",
    )(*args)


def _swiglu_kernel(be_ref, bv_ref, x_ref, gate_ref, w1_ref, w3_ref, w2_ref, o_ref):
    i = pl.program_id(0)
    j = pl.program_id(1)
    valid = bv_ref[i] > 0

    @pl.when(jnp.logical_not(valid) & (j == 0))
    def _():
        o_ref[...] = jnp.zeros_like(o_ref)

    @pl.when(valid)
    def _():
        x = x_ref[...]
        a = _dot(x, w1_ref[0])
        c = _dot(x, w3_ref[0])
        hmid = (a * (1.0 / (1.0 + jnp.exp(-a))) * c).astype(BF16)
        y = _dot(hmid, w2_ref[0])

        @pl.when(j == 0)
        def _():
            o_ref[...] = y

        @pl.when(j > 0)
        def _():
            o_ref[...] += y

        @pl.when(j == pl.num_programs(1) - 1)
        def _():
            o_ref[...] = o_ref[...] * gate_ref[...]


def _swiglu(x, gate, blk_expert, blk_valid, w1, w3, w2, tm):
    r, d = x.shape
    f = w1.shape[2]
    tf = f // 2 if (f // 2) % LANES == 0 else f
    grid_spec = pltpu.PrefetchScalarGridSpec(
        num_scalar_prefetch=2,
        grid=(r // tm, f // tf),
        in_specs=[pl.BlockSpec((tm, d), lambda i, j, be, bv: (i, 0)),
                  pl.BlockSpec((tm, 1), lambda i, j, be, bv: (i, 0)),
                  pl.BlockSpec((1, d, tf), lambda i, j, be, bv: (be[i], 0, j)),
                  pl.BlockSpec((1, d, tf), lambda i, j, be, bv: (be[i], 0, j)),
                  pl.BlockSpec((1, tf, d), lambda i, j, be, bv: (be[i], j, 0))],
        out_specs=pl.BlockSpec((tm, d), lambda i, j, be, bv: (i, 0)),
    )
    return pl.pallas_call(
        _swiglu_kernel, grid_spec=grid_spec,
        out_shape=jax.ShapeDtypeStruct((r, d), F32),
        compiler_params=_params("arbitrary", "arbitrary"), name="swiglu",
    )(blk_expert, blk_valid, x, gate, w1, w3, w2)


def _dense_ffn(h, hn, w1, w3, w2):
    n = h.shape[0]
    tm = _tile(n, 704)
    nblk = n // tm
    y = _swiglu(hn, jnp.ones((n, 1), F32), jnp.zeros((nblk,), jnp.int32), jnp.ones((nblk,), jnp.int32),
                w1[None].astype(BF16), w3[None].astype(BF16), w2[None].astype(BF16), tm)
    return h + y


def _moe_ffn(h, hn, logits, w1, w3, w2):
    n = h.shape[0]
    ne = w1.shape[0]
    tm = 512
    top_v, top_i = lax.top_k(logits[:, :ne], TOP_K)
    gate = jax.nn.softmax(top_v, axis=-1)
    a = n * TOP_K
    e_a = top_i.reshape(-1)
    tok_a = jnp.repeat(jnp.arange(n), TOP_K)
    g_a = gate.reshape(-1)
    order = jnp.argsort(e_a)
    e_s, tok_s, g_s = e_a[order], tok_a[order], g_a[order]
    counts = jnp.bincount(e_a, length=ne)
    start = jnp.cumsum(counts) - counts
    padded = (counts + tm - 1) // tm * tm
    p_end = jnp.cumsum(padded)
    p_start = p_end - padded
    dest = p_start[e_s] + (jnp.arange(a) - start[e_s])
    nblk = -(-a // tm) + ne
    r = nblk * tm
    row_tok = jnp.zeros((r,), jnp.int32).at[dest].set(tok_s.astype(jnp.int32))
    row_gate = jnp.zeros((r,), F32).at[dest].set(g_s)
    blk0 = jnp.arange(nblk) * tm
    blk_valid = (blk0 < p_end[-1]).astype(jnp.int32)
    last = jnp.clip(jnp.searchsorted(p_end, p_end[-1] - 1, side='right'), 0, ne - 1)
    blk_expert = jnp.clip(jnp.searchsorted(p_end, blk0, side='right'), 0, ne - 1)
    blk_expert = jnp.where(blk_valid > 0, blk_expert, last).astype(jnp.int32)
    yb = _swiglu(hn[row_tok], row_gate[:, None], blk_expert, blk_valid,
                 w1.astype(BF16), w3.astype(BF16), w2.astype(BF16), tm)
    pos = jnp.zeros((a,), jnp.int32).at[order].set(dest.astype(jnp.int32)).reshape(n, TOP_K)
    return h + (yb[pos[:, 0]] + yb[pos[:, 1]])


def kernel(x_prompt, x_sample, cache_nsa_kv, state_nsa_win, state_gla, state_pool, page_table,
           norm_mix, norm_ffn, w_in, w_out, pool_w, pool_scale, nsa_q_norm, nsa_k_norm,
           nsa_cmp_w, nsa_cmp_pe, gla_wa2, gla_ba, gla_norm, ffn_w1, ffn_w3, ffn_w2,
           moe_router, moe_w1, moe_w3, moe_w2):
    bp, tp, d = x_prompt.shape
    bs, ts, _ = x_sample.shape
    depth = w_in.shape[0]
    n_pool = cache_nsa_kv.shape[1]
    n_pages = page_table.shape[1]
    past_len = n_pages * PAGE_SIZE
    w_buf = state_nsa_win.shape[2]
    npr = bp * tp
    cache2 = cache_nsa_kv.reshape(depth * n_pool, PAGE_SIZE, 4 * NSA_KV * HEAD_DIM)
    cache6 = cache_nsa_kv.reshape(depth * n_pool * (PAGE_SIZE // L_SLC), L_SLC, 4, NSA_KV, HEAD_DIM)
    x = jnp.concatenate([x_prompt.reshape(npr, d), x_sample.reshape(bs * ts, d)], axis=0)
    kv_p, kv_s, win_p, win_s, gla_p, gla_s, pool_p, pool_s = [], [], [], [], [], [], [], []
    n_win_p = min(WINDOW, tp)
    for l in range(depth):
        u_pool, q_rows, rows, win, misc, gla_rows = _inproj(x, norm_mix[l], w_in[l], nsa_q_norm[l], nsa_k_norm[l])
        cw = _cmp_weights(nsa_cmp_w[l], nsa_cmp_pe[l], nsa_k_norm[l])
        up = u_pool[:npr].reshape(bp, tp, POOL_WIDTH)
        us = u_pool[npr:].reshape(bs, ts, POOL_WIDTH)
        yp_pool = _pool(up, jnp.zeros((bp, POOL_BUF, POOL_WIDTH), F32), pool_w[l], pool_scale[l], 0)
        ys_pool = _pool(us, state_pool[l], pool_w[l], pool_scale[l], past_len)
        pool_p.append(jnp.concatenate([jnp.zeros((bp, POOL_BUF, POOL_WIDTH), F32), up], axis=1)[:, -POOL_BUF:])
        pool_s.append(jnp.concatenate([state_pool[l], us], axis=1)[:, -POOL_BUF:])
        yp_nsa = _nsa_prompt(q_rows[:npr], rows[:npr], win[:npr], misc[:npr], cw, bp, tp)
        ys_nsa, win_new_s = _nsa_sample(q_rows[npr:], rows[npr:], win[npr:], misc[npr:], cw, cache2, cache6,
                                        page_table, l * n_pool, state_nsa_win[l], bs, ts, past_len)
        kv_p.append(rows[:npr].reshape(bp, tp, 4, NSA_KV, HEAD_DIM))
        kv_s.append(rows[npr:].reshape(bs, ts, 4, NSA_KV, HEAD_DIM))
        win_full_p = jnp.concatenate([jnp.zeros((bp, WINDOW, 256), F32), win[:npr].reshape(bp, tp, 256)], axis=1)
        win_p.append(win_full_p[:, -n_win_p:].reshape(bp, n_win_p, 2, NSA_KV, HEAD_DIM))
        win_s.append(win_new_s.reshape(bs, w_buf, 2, NSA_KV, HEAD_DIM))
        yp_gla, sp = _gla(gla_rows[:npr], misc[:npr], jnp.zeros((bp, GLA_HEADS, GLA_DK, GLA_DV), F32),
                          gla_wa2[l], gla_ba[l], gla_norm[l], bp, tp)
        ys_gla, ss = _gla(gla_rows[npr:], misc[npr:], state_gla[l], gla_wa2[l], gla_ba[l], gla_norm[l], bs, ts)
        gla_p.append(sp.astype(state_gla.dtype))
        gla_s.append(ss.astype(state_gla.dtype))
        i = l // 2
        router = moe_router[i] if l % 2 else None
        res = _outproj(jnp.concatenate([yp_pool.reshape(npr, -1), ys_pool.reshape(bs * ts, -1)], axis=0),
                       jnp.concatenate([yp_nsa, ys_nsa], axis=0),
                       jnp.concatenate([yp_gla, ys_gla], axis=0), x, w_out[l], norm_ffn[l], router)
        if l % 2 == 0:
            h, hn = res
            x = _dense_ffn(h, hn, ffn_w1[i], ffn_w3[i], ffn_w2[i])
        else:
            h, hn, logits = res
            x = _moe_ffn(h, hn, logits, moe_w1[i], moe_w3[i], moe_w2[i])
    return (x[:npr].reshape(bp, tp, d), x[npr:].reshape(bs, ts, d),
            jnp.stack(kv_p), jnp.stack(kv_s), jnp.stack(win_p), jnp.stack(win_s),
            jnp.stack(gla_p), jnp.stack(gla_s), jnp.stack(pool_p), jnp.stack(pool_s))
```

```python
import functools
import math

import jax
import jax.numpy as jnp
from jax import lax
from jax.experimental import pallas as pl
from jax.experimental.pallas import tpu as pltpu

F32 = jnp.float32
BF16 = jnp.bfloat16

EPS = 1e-6
LANES = 128
HEAD_DIM = 64
PAGE_SIZE = 128
POOL_WINDOWS = (2, 4, 8, 16)
POOL_BUF = 15
POOL_WIDTH = 256
NSA_WIDTH = 512
NSA_HEADS = 8
NSA_KV = 2
NSA_HPG = 4
L_CMP = 32
L_SLC = 64
N_SEL = 16
WINDOW = 512
GLA_HEADS = 4
GLA_DK = 32
GLA_DV = 64
GLA_WIDTH = 256
GLA_RANK = 16
GLA_TAU = 16.0
GLA_SUB = 16
TOP_K = 2
N_IN_PAD = 2432
MASKED = -1e30
VMEM_LIMIT = 56 * 1024 * 1024


def _params(*sem):
    return pltpu.CompilerParams(dimension_semantics=sem, vmem_limit_bytes=VMEM_LIMIT)


def _tile(n, target):
    best = None
    for t in range(8, min(n, target) + 1, 8):
        if n % t == 0:
            best = t
    assert best is not None, (n, target)
    return best


def _dot(a, b):
    return jnp.dot(a, b, preferred_element_type=F32)


def _dot_nt(a, b):
    return lax.dot_general(a, b, (((1,), (1,)), ((), ())), preferred_element_type=F32)


def _dot_tn(a, b):
    return lax.dot_general(a, b, (((0,), (0,)), ((), ())), preferred_element_type=F32)


def _split2_dot(a, ones):
    hi = a.astype(BF16)
    lo = (a - hi.astype(F32)).astype(BF16)
    return _dot(hi, ones) + _dot(lo, ones)


def _split3_dot(ones, a):
    a1 = a.astype(BF16)
    r1 = a - a1.astype(F32)
    a2 = r1.astype(BF16)
    a3 = (r1 - a2.astype(F32)).astype(BF16)
    return _dot(ones, a1) + _dot(ones, a2) + _dot(ones, a3)


def _head_rmsnorm(a, gain_row, seg_ones):
    ms = _split2_dot(a * a, seg_ones) * (1.0 / HEAD_DIM)
    return a * lax.rsqrt(ms + EPS) * gain_row


def _masked_softmax(s, mask):
    sm = jnp.where(mask, s, MASKED)
    m = jnp.max(sm, axis=-1, keepdims=True)
    e = jnp.where(mask, jnp.exp(sm - m), 0.0)
    den = jnp.sum(e, axis=-1, keepdims=True)
    return e * (1.0 / jnp.where(den > 0.0, den, 1.0))


def _seg_ones():
    i = jnp.arange(LANES)
    return (i[:, None] // HEAD_DIM == i[None, :] // HEAD_DIM).astype(BF16)


def _inproj_kernel(x_ref, g_ref, w_ref, qg_ref, kg_ref, seg_ref,
                   pool_o, q_o, rows_o, win_o, misc_o, gla_o):
    x = x_ref[...]
    ms = jnp.mean(x * x, axis=-1, keepdims=True)
    xn = (x * lax.rsqrt(ms + EPS) * g_ref[...]).astype(BF16)
    seg = seg_ref[...]

    def mm(c0, c1):
        return _dot(xn, w_ref[:, c0:c1])

    pool_o[...] = mm(0, 256)
    for c in range(4):
        a = mm(256 + LANES * c, 256 + LANES * (c + 1))
        q_o[:, LANES * c:LANES * (c + 1)] = (
            _head_rmsnorm(a, qg_ref[...], seg) * (HEAD_DIM ** -0.5)).astype(BF16)
    kv0 = 768
    rows_o[:, 0:256] = mm(kv0, kv0 + 256)
    rows_o[:, 256:384] = _head_rmsnorm(mm(kv0 + 256, kv0 + 384), kg_ref[1:2, :], seg)
    rows_o[:, 384:512] = mm(kv0 + 384, kv0 + 512)
    win_o[:, 0:128] = _head_rmsnorm(mm(kv0 + 512, kv0 + 640), kg_ref[2:3, :], seg)
    win_o[:, 128:256] = mm(kv0 + 640, kv0 + 768)
    misc_o[...] = mm(1536, 1664)
    gla_o[...] = mm(1664, 2432)


def _pad_w_in(w):
    d = w.shape[0]
    return jnp.concatenate([
        w[:, 0:1560], w[:, 2072:2088], jnp.zeros((d, 88), w.dtype),
        w[:, 1560:2072], w[:, 2088:2344]], axis=1)


def _inproj(x, g_mix, w_in_l, q_gain, k_gain):
    n, d = x.shape
    tm = _tile(n, 384)
    w = _pad_w_in(w_in_l).astype(BF16)
    qg = jnp.tile(q_gain, 2)[None, :]
    kg = jnp.zeros((8, LANES), F32).at[0:3].set(jnp.tile(k_gain, (1, 2)))
    full = lambda shape: pl.BlockSpec(shape, lambda i: (0, 0))
    row = lambda c: pl.BlockSpec((tm, c), lambda i: (i, 0))
    return pl.pallas_call(
        _inproj_kernel,
        grid=(n // tm,),
        in_specs=[row(d), full((1, d)), full((d, N_IN_PAD)), full((1, LANES)),
                  full((8, LANES)), full((LANES, LANES))],
        out_specs=[row(256), row(512), row(512), row(256), row(128), row(768)],
        out_shape=[jax.ShapeDtypeStruct((n, 256), F32), jax.ShapeDtypeStruct((n, 512), BF16),
                   jax.ShapeDtypeStruct((n, 512), F32), jax.ShapeDtypeStruct((n, 256), F32),
                   jax.ShapeDtypeStruct((n, 128), F32), jax.ShapeDtypeStruct((n, 768), F32)],
        compiler_params=_params("arbitrary"),
        name="inproj",
    )(x, g_mix[None, :], w, qg, kg, _seg_ones())


def _pool_kernel(buf_ref, u_ref, w_ref, sc_ref, o_ref, ext, *, pos0, tp):
    i = pl.program_id(1)

    @pl.when(i == 0)
    def _():
        ext[0:16, :] = buf_ref[0]

    ext[16:16 + tp, :] = u_ref[0]
    u0 = ext[16:16 + tp, :]
    acc = u0
    sums = {}
    for k in range(1, 16):
        acc = acc + ext[16 - k:16 - k + tp, :]
        if k + 1 in POOL_WINDOWS:
            sums[k + 1] = acc
    lane = lax.broadcasted_iota(jnp.int32, (tp, POOL_WIDTH), 1)
    pos = pos0 + i * tp + lax.broadcasted_iota(jnp.int32, (tp, POOL_WIDTH), 0)
    grp = lane // (POOL_WIDTH // len(POOL_WINDOWS))
    total = sums[16]
    wsize = jnp.full((tp, POOL_WIDTH), 16, jnp.int32)
    for gi, wz in enumerate(POOL_WINDOWS[:-1]):
        total = jnp.where(grp == gi, sums[wz], total)
        wsize = jnp.where(grp == gi, wz, wsize)
    cnt = jnp.minimum(wsize, pos + 1).astype(F32)
    dlt = total / cnt - u0
    o_ref[0] = (_dot(dlt.astype(BF16), w_ref[...]) * sc_ref[...]).astype(o_ref.dtype)
    if tp >= 16:
        ext[0:16, :] = ext[tp:tp + 16, :]


def _pool(u, buf, pool_w_l, pool_scale_l, pos0):
    b, t, c = u.shape
    tp = _tile(t, 512) if t >= 8 else t
    buf16 = jnp.concatenate([jnp.zeros((b, 1, c), F32), buf.astype(F32)], axis=1)
    gw = c // len(POOL_WINDOWS)
    wbd = jnp.zeros((c, c), F32)
    for gi in range(len(POOL_WINDOWS)):
        wbd = wbd.at[gi * gw:(gi + 1) * gw, gi * gw:(gi + 1) * gw].set(pool_w_l[gi])
    return pl.pallas_call(
        functools.partial(_pool_kernel, pos0=pos0, tp=tp),
        grid=(b, t // tp),
        in_specs=[pl.BlockSpec((1, 16, c), lambda i, j: (i, 0, 0)),
                  pl.BlockSpec((1, tp, c), lambda i, j: (i, j, 0)),
                  pl.BlockSpec((c, c), lambda i, j: (0, 0)),
                  pl.BlockSpec((1, c), lambda i, j: (0, 0))],
        out_specs=pl.BlockSpec((1, tp, c), lambda i, j: (i, j, 0)),
        out_shape=jax.ShapeDtypeStruct((b, t, c), BF16),
        scratch_shapes=[pltpu.VMEM((16 + tp, c), F32)],
        compiler_params=_params("arbitrary", "arbitrary"),
        name="pool",
    )(buf16, u, wbd.astype(BF16), pool_scale_l[None, :])


def _gla_kernel(gla_ref, misc_ref, s0_ref, wa_ref, ba_ref, og_ref, seg_ref, eb_ref, mk_ref,
                o_ref, sT_ref, st, qs, ks, bs, qts, kts, vs, os_, *, tg, t_valid):
    i = pl.program_id(1)
    c = GLA_SUB
    nsub = tg // c

    @pl.when(i == 0)
    def _():
        st[...] = s0_ref[0]

    gl = gla_ref[...]
    q = gl[:, 0:128] * (GLA_DK ** -0.5)
    k = gl[:, 128:256]
    v = gl[:, 256:512]
    r = gl[:, 512:768]
    x = _dot(misc_ref[...].astype(BF16), wa_ref[...]) + ba_ref[...]
    la = (jnp.minimum(x, 0.0) - jnp.log1p(jnp.exp(-jnp.abs(x)))) * (1.0 / GLA_TAU)
    row = lax.broadcasted_iota(jnp.int32, (tg, LANES), 0)
    if t_valid is not None:
        la = jnp.where(i * tg + row < t_valid, la, 0.0)
    rr = lax.broadcasted_iota(jnp.int32, (tg, tg), 0)
    cc = lax.broadcasted_iota(jnp.int32, (tg, tg), 1)
    same = (rr // c) == (cc // c)
    tri = (same & (cc <= rr)).astype(BF16)
    allo = same.astype(BF16)
    b = _split3_dot(tri, la)
    blast = _split3_dot(allo, la)
    qs[...] = q
    ks[...] = k
    bs[...] = b
    qts[...] = (q * jnp.exp(b)).astype(BF16)
    kts[...] = (k * jnp.exp(blast - b)).astype(BF16)
    vs[...] = v
    eb = eb_ref[...]
    mk = mk_ref[...]
    tt = lax.broadcasted_iota(jnp.int32, (c, LANES), 0)

    def sub(j, carry):
        r0 = pl.multiple_of(j * c, c)
        qi = qs[pl.ds(r0, c), :]
        ki = ks[pl.ds(r0, c), :]
        bi = bs[pl.ds(r0, c), :]
        vi = vs[pl.ds(r0, c), :]
        s_t = st[...]
        o_inter = _dot_nt(qts[pl.ds(r0, c), :], s_t.astype(BF16))
        parts = []
        for s in range(c):
            dec = jnp.exp(jnp.minimum(bi - bi[s:s + 1, :], 0.0))
            parts.append(jnp.where(tt >= s, qi * ki[s:s + 1, :] * dec, 0.0))
        p_all = jnp.concatenate(parts, axis=0).astype(BF16)
        a_all = _dot(p_all, eb)
        o_diag = a_all[0:c, :] * vi[0:1, :]
        for s in range(1, c):
            o_diag = o_diag + a_all[s * c:(s + 1) * c, :] * vi[s:s + 1, :]
        os_[pl.ds(r0, c), :] = o_inter + o_diag
        u_t = _dot_tn(vi.astype(BF16), kts[pl.ds(r0, c), :])
        dl = jnp.exp(bi[c - 1:c, :])
        st[...] = s_t * dl + u_t * mk
        return carry

    lax.fori_loop(0, nsub, sub, 0)

    o = os_[...]
    seg = seg_ref[...]
    og = og_ref[...]
    sil = r * (1.0 / (1.0 + jnp.exp(-r)))
    for h in range(2):
        sl = slice(h * LANES, (h + 1) * LANES)
        o_ref[:, sl] = (_head_rmsnorm(o[:, sl], og[:, sl], seg) * sil[:, sl]).astype(o_ref.dtype)
    sT_ref[0] = st[...]


def _gla(gla_rows, misc_rows, s0, wa2, ba, o_gain, b, t):
    t_valid = None
    if t % GLA_SUB:
        t_valid = t
        tp = -(-t // GLA_SUB) * GLA_SUB
        pad = lambda z: jnp.pad(z.reshape(b, t, -1), ((0, 0), (0, tp - t), (0, 0))).reshape(b * tp, -1)
        gla_rows, misc_rows = pad(gla_rows), pad(misc_rows)
    else:
        tp = t
    tg = _tile(tp, 256)
    assert tg % GLA_SUB == 0
    nt = tp // tg
    kk = GLA_HEADS * GLA_DK
    vv = GLA_HEADS * GLA_DV
    ki = jnp.arange(kk)
    vi = jnp.arange(vv)
    head_eq = (vi[:, None] // GLA_DV == ki[None, :] // GLA_DK)
    mk = head_eq.astype(F32)
    eb = head_eq.T.astype(BF16)
    s0t = jnp.einsum('bhkv,hg->bhvgk', s0.astype(F32), jnp.eye(GLA_HEADS, dtype=F32)).reshape(b, vv, kk)
    wa = jnp.zeros((LANES, kk), F32).at[24:24 + GLA_RANK].set(wa2).astype(BF16)
    full = lambda shape: pl.BlockSpec(shape, lambda i, j: (0,) * len(shape))
    rows = lambda cdim: pl.BlockSpec((tg, cdim), lambda i, j: (i * nt + j, 0))
    o, s_t = pl.pallas_call(
        functools.partial(_gla_kernel, tg=tg, t_valid=t_valid),
        grid=(b, nt),
        in_specs=[rows(768), rows(128), pl.BlockSpec((1, vv, kk), lambda i, j: (i, 0, 0)),
                  full((LANES, kk)), full((1, kk)), full((1, vv)), full((LANES, LANES)),
                  full((kk, vv)), full((vv, kk))],
        out_specs=[rows(vv), pl.BlockSpec((1, vv, kk), lambda i, j: (i, 0, 0))],
        out_shape=[jax.ShapeDtypeStruct((b * tp, vv), BF16), jax.ShapeDtypeStruct((b, vv, kk), F32)],
        scratch_shapes=[pltpu.VMEM((vv, kk), F32), pltpu.VMEM((tg, kk), F32), pltpu.VMEM((tg, kk), F32),
                        pltpu.VMEM((tg, kk), F32), pltpu.VMEM((tg, kk), BF16), pltpu.VMEM((tg, kk), BF16),
                        pltpu.VMEM((tg, vv), F32), pltpu.VMEM((tg, vv), F32)],
        compiler_params=_params("arbitrary", "arbitrary"),
        name="gla",
    )(gla_rows, misc_rows, s0t, wa, ba[None, :], jnp.tile(o_gain, GLA_HEADS)[None, :], _seg_ones(), eb, mk)
    if tp != t:
        o = o.reshape(b, tp, vv)[:, :t].reshape(b * t, vv)
    s5 = s_t.reshape(b, GLA_HEADS, GLA_DV, GLA_HEADS, GLA_DK)
    s_new = jnp.einsum('bhvgk,hg->bhkv', s5, jnp.eye(GLA_HEADS, dtype=F32))
    return o, s_new


def _compress_rows(read, nb, pe_ref, wk_ref, wv_ref, kg_ref, seg_ref):
    acck = jnp.zeros((nb, LANES), F32)
    accv = jnp.zeros((nb, LANES), F32)
    for j in range(L_CMP):
        xk = (read(0, j) + pe_ref[0, j:j + 1, :]).astype(BF16)
        xv = (read(1, j) + pe_ref[1, j:j + 1, :]).astype(BF16)
        acck = acck + _dot(xk, wk_ref[j])
        accv = accv + _dot(xv, wv_ref[j])
    kc = _head_rmsnorm(acck, kg_ref[0:1, :], seg_ref[...])
    return kc, accv


def _compress_kernel(rows_ref, pe_ref, wk_ref, wv_ref, kg_ref, seg_ref, o_ref, *, nb):
    kc, vc = _compress_rows(lambda kind, j: rows_ref[0, kind, pl.ds(j, nb, stride=L_CMP), :], nb,
                            pe_ref, wk_ref, wv_ref, kg_ref, seg_ref)
    o_ref[0, :, 0:LANES] = kc
    o_ref[0, :, LANES:2 * LANES] = vc


def _cmp_weights(cmp_w, cmp_pe, k_gain):
    def bd(w):
        z = jnp.zeros_like(w)
        return jnp.concatenate([jnp.concatenate([w, z], axis=2), jnp.concatenate([z, w], axis=2)], axis=1)
    pe = jnp.tile(cmp_pe, (1, 1, 2))
    kg = jnp.zeros((8, LANES), F32).at[0].set(jnp.tile(k_gain[0], 2))
    return pe, bd(cmp_w[0]).astype(BF16), bd(cmp_w[1]).astype(BF16), kg


def _compress(rows, cw):
    b, _, tp, _ = rows.shape
    nb = tp // L_CMP
    pe, wk, wv, kg = cw
    full = lambda shape: pl.BlockSpec(shape, lambda i: (0,) * len(shape))
    return pl.pallas_call(
        functools.partial(_compress_kernel, nb=nb),
        grid=(b,),
        in_specs=[pl.BlockSpec((1, 2, tp, LANES), lambda i: (i, 0, 0, 0)), full((2, L_CMP, LANES)),
                  full((L_CMP, LANES, LANES)), full((L_CMP, LANES, LANES)), full((8, LANES)),
                  full((LANES, LANES))],
        out_specs=pl.BlockSpec((1, nb, 256), lambda i: (i, 0, 0)),
        out_shape=jax.ShapeDtypeStruct((b, nb, 256), F32),
        compiler_params=_params("arbitrary"),
        name="compress",
    )(rows, pe, wk, wv, kg, _seg_ones())


SLAB_PITCH = 2 * NSA_KV * HEAD_DIM + 8


def _paged_compress_kernel(pt_ref, cache_ref, pe_ref, m_ref, kg_ref, seg_ref, o_ref,
                           slab, sem, *, pages, n_slab, layer):
    bi = pl.program_id(0)
    si = pl.program_id(1)
    step = bi * n_slab + si
    nsteps = pl.num_programs(0) * n_slab
    slot = step % 2
    rows_cmp = 2 * NSA_KV * HEAD_DIM

    def copy(page, slot_, p):
        return pltpu.make_async_copy(cache_ref.at[layer, page, pl.ds(0, rows_cmp), :],
                                     slab.at[slot_, pl.ds(p * SLAB_PITCH, rows_cmp), :], sem.at[slot_])

    def issue(bb, ss, slot_):
        for p in range(pages):
            copy(pt_ref[bb, ss * pages + p], slot_, p).start()

    @pl.when(step == 0)
    def _():
        issue(0, 0, 0)

    @pl.when(step + 1 < nsteps)
    def _():
        nxt = step + 1
        issue(nxt // n_slab, nxt % n_slab, 1 - slot)

    for p in range(pages):
        copy(0, slot, p).wait()

    def rows(r0):
        return slab[slot, pl.ds(r0, pages, stride=SLAB_PITCH), :]

    for c in range(2):
        acc = jnp.zeros((NSA_KV * pages, 2 * LANES), F32)
        for dp in range(HEAD_DIM // 2):
            parts = []
            for g in range(NSA_KV):
                r0 = (c * NSA_KV + g) * HEAD_DIM + 2 * dp
                parts.append(jnp.concatenate([rows(r0), rows(r0 + 1)], axis=1))
            a = jnp.concatenate(parts, axis=0) + pe_ref[c, dp:dp + 1, :]
            acc = acc + _dot(a.astype(BF16), m_ref[c, dp])
        for g in range(NSA_KV):
            blk = acc[g * pages:(g + 1) * pages, :]
            if c == 0:
                for h in range(2):
                    sl = slice(h * LANES, (h + 1) * LANES)
                    o_ref[0, c, g, :, sl] = _head_rmsnorm(blk[:, sl], kg_ref[0:1, :], seg_ref[...])
            else:
                o_ref[0, c, g] = blk


def _paged_cmp_weights(cmp_w, cmp_pe):
    nblk = PAGE_SIZE // L_CMP
    k6 = jnp.einsum('nm,cjde->cdnjme', jnp.eye(nblk, dtype=F32), cmp_w)
    m = k6.reshape(2, HEAD_DIM // 2, 2 * PAGE_SIZE, nblk * HEAD_DIM).astype(BF16)
    pe = jnp.tile(cmp_pe.transpose(0, 2, 1), (1, 1, nblk)).reshape(2, HEAD_DIM // 2, 2 * PAGE_SIZE)
    return pe, m


def _paged_compress(cache_v, page_table, layer, cw, pw):
    b, n_pages = page_table.shape
    pages = math.gcd(n_pages, 64)
    n_slab = n_pages // pages
    pe, m = pw
    kg = cw[3]
    nblk = PAGE_SIZE // L_CMP
    full = lambda shape: pl.BlockSpec(shape, lambda i, j, pt: (0,) * len(shape))
    grid_spec = pltpu.PrefetchScalarGridSpec(
        num_scalar_prefetch=1,
        grid=(b, n_slab),
        in_specs=[pl.BlockSpec(memory_space=pl.ANY), full(pe.shape), full(m.shape), full((8, LANES)),
                  full((LANES, LANES))],
        out_specs=pl.BlockSpec((1, 2, NSA_KV, pages, nblk * HEAD_DIM), lambda i, j, pt: (i, 0, 0, j, 0)),
        scratch_shapes=[pltpu.VMEM((2, pages * SLAB_PITCH, LANES), F32), pltpu.SemaphoreType.DMA((2,))],
    )
    return pl.pallas_call(
        functools.partial(_paged_compress_kernel, pages=pages, n_slab=n_slab, layer=layer),
        grid_spec=grid_spec,
        out_shape=jax.ShapeDtypeStruct((b, 2, NSA_KV, n_pages, nblk * HEAD_DIM), F32),
        compiler_params=_params("arbitrary", "arbitrary"),
        name="paged_compress",
    )(page_table, cache_v, pe, m, kg, _seg_ones())


def _nsa_prompt_kernel(q_ref, kc_ref, vc_ref, ks_ref, vs_ref, kw_ref, vw_ref, gate_ref, ex_ref,
                       o_ref, *, tq, ck):
    i = pl.program_id(1)
    t0 = i * tq
    n_slc = LANES // 2
    row_t = t0 + lax.broadcasted_iota(jnp.int32, (tq, 1), 0)
    lane = lax.broadcasted_iota(jnp.int32, (1, LANES), 1)
    blk = lane % n_slc
    nat = jnp.where(lane < n_slc, 2 * lane, 2 * (lane - n_slc) + 1)
    cmp_ok = ((nat + 1) * L_CMP - 1) <= row_t
    cur = row_t // L_SLC
    done = blk < cur
    gx = gate_ref[...]
    gates = 1.0 / (1.0 + jnp.exp(-gx))
    n_chunks = (t0 + tq + ck - 1) // ck
    kstart = pl.multiple_of(jnp.maximum(t0 - WINDOW, 0), tq)
    wlen = WINDOW + tq
    wpos = kstart + lax.broadcasted_iota(jnp.int32, (1, wlen), 1)
    win_bias = jnp.where((wpos <= row_t) & (wpos >= row_t - WINDOW), 0.0, MASKED)

    for g in range(NSA_KV):
        qg = q_ref[0, g].reshape(NSA_HPG * tq, HEAD_DIM)
        s_c = _dot_nt(qg, kc_ref[0, g]).reshape(NSA_HPG, tq, LANES)
        p_c = _masked_softmax(s_c, cmp_ok[None])
        o_c = _dot(p_c.reshape(NSA_HPG * tq, LANES).astype(BF16), vc_ref[0, g])
        imp = p_c[0] + p_c[1] + p_c[2] + p_c[3]
        imp = imp + pltpu.roll(imp, n_slc, axis=1)
        key = jnp.where(done, lax.bitcast_convert_type(imp, jnp.int32), -1)
        key_m1 = key - 1
        rank = jnp.zeros((tq, LANES), jnp.int32)
        for r in range(1, n_slc):
            vm = pltpu.roll(key, r, axis=1)
            rank = rank + jnp.where(vm > jnp.where(blk >= r, key_m1, key), 1, 0)
        sel = (done & (rank < N_SEL - 1)) | (blk == cur)
        selb = jnp.where(sel, 1.0, 0.0).astype(BF16)

        def chunk(c, carry):
            m, l, acc = carry
            k0 = pl.multiple_of(c * ck, ck)
            s = _dot_nt(qg, ks_ref[0, g, pl.ds(k0, ck), :]).reshape(NSA_HPG, tq, ck)
            hit = _dot(selb, ex_ref[c])
            kpos = k0 + lax.broadcasted_iota(jnp.int32, (1, ck), 1)
            bias = jnp.where((hit > 0.5) & (kpos <= row_t), 0.0, MASKED)
            sm = s + bias[None]
            m_new = jnp.maximum(m, jnp.max(sm, axis=-1, keepdims=True))
            p = jnp.exp(sm - m_new)
            alpha = jnp.exp(m - m_new)
            l = alpha * l + jnp.sum(p, axis=-1, keepdims=True)
            pv = _dot(p.reshape(NSA_HPG * tq, ck).astype(BF16), vs_ref[0, g, pl.ds(k0, ck), :])
            acc = alpha * acc + pv.reshape(NSA_HPG, tq, HEAD_DIM)
            return m_new, l, acc

        m0 = jnp.full((NSA_HPG, tq, 1), MASKED, F32)
        l0 = jnp.zeros((NSA_HPG, tq, 1), F32)
        a0 = jnp.zeros((NSA_HPG, tq, HEAD_DIM), F32)
        _, l_s, acc_s = lax.fori_loop(0, n_chunks, chunk, (m0, l0, a0))
        o_s = acc_s * (1.0 / jnp.where(l_s > 0.0, l_s, 1.0))
        s_w = _dot_nt(qg, kw_ref[0, g, pl.ds(kstart, wlen), :]).reshape(NSA_HPG, tq, wlen) + win_bias[None]
        e_w = jnp.exp(s_w - jnp.max(s_w, axis=-1, keepdims=True))
        p_w = e_w * (1.0 / jnp.sum(e_w, axis=-1, keepdims=True))
        o_w = _dot(p_w.reshape(NSA_HPG * tq, wlen).astype(BF16), vw_ref[0, g, pl.ds(kstart, wlen), :])
        o_c = o_c.reshape(NSA_HPG, tq, HEAD_DIM)
        o_w = o_w.reshape(NSA_HPG, tq, HEAD_DIM)
        for j in range(NSA_HPG):
            h = g * NSA_HPG + j
            o = (gates[:, 3 * h:3 * h + 1] * o_c[j] + gates[:, 3 * h + 1:3 * h + 2] * o_s[j]
                 + gates[:, 3 * h + 2:3 * h + 3] * o_w[j])
            o_ref[0, g, j] = o.astype(o_ref.dtype)


def _nsa_prompt(q_rows, rows, win, misc, cw, b, t):
    assert t // L_CMP == LANES and t % L_SLC == 0, "prompt kernel is laid out for 128 compressed blocks"
    tq = 256
    ck = 512
    n_slc = t // L_SLC
    cmp = _compress(rows[:, 0:256].reshape(b, t, 2, LANES).transpose(0, 2, 1, 3), cw)
    order = jnp.concatenate([jnp.arange(0, LANES, 2), jnp.arange(1, LANES, 2)])
    cmp = cmp[:, order].reshape(b, LANES, 2, NSA_KV, HEAD_DIM).astype(BF16)
    kc = cmp[:, :, 0].transpose(0, 2, 1, 3)
    vc = cmp[:, :, 1].transpose(0, 2, 1, 3)
    split = lambda z: z.reshape(b, t, NSA_KV, HEAD_DIM).transpose(0, 2, 1, 3).astype(BF16)
    ks, vs = split(rows[:, 256:384]), split(rows[:, 384:512])
    kw, vw = split(win[:, 0:128]), split(win[:, 128:256])
    q5 = q_rows.reshape(b, t, NSA_KV, NSA_HPG, HEAD_DIM).transpose(0, 2, 3, 1, 4)
    key_blk = jnp.arange(t) // L_SLC
    ex = (jnp.arange(LANES)[:, None] == key_blk[None, :]) & (jnp.arange(LANES)[:, None] < n_slc)
    ex = ex.astype(BF16).reshape(LANES, t // ck, ck).transpose(1, 0, 2)
    nt = t // tq
    per_b = lambda shape: pl.BlockSpec(shape, lambda i, j: (i,) + (0,) * (len(shape) - 1))
    out = pl.pallas_call(
        functools.partial(_nsa_prompt_kernel, tq=tq, ck=ck),
        grid=(b, nt),
        in_specs=[pl.BlockSpec((1, NSA_KV, NSA_HPG, tq, HEAD_DIM), lambda i, j: (i, 0, 0, j, 0)),
                  per_b((1, NSA_KV, LANES, HEAD_DIM)), per_b((1, NSA_KV, LANES, HEAD_DIM)),
                  per_b((1, NSA_KV, t, HEAD_DIM)), per_b((1, NSA_KV, t, HEAD_DIM)),
                  per_b((1, NSA_KV, t, HEAD_DIM)), per_b((1, NSA_KV, t, HEAD_DIM)),
                  pl.BlockSpec((tq, LANES), lambda i, j: (i * nt + j, 0)),
                  pl.BlockSpec((t // ck, LANES, ck), lambda i, j: (0, 0, 0))],
        out_specs=pl.BlockSpec((1, NSA_KV, NSA_HPG, tq, HEAD_DIM), lambda i, j: (i, 0, 0, j, 0)),
        out_shape=jax.ShapeDtypeStruct((b, NSA_KV, NSA_HPG, t, HEAD_DIM), BF16),
        compiler_params=_params("arbitrary", "arbitrary"),
        name="nsa_prompt",
    )(q5, kc, vc, ks, vs, kw, vw, misc, ex)
    return out.transpose(0, 3, 1, 2, 4).reshape(b * t, NSA_WIDTH)


def _softmax_masked_jnp(s, mask):
    s = jnp.where(mask, s, -jnp.inf)
    m = jnp.max(s, axis=-1, keepdims=True)
    m = jnp.where(jnp.isfinite(m), m, 0.0)
    e = jnp.exp(s - m)
    den = jnp.sum(e, axis=-1, keepdims=True)
    return e / jnp.where(den > 0.0, den, 1.0)


def _bdot(eq, a, b):
    return jnp.einsum(eq, a.astype(BF16).astype(F32), b.astype(BF16).astype(F32),
                      precision=lax.Precision.HIGHEST)


def _nsa_sample(q_rows, rows, win, misc, cw, pw, cache_v, page_table, layer, win_state, b, t, past_len):
    t_all = past_len + t
    t_pad = -(-t_all // L_SLC) * L_SLC
    n_cmp = t_pad // L_CMP
    n_slc = t_pad // L_SLC
    assert past_len % PAGE_SIZE == 0
    n_pages = past_len // PAGE_SIZE
    cmp_past = _paged_compress(cache_v, page_table, layer, cw, pw)
    cmp_past = cmp_past.reshape(b, 2, NSA_KV, n_pages * (PAGE_SIZE // L_CMP), HEAD_DIM)
    tail = jnp.concatenate([rows[:, 0:256].reshape(b, t, 256),
                            jnp.zeros((b, t_pad - t_all, 256), F32)], axis=1)
    n_tail = (t_pad - past_len) // L_CMP
    cmp_tail = _compress(tail.reshape(1, b * (t_pad - past_len), 2, LANES).transpose(0, 2, 1, 3), cw)
    cmp_tail = cmp_tail.reshape(b, n_tail, 2, NSA_KV, HEAD_DIM).transpose(0, 2, 3, 1, 4)
    cmp = jnp.concatenate([cmp_past, cmp_tail], axis=3)
    kc, vc = cmp[:, 0], cmp[:, 1]
    q = q_rows.reshape(b, t, NSA_KV, NSA_HPG, HEAD_DIM)
    gates = jax.nn.sigmoid(misc[:, 0:3 * NSA_HEADS]).reshape(b, t, NSA_KV, NSA_HPG, 3)
    pos = past_len + jnp.arange(t)
    cmp_end = (jnp.arange(n_cmp) + 1) * L_CMP - 1
    s_c = _bdot('bqgjd,bgnd->bgjqn', q, kc)
    p_c = _softmax_masked_jnp(s_c, (cmp_end[None, :] <= pos[:, None])[None, None, None])
    o_c = _bdot('bgjqn,bgnd->bqgjd', p_c, vc)
    imp = p_c.sum(axis=2).reshape(b, NSA_KV, t, n_slc, L_SLC // L_CMP).sum(-1)
    cur = pos // L_SLC
    done = jnp.arange(n_slc)[None, :] < cur[:, None]
    imp = jnp.where(done[None, None], imp, -jnp.inf)
    k_sel = min(N_SEL - 1, n_slc)
    _, top = lax.top_k(imp, k_sel)
    cur_b = jnp.broadcast_to(cur[None, None, :, None], (b, NSA_KV, t, 1))
    sel = jnp.concatenate([top, cur_b], axis=-1)
    sel_ok = jnp.concatenate([top < cur_b, jnp.ones_like(cur_b, dtype=bool)], axis=-1)
    n_past_blk = past_len // L_SLC
    per_page = PAGE_SIZE // L_SLC
    selp = jnp.minimum(sel, n_past_blk - 1)
    b_ix = jnp.arange(b)[:, None, None, None]
    g_ix = jnp.arange(NSA_KV)[None, :, None, None]
    cache_p = cache_v.reshape(cache_v.shape[0], cache_v.shape[1], 4, NSA_KV, HEAD_DIM, PAGE_SIZE)
    pg = page_table[b_ix, selp // per_page]
    first_half = (selp % per_page == 0)[..., None, None]

    def past_block(kind):
        plane = cache_p[layer, pg, kind, g_ix]
        return jnp.where(first_half, plane[..., :L_SLC], plane[..., L_SLC:])

    tail_s = jnp.concatenate([rows[:, 256:512].reshape(b, t, 256),
                              jnp.zeros((b, t_pad - t_all, 256), F32)], axis=1)
    tail_s = tail_s.reshape(b, (t_pad - past_len) // L_SLC, L_SLC, 2, NSA_KV, HEAD_DIM)
    selt = jnp.clip(sel - n_past_blk, 0, tail_s.shape[1] - 1)
    k_tail = jnp.swapaxes(tail_s[b_ix, selt, :, 0, g_ix, :], -1, -2)
    v_tail = jnp.swapaxes(tail_s[b_ix, selt, :, 1, g_ix, :], -1, -2)
    from_tail = (sel >= n_past_blk)[..., None, None]
    k_g = jnp.where(from_tail, k_tail, past_block(2))
    v_g = jnp.where(from_tail, v_tail, past_block(3))
    n_keys = sel.shape[-1] * L_SLC
    key_pos = sel[..., None] * L_SLC + jnp.arange(L_SLC)
    m_s = (sel_ok[..., None] & (key_pos <= pos[None, None, :, None, None])).reshape(b, NSA_KV, 1, t, n_keys)
    s_s = _bdot('bqgjd,bgqkdl->bgjqkl', q, k_g)
    p_s = _softmax_masked_jnp(s_s.reshape(b, NSA_KV, NSA_HPG, t, n_keys), m_s).reshape(s_s.shape)
    o_s = _bdot('bgjqkl,bgqkdl->bqgjd', p_s, v_g)
    w_buf = win_state.shape[1]
    win_ext = jnp.concatenate([win_state.reshape(b, w_buf, 2, NSA_KV, HEAD_DIM),
                               win.reshape(b, t, 2, NSA_KV, HEAD_DIM)], axis=1)
    w_pos = past_len - w_buf + jnp.arange(w_buf + t)
    m_w = (w_pos[None, :] >= 0) & (w_pos[None, :] <= pos[:, None]) & (w_pos[None, :] >= pos[:, None] - WINDOW)
    s_w = _bdot('bqgjd,bkgd->bgjqk', q, win_ext[:, :, 0])
    p_w = _softmax_masked_jnp(s_w, m_w[None, None, None])
    o_w = _bdot('bgjqk,bkgd->bqgjd', p_w, win_ext[:, :, 1])
    o = gates[..., 0:1] * o_c + gates[..., 1:2] * o_s + gates[..., 2:3] * o_w
    return o.reshape(b * t, NSA_WIDTH).astype(BF16), win_ext[:, -w_buf:]


def _outproj_kernel(yp_ref, yn_ref, yg_ref, x_ref, w_ref, g_ref, *rest, with_router):
    if with_router:
        r_ref, h_o, hn_o, lg_o = rest
    else:
        h_o, hn_o = rest
    h = (x_ref[...] + _dot(yp_ref[...], w_ref[0:256, :]) + _dot(yn_ref[...], w_ref[256:768, :])
         + _dot(yg_ref[...], w_ref[768:1024, :]))
    h_o[...] = h
    ms = jnp.mean(h * h, axis=-1, keepdims=True)
    hn = (h * lax.rsqrt(ms + EPS) * g_ref[...]).astype(BF16)
    hn_o[...] = hn
    if with_router:
        lg_o[...] = _dot(hn, r_ref[...])


def _outproj(y_pool, y_nsa, y_gla, x, w_out_l, g_ffn, router):
    n, d = x.shape
    tm = _tile(n, 384)
    with_router = router is not None
    row = lambda c: pl.BlockSpec((tm, c), lambda i: (i, 0))
    full = lambda shape: pl.BlockSpec(shape, lambda i: (0, 0))
    in_specs = [row(256), row(512), row(256), row(d), full((d, d)), full((1, d))]
    args = [y_pool, y_nsa, y_gla, x, w_out_l.astype(BF16), g_ffn[None, :]]
    out_specs = [row(d), row(d)]
    out_shape = [jax.ShapeDtypeStruct((n, d), F32), jax.ShapeDtypeStruct((n, d), BF16)]
    if with_router:
        ne = router.shape[1]
        in_specs.append(full((d, LANES)))
        args.append(jnp.zeros((d, LANES), F32).at[:, :ne].set(router).astype(BF16))
        out_specs.append(row(LANES))
        out_shape.append(jax.ShapeDtypeStruct((n, LANES), F32))
    return pl.pallas_call(
        functools.partial(_outproj_kernel, with_router=with_router),
        grid=(n // tm,), in_specs=in_specs, out_specs=out_specs, out_shape=out_shape,
        compiler_params=_params("arbitrary"), name="outproj",
    )(*args)


def _swiglu_kernel(be_ref, bv_ref, x_ref, gate_ref, w1_ref, w3_ref, w2_ref, o_ref):
    i = pl.program_id(0)
    j = pl.program_id(1)
    valid = bv_ref[i] > 0

    @pl.when(jnp.logical_not(valid) & (j == 0))
    def _():
        o_ref[...] = jnp.zeros_like(o_ref)

    @pl.when(valid)
    def _():
        x = x_ref[...]
        a = _dot(x, w1_ref[0])
        c = _dot(x, w3_ref[0])
        hmid = (a * (1.0 / (1.0 + jnp.exp(-a))) * c).astype(BF16)
        y = _dot(hmid, w2_ref[0])

        @pl.when(j == 0)
        def _():
            o_ref[...] = y

        @pl.when(j > 0)
        def _():
            o_ref[...] += y

        @pl.when(j == pl.num_programs(1) - 1)
        def _():
            o_ref[...] = o_ref[...] * gate_ref[...]


def _swiglu(x, gate, blk_expert, blk_valid, w1, w3, w2, tm):
    r, d = x.shape
    f = w1.shape[2]
    tf = f // 2 if (f // 2) % LANES == 0 else f
    grid_spec = pltpu.PrefetchScalarGridSpec(
        num_scalar_prefetch=2,
        grid=(r // tm, f // tf),
        in_specs=[pl.BlockSpec((tm, d), lambda i, j, be, bv: (i, 0)),
                  pl.BlockSpec((tm, 1), lambda i, j, be, bv: (i, 0)),
                  pl.BlockSpec((1, d, tf), lambda i, j, be, bv: (be[i], 0, j)),
                  pl.BlockSpec((1, d, tf), lambda i, j, be, bv: (be[i], 0, j)),
                  pl.BlockSpec((1, tf, d), lambda i, j, be, bv: (be[i], j, 0))],
        out_specs=pl.BlockSpec((tm, d), lambda i, j, be, bv: (i, 0)),
    )
    return pl.pallas_call(
        _swiglu_kernel, grid_spec=grid_spec,
        out_shape=jax.ShapeDtypeStruct((r, d), F32),
        compiler_params=_params("arbitrary", "arbitrary"), name="swiglu",
    )(blk_expert, blk_valid, x, gate, w1, w3, w2)


def _dense_ffn(h, hn, w1, w3, w2):
    n = h.shape[0]
    tm = _tile(n, 704)
    nblk = n // tm
    y = _swiglu(hn, jnp.ones((n, 1), F32), jnp.zeros((nblk,), jnp.int32), jnp.ones((nblk,), jnp.int32),
                w1[None].astype(BF16), w3[None].astype(BF16), w2[None].astype(BF16), tm)
    return h + y


def _moe_ffn(h, hn, logits, w1, w3, w2):
    n = h.shape[0]
    ne = w1.shape[0]
    tm = 512
    top_v, top_i = lax.top_k(logits[:, :ne], TOP_K)
    gate = jax.nn.softmax(top_v, axis=-1)
    a = n * TOP_K
    e_a = top_i.reshape(-1)
    tok_a = jnp.repeat(jnp.arange(n), TOP_K)
    g_a = gate.reshape(-1)
    order = jnp.argsort(e_a)
    e_s, tok_s, g_s = e_a[order], tok_a[order], g_a[order]
    counts = jnp.bincount(e_a, length=ne)
    start = jnp.cumsum(counts) - counts
    padded = (counts + tm - 1) // tm * tm
    p_end = jnp.cumsum(padded)
    p_start = p_end - padded
    dest = p_start[e_s] + (jnp.arange(a) - start[e_s])
    nblk = -(-a // tm) + ne
    r = nblk * tm
    row_tok = jnp.zeros((r,), jnp.int32).at[dest].set(tok_s.astype(jnp.int32))
    row_gate = jnp.zeros((r,), F32).at[dest].set(g_s)
    blk0 = jnp.arange(nblk) * tm
    blk_valid = (blk0 < p_end[-1]).astype(jnp.int32)
    last = jnp.clip(jnp.searchsorted(p_end, p_end[-1] - 1, side='right'), 0, ne - 1)
    blk_expert = jnp.clip(jnp.searchsorted(p_end, blk0, side='right'), 0, ne - 1)
    blk_expert = jnp.where(blk_valid > 0, blk_expert, last).astype(jnp.int32)
    yb = _swiglu(hn[row_tok], row_gate[:, None], blk_expert, blk_valid,
                 w1.astype(BF16), w3.astype(BF16), w2.astype(BF16), tm)
    pos = jnp.zeros((a,), jnp.int32).at[order].set(dest.astype(jnp.int32)).reshape(n, TOP_K)
    return h + (yb[pos[:, 0]] + yb[pos[:, 1]])


def kernel(x_prompt, x_sample, cache_nsa_kv, state_nsa_win, state_gla, state_pool, page_table,
           norm_mix, norm_ffn, w_in, w_out, pool_w, pool_scale, nsa_q_norm, nsa_k_norm,
           nsa_cmp_w, nsa_cmp_pe, gla_wa2, gla_ba, gla_norm, ffn_w1, ffn_w3, ffn_w2,
           moe_router, moe_w1, moe_w3, moe_w2):
    bp, tp, d = x_prompt.shape
    bs, ts, _ = x_sample.shape
    depth = w_in.shape[0]
    n_pool = cache_nsa_kv.shape[1]
    n_pages = page_table.shape[1]
    past_len = n_pages * PAGE_SIZE
    w_buf = state_nsa_win.shape[2]
    npr = bp * tp
    cache_v = cache_nsa_kv.transpose(0, 1, 3, 4, 5, 2).reshape(depth, n_pool, 4 * NSA_KV * HEAD_DIM, PAGE_SIZE)
    x = jnp.concatenate([x_prompt.reshape(npr, d), x_sample.reshape(bs * ts, d)], axis=0)
    kv_p, kv_s, win_p, win_s, gla_p, gla_s, pool_p, pool_s = [], [], [], [], [], [], [], []
    n_win_p = min(WINDOW, tp)
    for l in range(depth):
        u_pool, q_rows, rows, win, misc, gla_rows = _inproj(x, norm_mix[l], w_in[l], nsa_q_norm[l], nsa_k_norm[l])
        cw = _cmp_weights(nsa_cmp_w[l], nsa_cmp_pe[l], nsa_k_norm[l])
        up = u_pool[:npr].reshape(bp, tp, POOL_WIDTH)
        us = u_pool[npr:].reshape(bs, ts, POOL_WIDTH)
        yp_pool = _pool(up, jnp.zeros((bp, POOL_BUF, POOL_WIDTH), F32), pool_w[l], pool_scale[l], 0)
        ys_pool = _pool(us, state_pool[l], pool_w[l], pool_scale[l], past_len)
        pool_p.append(jnp.concatenate([jnp.zeros((bp, POOL_BUF, POOL_WIDTH), F32), up], axis=1)[:, -POOL_BUF:])
        pool_s.append(jnp.concatenate([state_pool[l], us], axis=1)[:, -POOL_BUF:])
        yp_nsa = _nsa_prompt(q_rows[:npr], rows[:npr], win[:npr], misc[:npr], cw, bp, tp)
        pw = _paged_cmp_weights(nsa_cmp_w[l], nsa_cmp_pe[l])
        ys_nsa, win_new_s = _nsa_sample(q_rows[npr:], rows[npr:], win[npr:], misc[npr:], cw, pw, cache_v,
                                        page_table, l, state_nsa_win[l], bs, ts, past_len)
        kv_p.append(rows[:npr].reshape(bp, tp, 4, NSA_KV, HEAD_DIM))
        kv_s.append(rows[npr:].reshape(bs, ts, 4, NSA_KV, HEAD_DIM))
        win_full_p = jnp.concatenate([jnp.zeros((bp, WINDOW, 256), F32), win[:npr].reshape(bp, tp, 256)], axis=1)
        win_p.append(win_full_p[:, -n_win_p:].reshape(bp, n_win_p, 2, NSA_KV, HEAD_DIM))
        win_s.append(win_new_s.reshape(bs, w_buf, 2, NSA_KV, HEAD_DIM))
        yp_gla, sp = _gla(gla_rows[:npr], misc[:npr], jnp.zeros((bp, GLA_HEADS, GLA_DK, GLA_DV), F32),
                          gla_wa2[l], gla_ba[l], gla_norm[l], bp, tp)
        ys_gla, ss = _gla(gla_rows[npr:], misc[npr:], state_gla[l], gla_wa2[l], gla_ba[l], gla_norm[l], bs, ts)
        gla_p.append(sp.astype(state_gla.dtype))
        gla_s.append(ss.astype(state_gla.dtype))
        i = l // 2
        router = moe_router[i] if l % 2 else None
        res = _outproj(jnp.concatenate([yp_pool.reshape(npr, -1), ys_pool.reshape(bs * ts, -1)], axis=0),
                       jnp.concatenate([yp_nsa, ys_nsa], axis=0),
                       jnp.concatenate([yp_gla, ys_gla], axis=0), x, w_out[l], norm_ffn[l], router)
        if l % 2 == 0:
            h, hn = res
            x = _dense_ffn(h, hn, ffn_w1[i], ffn_w3[i], ffn_w2[i])
        else:
            h, hn, logits = res
            x = _moe_ffn(h, hn, logits, moe_w1[i], moe_w3[i], moe_w2[i])
    return (x[:npr].reshape(bp, tp, d), x[npr:].reshape(bs, ts, d),
            jnp.stack(kv_p), jnp.stack(kv_s), jnp.stack(win_p), jnp.stack(win_s),
            jnp.stack(gla_p), jnp.stack(gla_s), jnp.stack(pool_p), jnp.stack(pool_s))
```

```python
import functools
import math

import jax
import jax.numpy as jnp
from jax import lax
from jax.experimental import pallas as pl
from jax.experimental.pallas import tpu as pltpu

F32 = jnp.float32
BF16 = jnp.bfloat16

EPS = 1e-6
LANES = 128
HEAD_DIM = 64
PAGE_SIZE = 128
POOL_WINDOWS = (2, 4, 8, 16)
POOL_BUF = 15
POOL_WIDTH = 256
NSA_WIDTH = 512
NSA_HEADS = 8
NSA_KV = 2
NSA_HPG = 4
L_CMP = 32
L_SLC = 64
N_SEL = 16
WINDOW = 512
GLA_HEADS = 4
GLA_DK = 32
GLA_DV = 64
GLA_WIDTH = 256
GLA_RANK = 16
GLA_TAU = 16.0
GLA_SUB = 16
TOP_K = 2
N_IN_PAD = 2432
MASKED = -1e30
VMEM_LIMIT = 56 * 1024 * 1024


def _params(*sem):
    return pltpu.CompilerParams(dimension_semantics=sem, vmem_limit_bytes=VMEM_LIMIT)


def _tile(n, target):
    best = None
    for t in range(8, min(n, target) + 1, 8):
        if n % t == 0:
            best = t
    assert best is not None, (n, target)
    return best


def _dot(a, b):
    return jnp.dot(a, b, preferred_element_type=F32)


def _dot_nt(a, b):
    return lax.dot_general(a, b, (((1,), (1,)), ((), ())), preferred_element_type=F32)


def _dot_tn(a, b):
    return lax.dot_general(a, b, (((0,), (0,)), ((), ())), preferred_element_type=F32)


def _split2_dot(a, ones):
    hi = a.astype(BF16)
    lo = (a - hi.astype(F32)).astype(BF16)
    return _dot(hi, ones) + _dot(lo, ones)


def _split3_dot(ones, a):
    a1 = a.astype(BF16)
    r1 = a - a1.astype(F32)
    a2 = r1.astype(BF16)
    a3 = (r1 - a2.astype(F32)).astype(BF16)
    return _dot(ones, a1) + _dot(ones, a2) + _dot(ones, a3)


def _head_rmsnorm(a, gain_row, seg_ones):
    ms = _split2_dot(a * a, seg_ones) * (1.0 / HEAD_DIM)
    return a * lax.rsqrt(ms + EPS) * gain_row


def _masked_softmax(s, mask):
    sm = jnp.where(mask, s, MASKED)
    m = jnp.max(sm, axis=-1, keepdims=True)
    e = jnp.where(mask, jnp.exp(sm - m), 0.0)
    den = jnp.sum(e, axis=-1, keepdims=True)
    return e * (1.0 / jnp.where(den > 0.0, den, 1.0))


def _seg_ones():
    i = jnp.arange(LANES)
    return (i[:, None] // HEAD_DIM == i[None, :] // HEAD_DIM).astype(BF16)


def _inproj_kernel(x_ref, g_ref, w_ref, qg_ref, kg_ref, seg_ref,
                   pool_o, q_o, rows_o, win_o, misc_o, gla_o):
    x = x_ref[...]
    ms = jnp.mean(x * x, axis=-1, keepdims=True)
    xn = (x * lax.rsqrt(ms + EPS) * g_ref[...]).astype(BF16)
    seg = seg_ref[...]

    def mm(c0, c1):
        return _dot(xn, w_ref[:, c0:c1])

    pool_o[...] = mm(0, 256)
    for c in range(4):
        a = mm(256 + LANES * c, 256 + LANES * (c + 1))
        q_o[:, LANES * c:LANES * (c + 1)] = (
            _head_rmsnorm(a, qg_ref[...], seg) * (HEAD_DIM ** -0.5)).astype(BF16)
    kv0 = 768
    rows_o[:, 0:256] = mm(kv0, kv0 + 256)
    rows_o[:, 256:384] = _head_rmsnorm(mm(kv0 + 256, kv0 + 384), kg_ref[1:2, :], seg)
    rows_o[:, 384:512] = mm(kv0 + 384, kv0 + 512)
    win_o[:, 0:128] = _head_rmsnorm(mm(kv0 + 512, kv0 + 640), kg_ref[2:3, :], seg)
    win_o[:, 128:256] = mm(kv0 + 640, kv0 + 768)
    misc_o[...] = mm(1536, 1664)
    gla_o[...] = mm(1664, 2432)


def _pad_w_in(w):
    d = w.shape[0]
    return jnp.concatenate([
        w[:, 0:1560], w[:, 2072:2088], jnp.zeros((d, 88), w.dtype),
        w[:, 1560:2072], w[:, 2088:2344]], axis=1)


def _inproj(x, g_mix, w_in_l, q_gain, k_gain):
    n, d = x.shape
    tm = _tile(n, 384)
    w = _pad_w_in(w_in_l).astype(BF16)
    qg = jnp.tile(q_gain, 2)[None, :]
    kg = jnp.zeros((8, LANES), F32).at[0:3].set(jnp.tile(k_gain, (1, 2)))
    full = lambda shape: pl.BlockSpec(shape, lambda i: (0, 0))
    row = lambda c: pl.BlockSpec((tm, c), lambda i: (i, 0))
    return pl.pallas_call(
        _inproj_kernel,
        grid=(n // tm,),
        in_specs=[row(d), full((1, d)), full((d, N_IN_PAD)), full((1, LANES)),
                  full((8, LANES)), full((LANES, LANES))],
        out_specs=[row(256), row(512), row(512), row(256), row(128), row(768)],
        out_shape=[jax.ShapeDtypeStruct((n, 256), F32), jax.ShapeDtypeStruct((n, 512), BF16),
                   jax.ShapeDtypeStruct((n, 512), F32), jax.ShapeDtypeStruct((n, 256), F32),
                   jax.ShapeDtypeStruct((n, 128), F32), jax.ShapeDtypeStruct((n, 768), F32)],
        compiler_params=_params("arbitrary"),
        name="inproj",
    )(x, g_mix[None, :], w, qg, kg, _seg_ones())


def _pool_kernel(buf_ref, u_ref, w_ref, sc_ref, o_ref, ext, *, pos0, tp):
    i = pl.program_id(1)

    @pl.when(i == 0)
    def _():
        ext[0:16, :] = buf_ref[0]

    ext[16:16 + tp, :] = u_ref[0]
    u0 = ext[16:16 + tp, :]
    acc = u0
    sums = {}
    for k in range(1, 16):
        acc = acc + ext[16 - k:16 - k + tp, :]
        if k + 1 in POOL_WINDOWS:
            sums[k + 1] = acc
    lane = lax.broadcasted_iota(jnp.int32, (tp, POOL_WIDTH), 1)
    pos = pos0 + i * tp + lax.broadcasted_iota(jnp.int32, (tp, POOL_WIDTH), 0)
    grp = lane // (POOL_WIDTH // len(POOL_WINDOWS))
    total = sums[16]
    wsize = jnp.full((tp, POOL_WIDTH), 16, jnp.int32)
    for gi, wz in enumerate(POOL_WINDOWS[:-1]):
        total = jnp.where(grp == gi, sums[wz], total)
        wsize = jnp.where(grp == gi, wz, wsize)
    cnt = jnp.minimum(wsize, pos + 1).astype(F32)
    dlt = total / cnt - u0
    o_ref[0] = (_dot(dlt.astype(BF16), w_ref[...]) * sc_ref[...]).astype(o_ref.dtype)
    if tp >= 16:
        ext[0:16, :] = ext[tp:tp + 16, :]


def _pool(u, buf, pool_w_l, pool_scale_l, pos0):
    b, t, c = u.shape
    tp = _tile(t, 512) if t >= 8 else t
    buf16 = jnp.concatenate([jnp.zeros((b, 1, c), F32), buf.astype(F32)], axis=1)
    gw = c // len(POOL_WINDOWS)
    wbd = jnp.zeros((c, c), F32)
    for gi in range(len(POOL_WINDOWS)):
        wbd = wbd.at[gi * gw:(gi + 1) * gw, gi * gw:(gi + 1) * gw].set(pool_w_l[gi])
    return pl.pallas_call(
        functools.partial(_pool_kernel, pos0=pos0, tp=tp),
        grid=(b, t // tp),
        in_specs=[pl.BlockSpec((1, 16, c), lambda i, j: (i, 0, 0)),
                  pl.BlockSpec((1, tp, c), lambda i, j: (i, j, 0)),
                  pl.BlockSpec((c, c), lambda i, j: (0, 0)),
                  pl.BlockSpec((1, c), lambda i, j: (0, 0))],
        out_specs=pl.BlockSpec((1, tp, c), lambda i, j: (i, j, 0)),
        out_shape=jax.ShapeDtypeStruct((b, t, c), BF16),
        scratch_shapes=[pltpu.VMEM((16 + tp, c), F32)],
        compiler_params=_params("arbitrary", "arbitrary"),
        name="pool",
    )(buf16, u, wbd.astype(BF16), pool_scale_l[None, :])


def _gla_kernel(gla_ref, misc_ref, s0_ref, wa_ref, ba_ref, og_ref, seg_ref, eb_ref, mk_ref,
                o_ref, sT_ref, st, qs, ks, bs, qts, kts, vs, os_, *, tg, t_valid):
    i = pl.program_id(1)
    c = GLA_SUB
    nsub = tg // c

    @pl.when(i == 0)
    def _():
        st[...] = s0_ref[0]

    gl = gla_ref[...]
    q = gl[:, 0:128] * (GLA_DK ** -0.5)
    k = gl[:, 128:256]
    v = gl[:, 256:512]
    r = gl[:, 512:768]
    x = _dot(misc_ref[...].astype(BF16), wa_ref[...]) + ba_ref[...]
    la = (jnp.minimum(x, 0.0) - jnp.log1p(jnp.exp(-jnp.abs(x)))) * (1.0 / GLA_TAU)
    row = lax.broadcasted_iota(jnp.int32, (tg, LANES), 0)
    if t_valid is not None:
        la = jnp.where(i * tg + row < t_valid, la, 0.0)
    rr = lax.broadcasted_iota(jnp.int32, (tg, tg), 0)
    cc = lax.broadcasted_iota(jnp.int32, (tg, tg), 1)
    same = (rr // c) == (cc // c)
    tri = (same & (cc <= rr)).astype(BF16)
    allo = same.astype(BF16)
    b = _split3_dot(tri, la)
    blast = _split3_dot(allo, la)
    qs[...] = q
    ks[...] = k
    bs[...] = b
    qts[...] = (q * jnp.exp(b)).astype(BF16)
    kts[...] = (k * jnp.exp(blast - b)).astype(BF16)
    vs[...] = v
    eb = eb_ref[...]
    mk = mk_ref[...]
    tt = lax.broadcasted_iota(jnp.int32, (c, LANES), 0)

    def sub(j, carry):
        r0 = pl.multiple_of(j * c, c)
        qi = qs[pl.ds(r0, c), :]
        ki = ks[pl.ds(r0, c), :]
        bi = bs[pl.ds(r0, c), :]
        vi = vs[pl.ds(r0, c), :]
        s_t = st[...]
        o_inter = _dot_nt(qts[pl.ds(r0, c), :], s_t.astype(BF16))
        parts = []
        for s in range(c):
            dec = jnp.exp(jnp.minimum(bi - bi[s:s + 1, :], 0.0))
            parts.append(jnp.where(tt >= s, qi * ki[s:s + 1, :] * dec, 0.0))
        p_all = jnp.concatenate(parts, axis=0).astype(BF16)
        a_all = _dot(p_all, eb)
        o_diag = a_all[0:c, :] * vi[0:1, :]
        for s in range(1, c):
            o_diag = o_diag + a_all[s * c:(s + 1) * c, :] * vi[s:s + 1, :]
        os_[pl.ds(r0, c), :] = o_inter + o_diag
        u_t = _dot_tn(vi.astype(BF16), kts[pl.ds(r0, c), :])
        dl = jnp.exp(bi[c - 1:c, :])
        st[...] = s_t * dl + u_t * mk
        return carry

    lax.fori_loop(0, nsub, sub, 0)

    o = os_[...]
    seg = seg_ref[...]
    og = og_ref[...]
    sil = r * (1.0 / (1.0 + jnp.exp(-r)))
    for h in range(2):
        sl = slice(h * LANES, (h + 1) * LANES)
        o_ref[:, sl] = (_head_rmsnorm(o[:, sl], og[:, sl], seg) * sil[:, sl]).astype(o_ref.dtype)
    sT_ref[0] = st[...]


def _gla(gla_rows, misc_rows, s0, wa2, ba, o_gain, b, t):
    t_valid = None
    if t % GLA_SUB:
        t_valid = t
        tp = -(-t // GLA_SUB) * GLA_SUB
        pad = lambda z: jnp.pad(z.reshape(b, t, -1), ((0, 0), (0, tp - t), (0, 0))).reshape(b * tp, -1)
        gla_rows, misc_rows = pad(gla_rows), pad(misc_rows)
    else:
        tp = t
    tg = _tile(tp, 256)
    assert tg % GLA_SUB == 0
    nt = tp // tg
    kk = GLA_HEADS * GLA_DK
    vv = GLA_HEADS * GLA_DV
    ki = jnp.arange(kk)
    vi = jnp.arange(vv)
    head_eq = (vi[:, None] // GLA_DV == ki[None, :] // GLA_DK)
    mk = head_eq.astype(F32)
    eb = head_eq.T.astype(BF16)
    s0t = jnp.einsum('bhkv,hg->bhvgk', s0.astype(F32), jnp.eye(GLA_HEADS, dtype=F32)).reshape(b, vv, kk)
    wa = jnp.zeros((LANES, kk), F32).at[24:24 + GLA_RANK].set(wa2).astype(BF16)
    full = lambda shape: pl.BlockSpec(shape, lambda i, j: (0,) * len(shape))
    rows = lambda cdim: pl.BlockSpec((tg, cdim), lambda i, j: (i * nt + j, 0))
    o, s_t = pl.pallas_call(
        functools.partial(_gla_kernel, tg=tg, t_valid=t_valid),
        grid=(b, nt),
        in_specs=[rows(768), rows(128), pl.BlockSpec((1, vv, kk), lambda i, j: (i, 0, 0)),
                  full((LANES, kk)), full((1, kk)), full((1, vv)), full((LANES, LANES)),
                  full((kk, vv)), full((vv, kk))],
        out_specs=[rows(vv), pl.BlockSpec((1, vv, kk), lambda i, j: (i, 0, 0))],
        out_shape=[jax.ShapeDtypeStruct((b * tp, vv), BF16), jax.ShapeDtypeStruct((b, vv, kk), F32)],
        scratch_shapes=[pltpu.VMEM((vv, kk), F32), pltpu.VMEM((tg, kk), F32), pltpu.VMEM((tg, kk), F32),
                        pltpu.VMEM((tg, kk), F32), pltpu.VMEM((tg, kk), BF16), pltpu.VMEM((tg, kk), BF16),
                        pltpu.VMEM((tg, vv), F32), pltpu.VMEM((tg, vv), F32)],
        compiler_params=_params("arbitrary", "arbitrary"),
        name="gla",
    )(gla_rows, misc_rows, s0t, wa, ba[None, :], jnp.tile(o_gain, GLA_HEADS)[None, :], _seg_ones(), eb, mk)
    if tp != t:
        o = o.reshape(b, tp, vv)[:, :t].reshape(b * t, vv)
    s5 = s_t.reshape(b, GLA_HEADS, GLA_DV, GLA_HEADS, GLA_DK)
    s_new = jnp.einsum('bhvgk,hg->bhkv', s5, jnp.eye(GLA_HEADS, dtype=F32))
    return o, s_new


def _compress_rows(read, nb, pe_ref, wk_ref, wv_ref, kg_ref, seg_ref):
    acck = jnp.zeros((nb, LANES), F32)
    accv = jnp.zeros((nb, LANES), F32)
    for j in range(L_CMP):
        xk = (read(0, j) + pe_ref[0, j:j + 1, :]).astype(BF16)
        xv = (read(1, j) + pe_ref[1, j:j + 1, :]).astype(BF16)
        acck = acck + _dot(xk, wk_ref[j])
        accv = accv + _dot(xv, wv_ref[j])
    kc = _head_rmsnorm(acck, kg_ref[0:1, :], seg_ref[...])
    return kc, accv


def _compress_kernel(rk_ref, rv_ref, pe_ref, wk_ref, wv_ref, kg_ref, seg_ref, o_ref, *, nb):
    refs = (rk_ref, rv_ref)
    kc, vc = _compress_rows(lambda kind, j: refs[kind][pl.ds(j, nb, stride=L_CMP), :], nb,
                            pe_ref, wk_ref, wv_ref, kg_ref, seg_ref)
    o_ref[0, :, 0:LANES] = kc
    o_ref[0, :, LANES:2 * LANES] = vc


def _cmp_weights(cmp_w, cmp_pe, k_gain):
    def bd(w):
        z = jnp.zeros_like(w)
        return jnp.concatenate([jnp.concatenate([w, z], axis=2), jnp.concatenate([z, w], axis=2)], axis=1)
    pe = jnp.tile(cmp_pe, (1, 1, 2))
    kg = jnp.zeros((8, LANES), F32).at[0].set(jnp.tile(k_gain[0], 2))
    return pe, bd(cmp_w[0]).astype(BF16), bd(cmp_w[1]).astype(BF16), kg


def _compress(rk, rv, k_col, v_col, b, tp, cw):
    nb = tp // L_CMP
    pe, wk, wv, kg = cw
    full = lambda shape: pl.BlockSpec(shape, lambda i: (0,) * len(shape))
    return pl.pallas_call(
        functools.partial(_compress_kernel, nb=nb),
        grid=(b,),
        in_specs=[pl.BlockSpec((tp, LANES), lambda i: (i, k_col)), pl.BlockSpec((tp, LANES), lambda i: (i, v_col)),
                  full((2, L_CMP, LANES)), full((L_CMP, LANES, LANES)), full((L_CMP, LANES, LANES)),
                  full((8, LANES)), full((LANES, LANES))],
        out_specs=pl.BlockSpec((1, nb, 256), lambda i: (i, 0, 0)),
        out_shape=jax.ShapeDtypeStruct((b, nb, 256), F32),
        compiler_params=_params("arbitrary"),
        name="compress",
    )(rk, rv, pe, wk, wv, kg, _seg_ones())


SLAB_PITCH = 2 * NSA_KV * HEAD_DIM + 8


def _paged_compress_kernel(pt_ref, cache_ref, pe_ref, m_ref, kg_ref, seg_ref, o_ref,
                           slab, sem, *, pages, n_slab, layer):
    bi = pl.program_id(0)
    si = pl.program_id(1)
    step = bi * n_slab + si
    nsteps = pl.num_programs(0) * n_slab
    slot = step % 2
    rows_cmp = 2 * NSA_KV * HEAD_DIM

    def copy(page, slot_, p):
        return pltpu.make_async_copy(cache_ref.at[layer, page, pl.ds(0, rows_cmp), :],
                                     slab.at[slot_, pl.ds(p * SLAB_PITCH, rows_cmp), :], sem.at[slot_])

    def issue(bb, ss, slot_):
        for p in range(pages):
            copy(pt_ref[bb, ss * pages + p], slot_, p).start()

    @pl.when(step == 0)
    def _():
        issue(0, 0, 0)

    @pl.when(step + 1 < nsteps)
    def _():
        nxt = step + 1
        issue(nxt // n_slab, nxt % n_slab, 1 - slot)

    for p in range(pages):
        copy(0, slot, p).wait()

    def rows(r0):
        return slab[slot, pl.ds(r0, pages, stride=SLAB_PITCH), :]

    for c in range(2):
        acc = jnp.zeros((NSA_KV * pages, 2 * LANES), F32)
        for dp in range(HEAD_DIM // 2):
            parts = []
            for g in range(NSA_KV):
                r0 = (c * NSA_KV + g) * HEAD_DIM + 2 * dp
                parts.append(jnp.concatenate([rows(r0), rows(r0 + 1)], axis=1))
            a = jnp.concatenate(parts, axis=0) + pe_ref[c, dp:dp + 1, :]
            acc = acc + _dot(a.astype(BF16), m_ref[c, dp])
        for g in range(NSA_KV):
            blk = acc[g * pages:(g + 1) * pages, :]
            if c == 0:
                for h in range(2):
                    sl = slice(h * LANES, (h + 1) * LANES)
                    o_ref[0, c, g, :, sl] = _head_rmsnorm(blk[:, sl], kg_ref[0:1, :], seg_ref[...])
            else:
                o_ref[0, c, g] = blk


def _paged_cmp_weights(cmp_w, cmp_pe):
    nblk = PAGE_SIZE // L_CMP
    k6 = jnp.einsum('nm,cjde->cdnjme', jnp.eye(nblk, dtype=F32), cmp_w)
    m = k6.reshape(2, HEAD_DIM // 2, 2 * PAGE_SIZE, nblk * HEAD_DIM).astype(BF16)
    pe = jnp.tile(cmp_pe.transpose(0, 2, 1), (1, 1, nblk)).reshape(2, HEAD_DIM // 2, 2 * PAGE_SIZE)
    return pe, m


def _paged_compress(cache_v, page_table, layer, cw, pw):
    b, n_pages = page_table.shape
    pages = math.gcd(n_pages, 64)
    n_slab = n_pages // pages
    pe, m = pw
    kg = cw[3]
    nblk = PAGE_SIZE // L_CMP
    full = lambda shape: pl.BlockSpec(shape, lambda i, j, pt: (0,) * len(shape))
    grid_spec = pltpu.PrefetchScalarGridSpec(
        num_scalar_prefetch=1,
        grid=(b, n_slab),
        in_specs=[pl.BlockSpec(memory_space=pl.ANY), full(pe.shape), full(m.shape), full((8, LANES)),
                  full((LANES, LANES))],
        out_specs=pl.BlockSpec((1, 2, NSA_KV, pages, nblk * HEAD_DIM), lambda i, j, pt: (i, 0, 0, j, 0)),
        scratch_shapes=[pltpu.VMEM((2, pages * SLAB_PITCH, LANES), F32), pltpu.SemaphoreType.DMA((2,))],
    )
    return pl.pallas_call(
        functools.partial(_paged_compress_kernel, pages=pages, n_slab=n_slab, layer=layer),
        grid_spec=grid_spec,
        out_shape=jax.ShapeDtypeStruct((b, 2, NSA_KV, n_pages, nblk * HEAD_DIM), F32),
        compiler_params=_params("arbitrary", "arbitrary"),
        name="paged_compress",
    )(page_table, cache_v, pe, m, kg, _seg_ones())


def _nsa_prompt_kernel(q_ref, kc_ref, vc_ref, ks_ref, vs_ref, kw_ref, vw_ref, gate_ref, ex_ref,
                       o_ref, ksb, vsb, kwb, vwb, *, tq, ck):
    i = pl.program_id(1)
    t0 = i * tq

    @pl.when(i == 0)
    def _():
        ksb[...] = ks_ref[...].astype(BF16)
        vsb[...] = vs_ref[...].astype(BF16)
        kwb[...] = kw_ref[...].astype(BF16)
        vwb[...] = vw_ref[...].astype(BF16)

    n_slc = LANES // 2
    row_t = t0 + lax.broadcasted_iota(jnp.int32, (tq, 1), 0)
    lane = lax.broadcasted_iota(jnp.int32, (1, LANES), 1)
    blk = lane % n_slc
    nat = jnp.where(lane < n_slc, 2 * lane, 2 * (lane - n_slc) + 1)
    cmp_ok = ((nat + 1) * L_CMP - 1) <= row_t
    cur = row_t // L_SLC
    done = blk < cur
    gx = gate_ref[...]
    gates = 1.0 / (1.0 + jnp.exp(-gx))
    n_chunks = (t0 + tq + ck - 1) // ck
    kstart = pl.multiple_of(jnp.maximum(t0 - WINDOW, 0), tq)
    wlen = WINDOW + tq
    wpos = kstart + lax.broadcasted_iota(jnp.int32, (1, wlen), 1)
    win_bias = jnp.where((wpos <= row_t) & (wpos >= row_t - WINDOW), 0.0, MASKED)

    for g in range(NSA_KV):
        in_g = (lane // HEAD_DIM) == g
        heads = []
        for j in range(NSA_HPG):
            h = g * NSA_HPG + j
            qh = q_ref[:, (h // 2) * LANES:(h // 2 + 1) * LANES].astype(F32)
            if h % 2 != g:
                qh = pltpu.roll(qh, HEAD_DIM, axis=1)
            heads.append(jnp.where(in_g, qh, 0.0).astype(BF16))
        qg = jnp.concatenate(heads, axis=0)
        s_c = _dot_nt(qg, kc_ref[0]).reshape(NSA_HPG, tq, LANES)
        p_c = _masked_softmax(s_c, cmp_ok[None])
        o_c = _dot(p_c.reshape(NSA_HPG * tq, LANES).astype(BF16), vc_ref[0])
        imp = p_c[0] + p_c[1] + p_c[2] + p_c[3]
        imp = imp + pltpu.roll(imp, n_slc, axis=1)
        key = jnp.where(done, lax.bitcast_convert_type(imp, jnp.int32), -1)
        key_m1 = key - 1
        rank = jnp.zeros((tq, LANES), jnp.int32)
        for r in range(1, n_slc):
            vm = pltpu.roll(key, r, axis=1)
            rank = rank + jnp.where(vm > jnp.where(blk >= r, key_m1, key), 1, 0)
        sel = (done & (rank < N_SEL - 1)) | (blk == cur)
        selb = jnp.where(sel, 1.0, 0.0).astype(BF16)

        def chunk(c, carry):
            m, l, acc = carry
            k0 = pl.multiple_of(c * ck, ck)
            s = _dot_nt(qg, ksb[pl.ds(k0, ck), :]).reshape(NSA_HPG, tq, ck)
            hit = _dot(selb, ex_ref[c])
            kpos = k0 + lax.broadcasted_iota(jnp.int32, (1, ck), 1)
            bias = jnp.where((hit > 0.5) & (kpos <= row_t), 0.0, MASKED)
            sm = s + bias[None]
            m_new = jnp.maximum(m, jnp.max(sm, axis=-1, keepdims=True))
            p = jnp.exp(sm - m_new)
            alpha = jnp.exp(m - m_new)
            l = alpha * l + jnp.sum(p, axis=-1, keepdims=True)
            pv = _dot(p.reshape(NSA_HPG * tq, ck).astype(BF16), vsb[pl.ds(k0, ck), :])
            acc = alpha * acc + pv.reshape(NSA_HPG, tq, LANES)
            return m_new, l, acc

        m0 = jnp.full((NSA_HPG, tq, 1), MASKED, F32)
        l0 = jnp.zeros((NSA_HPG, tq, 1), F32)
        a0 = jnp.zeros((NSA_HPG, tq, LANES), F32)
        _, l_s, acc_s = lax.fori_loop(0, n_chunks, chunk, (m0, l0, a0))
        o_s = acc_s * (1.0 / jnp.where(l_s > 0.0, l_s, 1.0))
        s_w = _dot_nt(qg, kwb[pl.ds(kstart, wlen), :]).reshape(NSA_HPG, tq, wlen) + win_bias[None]
        e_w = jnp.exp(s_w - jnp.max(s_w, axis=-1, keepdims=True))
        p_w = e_w * (1.0 / jnp.sum(e_w, axis=-1, keepdims=True))
        o_w = _dot(p_w.reshape(NSA_HPG * tq, wlen).astype(BF16), vwb[pl.ds(kstart, wlen), :])
        o_c = o_c.reshape(NSA_HPG, tq, LANES)
        o_w = o_w.reshape(NSA_HPG, tq, LANES)
        for pr in range(NSA_HPG // 2):
            pair = []
            for jj in range(2):
                j = 2 * pr + jj
                h = g * NSA_HPG + j
                o = (gates[:, 3 * h:3 * h + 1] * o_c[j] + gates[:, 3 * h + 1:3 * h + 2] * o_s[j]
                     + gates[:, 3 * h + 2:3 * h + 3] * o_w[j])
                pair.append(o if jj == g else pltpu.roll(o, HEAD_DIM, axis=1))
            col = (g * (NSA_HPG // 2) + pr) * LANES
            o_ref[:, col:col + LANES] = jnp.where(lane < HEAD_DIM, pair[0], pair[1]).astype(o_ref.dtype)


def _nsa_prompt(q_rows, rows, win, misc, cw, b, t):
    assert t // L_CMP == LANES and t % L_SLC == 0, "prompt kernel is laid out for 128 compressed blocks"
    tq = 256
    ck = 512
    n_slc = t // L_SLC
    cmp = _compress(rows, rows, 0, 1, b, t, cw)
    order = jnp.concatenate([jnp.arange(0, LANES, 2), jnp.arange(1, LANES, 2)])
    cmp = cmp[:, order].astype(BF16)
    kc, vc = cmp[:, :, 0:LANES], cmp[:, :, LANES:2 * LANES]
    key_blk = jnp.arange(t) // L_SLC
    ex = (jnp.arange(LANES)[:, None] == key_blk[None, :]) & (jnp.arange(LANES)[:, None] < n_slc)
    ex = ex.astype(BF16).reshape(LANES, t // ck, ck).transpose(1, 0, 2)
    nt = t // tq
    col = lambda c: pl.BlockSpec((t, LANES), lambda i, j: (i, c))
    return pl.pallas_call(
        functools.partial(_nsa_prompt_kernel, tq=tq, ck=ck),
        grid=(b, nt),
        in_specs=[pl.BlockSpec((tq, NSA_WIDTH), lambda i, j: (i * nt + j, 0)),
                  pl.BlockSpec((1, LANES, LANES), lambda i, j: (i, 0, 0)),
                  pl.BlockSpec((1, LANES, LANES), lambda i, j: (i, 0, 0)),
                  col(2), col(3), col(0), col(1),
                  pl.BlockSpec((tq, LANES), lambda i, j: (i * nt + j, 0)),
                  pl.BlockSpec((t // ck, LANES, ck), lambda i, j: (0, 0, 0))],
        out_specs=pl.BlockSpec((tq, NSA_WIDTH), lambda i, j: (i * nt + j, 0)),
        out_shape=jax.ShapeDtypeStruct((b * t, NSA_WIDTH), BF16),
        scratch_shapes=[pltpu.VMEM((t, LANES), BF16)] * 4,
        compiler_params=_params("arbitrary", "arbitrary"),
        name="nsa_prompt",
    )(q_rows, kc, vc, rows, rows, win, win, misc, ex)


def _split3_dot_r(a, ones):
    a1 = a.astype(BF16)
    r1 = a - a1.astype(F32)
    a2 = r1.astype(BF16)
    a3 = (r1 - a2.astype(F32)).astype(BF16)
    return _dot(a1, ones) + _dot(a2, ones) + _dot(a3, ones)


def _nsa_select_kernel(q_ref, cp_ref, ct_ref, pair_ref, oc_ref, sel_ref, *, t, past_len, n_past):
    nrow = NSA_HPG * t
    npad = n_past + LANES
    n_slc_pad = pair_ref.shape[1]
    row = lax.broadcasted_iota(jnp.int32, (nrow, 1), 0)
    pos = past_len + row % t
    lane_c = lax.broadcasted_iota(jnp.int32, (1, npad), 1)
    cmp_ok = ((lane_c + 1) * L_CMP - 1) <= pos
    cur = (past_len + lax.broadcasted_iota(jnp.int32, (t, 1), 0)) // L_SLC
    lane_s = lax.broadcasted_iota(jnp.int32, (1, n_slc_pad), 1)
    lane_o = lax.broadcasted_iota(jnp.int32, (1, LANES), 1)
    done = lane_s < cur
    for g in range(NSA_KV):
        q = q_ref[0, g]
        s = jnp.concatenate([_dot_nt(q, cp_ref[0, 0, g].astype(BF16)),
                             _dot_nt(q, ct_ref[0, 0, g].astype(BF16))], axis=1)
        p = _masked_softmax(s, cmp_ok)
        pb = p.astype(BF16)
        oc_ref[0, g] = (_dot(pb[:, :n_past], cp_ref[0, 1, g].astype(BF16))
                        + _dot(pb[:, n_past:], ct_ref[0, 1, g].astype(BF16)))
        imp_c = p[0:t]
        for j in range(1, NSA_HPG):
            imp_c = imp_c + p[j * t:(j + 1) * t]
        imp = _split3_dot_r(imp_c, pair_ref[...])
        key = jnp.where(done, lax.bitcast_convert_type(imp, jnp.int32), -1)
        res = jnp.zeros((t, LANES), jnp.int32)
        for r in range(N_SEL - 1):
            best = jnp.max(key, axis=1, keepdims=True)
            idx = jnp.min(jnp.where(key == best, lane_s, n_slc_pad), axis=1, keepdims=True)
            res = jnp.where(lane_o == r, idx, res)
            key = jnp.where(lane_s == idx, -2, key)
        sel_ref[0, g * t:(g + 1) * t, :] = jnp.where(lane_o == N_SEL - 1, cur, res)


def _nsa_attend_kernel(pt_ref, sel_ref, cache_ref, q_ref, qp_ref, oc_ref, gate_ref, snew_ref, wnew_ref,
                       wst_ref, o_ref, buf, sc, sem, *, t, past_len, layer, w_buf):
    bi = pl.program_id(0)
    slot = bi % 2
    nsel = N_SEL - 1
    n_past_blk = past_len // L_SLC
    per_page = PAGE_SIZE // L_SLC
    nrow = t * NSA_HPG

    def block_of(bb, g, tok, k):
        return jnp.minimum(sel_ref[bb, (g * t + tok) * N_SEL + k], n_past_blk - 1)

    def copy(page, slot_, g, idx):
        return pltpu.make_async_copy(cache_ref.at[layer, page, pl.ds(2, 2), g], buf.at[slot_, idx], sem.at[slot_])

    def issue(bb, slot_):
        for g in range(NSA_KV):
            for tok in range(t):
                for k in range(nsel):
                    page = pt_ref[bb, block_of(bb, g, tok, k) // per_page]
                    copy(page, slot_, g, (g * t + tok) * nsel + k).start()

    @pl.when(bi == 0)
    def _():
        issue(0, 0)

    @pl.when(bi + 1 < pl.num_programs(0))
    def _():
        issue(bi + 1, 1 - slot)

    for g in range(NSA_KV):
        for i in range(t * nsel):
            copy(0, slot, g, g * t * nsel + i).wait()

    row = lax.broadcasted_iota(jnp.int32, (nrow, 1), 0)
    tok_r = row // NSA_HPG
    pos = past_len + tok_r
    lane = lax.broadcasted_iota(jnp.int32, (1, LANES), 1)
    new_bias = jnp.where(lane <= tok_r, 0.0, MASKED)
    wpos = past_len - w_buf + lax.broadcasted_iota(jnp.int32, (1, w_buf), 1)
    win_bias = jnp.where((wpos >= 0) & (wpos <= pos) & (wpos >= pos - WINDOW), 0.0, MASKED)
    for g in range(NSA_KV):
        q = q_ref[0, g]
        qp = qp_ref[0, g]
        glanes = slice(g * HEAD_DIM, (g + 1) * HEAD_DIM)
        m = jnp.full((nrow, 1), MASKED, F32)
        for tok in range(t):
            cur = (past_len + tok) // L_SLC
            for k in range(nsel):
                n = sel_ref[bi, (g * t + tok) * N_SEL + k]
                half = block_of(bi, g, tok, k) % per_page
                valid = (tok_r == tok) & (lane // L_SLC == half) & (n < cur)
                i = tok * nsel + k
                s = _dot(q, buf[slot, g * t * nsel + i, 0].astype(BF16)) + jnp.where(valid, 0.0, MASKED)
                sc[i] = s
                m = jnp.maximum(m, jnp.max(s, axis=1, keepdims=True))
        s_new = _dot_nt(qp, snew_ref[0, :, 0:LANES].astype(BF16)) + new_bias
        m = jnp.maximum(m, jnp.max(s_new, axis=1, keepdims=True))
        p_new = jnp.exp(s_new - m)
        l = jnp.sum(p_new, axis=1, keepdims=True)
        acc = _dot(p_new.astype(BF16), snew_ref[0, :, LANES:2 * LANES].astype(BF16))[:, glanes]
        for i in range(t * nsel):
            p = jnp.exp(sc[i] - m)
            l = l + jnp.sum(p, axis=1, keepdims=True)
            acc = acc + _dot_nt(p.astype(BF16), buf[slot, g * t * nsel + i, 1].astype(BF16))
        o_s = acc * (1.0 / l)
        s_w = _dot(q, wst_ref[0, 0, 0, g].astype(BF16)) + win_bias
        s_wn = _dot_nt(qp, wnew_ref[0, :, 0:LANES].astype(BF16)) + new_bias
        m_w = jnp.maximum(jnp.max(s_w, axis=1, keepdims=True), jnp.max(s_wn, axis=1, keepdims=True))
        e_w = jnp.exp(s_w - m_w)
        e_n = jnp.exp(s_wn - m_w)
        l_w = jnp.sum(e_w, axis=1, keepdims=True) + jnp.sum(e_n, axis=1, keepdims=True)
        acc_w = (_dot_nt(e_w.astype(BF16), wst_ref[0, 0, 1, g].astype(BF16))
                 + _dot(e_n.astype(BF16), wnew_ref[0, :, LANES:2 * LANES].astype(BF16))[:, glanes])
        o_w = acc_w * (1.0 / l_w)
        gate = 1.0 / (1.0 + jnp.exp(-gate_ref[0, g]))
        o_ref[0, g] = gate[:, 0:1] * oc_ref[0, g] + gate[:, 1:2] * o_s + gate[:, 2:3] * o_w


def _nsa_sample(q_rows, rows, win, misc, cw, pw, cache_v, page_table, layer, win_state_v, b, t, past_len):
    t_all = past_len + t
    t_pad = -(-t_all // L_SLC) * L_SLC
    n_cmp = t_pad // L_CMP
    n_tail = (t_pad - past_len) // L_CMP
    n_past = past_len // L_CMP
    w_buf = win_state_v.shape[-1]
    assert past_len % PAGE_SIZE == 0 and t <= L_SLC and n_tail <= LANES and NSA_KV * t * N_SEL <= LANES
    cmp_past = _paged_compress(cache_v, page_table, layer, cw, pw).reshape(b, 2, NSA_KV, n_past, HEAD_DIM)
    tail = jnp.concatenate([rows[:, 0:256].reshape(b, t, 256),
                            jnp.zeros((b, t_pad - t_all, 256), F32)], axis=1).reshape(b * (t_pad - past_len), 256)
    cmp_tail = _compress(tail, tail, 0, 1, 1, b * (t_pad - past_len), cw)
    cmp_tail = cmp_tail.reshape(b, n_tail, 2, NSA_KV, HEAD_DIM).transpose(0, 2, 3, 1, 4)
    cmp_tail = jnp.pad(cmp_tail, ((0, 0), (0, 0), (0, 0), (0, LANES - n_tail), (0, 0)))
    q5 = q_rows.reshape(b, t, NSA_KV, NSA_HPG, HEAD_DIM)
    nrow = NSA_HPG * t
    q_jt = q5.transpose(0, 2, 3, 1, 4).reshape(b, NSA_KV, nrow, HEAD_DIM)
    q_tj = q5.transpose(0, 2, 1, 3, 4).reshape(b, NSA_KV, nrow, HEAD_DIM)
    qp_tj = jnp.zeros((b, NSA_KV, nrow, LANES), BF16)
    for g in range(NSA_KV):
        qp_tj = qp_tj.at[:, g, :, g * HEAD_DIM:(g + 1) * HEAD_DIM].set(q_tj[:, g])
    npad = n_past + LANES
    n_slc_pad = -(-(t_pad // L_SLC) // LANES) * LANES
    nn = jnp.arange(npad)
    pair = ((nn[:, None] // (L_SLC // L_CMP) == jnp.arange(n_slc_pad)[None, :]) & (nn[:, None] < n_cmp)).astype(BF16)
    per_b = lambda shape: pl.BlockSpec(shape, lambda i: (i,) + (0,) * (len(shape) - 1))
    o_c, sel = pl.pallas_call(
        functools.partial(_nsa_select_kernel, t=t, past_len=past_len, n_past=n_past),
        grid=(b,),
        in_specs=[per_b((1, NSA_KV, nrow, HEAD_DIM)), per_b((1, 2, NSA_KV, n_past, HEAD_DIM)),
                  per_b((1, 2, NSA_KV, LANES, HEAD_DIM)), pl.BlockSpec((npad, n_slc_pad), lambda i: (0, 0))],
        out_specs=[per_b((1, NSA_KV, nrow, HEAD_DIM)), per_b((1, NSA_KV * t, LANES))],
        out_shape=[jax.ShapeDtypeStruct((b, NSA_KV, nrow, HEAD_DIM), F32),
                   jax.ShapeDtypeStruct((b, NSA_KV * t, LANES), jnp.int32)],
        compiler_params=_params("arbitrary"),
        name="nsa_select",
    )(q_jt, cmp_past, cmp_tail, pair)
    sel_c = sel[:, :, :N_SEL].reshape(b, NSA_KV * t * N_SEL)
    o_c = o_c.reshape(b, NSA_KV, NSA_HPG, t, HEAD_DIM).transpose(0, 1, 3, 2, 4).reshape(b, NSA_KV, nrow, HEAD_DIM)
    gates = misc[:, 0:3 * NSA_HEADS].reshape(b, t, NSA_KV, NSA_HPG, 3).transpose(0, 2, 1, 3, 4)
    gates = jnp.pad(gates.reshape(b, NSA_KV, nrow, 3), ((0, 0), (0, 0), (0, 0), (0, LANES - 3)))
    pad_rows = lambda z: jnp.pad(z.reshape(b, t, 256), ((0, 0), (0, LANES - t), (0, 0)))
    cache6 = cache_v.reshape(cache_v.shape[0], cache_v.shape[1], 4, NSA_KV, HEAD_DIM, PAGE_SIZE)
    nsel = N_SEL - 1
    pb = lambda shape: pl.BlockSpec(shape, lambda i, pt, sl: (i,) + (0,) * (len(shape) - 1))
    grid_spec = pltpu.PrefetchScalarGridSpec(
        num_scalar_prefetch=2,
        grid=(b,),
        in_specs=[pl.BlockSpec(memory_space=pl.ANY), pb((1, NSA_KV, nrow, HEAD_DIM)), pb((1, NSA_KV, nrow, LANES)),
                  pb((1, NSA_KV, nrow, HEAD_DIM)), pb((1, NSA_KV, nrow, LANES)), pb((1, LANES, 256)),
                  pb((1, LANES, 256)),
                  pl.BlockSpec((1, 1, 2, NSA_KV, HEAD_DIM, w_buf), lambda i, pt, sl: (layer, i, 0, 0, 0, 0))],
        out_specs=pb((1, NSA_KV, nrow, HEAD_DIM)),
        scratch_shapes=[pltpu.VMEM((2, NSA_KV * t * nsel, 2, HEAD_DIM, PAGE_SIZE), F32),
                        pltpu.VMEM((t * nsel, nrow, LANES), F32), pltpu.SemaphoreType.DMA((2,))],
    )
    y = pl.pallas_call(
        functools.partial(_nsa_attend_kernel, t=t, past_len=past_len, layer=layer, w_buf=w_buf),
        grid_spec=grid_spec,
        out_shape=jax.ShapeDtypeStruct((b, NSA_KV, nrow, HEAD_DIM), F32),
        compiler_params=_params("arbitrary"),
        name="nsa_attend",
    )(page_table, sel_c, cache6, q_tj, qp_tj, o_c, gates, pad_rows(rows[:, 256:512]), pad_rows(win), win_state_v)
    y = y.reshape(b, NSA_KV, t, NSA_HPG, HEAD_DIM).transpose(0, 2, 1, 3, 4).reshape(b * t, NSA_WIDTH)
    return y.astype(BF16)


def _outproj_kernel(yp_ref, yn_ref, yg_ref, x_ref, w_ref, g_ref, *rest, n_experts):
    with_router = n_experts > 0
    if with_router:
        r_ref, h_o, hn_o, lg_o = rest
    else:
        h_o, hn_o = rest
    h = (x_ref[...] + _dot(yp_ref[...], w_ref[0:256, :]) + _dot(yn_ref[...], w_ref[256:768, :])
         + _dot(yg_ref[...], w_ref[768:1024, :]))
    h_o[...] = h
    ms = jnp.mean(h * h, axis=-1, keepdims=True)
    hn = (h * lax.rsqrt(ms + EPS) * g_ref[...]).astype(BF16)
    hn_o[...] = hn
    if with_router:
        lane = lax.broadcasted_iota(jnp.int32, (1, LANES), 1)
        lg = jnp.where(lane < n_experts, _dot(hn, r_ref[...]), -jnp.inf)
        v1 = jnp.max(lg, axis=1, keepdims=True)
        i1 = jnp.min(jnp.where(lg == v1, lane, LANES), axis=1, keepdims=True)
        lg2 = jnp.where(lane == i1, -jnp.inf, lg)
        v2 = jnp.max(lg2, axis=1, keepdims=True)
        i2 = jnp.min(jnp.where(lg2 == v2, lane, LANES), axis=1, keepdims=True)
        e2 = jnp.exp(v2 - v1)
        den = 1.0 + e2
        lg_o[...] = jnp.where(lane == 0, 1.0 / den, jnp.where(lane == 1, e2 / den, jnp.where(
            lane == 2, i1.astype(F32), jnp.where(lane == 3, i2.astype(F32), 0.0))))


def _outproj(y_pool, y_nsa, y_gla, x, w_out_l, g_ffn, router):
    n, d = x.shape
    tm = _tile(n, 384)
    with_router = router is not None
    row = lambda c: pl.BlockSpec((tm, c), lambda i: (i, 0))
    full = lambda shape: pl.BlockSpec(shape, lambda i: (0, 0))
    in_specs = [row(256), row(512), row(256), row(d), full((d, d)), full((1, d))]
    args = [y_pool, y_nsa, y_gla, x, w_out_l.astype(BF16), g_ffn[None, :]]
    out_specs = [row(d), row(d)]
    out_shape = [jax.ShapeDtypeStruct((n, d), F32), jax.ShapeDtypeStruct((n, d), BF16)]
    if with_router:
        ne = router.shape[1]
        in_specs.append(full((d, LANES)))
        args.append(jnp.zeros((d, LANES), F32).at[:, :ne].set(router).astype(BF16))
        out_specs.append(row(LANES))
        out_shape.append(jax.ShapeDtypeStruct((n, LANES), F32))
    return pl.pallas_call(
        functools.partial(_outproj_kernel, n_experts=router.shape[1] if with_router else 0),
        grid=(n // tm,), in_specs=in_specs, out_specs=out_specs, out_shape=out_shape,
        compiler_params=_params("arbitrary"), name="outproj",
    )(*args)


def _swiglu_kernel(be_ref, bv_ref, x_ref, gate_ref, w1_ref, w3_ref, w2_ref, o_ref):
    i = pl.program_id(0)
    j = pl.program_id(1)
    valid = bv_ref[i] > 0

    @pl.when(jnp.logical_not(valid) & (j == 0))
    def _():
        o_ref[...] = jnp.zeros_like(o_ref)

    @pl.when(valid)
    def _():
        x = x_ref[...]
        a = _dot(x, w1_ref[0])
        c = _dot(x, w3_ref[0])
        hmid = (a * (1.0 / (1.0 + jnp.exp(-a))) * c).astype(BF16)
        y = _dot(hmid, w2_ref[0])

        @pl.when(j == 0)
        def _():
            o_ref[...] = y

        @pl.when(j > 0)
        def _():
            o_ref[...] += y

        @pl.when(j == pl.num_programs(1) - 1)
        def _():
            o_ref[...] = o_ref[...] * gate_ref[...]


def _swiglu(x, gate, blk_expert, blk_valid, w1, w3, w2, tm):
    r, d = x.shape
    f = w1.shape[2]
    tf = f // 2 if (f // 2) % LANES == 0 else f
    grid_spec = pltpu.PrefetchScalarGridSpec(
        num_scalar_prefetch=2,
        grid=(r // tm, f // tf),
        in_specs=[pl.BlockSpec((tm, d), lambda i, j, be, bv: (i, 0)),
                  pl.BlockSpec((tm, 1), lambda i, j, be, bv: (i, 0)),
                  pl.BlockSpec((1, d, tf), lambda i, j, be, bv: (be[i], 0, j)),
                  pl.BlockSpec((1, d, tf), lambda i, j, be, bv: (be[i], 0, j)),
                  pl.BlockSpec((1, tf, d), lambda i, j, be, bv: (be[i], j, 0))],
        out_specs=pl.BlockSpec((tm, d), lambda i, j, be, bv: (i, 0)),
    )
    return pl.pallas_call(
        _swiglu_kernel, grid_spec=grid_spec,
        out_shape=jax.ShapeDtypeStruct((r, d), F32),
        compiler_params=_params("arbitrary", "arbitrary"), name="swiglu",
    )(blk_expert, blk_valid, x, gate, w1, w3, w2)


def _dense_ffn(h, hn, w1, w3, w2):
    n = h.shape[0]
    tm = _tile(n, 704)
    nblk = n // tm
    y = _swiglu(hn, jnp.ones((n, 1), F32), jnp.zeros((nblk,), jnp.int32), jnp.ones((nblk,), jnp.int32),
                w1[None].astype(BF16), w3[None].astype(BF16), w2[None].astype(BF16), tm)
    return h + y


def _moe_ffn(h, hn, route, w1, w3, w2):
    n = h.shape[0]
    ne = w1.shape[0]
    tm = 512
    a = n * TOP_K
    e_a = route[:, TOP_K:2 * TOP_K].astype(jnp.int32).reshape(-1)
    g_a = route[:, 0:TOP_K].reshape(-1)
    tok_a = jnp.repeat(jnp.arange(n), TOP_K)
    onehot = (e_a[:, None] == jnp.arange(ne)[None, :]).astype(jnp.int32)
    csum = jnp.cumsum(onehot, axis=0)
    counts = csum[-1]
    padded = (counts + tm - 1) // tm * tm
    p_end = jnp.cumsum(padded)
    p_start = p_end - padded
    dest = jnp.sum(onehot * (csum - onehot + p_start[None, :]), axis=1)
    nblk = -(-a // tm) + ne
    r = nblk * tm
    row_info = jnp.zeros((r, 2), F32).at[dest].set(jnp.stack([tok_a.astype(F32), g_a], axis=1))
    row_tok = row_info[:, 0].astype(jnp.int32)
    row_gate = row_info[:, 1]
    blk0 = jnp.arange(nblk) * tm
    blk_valid = (blk0 < p_end[-1]).astype(jnp.int32)
    last = jnp.clip(jnp.searchsorted(p_end, p_end[-1] - 1, side='right'), 0, ne - 1)
    blk_expert = jnp.clip(jnp.searchsorted(p_end, blk0, side='right'), 0, ne - 1)
    blk_expert = jnp.where(blk_valid > 0, blk_expert, last).astype(jnp.int32)
    yb = _swiglu(hn[row_tok], row_gate[:, None], blk_expert, blk_valid,
                 w1.astype(BF16), w3.astype(BF16), w2.astype(BF16), tm)
    pos = dest.reshape(n, TOP_K)
    return h + (yb[pos[:, 0]] + yb[pos[:, 1]])


def kernel(x_prompt, x_sample, cache_nsa_kv, state_nsa_win, state_gla, state_pool, page_table,
           norm_mix, norm_ffn, w_in, w_out, pool_w, pool_scale, nsa_q_norm, nsa_k_norm,
           nsa_cmp_w, nsa_cmp_pe, gla_wa2, gla_ba, gla_norm, ffn_w1, ffn_w3, ffn_w2,
           moe_router, moe_w1, moe_w3, moe_w2):
    bp, tp, d = x_prompt.shape
    bs, ts, _ = x_sample.shape
    depth = w_in.shape[0]
    n_pool = cache_nsa_kv.shape[1]
    n_pages = page_table.shape[1]
    past_len = n_pages * PAGE_SIZE
    w_buf = state_nsa_win.shape[2]
    npr = bp * tp
    cache_v = cache_nsa_kv.transpose(0, 1, 3, 4, 5, 2).reshape(depth, n_pool, 4 * NSA_KV * HEAD_DIM, PAGE_SIZE)
    win_state_v = state_nsa_win.transpose(0, 1, 3, 4, 5, 2)
    x = jnp.concatenate([x_prompt.reshape(npr, d), x_sample.reshape(bs * ts, d)], axis=0)
    kv_p, kv_s, win_p, win_s, gla_p, gla_s, pool_p, pool_s = [], [], [], [], [], [], [], []
    n_win_p = min(WINDOW, tp)
    for l in range(depth):
        u_pool, q_rows, rows, win, misc, gla_rows = _inproj(x, norm_mix[l], w_in[l], nsa_q_norm[l], nsa_k_norm[l])
        cw = _cmp_weights(nsa_cmp_w[l], nsa_cmp_pe[l], nsa_k_norm[l])
        up = u_pool[:npr].reshape(bp, tp, POOL_WIDTH)
        us = u_pool[npr:].reshape(bs, ts, POOL_WIDTH)
        yp_pool = _pool(up, jnp.zeros((bp, POOL_BUF, POOL_WIDTH), F32), pool_w[l], pool_scale[l], 0)
        ys_pool = _pool(us, state_pool[l], pool_w[l], pool_scale[l], past_len)
        pool_p.append(jnp.concatenate([jnp.zeros((bp, POOL_BUF, POOL_WIDTH), F32), up], axis=1)[:, -POOL_BUF:])
        pool_s.append(jnp.concatenate([state_pool[l], us], axis=1)[:, -POOL_BUF:])
        yp_nsa = _nsa_prompt(q_rows, rows, win, misc, cw, bp, tp)
        pw = _paged_cmp_weights(nsa_cmp_w[l], nsa_cmp_pe[l])
        ys_nsa = _nsa_sample(q_rows[npr:], rows[npr:], win[npr:], misc[npr:], cw, pw, cache_v,
                             page_table, l, win_state_v, bs, ts, past_len)
        kv_p.append(rows[:npr].reshape(bp, tp, 4, NSA_KV, HEAD_DIM))
        kv_s.append(rows[npr:].reshape(bs, ts, 4, NSA_KV, HEAD_DIM))
        win_full_p = jnp.concatenate([jnp.zeros((bp, WINDOW, 256), F32), win[:npr].reshape(bp, tp, 256)], axis=1)
        win_p.append(win_full_p[:, -n_win_p:].reshape(bp, n_win_p, 2, NSA_KV, HEAD_DIM))
        win_ext_s = jnp.concatenate([state_nsa_win[l], win[npr:].reshape(bs, ts, 2, NSA_KV, HEAD_DIM)], axis=1)
        win_s.append(win_ext_s[:, -w_buf:])
        yp_gla, sp = _gla(gla_rows, misc, jnp.zeros((bp, GLA_HEADS, GLA_DK, GLA_DV), F32),
                          gla_wa2[l], gla_ba[l], gla_norm[l], bp, tp)
        ys_gla, ss = _gla(gla_rows[npr:], misc[npr:], state_gla[l], gla_wa2[l], gla_ba[l], gla_norm[l], bs, ts)
        gla_p.append(sp.astype(state_gla.dtype))
        gla_s.append(ss.astype(state_gla.dtype))
        i = l // 2
        router = moe_router[i] if l % 2 else None
        res = _outproj(jnp.concatenate([yp_pool.reshape(npr, -1), ys_pool.reshape(bs * ts, -1)], axis=0),
                       jnp.concatenate([yp_nsa, ys_nsa], axis=0),
                       jnp.concatenate([yp_gla, ys_gla], axis=0), x, w_out[l], norm_ffn[l], router)
        if l % 2 == 0:
            h, hn = res
            x = _dense_ffn(h, hn, ffn_w1[i], ffn_w3[i], ffn_w2[i])
        else:
            h, hn, logits = res
            x = _moe_ffn(h, hn, logits, moe_w1[i], moe_w3[i], moe_w2[i])
    return (x[:npr].reshape(bp, tp, d), x[npr:].reshape(bs, ts, d),
            jnp.stack(kv_p), jnp.stack(kv_s), jnp.stack(win_p), jnp.stack(win_s),
            jnp.stack(gla_p), jnp.stack(gla_s), jnp.stack(pool_p), jnp.stack(pool_s))
```

```python
import functools
import math

import jax
import jax.numpy as jnp
from jax import lax
from jax.experimental import pallas as pl
from jax.experimental.pallas import tpu as pltpu

F32 = jnp.float32
BF16 = jnp.bfloat16

EPS = 1e-6
LANES = 128
HEAD_DIM = 64
PAGE_SIZE = 128
POOL_WINDOWS = (2, 4, 8, 16)
POOL_BUF = 15
POOL_WIDTH = 256
NSA_WIDTH = 512
NSA_HEADS = 8
NSA_KV = 2
NSA_HPG = 4
L_CMP = 32
L_SLC = 64
N_SEL = 16
WINDOW = 512
GLA_HEADS = 4
GLA_DK = 32
GLA_DV = 64
GLA_WIDTH = 256
GLA_RANK = 16
GLA_TAU = 16.0
GLA_SUB = 16
TOP_K = 2
N_IN_PAD = 2432
MASKED = -1e30
VMEM_LIMIT = 56 * 1024 * 1024


def _params(*sem):
    return pltpu.CompilerParams(dimension_semantics=sem, vmem_limit_bytes=VMEM_LIMIT)


def _tile(n, target):
    best = None
    for t in range(8, min(n, target) + 1, 8):
        if n % t == 0:
            best = t
    assert best is not None, (n, target)
    return best


def _dot(a, b):
    return jnp.dot(a, b, preferred_element_type=F32)


def _dot_nt(a, b):
    return lax.dot_general(a, b, (((1,), (1,)), ((), ())), preferred_element_type=F32)


def _dot_tn(a, b):
    return lax.dot_general(a, b, (((0,), (0,)), ((), ())), preferred_element_type=F32)


def _split2_dot(a, ones):
    hi = a.astype(BF16)
    lo = (a - hi.astype(F32)).astype(BF16)
    return _dot(hi, ones) + _dot(lo, ones)


def _split3_dot(ones, a):
    a1 = a.astype(BF16)
    r1 = a - a1.astype(F32)
    a2 = r1.astype(BF16)
    a3 = (r1 - a2.astype(F32)).astype(BF16)
    return _dot(ones, a1) + _dot(ones, a2) + _dot(ones, a3)


def _head_rmsnorm(a, gain_row, seg_ones):
    ms = _split2_dot(a * a, seg_ones) * (1.0 / HEAD_DIM)
    return a * lax.rsqrt(ms + EPS) * gain_row


def _masked_softmax(s, mask):
    sm = jnp.where(mask, s, MASKED)
    m = jnp.max(sm, axis=-1, keepdims=True)
    e = jnp.where(mask, jnp.exp(sm - m), 0.0)
    den = jnp.sum(e, axis=-1, keepdims=True)
    return e * (1.0 / jnp.where(den > 0.0, den, 1.0))


def _seg_ones():
    i = jnp.arange(LANES)
    return (i[:, None] // HEAD_DIM == i[None, :] // HEAD_DIM).astype(BF16)


def _inproj_kernel(x_ref, g_ref, w_ref, qg_ref, kg_ref, seg_ref,
                   pool_o, q_o, rows_o, win_o, misc_o, gla_o):
    x = x_ref[...]
    ms = jnp.mean(x * x, axis=-1, keepdims=True)
    xn = (x * lax.rsqrt(ms + EPS) * g_ref[...]).astype(BF16)
    seg = seg_ref[...]

    def mm(c0, c1):
        return _dot(xn, w_ref[:, c0:c1])

    pool_o[...] = mm(0, 256)
    zq = mm(256, 768)
    for c in range(4):
        q_o[:, LANES * c:LANES * (c + 1)] = (
            _head_rmsnorm(zq[:, LANES * c:LANES * (c + 1)], qg_ref[...], seg) * (HEAD_DIM ** -0.5)).astype(BF16)
    zkv = mm(768, 1536)
    rows_o[:, 0:256] = zkv[:, 0:256]
    rows_o[:, 256:384] = _head_rmsnorm(zkv[:, 256:384], kg_ref[1:2, :], seg)
    rows_o[:, 384:512] = zkv[:, 384:512]
    win_o[:, 0:128] = _head_rmsnorm(zkv[:, 512:640], kg_ref[2:3, :], seg)
    win_o[:, 128:256] = zkv[:, 640:768]
    zg = mm(1536, 2432)
    misc_o[...] = zg[:, 0:128]
    gla_o[...] = zg[:, 128:896]


def _pad_w_in(w):
    d = w.shape[0]
    return jnp.concatenate([
        w[:, 0:1560], w[:, 2072:2088], jnp.zeros((d, 88), w.dtype),
        w[:, 1560:2072], w[:, 2088:2344]], axis=1)


def _inproj(x, g_mix, w_in_l, q_gain, k_gain):
    n, d = x.shape
    tm = _tile(n, 384)
    w = _pad_w_in(w_in_l).astype(BF16)
    qg = jnp.tile(q_gain, 2)[None, :]
    kg = jnp.zeros((8, LANES), F32).at[0:3].set(jnp.tile(k_gain, (1, 2)))
    full = lambda shape: pl.BlockSpec(shape, lambda i: (0, 0))
    row = lambda c: pl.BlockSpec((tm, c), lambda i: (i, 0))
    return pl.pallas_call(
        _inproj_kernel,
        grid=(n // tm,),
        in_specs=[row(d), full((1, d)), full((d, N_IN_PAD)), full((1, LANES)),
                  full((8, LANES)), full((LANES, LANES))],
        out_specs=[row(256), row(512), row(512), row(256), row(128), row(768)],
        out_shape=[jax.ShapeDtypeStruct((n, 256), F32), jax.ShapeDtypeStruct((n, 512), BF16),
                   jax.ShapeDtypeStruct((n, 512), F32), jax.ShapeDtypeStruct((n, 256), F32),
                   jax.ShapeDtypeStruct((n, 128), F32), jax.ShapeDtypeStruct((n, 768), F32)],
        compiler_params=_params("arbitrary"),
        name="inproj",
    )(x, g_mix[None, :], w, qg, kg, _seg_ones())


def _pool_kernel(buf_ref, u_ref, w_ref, sc_ref, o_ref, ext, *, pos0, tp):
    i = pl.program_id(1)

    @pl.when(i == 0)
    def _():
        ext[0:16, :] = buf_ref[0]

    ext[16:16 + tp, :] = u_ref[0]
    u0 = ext[16:16 + tp, :]
    acc = u0
    sums = {}
    for k in range(1, 16):
        acc = acc + ext[16 - k:16 - k + tp, :]
        if k + 1 in POOL_WINDOWS:
            sums[k + 1] = acc
    lane = lax.broadcasted_iota(jnp.int32, (tp, POOL_WIDTH), 1)
    pos = pos0 + i * tp + lax.broadcasted_iota(jnp.int32, (tp, POOL_WIDTH), 0)
    grp = lane // (POOL_WIDTH // len(POOL_WINDOWS))
    total = sums[16]
    wsize = jnp.full((tp, POOL_WIDTH), 16, jnp.int32)
    for gi, wz in enumerate(POOL_WINDOWS[:-1]):
        total = jnp.where(grp == gi, sums[wz], total)
        wsize = jnp.where(grp == gi, wz, wsize)
    cnt = jnp.minimum(wsize, pos + 1).astype(F32)
    dlt = total / cnt - u0
    o_ref[0] = (_dot(dlt.astype(BF16), w_ref[...]) * sc_ref[...]).astype(o_ref.dtype)
    if tp >= 16:
        ext[0:16, :] = ext[tp:tp + 16, :]


def _pool(u, buf, pool_w_l, pool_scale_l, pos0):
    b, t, c = u.shape
    tp = _tile(t, 512) if t >= 8 else t
    buf16 = jnp.concatenate([jnp.zeros((b, 1, c), F32), buf.astype(F32)], axis=1)
    gw = c // len(POOL_WINDOWS)
    wbd = jnp.zeros((c, c), F32)
    for gi in range(len(POOL_WINDOWS)):
        wbd = wbd.at[gi * gw:(gi + 1) * gw, gi * gw:(gi + 1) * gw].set(pool_w_l[gi])
    return pl.pallas_call(
        functools.partial(_pool_kernel, pos0=pos0, tp=tp),
        grid=(b, t // tp),
        in_specs=[pl.BlockSpec((1, 16, c), lambda i, j: (i, 0, 0)),
                  pl.BlockSpec((1, tp, c), lambda i, j: (i, j, 0)),
                  pl.BlockSpec((c, c), lambda i, j: (0, 0)),
                  pl.BlockSpec((1, c), lambda i, j: (0, 0))],
        out_specs=pl.BlockSpec((1, tp, c), lambda i, j: (i, j, 0)),
        out_shape=jax.ShapeDtypeStruct((b, t, c), BF16),
        scratch_shapes=[pltpu.VMEM((16 + tp, c), F32)],
        compiler_params=_params("arbitrary", "arbitrary"),
        name="pool",
    )(buf16, u, wbd.astype(BF16), pool_scale_l[None, :])


def _gla_kernel(gla_ref, misc_ref, s0_ref, wa_ref, ba_ref, og_ref, seg_ref, eb_ref, mk_ref,
                o_ref, sT_ref, st, qs, ks, bs, qts, kts, vs, os_, dls, us, ss, *, tg, t_valid):
    i = pl.program_id(1)
    c = GLA_SUB
    nsub = tg // c

    @pl.when(i == 0)
    def _():
        st[...] = s0_ref[0]

    gl = gla_ref[...]
    q = gl[:, 0:128] * (GLA_DK ** -0.5)
    k = gl[:, 128:256]
    v = gl[:, 256:512]
    r = gl[:, 512:768]
    x = _dot(misc_ref[...].astype(BF16), wa_ref[...]) + ba_ref[...]
    la = (jnp.minimum(x, 0.0) - jnp.log1p(jnp.exp(-jnp.abs(x)))) * (1.0 / GLA_TAU)
    row = lax.broadcasted_iota(jnp.int32, (tg, LANES), 0)
    if t_valid is not None:
        la = jnp.where(i * tg + row < t_valid, la, 0.0)
    rr = lax.broadcasted_iota(jnp.int32, (tg, tg), 0)
    cc = lax.broadcasted_iota(jnp.int32, (tg, tg), 1)
    same = (rr // c) == (cc // c)
    tri = (same & (cc <= rr)).astype(BF16)
    allo = same.astype(BF16)
    b = _split3_dot(tri, la)
    blast = _split3_dot(allo, la)
    qs[...] = q
    ks[...] = k
    bs[...] = b
    qts[...] = (q * jnp.exp(b)).astype(BF16)
    kts[...] = (k * jnp.exp(blast - b)).astype(BF16)
    vs[...] = v
    eb = eb_ref[...]
    mk = mk_ref[...]
    tt = lax.broadcasted_iota(jnp.int32, (c, LANES), 0)

    dls[...] = jnp.exp(blast)

    def local(j, carry):
        r0 = pl.multiple_of(j * c, c)
        qi = qs[pl.ds(r0, c), :]
        ki = ks[pl.ds(r0, c), :]
        bi = bs[pl.ds(r0, c), :]
        vi = vs[pl.ds(r0, c), :]
        parts = []
        for s in range(c):
            dec = jnp.exp(jnp.minimum(bi - bi[s:s + 1, :], 0.0))
            parts.append(jnp.where(tt >= s, qi * ki[s:s + 1, :] * dec, 0.0))
        p_all = jnp.concatenate(parts, axis=0).astype(BF16)
        a_all = _dot(p_all, eb)
        o_diag = a_all[0:c, :] * vi[0:1, :]
        for s in range(1, c):
            o_diag = o_diag + a_all[s * c:(s + 1) * c, :] * vi[s:s + 1, :]
        os_[pl.ds(r0, c), :] = o_diag
        us[j] = _dot_tn(vi.astype(BF16), kts[pl.ds(r0, c), :]) * mk
        return carry

    lax.fori_loop(0, nsub, local, 0)

    def recur(j, carry):
        s_t = st[...]
        ss[j] = s_t.astype(BF16)
        st[...] = s_t * dls[pl.ds(pl.multiple_of(j * c, c), 1), :] + us[j]
        return carry

    lax.fori_loop(0, nsub, recur, 0)

    def inter(j, carry):
        r0 = pl.multiple_of(j * c, c)
        os_[pl.ds(r0, c), :] += _dot_nt(qts[pl.ds(r0, c), :], ss[j])
        return carry

    lax.fori_loop(0, nsub, inter, 0)

    o = os_[...]
    seg = seg_ref[...]
    og = og_ref[...]
    sil = r * (1.0 / (1.0 + jnp.exp(-r)))
    for h in range(2):
        sl = slice(h * LANES, (h + 1) * LANES)
        o_ref[:, sl] = (_head_rmsnorm(o[:, sl], og[:, sl], seg) * sil[:, sl]).astype(o_ref.dtype)
    sT_ref[0] = st[...]


def _gla(gla_rows, misc_rows, s0, wa2, ba, o_gain, b, t):
    t_valid = None
    if t % GLA_SUB:
        t_valid = t
        tp = -(-t // GLA_SUB) * GLA_SUB
        pad = lambda z: jnp.pad(z.reshape(b, t, -1), ((0, 0), (0, tp - t), (0, 0))).reshape(b * tp, -1)
        gla_rows, misc_rows = pad(gla_rows), pad(misc_rows)
    else:
        tp = t
    tg = _tile(tp, 256)
    assert tg % GLA_SUB == 0
    nt = tp // tg
    kk = GLA_HEADS * GLA_DK
    vv = GLA_HEADS * GLA_DV
    ki = jnp.arange(kk)
    vi = jnp.arange(vv)
    head_eq = (vi[:, None] // GLA_DV == ki[None, :] // GLA_DK)
    mk = head_eq.astype(F32)
    eb = head_eq.T.astype(BF16)
    s0t = jnp.einsum('bhkv,hg->bhvgk', s0.astype(F32), jnp.eye(GLA_HEADS, dtype=F32)).reshape(b, vv, kk)
    wa = jnp.zeros((LANES, kk), F32).at[24:24 + GLA_RANK].set(wa2).astype(BF16)
    full = lambda shape: pl.BlockSpec(shape, lambda i, j: (0,) * len(shape))
    rows = lambda cdim: pl.BlockSpec((tg, cdim), lambda i, j: (i * nt + j, 0))
    o, s_t = pl.pallas_call(
        functools.partial(_gla_kernel, tg=tg, t_valid=t_valid),
        grid=(b, nt),
        in_specs=[rows(768), rows(128), pl.BlockSpec((1, vv, kk), lambda i, j: (i, 0, 0)),
                  full((LANES, kk)), full((1, kk)), full((1, vv)), full((LANES, LANES)),
                  full((kk, vv)), full((vv, kk))],
        out_specs=[rows(vv), pl.BlockSpec((1, vv, kk), lambda i, j: (i, 0, 0))],
        out_shape=[jax.ShapeDtypeStruct((b * tp, vv), BF16), jax.ShapeDtypeStruct((b, vv, kk), F32)],
        scratch_shapes=[pltpu.VMEM((vv, kk), F32), pltpu.VMEM((tg, kk), F32), pltpu.VMEM((tg, kk), F32),
                        pltpu.VMEM((tg, kk), F32), pltpu.VMEM((tg, kk), BF16), pltpu.VMEM((tg, kk), BF16),
                        pltpu.VMEM((tg, vv), F32), pltpu.VMEM((tg, vv), F32), pltpu.VMEM((tg, kk), F32),
                        pltpu.VMEM((tg // GLA_SUB, vv, kk), F32), pltpu.VMEM((tg // GLA_SUB, vv, kk), BF16)],
        compiler_params=_params("arbitrary", "arbitrary"),
        name="gla",
    )(gla_rows, misc_rows, s0t, wa, ba[None, :], jnp.tile(o_gain, GLA_HEADS)[None, :], _seg_ones(), eb, mk)
    if tp != t:
        o = o.reshape(b, tp, vv)[:, :t].reshape(b * t, vv)
    s5 = s_t.reshape(b, GLA_HEADS, GLA_DV, GLA_HEADS, GLA_DK)
    s_new = jnp.einsum('bhvgk,hg->bhkv', s5, jnp.eye(GLA_HEADS, dtype=F32))
    return o, s_new


def _compress_rows(read, nb, pe_ref, wk_ref, wv_ref, kg_ref, seg_ref):
    acck = jnp.zeros((nb, LANES), F32)
    accv = jnp.zeros((nb, LANES), F32)
    for j in range(L_CMP):
        xk = (read(0, j) + pe_ref[0, j:j + 1, :]).astype(BF16)
        xv = (read(1, j) + pe_ref[1, j:j + 1, :]).astype(BF16)
        acck = acck + _dot(xk, wk_ref[j])
        accv = accv + _dot(xv, wv_ref[j])
    kc = _head_rmsnorm(acck, kg_ref[0:1, :], seg_ref[...])
    return kc, accv


def _compress_kernel(rk_ref, rv_ref, pe_ref, wk_ref, wv_ref, kg_ref, seg_ref, o_ref, *, nb):
    refs = (rk_ref, rv_ref)
    kc, vc = _compress_rows(lambda kind, j: refs[kind][pl.ds(j, nb, stride=L_CMP), :], nb,
                            pe_ref, wk_ref, wv_ref, kg_ref, seg_ref)
    o_ref[0, :, 0:LANES] = kc
    o_ref[0, :, LANES:2 * LANES] = vc


def _cmp_weights(cmp_w, cmp_pe, k_gain):
    def bd(w):
        z = jnp.zeros_like(w)
        return jnp.concatenate([jnp.concatenate([w, z], axis=2), jnp.concatenate([z, w], axis=2)], axis=1)
    pe = jnp.tile(cmp_pe, (1, 1, 2))
    kg = jnp.zeros((8, LANES), F32).at[0].set(jnp.tile(k_gain[0], 2))
    return pe, bd(cmp_w[0]).astype(BF16), bd(cmp_w[1]).astype(BF16), kg


def _compress(rk, rv, k_col, v_col, b, tp, cw):
    nb = tp // L_CMP
    pe, wk, wv, kg = cw
    full = lambda shape: pl.BlockSpec(shape, lambda i: (0,) * len(shape))
    return pl.pallas_call(
        functools.partial(_compress_kernel, nb=nb),
        grid=(b,),
        in_specs=[pl.BlockSpec((tp, LANES), lambda i: (i, k_col)), pl.BlockSpec((tp, LANES), lambda i: (i, v_col)),
                  full((2, L_CMP, LANES)), full((L_CMP, LANES, LANES)), full((L_CMP, LANES, LANES)),
                  full((8, LANES)), full((LANES, LANES))],
        out_specs=pl.BlockSpec((1, nb, 256), lambda i: (i, 0, 0)),
        out_shape=jax.ShapeDtypeStruct((b, nb, 256), F32),
        compiler_params=_params("arbitrary"),
        name="compress",
    )(rk, rv, pe, wk, wv, kg, _seg_ones())


SLAB_PITCH = 2 * NSA_KV * HEAD_DIM + 8


def _paged_compress_kernel(pt_ref, cache_ref, pe_ref, m_ref, kg_ref, seg_ref, o_ref,
                           slab, sem, *, pages, n_slab, layer):
    bi = pl.program_id(0)
    si = pl.program_id(1)
    step = bi * n_slab + si
    nsteps = pl.num_programs(0) * n_slab
    slot = step % 2
    rows_cmp = 2 * NSA_KV * HEAD_DIM

    def copy(page, slot_, p):
        return pltpu.make_async_copy(cache_ref.at[layer, page, pl.ds(0, rows_cmp), :],
                                     slab.at[slot_, pl.ds(p * SLAB_PITCH, rows_cmp), :], sem.at[slot_])

    def issue(bb, ss, slot_):
        for p in range(pages):
            copy(pt_ref[bb, ss * pages + p], slot_, p).start()

    @pl.when(step == 0)
    def _():
        issue(0, 0, 0)

    @pl.when(step + 1 < nsteps)
    def _():
        nxt = step + 1
        issue(nxt // n_slab, nxt % n_slab, 1 - slot)

    for p in range(pages):
        copy(0, slot, p).wait()

    def rows(r0):
        return slab[slot, pl.ds(r0, pages, stride=SLAB_PITCH), :]

    for c in range(2):
        acc = jnp.zeros((NSA_KV * pages, 2 * LANES), F32)
        for dp in range(HEAD_DIM // 2):
            parts = []
            for g in range(NSA_KV):
                r0 = (c * NSA_KV + g) * HEAD_DIM + 2 * dp
                parts.append(jnp.concatenate([rows(r0), rows(r0 + 1)], axis=1))
            a = jnp.concatenate(parts, axis=0) + pe_ref[c, dp:dp + 1, :]
            acc = acc + _dot(a.astype(BF16), m_ref[c, dp])
        for g in range(NSA_KV):
            blk = acc[g * pages:(g + 1) * pages, :]
            if c == 0:
                for h in range(2):
                    sl = slice(h * LANES, (h + 1) * LANES)
                    o_ref[0, c, g, :, sl] = _head_rmsnorm(blk[:, sl], kg_ref[0:1, :], seg_ref[...])
            else:
                o_ref[0, c, g] = blk


def _paged_cmp_weights(cmp_w, cmp_pe):
    nblk = PAGE_SIZE // L_CMP
    k6 = jnp.einsum('nm,cjde->cdnjme', jnp.eye(nblk, dtype=F32), cmp_w)
    m = k6.reshape(2, HEAD_DIM // 2, 2 * PAGE_SIZE, nblk * HEAD_DIM).astype(BF16)
    pe = jnp.tile(cmp_pe.transpose(0, 2, 1), (1, 1, nblk)).reshape(2, HEAD_DIM // 2, 2 * PAGE_SIZE)
    return pe, m


def _paged_compress(cache_v, page_table, layer, cw, pw):
    b, n_pages = page_table.shape
    pages = math.gcd(n_pages, 64)
    n_slab = n_pages // pages
    pe, m = pw
    kg = cw[3]
    nblk = PAGE_SIZE // L_CMP
    full = lambda shape: pl.BlockSpec(shape, lambda i, j, pt: (0,) * len(shape))
    grid_spec = pltpu.PrefetchScalarGridSpec(
        num_scalar_prefetch=1,
        grid=(b, n_slab),
        in_specs=[pl.BlockSpec(memory_space=pl.ANY), full(pe.shape), full(m.shape), full((8, LANES)),
                  full((LANES, LANES))],
        out_specs=pl.BlockSpec((1, 2, NSA_KV, pages, nblk * HEAD_DIM), lambda i, j, pt: (i, 0, 0, j, 0)),
        scratch_shapes=[pltpu.VMEM((2, pages * SLAB_PITCH, LANES), F32), pltpu.SemaphoreType.DMA((2,))],
    )
    return pl.pallas_call(
        functools.partial(_paged_compress_kernel, pages=pages, n_slab=n_slab, layer=layer),
        grid_spec=grid_spec,
        out_shape=jax.ShapeDtypeStruct((b, 2, NSA_KV, n_pages, nblk * HEAD_DIM), F32),
        compiler_params=_params("arbitrary", "arbitrary"),
        name="paged_compress",
    )(page_table, cache_v, pe, m, kg, _seg_ones())


def _nsa_prompt_kernel(q_ref, kc_ref, vc_ref, ks_ref, vs_ref, kw_ref, vw_ref, gate_ref,
                       o_ref, ksb, vsb, kwb, vwb, *, tq, ck):
    i = pl.program_id(1)
    t0 = i * tq

    n_slc = LANES // 2
    t_keys = ks_ref.shape[0]

    @pl.when(i == 0)
    def _():
        ksb[:, 0:LANES] = ks_ref[...].astype(BF16)
        kblk = lax.broadcasted_iota(jnp.int32, (t_keys, LANES), 0) // L_SLC
        klane = lax.broadcasted_iota(jnp.int32, (t_keys, LANES), 1)
        ksb[:, LANES:2 * LANES] = jnp.where(kblk == klane, 1.0, 0.0).astype(BF16)
        vsb[...] = vs_ref[...].astype(BF16)
        kwb[...] = kw_ref[...].astype(BF16)
        vwb[...] = vw_ref[...].astype(BF16)

    row_t = t0 + lax.broadcasted_iota(jnp.int32, (tq, 1), 0)
    col_t = t0 + lax.broadcasted_iota(jnp.int32, (1, tq), 1)
    lane = lax.broadcasted_iota(jnp.int32, (1, LANES), 1)
    crow = lax.broadcasted_iota(jnp.int32, (LANES, 1), 0)
    nat = jnp.where(crow < n_slc, 2 * crow, 2 * (crow - n_slc) + 1)
    cmp_ok = ((nat + 1) * L_CMP - 1) <= col_t
    blk = lax.broadcasted_iota(jnp.int32, (n_slc, 1), 0)
    cur = col_t // L_SLC
    done = blk < cur
    gx = gate_ref[...]
    gates = 1.0 / (1.0 + jnp.exp(-gx))
    n_full = t0 // ck
    kstart = pl.multiple_of(jnp.maximum(t0 - WINDOW, 0), tq)
    wlen = WINDOW + tq
    wpos = kstart + lax.broadcasted_iota(jnp.int32, (1, wlen), 1)
    win_bias = jnp.where((wpos <= row_t) & (wpos >= row_t - WINDOW), 0.0, MASKED)

    for g in range(NSA_KV):
        in_g = (lane // HEAD_DIM) == g
        heads = []
        for j in range(NSA_HPG):
            h = g * NSA_HPG + j
            qh = q_ref[:, (h // 2) * LANES:(h // 2 + 1) * LANES].astype(F32)
            if h % 2 != g:
                qh = pltpu.roll(qh, HEAD_DIM, axis=1)
            heads.append(jnp.where(in_g, qh, 0.0).astype(BF16))
        qg = jnp.concatenate(heads, axis=0)
        s_c = _dot_nt(kc_ref[0], qg)
        imp = jnp.zeros((LANES, tq), F32)
        o_c = []
        for j in range(NSA_HPG):
            sj = jnp.where(cmp_ok, s_c[:, j * tq:(j + 1) * tq], MASKED)
            ej = jnp.where(cmp_ok, jnp.exp(sj - jnp.max(sj, axis=0, keepdims=True)), 0.0)
            den = jnp.sum(ej, axis=0, keepdims=True)
            pj = ej * (1.0 / jnp.where(den > 0.0, den, 1.0))
            imp = imp + pj
            o_c.append(_dot_tn(pj.astype(BF16), vc_ref[0]))
        imp = imp[0:n_slc] + imp[n_slc:LANES]
        key = jnp.where(done, lax.bitcast_convert_type(imp, jnp.int32), -1)
        key_m1 = key - 1
        rank = jnp.zeros((n_slc, tq), jnp.int32)
        for r in range(1, n_slc):
            vm = pltpu.roll(key, r, axis=0)
            rank = rank + jnp.where(vm > jnp.where(blk >= r, key_m1, key), 1, 0)
        sel = (done & (rank < N_SEL - 1)) | (blk == cur)
        sel_bias = jnp.concatenate([jnp.where(sel, 0.0, MASKED), jnp.zeros((n_slc, tq), F32)], axis=0)
        sel_bias = sel_bias.T.astype(BF16)
        qx = jnp.concatenate([qg, jnp.concatenate([sel_bias] * NSA_HPG, axis=0)], axis=1)

        def chunk(c, carry, diagonal):
            m, l, acc = carry
            k0 = pl.multiple_of(c * ck, ck)
            sm = _dot_nt(qx, ksb[pl.ds(k0, ck), :]).reshape(NSA_HPG, tq, ck)
            if diagonal:
                kpos = k0 + lax.broadcasted_iota(jnp.int32, (1, ck), 1)
                sm = sm + jnp.where(kpos <= row_t, 0.0, MASKED)[None]
            m_new = jnp.maximum(m, jnp.max(sm, axis=-1, keepdims=True))
            p = jnp.exp(sm - m_new)
            alpha = jnp.exp(m - m_new)
            l = alpha * l + jnp.sum(p, axis=-1, keepdims=True)
            pv = _dot(p.reshape(NSA_HPG * tq, ck).astype(BF16), vsb[pl.ds(k0, ck), :])
            acc = alpha * acc + pv.reshape(NSA_HPG, tq, LANES)
            return m_new, l, acc

        m0 = jnp.full((NSA_HPG, tq, 1), MASKED, F32)
        l0 = jnp.zeros((NSA_HPG, tq, 1), F32)
        a0 = jnp.zeros((NSA_HPG, tq, LANES), F32)
        carry = lax.fori_loop(0, n_full, functools.partial(chunk, diagonal=False), (m0, l0, a0))
        _, l_s, acc_s = chunk(n_full, carry, True)
        o_s = acc_s * (1.0 / jnp.where(l_s > 0.0, l_s, 1.0))
        s_w = _dot_nt(qg, kwb[pl.ds(kstart, wlen), :]).reshape(NSA_HPG, tq, wlen) + win_bias[None]
        e_w = jnp.exp(s_w - jnp.max(s_w, axis=-1, keepdims=True))
        p_w = e_w * (1.0 / jnp.sum(e_w, axis=-1, keepdims=True))
        o_w = _dot(p_w.reshape(NSA_HPG * tq, wlen).astype(BF16), vwb[pl.ds(kstart, wlen), :])
        o_w = o_w.reshape(NSA_HPG, tq, LANES)
        for pr in range(NSA_HPG // 2):
            pair = []
            for jj in range(2):
                j = 2 * pr + jj
                h = g * NSA_HPG + j
                o = (gates[:, 3 * h:3 * h + 1] * o_c[j] + gates[:, 3 * h + 1:3 * h + 2] * o_s[j]
                     + gates[:, 3 * h + 2:3 * h + 3] * o_w[j])
                pair.append(o if jj == g else pltpu.roll(o, HEAD_DIM, axis=1))
            col = (g * (NSA_HPG // 2) + pr) * LANES
            o_ref[:, col:col + LANES] = jnp.where(lane < HEAD_DIM, pair[0], pair[1]).astype(o_ref.dtype)


def _nsa_prompt(q_rows, rows, win, misc, cw, b, t):
    assert t // L_CMP == LANES and t % L_SLC == 0, "prompt kernel is laid out for 128 compressed blocks"
    tq = 256
    ck = 512
    assert ck % tq == 0 and t % ck == 0
    cmp = _compress(rows, rows, 0, 1, b, t, cw)
    order = jnp.concatenate([jnp.arange(0, LANES, 2), jnp.arange(1, LANES, 2)])
    cmp = cmp[:, order].astype(BF16)
    kc, vc = cmp[:, :, 0:LANES], cmp[:, :, LANES:2 * LANES]
    nt = t // tq
    col = lambda c: pl.BlockSpec((t, LANES), lambda i, j: (i, c))
    return pl.pallas_call(
        functools.partial(_nsa_prompt_kernel, tq=tq, ck=ck),
        grid=(b, nt),
        in_specs=[pl.BlockSpec((tq, NSA_WIDTH), lambda i, j: (i * nt + j, 0)),
                  pl.BlockSpec((1, LANES, LANES), lambda i, j: (i, 0, 0)),
                  pl.BlockSpec((1, LANES, LANES), lambda i, j: (i, 0, 0)),
                  col(2), col(3), col(0), col(1),
                  pl.BlockSpec((tq, LANES), lambda i, j: (i * nt + j, 0))],
        out_specs=pl.BlockSpec((tq, NSA_WIDTH), lambda i, j: (i * nt + j, 0)),
        out_shape=jax.ShapeDtypeStruct((b * t, NSA_WIDTH), BF16),
        scratch_shapes=[pltpu.VMEM((t, 2 * LANES), BF16)] + [pltpu.VMEM((t, LANES), BF16)] * 3,
        compiler_params=_params("arbitrary", "arbitrary"),
        name="nsa_prompt",
    )(q_rows, kc, vc, rows, rows, win, win, misc)


def _split3_dot_r(a, ones):
    a1 = a.astype(BF16)
    r1 = a - a1.astype(F32)
    a2 = r1.astype(BF16)
    a3 = (r1 - a2.astype(F32)).astype(BF16)
    return _dot(a1, ones) + _dot(a2, ones) + _dot(a3, ones)


def _nsa_select_kernel(q_ref, cp_ref, ct_ref, pair_ref, oc_ref, sel_ref, *, t, past_len, n_past):
    nrow = NSA_HPG * t
    npad = n_past + LANES
    n_slc_pad = pair_ref.shape[1]
    row = lax.broadcasted_iota(jnp.int32, (nrow, 1), 0)
    pos = past_len + row % t
    lane_c = lax.broadcasted_iota(jnp.int32, (1, npad), 1)
    cmp_ok = ((lane_c + 1) * L_CMP - 1) <= pos
    cur = (past_len + lax.broadcasted_iota(jnp.int32, (t, 1), 0)) // L_SLC
    lane_s = lax.broadcasted_iota(jnp.int32, (1, n_slc_pad), 1)
    lane_o = lax.broadcasted_iota(jnp.int32, (1, LANES), 1)
    done = lane_s < cur
    keys = []
    for g in range(NSA_KV):
        q = q_ref[0, g]
        s = jnp.concatenate([_dot_nt(q, cp_ref[0, 0, g].astype(BF16)),
                             _dot_nt(q, ct_ref[0, 0, g].astype(BF16))], axis=1)
        p = _masked_softmax(s, cmp_ok)
        pb = p.astype(BF16)
        oc_ref[0, g] = (_dot(pb[:, :n_past], cp_ref[0, 1, g].astype(BF16))
                        + _dot(pb[:, n_past:], ct_ref[0, 1, g].astype(BF16)))
        imp_c = p[0:t]
        for j in range(1, NSA_HPG):
            imp_c = imp_c + p[j * t:(j + 1) * t]
        imp = _split3_dot_r(imp_c, pair_ref[...])
        keys.append(jnp.where(done, lax.bitcast_convert_type(imp, jnp.int32), -1))
    key = jnp.concatenate(keys, axis=0)
    res = jnp.zeros((NSA_KV * t, LANES), jnp.int32)
    for r in range(N_SEL - 1):
        best = jnp.max(key, axis=1, keepdims=True)
        idx = jnp.min(jnp.where(key == best, lane_s, n_slc_pad), axis=1, keepdims=True)
        res = jnp.where(lane_o == r, idx, res)
        key = jnp.where(lane_s == idx, -2, key)
    sel_ref[0] = jnp.where(lane_o == N_SEL - 1, jnp.concatenate([cur] * NSA_KV, axis=0), res)


def _nsa_attend_kernel(pt_ref, sel_ref, cache_ref, q_ref, qp_ref, oc_ref, gate_ref, snew_ref, wnew_ref,
                       wst_ref, o_ref, buf, sc, sem, *, t, past_len, layer, w_buf):
    bi = pl.program_id(0)
    slot = bi % 2
    nsel = N_SEL - 1
    n_past_blk = past_len // L_SLC
    per_page = PAGE_SIZE // L_SLC
    nrow = t * NSA_HPG

    def block_of(bb, g, tok, k):
        return jnp.minimum(sel_ref[bb, (g * t + tok) * N_SEL + k], n_past_blk - 1)

    def copy(page, slot_, g, idx):
        return pltpu.make_async_copy(cache_ref.at[layer, page, pl.ds(2, 2), g], buf.at[slot_, idx], sem.at[slot_])

    def issue(bb, slot_):
        for g in range(NSA_KV):
            for tok in range(t):
                for k in range(nsel):
                    page = pt_ref[bb, block_of(bb, g, tok, k) // per_page]
                    copy(page, slot_, g, (g * t + tok) * nsel + k).start()

    @pl.when(bi == 0)
    def _():
        issue(0, 0)

    @pl.when(bi + 1 < pl.num_programs(0))
    def _():
        issue(bi + 1, 1 - slot)

    for g in range(NSA_KV):
        for i in range(t * nsel):
            copy(0, slot, g, g * t * nsel + i).wait()

    row = lax.broadcasted_iota(jnp.int32, (nrow, 1), 0)
    tok_r = row // NSA_HPG
    pos = past_len + tok_r
    lane = lax.broadcasted_iota(jnp.int32, (1, LANES), 1)
    new_bias = jnp.where(lane <= tok_r, 0.0, MASKED)
    wpos = past_len - w_buf + lax.broadcasted_iota(jnp.int32, (1, w_buf), 1)
    win_bias = jnp.where((wpos >= 0) & (wpos <= pos) & (wpos >= pos - WINDOW), 0.0, MASKED)
    for g in range(NSA_KV):
        q = q_ref[0, g]
        qp = qp_ref[0, g]
        glanes = slice(g * HEAD_DIM, (g + 1) * HEAD_DIM)
        m = jnp.full((nrow, 1), MASKED, F32)
        for tok in range(t):
            cur = (past_len + tok) // L_SLC
            for k in range(nsel):
                n = sel_ref[bi, (g * t + tok) * N_SEL + k]
                half = block_of(bi, g, tok, k) % per_page
                valid = (tok_r == tok) & (lane // L_SLC == half) & (n < cur)
                i = tok * nsel + k
                s = _dot(q, buf[slot, g * t * nsel + i, 0].astype(BF16)) + jnp.where(valid, 0.0, MASKED)
                sc[i] = s
                m = jnp.maximum(m, jnp.max(s, axis=1, keepdims=True))
        s_new = _dot_nt(qp, snew_ref[0, :, 0:LANES].astype(BF16)) + new_bias
        m = jnp.maximum(m, jnp.max(s_new, axis=1, keepdims=True))
        p_new = jnp.exp(s_new - m)
        l = jnp.sum(p_new, axis=1, keepdims=True)
        acc = _dot(p_new.astype(BF16), snew_ref[0, :, LANES:2 * LANES].astype(BF16))[:, glanes]
        for i in range(t * nsel):
            p = jnp.exp(sc[i] - m)
            l = l + jnp.sum(p, axis=1, keepdims=True)
            acc = acc + _dot_nt(p.astype(BF16), buf[slot, g * t * nsel + i, 1].astype(BF16))
        o_s = acc * (1.0 / l)
        s_w = _dot(q, wst_ref[0, 0, 0, g].astype(BF16)) + win_bias
        s_wn = _dot_nt(qp, wnew_ref[0, :, 0:LANES].astype(BF16)) + new_bias
        m_w = jnp.maximum(jnp.max(s_w, axis=1, keepdims=True), jnp.max(s_wn, axis=1, keepdims=True))
        e_w = jnp.exp(s_w - m_w)
        e_n = jnp.exp(s_wn - m_w)
        l_w = jnp.sum(e_w, axis=1, keepdims=True) + jnp.sum(e_n, axis=1, keepdims=True)
        acc_w = (_dot_nt(e_w.astype(BF16), wst_ref[0, 0, 1, g].astype(BF16))
                 + _dot(e_n.astype(BF16), wnew_ref[0, :, LANES:2 * LANES].astype(BF16))[:, glanes])
        o_w = acc_w * (1.0 / l_w)
        gate = 1.0 / (1.0 + jnp.exp(-gate_ref[0, g]))
        o_ref[0, g] = gate[:, 0:1] * oc_ref[0, g] + gate[:, 1:2] * o_s + gate[:, 2:3] * o_w


def _nsa_sample(q_rows, rows, win, misc, cw, pw, cache_v, page_table, layer, win_state_v, b, t, past_len):
    t_all = past_len + t
    t_pad = -(-t_all // L_SLC) * L_SLC
    n_cmp = t_pad // L_CMP
    n_tail = (t_pad - past_len) // L_CMP
    n_past = past_len // L_CMP
    w_buf = win_state_v.shape[-1]
    assert past_len % PAGE_SIZE == 0 and t <= L_SLC and n_tail <= LANES and NSA_KV * t * N_SEL <= LANES
    cmp_past = _paged_compress(cache_v, page_table, layer, cw, pw).reshape(b, 2, NSA_KV, n_past, HEAD_DIM)
    tail = jnp.concatenate([rows[:, 0:256].reshape(b, t, 256),
                            jnp.zeros((b, t_pad - t_all, 256), F32)], axis=1).reshape(b * (t_pad - past_len), 256)
    cmp_tail = _compress(tail, tail, 0, 1, 1, b * (t_pad - past_len), cw)
    cmp_tail = cmp_tail.reshape(b, n_tail, 2, NSA_KV, HEAD_DIM).transpose(0, 2, 3, 1, 4)
    cmp_tail = jnp.pad(cmp_tail, ((0, 0), (0, 0), (0, 0), (0, LANES - n_tail), (0, 0)))
    q5 = q_rows.reshape(b, t, NSA_KV, NSA_HPG, HEAD_DIM)
    nrow = NSA_HPG * t
    q_jt = q5.transpose(0, 2, 3, 1, 4).reshape(b, NSA_KV, nrow, HEAD_DIM)
    q_tj = q5.transpose(0, 2, 1, 3, 4).reshape(b, NSA_KV, nrow, HEAD_DIM)
    qp_tj = jnp.zeros((b, NSA_KV, nrow, LANES), BF16)
    for g in range(NSA_KV):
        qp_tj = qp_tj.at[:, g, :, g * HEAD_DIM:(g + 1) * HEAD_DIM].set(q_tj[:, g])
    npad = n_past + LANES
    n_slc_pad = -(-(t_pad // L_SLC) // LANES) * LANES
    nn = jnp.arange(npad)
    pair = ((nn[:, None] // (L_SLC // L_CMP) == jnp.arange(n_slc_pad)[None, :]) & (nn[:, None] < n_cmp)).astype(BF16)
    per_b = lambda shape: pl.BlockSpec(shape, lambda i: (i,) + (0,) * (len(shape) - 1))
    o_c, sel = pl.pallas_call(
        functools.partial(_nsa_select_kernel, t=t, past_len=past_len, n_past=n_past),
        grid=(b,),
        in_specs=[per_b((1, NSA_KV, nrow, HEAD_DIM)), per_b((1, 2, NSA_KV, n_past, HEAD_DIM)),
                  per_b((1, 2, NSA_KV, LANES, HEAD_DIM)), pl.BlockSpec((npad, n_slc_pad), lambda i: (0, 0))],
        out_specs=[per_b((1, NSA_KV, nrow, HEAD_DIM)), per_b((1, NSA_KV * t, LANES))],
        out_shape=[jax.ShapeDtypeStruct((b, NSA_KV, nrow, HEAD_DIM), F32),
                   jax.ShapeDtypeStruct((b, NSA_KV * t, LANES), jnp.int32)],
        compiler_params=_params("arbitrary"),
        name="nsa_select",
    )(q_jt, cmp_past, cmp_tail, pair)
    sel_c = sel[:, :, :N_SEL].reshape(b, NSA_KV * t * N_SEL)
    o_c = o_c.reshape(b, NSA_KV, NSA_HPG, t, HEAD_DIM).transpose(0, 1, 3, 2, 4).reshape(b, NSA_KV, nrow, HEAD_DIM)
    gates = misc[:, 0:3 * NSA_HEADS].reshape(b, t, NSA_KV, NSA_HPG, 3).transpose(0, 2, 1, 3, 4)
    gates = jnp.pad(gates.reshape(b, NSA_KV, nrow, 3), ((0, 0), (0, 0), (0, 0), (0, LANES - 3)))
    pad_rows = lambda z: jnp.pad(z.reshape(b, t, 256), ((0, 0), (0, LANES - t), (0, 0)))
    cache6 = cache_v.reshape(cache_v.shape[0], cache_v.shape[1], 4, NSA_KV, HEAD_DIM, PAGE_SIZE)
    nsel = N_SEL - 1
    pb = lambda shape: pl.BlockSpec(shape, lambda i, pt, sl: (i,) + (0,) * (len(shape) - 1))
    grid_spec = pltpu.PrefetchScalarGridSpec(
        num_scalar_prefetch=2,
        grid=(b,),
        in_specs=[pl.BlockSpec(memory_space=pl.ANY), pb((1, NSA_KV, nrow, HEAD_DIM)), pb((1, NSA_KV, nrow, LANES)),
                  pb((1, NSA_KV, nrow, HEAD_DIM)), pb((1, NSA_KV, nrow, LANES)), pb((1, LANES, 256)),
                  pb((1, LANES, 256)),
                  pl.BlockSpec((1, 1, 2, NSA_KV, HEAD_DIM, w_buf), lambda i, pt, sl: (layer, i, 0, 0, 0, 0))],
        out_specs=pb((1, NSA_KV, nrow, HEAD_DIM)),
        scratch_shapes=[pltpu.VMEM((2, NSA_KV * t * nsel, 2, HEAD_DIM, PAGE_SIZE), F32),
                        pltpu.VMEM((t * nsel, nrow, LANES), F32), pltpu.SemaphoreType.DMA((2,))],
    )
    y = pl.pallas_call(
        functools.partial(_nsa_attend_kernel, t=t, past_len=past_len, layer=layer, w_buf=w_buf),
        grid_spec=grid_spec,
        out_shape=jax.ShapeDtypeStruct((b, NSA_KV, nrow, HEAD_DIM), F32),
        compiler_params=_params("arbitrary"),
        name="nsa_attend",
    )(page_table, sel_c, cache6, q_tj, qp_tj, o_c, gates, pad_rows(rows[:, 256:512]), pad_rows(win), win_state_v)
    y = y.reshape(b, NSA_KV, t, NSA_HPG, HEAD_DIM).transpose(0, 2, 1, 3, 4).reshape(b * t, NSA_WIDTH)
    return y.astype(BF16)


def _outproj_kernel(yp_ref, yn_ref, yg_ref, x_ref, w_ref, g_ref, *rest, n_experts):
    with_router = n_experts > 0
    if with_router:
        r_ref, h_o, hn_o, lg_o = rest
    else:
        h_o, hn_o = rest
    h = (x_ref[...] + _dot(yp_ref[...], w_ref[0:256, :]) + _dot(yn_ref[...], w_ref[256:768, :])
         + _dot(yg_ref[...], w_ref[768:1024, :]))
    h_o[...] = h
    ms = jnp.mean(h * h, axis=-1, keepdims=True)
    hn = (h * lax.rsqrt(ms + EPS) * g_ref[...]).astype(BF16)
    hn_o[...] = hn
    if with_router:
        lane = lax.broadcasted_iota(jnp.int32, (1, LANES), 1)
        lg = jnp.where(lane < n_experts, _dot(hn, r_ref[...]), -jnp.inf)
        v1 = jnp.max(lg, axis=1, keepdims=True)
        i1 = jnp.min(jnp.where(lg == v1, lane, LANES), axis=1, keepdims=True)
        lg2 = jnp.where(lane == i1, -jnp.inf, lg)
        v2 = jnp.max(lg2, axis=1, keepdims=True)
        i2 = jnp.min(jnp.where(lg2 == v2, lane, LANES), axis=1, keepdims=True)
        e2 = jnp.exp(v2 - v1)
        den = 1.0 + e2
        lg_o[...] = jnp.where(lane == 0, 1.0 / den, jnp.where(lane == 1, e2 / den, jnp.where(
            lane == 2, i1.astype(F32), jnp.where(lane == 3, i2.astype(F32), 0.0))))


def _outproj(y_pool, y_nsa, y_gla, x, w_out_l, g_ffn, router):
    n, d = x.shape
    tm = _tile(n, 384)
    with_router = router is not None
    row = lambda c: pl.BlockSpec((tm, c), lambda i: (i, 0))
    full = lambda shape: pl.BlockSpec(shape, lambda i: (0, 0))
    in_specs = [row(256), row(512), row(256), row(d), full((d, d)), full((1, d))]
    args = [y_pool, y_nsa, y_gla, x, w_out_l.astype(BF16), g_ffn[None, :]]
    out_specs = [row(d), row(d)]
    out_shape = [jax.ShapeDtypeStruct((n, d), F32), jax.ShapeDtypeStruct((n, d), BF16)]
    if with_router:
        ne = router.shape[1]
        in_specs.append(full((d, LANES)))
        args.append(jnp.zeros((d, LANES), F32).at[:, :ne].set(router).astype(BF16))
        out_specs.append(row(LANES))
        out_shape.append(jax.ShapeDtypeStruct((n, LANES), F32))
    return pl.pallas_call(
        functools.partial(_outproj_kernel, n_experts=router.shape[1] if with_router else 0),
        grid=(n // tm,), in_specs=in_specs, out_specs=out_specs, out_shape=out_shape,
        compiler_params=_params("arbitrary"), name="outproj",
    )(*args)


def _swiglu_kernel(be_ref, bv_ref, x_ref, gate_ref, w1_ref, w3_ref, w2_ref, *rest, with_res):
    res_ref, o_ref = rest if with_res else (None,) + rest
    i = pl.program_id(0)
    j = pl.program_id(1)
    valid = bv_ref[i] > 0

    @pl.when(jnp.logical_not(valid) & (j == 0))
    def _():
        o_ref[...] = jnp.zeros_like(o_ref)

    @pl.when(valid)
    def _():
        x = x_ref[...]
        a = _dot(x, w1_ref[0])
        c = _dot(x, w3_ref[0])
        hmid = (a * (1.0 / (1.0 + jnp.exp(-a))) * c).astype(BF16)
        y = _dot(hmid, w2_ref[0])

        @pl.when(j == 0)
        def _():
            o_ref[...] = y

        @pl.when(j > 0)
        def _():
            o_ref[...] += y

        @pl.when(j == pl.num_programs(1) - 1)
        def _():
            y_all = o_ref[...] * gate_ref[...]
            o_ref[...] = y_all + res_ref[...] if with_res else y_all


def _swiglu(x, gate, blk_expert, blk_valid, w1, w3, w2, tm, res=None):
    r, d = x.shape
    f = w1.shape[2]
    tf = f // 2 if (f // 2) % LANES == 0 else f
    row = pl.BlockSpec((tm, d), lambda i, j, be, bv: (i, 0))
    in_specs = [row, pl.BlockSpec((tm, 1), lambda i, j, be, bv: (i, 0)),
                pl.BlockSpec((1, d, tf), lambda i, j, be, bv: (be[i], 0, j)),
                pl.BlockSpec((1, d, tf), lambda i, j, be, bv: (be[i], 0, j)),
                pl.BlockSpec((1, tf, d), lambda i, j, be, bv: (be[i], j, 0))]
    args = [blk_expert, blk_valid, x, gate, w1, w3, w2]
    if res is not None:
        in_specs.append(row)
        args.append(res)
    grid_spec = pltpu.PrefetchScalarGridSpec(
        num_scalar_prefetch=2, grid=(r // tm, f // tf), in_specs=in_specs, out_specs=row)
    return pl.pallas_call(
        functools.partial(_swiglu_kernel, with_res=res is not None), grid_spec=grid_spec,
        out_shape=jax.ShapeDtypeStruct((r, d), F32),
        compiler_params=_params("arbitrary", "arbitrary"), name="swiglu",
    )(*args)


def _dense_ffn(h, hn, w1, w3, w2):
    n = h.shape[0]
    tm = _tile(n, 704)
    nblk = n // tm
    return _swiglu(hn, jnp.ones((n, 1), F32), jnp.zeros((nblk,), jnp.int32), jnp.ones((nblk,), jnp.int32),
                   w1[None].astype(BF16), w3[None].astype(BF16), w2[None].astype(BF16), tm, res=h)


def _moe_ffn(h, hn, route, w1, w3, w2):
    n = h.shape[0]
    ne = w1.shape[0]
    tm = 512
    a = n * TOP_K
    e_a = route[:, TOP_K:2 * TOP_K].astype(jnp.int32).reshape(-1)
    g_a = route[:, 0:TOP_K].reshape(-1)
    tok_a = jnp.repeat(jnp.arange(n), TOP_K)
    onehot = (e_a[:, None] == jnp.arange(ne)[None, :]).astype(jnp.int32)
    csum = jnp.cumsum(onehot, axis=0)
    counts = csum[-1]
    padded = (counts + tm - 1) // tm * tm
    p_end = jnp.cumsum(padded)
    p_start = p_end - padded
    dest = jnp.sum(onehot * (csum - onehot + p_start[None, :]), axis=1)
    nblk = -(-a // tm) + ne
    r = nblk * tm
    row_info = jnp.zeros((r, 2), F32).at[dest].set(jnp.stack([tok_a.astype(F32), g_a], axis=1))
    row_tok = row_info[:, 0].astype(jnp.int32)
    row_gate = row_info[:, 1]
    blk0 = jnp.arange(nblk) * tm
    blk_valid = (blk0 < p_end[-1]).astype(jnp.int32)
    last = jnp.clip(jnp.searchsorted(p_end, p_end[-1] - 1, side='right'), 0, ne - 1)
    blk_expert = jnp.clip(jnp.searchsorted(p_end, blk0, side='right'), 0, ne - 1)
    blk_expert = jnp.where(blk_valid > 0, blk_expert, last).astype(jnp.int32)
    yb = _swiglu(hn[row_tok], row_gate[:, None], blk_expert, blk_valid,
                 w1.astype(BF16), w3.astype(BF16), w2.astype(BF16), tm)
    pos = dest.reshape(n, TOP_K)
    return h + (yb[pos[:, 0]] + yb[pos[:, 1]])


def kernel(x_prompt, x_sample, cache_nsa_kv, state_nsa_win, state_gla, state_pool, page_table,
           norm_mix, norm_ffn, w_in, w_out, pool_w, pool_scale, nsa_q_norm, nsa_k_norm,
           nsa_cmp_w, nsa_cmp_pe, gla_wa2, gla_ba, gla_norm, ffn_w1, ffn_w3, ffn_w2,
           moe_router, moe_w1, moe_w3, moe_w2):
    bp, tp, d = x_prompt.shape
    bs, ts, _ = x_sample.shape
    depth = w_in.shape[0]
    n_pool = cache_nsa_kv.shape[1]
    n_pages = page_table.shape[1]
    past_len = n_pages * PAGE_SIZE
    w_buf = state_nsa_win.shape[2]
    npr = bp * tp
    cache_v = cache_nsa_kv.transpose(0, 1, 3, 4, 5, 2).reshape(depth, n_pool, 4 * NSA_KV * HEAD_DIM, PAGE_SIZE)
    win_state_v = state_nsa_win.transpose(0, 1, 3, 4, 5, 2)
    x = jnp.concatenate([x_prompt.reshape(npr, d), x_sample.reshape(bs * ts, d)], axis=0)
    kv_p, kv_s, win_p, win_s, gla_p, gla_s, pool_p, pool_s = [], [], [], [], [], [], [], []
    n_win_p = min(WINDOW, tp)
    for l in range(depth):
        u_pool, q_rows, rows, win, misc, gla_rows = _inproj(x, norm_mix[l], w_in[l], nsa_q_norm[l], nsa_k_norm[l])
        cw = _cmp_weights(nsa_cmp_w[l], nsa_cmp_pe[l], nsa_k_norm[l])
        up = u_pool[:npr].reshape(bp, tp, POOL_WIDTH)
        us = u_pool[npr:].reshape(bs, ts, POOL_WIDTH)
        yp_pool = _pool(up, jnp.zeros((bp, POOL_BUF, POOL_WIDTH), F32), pool_w[l], pool_scale[l], 0)
        ys_pool = _pool(us, state_pool[l], pool_w[l], pool_scale[l], past_len)
        pool_p.append(jnp.concatenate([jnp.zeros((bp, POOL_BUF, POOL_WIDTH), F32), up], axis=1)[:, -POOL_BUF:])
        pool_s.append(jnp.concatenate([state_pool[l], us], axis=1)[:, -POOL_BUF:])
        yp_nsa = _nsa_prompt(q_rows, rows, win, misc, cw, bp, tp)
        pw = _paged_cmp_weights(nsa_cmp_w[l], nsa_cmp_pe[l])
        ys_nsa = _nsa_sample(q_rows[npr:], rows[npr:], win[npr:], misc[npr:], cw, pw, cache_v,
                             page_table, l, win_state_v, bs, ts, past_len)
        kv_p.append(rows[:npr].reshape(bp, tp, 4, NSA_KV, HEAD_DIM))
        kv_s.append(rows[npr:].reshape(bs, ts, 4, NSA_KV, HEAD_DIM))
        win_full_p = jnp.concatenate([jnp.zeros((bp, WINDOW, 256), F32), win[:npr].reshape(bp, tp, 256)], axis=1)
        win_p.append(win_full_p[:, -n_win_p:].reshape(bp, n_win_p, 2, NSA_KV, HEAD_DIM))
        win_ext_s = jnp.concatenate([state_nsa_win[l], win[npr:].reshape(bs, ts, 2, NSA_KV, HEAD_DIM)], axis=1)
        win_s.append(win_ext_s[:, -w_buf:])
        yp_gla, sp = _gla(gla_rows, misc, jnp.zeros((bp, GLA_HEADS, GLA_DK, GLA_DV), F32),
                          gla_wa2[l], gla_ba[l], gla_norm[l], bp, tp)
        ys_gla, ss = _gla(gla_rows[npr:], misc[npr:], state_gla[l], gla_wa2[l], gla_ba[l], gla_norm[l], bs, ts)
        gla_p.append(sp.astype(state_gla.dtype))
        gla_s.append(ss.astype(state_gla.dtype))
        i = l // 2
        router = moe_router[i] if l % 2 else None
        res = _outproj(jnp.concatenate([yp_pool.reshape(npr, -1), ys_pool.reshape(bs * ts, -1)], axis=0),
                       jnp.concatenate([yp_nsa, ys_nsa], axis=0),
                       jnp.concatenate([yp_gla, ys_gla], axis=0), x, w_out[l], norm_ffn[l], router)
        if l % 2 == 0:
            h, hn = res
            x = _dense_ffn(h, hn, ffn_w1[i], ffn_w3[i], ffn_w2[i])
        else:
            h, hn, logits = res
            x = _moe_ffn(h, hn, logits, moe_w1[i], moe_w3[i], moe_w2[i])
    return (x[:npr].reshape(bp, tp, d), x[npr:].reshape(bs, ts, d),
            jnp.stack(kv_p), jnp.stack(kv_s), jnp.stack(win_p), jnp.stack(win_s),
            jnp.stack(gla_p), jnp.stack(gla_s), jnp.stack(pool_p), jnp.stack(pool_s))
```

```python
import functools
import math

import jax
import jax.numpy as jnp
from jax import lax
from jax.experimental import pallas as pl
from jax.experimental.pallas import tpu as pltpu

F32 = jnp.float32
BF16 = jnp.bfloat16

EPS = 1e-6
LANES = 128
HEAD_DIM = 64
PAGE_SIZE = 128
POOL_WINDOWS = (2, 4, 8, 16)
POOL_BUF = 15
POOL_WIDTH = 256
NSA_WIDTH = 512
NSA_HEADS = 8
NSA_KV = 2
NSA_HPG = 4
L_CMP = 32
L_SLC = 64
N_SEL = 16
WINDOW = 512
GLA_HEADS = 4
GLA_DK = 32
GLA_DV = 64
GLA_WIDTH = 256
GLA_RANK = 16
GLA_TAU = 16.0
GLA_SUB = 16
TOP_K = 2
N_IN_PAD = 2432
MASKED = -1e30
VMEM_LIMIT = 56 * 1024 * 1024


def _params(*sem):
    return pltpu.CompilerParams(dimension_semantics=sem, vmem_limit_bytes=VMEM_LIMIT)


def _tile(n, target):
    best = None
    for t in range(8, min(n, target) + 1, 8):
        if n % t == 0:
            best = t
    assert best is not None, (n, target)
    return best


def _dot(a, b):
    return jnp.dot(a, b, preferred_element_type=F32)


def _dot_nt(a, b):
    return lax.dot_general(a, b, (((1,), (1,)), ((), ())), preferred_element_type=F32)


def _dot_tn(a, b):
    return lax.dot_general(a, b, (((0,), (0,)), ((), ())), preferred_element_type=F32)


def _split2_dot(a, ones):
    hi = a.astype(BF16)
    lo = (a - hi.astype(F32)).astype(BF16)
    return _dot(hi, ones) + _dot(lo, ones)


def _split3_dot(ones, a):
    a1 = a.astype(BF16)
    r1 = a - a1.astype(F32)
    a2 = r1.astype(BF16)
    a3 = (r1 - a2.astype(F32)).astype(BF16)
    return _dot(ones, a1) + _dot(ones, a2) + _dot(ones, a3)


def _head_rmsnorm(a, gain_row, seg_ones):
    ms = _split2_dot(a * a, seg_ones) * (1.0 / HEAD_DIM)
    return a * lax.rsqrt(ms + EPS) * gain_row


def _masked_softmax(s, mask):
    sm = jnp.where(mask, s, MASKED)
    m = jnp.max(sm, axis=-1, keepdims=True)
    e = jnp.where(mask, jnp.exp(sm - m), 0.0)
    den = jnp.sum(e, axis=-1, keepdims=True)
    return e * (1.0 / jnp.where(den > 0.0, den, 1.0))


def _seg_ones():
    i = jnp.arange(LANES)
    return (i[:, None] // HEAD_DIM == i[None, :] // HEAD_DIM).astype(BF16)


def _inproj_kernel(x_ref, g_ref, w_ref, qg_ref, kg_ref, seg_ref,
                   pool_o, q_o, rows_o, win_o, misc_o, gla_o):
    x = x_ref[...]
    ms = jnp.mean(x * x, axis=-1, keepdims=True)
    xn = (x * lax.rsqrt(ms + EPS) * g_ref[...]).astype(BF16)
    seg = seg_ref[...]

    def mm(c0, c1):
        return _dot(xn, w_ref[:, c0:c1])

    pool_o[...] = mm(0, 256)
    zq = mm(256, 768)
    for c in range(4):
        q_o[:, LANES * c:LANES * (c + 1)] = (
            _head_rmsnorm(zq[:, LANES * c:LANES * (c + 1)], qg_ref[...], seg) * (HEAD_DIM ** -0.5)).astype(BF16)
    zkv = mm(768, 1536)
    rows_o[:, 0:256] = zkv[:, 0:256]
    rows_o[:, 256:384] = _head_rmsnorm(zkv[:, 256:384], kg_ref[1:2, :], seg)
    rows_o[:, 384:512] = zkv[:, 384:512]
    win_o[:, 0:128] = _head_rmsnorm(zkv[:, 512:640], kg_ref[2:3, :], seg)
    win_o[:, 128:256] = zkv[:, 640:768]
    zg = mm(1536, 2432)
    misc_o[...] = zg[:, 0:128]
    gla_o[...] = zg[:, 128:896]


def _pad_w_in(w):
    d = w.shape[0]
    return jnp.concatenate([
        w[:, 0:1560], w[:, 2072:2088], jnp.zeros((d, 88), w.dtype),
        w[:, 1560:2072], w[:, 2088:2344]], axis=1)


def _inproj(x, g_mix, w_in_l, q_gain, k_gain):
    n, d = x.shape
    tm = _tile(n, 384)
    w = _pad_w_in(w_in_l).astype(BF16)
    qg = jnp.tile(q_gain, 2)[None, :]
    kg = jnp.zeros((8, LANES), F32).at[0:3].set(jnp.tile(k_gain, (1, 2)))
    full = lambda shape: pl.BlockSpec(shape, lambda i: (0, 0))
    row = lambda c: pl.BlockSpec((tm, c), lambda i: (i, 0))
    return pl.pallas_call(
        _inproj_kernel,
        grid=(n // tm,),
        in_specs=[row(d), full((1, d)), full((d, N_IN_PAD)), full((1, LANES)),
                  full((8, LANES)), full((LANES, LANES))],
        out_specs=[row(256), row(512), row(512), row(256), row(128), row(768)],
        out_shape=[jax.ShapeDtypeStruct((n, 256), F32), jax.ShapeDtypeStruct((n, 512), BF16),
                   jax.ShapeDtypeStruct((n, 512), F32), jax.ShapeDtypeStruct((n, 256), F32),
                   jax.ShapeDtypeStruct((n, 128), F32), jax.ShapeDtypeStruct((n, 768), F32)],
        compiler_params=_params("arbitrary"),
        name="inproj",
    )(x, g_mix[None, :], w, qg, kg, _seg_ones())


def _pool_kernel(buf_ref, u_ref, w_ref, sc_ref, *rest, pos0, tp):
    o_ref, ext = rest[-2:]
    i = pl.program_id(1)

    @pl.when(i == 0)
    def _():
        ext[0:16, :] = buf_ref[0]

    ext[16:16 + tp, :] = u_ref[...].reshape(tp, POOL_WIDTH)
    u0 = ext[16:16 + tp, :]
    acc = u0
    sums = {}
    for k in range(1, 16):
        acc = acc + ext[16 - k:16 - k + tp, :]
        if k + 1 in POOL_WINDOWS:
            sums[k + 1] = acc
    lane = lax.broadcasted_iota(jnp.int32, (tp, POOL_WIDTH), 1)
    pos = pos0 + i * tp + lax.broadcasted_iota(jnp.int32, (tp, POOL_WIDTH), 0)
    grp = lane // (POOL_WIDTH // len(POOL_WINDOWS))
    total = sums[16]
    wsize = jnp.full((tp, POOL_WIDTH), 16, jnp.int32)
    for gi, wz in enumerate(POOL_WINDOWS[:-1]):
        total = jnp.where(grp == gi, sums[wz], total)
        wsize = jnp.where(grp == gi, wz, wsize)
    cnt = jnp.minimum(wsize, pos + 1).astype(F32)
    dlt = total / cnt - u0
    o_ref[...] = (_dot(dlt.astype(BF16), w_ref[...]) * sc_ref[...]).astype(o_ref.dtype).reshape(o_ref.shape)
    if tp >= 16:
        ext[0:16, :] = ext[tp:tp + 16, :]


def _aliased_base(base, n_in):
    if base is None:
        return [], [], {}
    return [pl.BlockSpec(memory_space=pl.ANY)], [base], {n_in: 0}


def _pool(u, buf, pool_w_l, pool_scale_l, pos0, b, t, base=None):
    base_specs, base_args, alias = _aliased_base(base, 4)
    c = u.shape[-1]
    tp = _tile(t, 512) if t >= 8 else t
    nt = t // tp
    buf16 = jnp.concatenate([jnp.zeros((b, 1, c), F32), buf.astype(F32)], axis=1)
    gw = c // len(POOL_WINDOWS)
    wbd = jnp.zeros((c, c), F32)
    for gi in range(len(POOL_WINDOWS)):
        wbd = wbd.at[gi * gw:(gi + 1) * gw, gi * gw:(gi + 1) * gw].set(pool_w_l[gi])
    if u.ndim == 2:
        rows = pl.BlockSpec((tp, c), lambda i, j: (i * nt + j, 0))
    else:
        rows = pl.BlockSpec((1, tp, c), lambda i, j: (i, j, 0))
    return pl.pallas_call(
        functools.partial(_pool_kernel, pos0=pos0, tp=tp),
        grid=(b, nt),
        in_specs=[pl.BlockSpec((1, 16, c), lambda i, j: (i, 0, 0)), rows,
                  pl.BlockSpec((c, c), lambda i, j: (0, 0)),
                  pl.BlockSpec((1, c), lambda i, j: (0, 0))] + base_specs,
        out_specs=rows,
        out_shape=jax.ShapeDtypeStruct(u.shape, BF16),
        scratch_shapes=[pltpu.VMEM((16 + tp, c), F32)],
        input_output_aliases=alias,
        compiler_params=_params("arbitrary", "arbitrary"),
        name="pool",
    )(buf16, u, wbd.astype(BF16), pool_scale_l[None, :], *base_args)


def _gla_kernel(gla_ref, misc_ref, s0_ref, wa_ref, ba_ref, og_ref, seg_ref, eb_ref, mk_ref, *rest, tg, t_valid):
    o_ref, sT_ref, st, qs, ks, bs, qts, kts, vs, os_, dls, us, ss = rest[-13:]
    i = pl.program_id(1)
    c = GLA_SUB
    nsub = tg // c

    @pl.when(i == 0)
    def _():
        st[...] = s0_ref[0]

    gl = gla_ref[...]
    q = gl[:, 0:128] * (GLA_DK ** -0.5)
    k = gl[:, 128:256]
    v = gl[:, 256:512]
    r = gl[:, 512:768]
    x = _dot(misc_ref[...].astype(BF16), wa_ref[...]) + ba_ref[...]
    la = (jnp.minimum(x, 0.0) - jnp.log1p(jnp.exp(-jnp.abs(x)))) * (1.0 / GLA_TAU)
    row = lax.broadcasted_iota(jnp.int32, (tg, LANES), 0)
    if t_valid is not None:
        la = jnp.where(i * tg + row < t_valid, la, 0.0)
    rr = lax.broadcasted_iota(jnp.int32, (tg, tg), 0)
    cc = lax.broadcasted_iota(jnp.int32, (tg, tg), 1)
    same = (rr // c) == (cc // c)
    tri = (same & (cc <= rr)).astype(BF16)
    allo = same.astype(BF16)
    b = _split3_dot(tri, la)
    blast = _split3_dot(allo, la)
    qs[...] = q
    ks[...] = k
    bs[...] = b
    qts[...] = (q * jnp.exp(b)).astype(BF16)
    kts[...] = (k * jnp.exp(blast - b)).astype(BF16)
    vs[...] = v
    eb = eb_ref[...]
    mk = mk_ref[...]
    tt = lax.broadcasted_iota(jnp.int32, (c, LANES), 0)

    dls[...] = jnp.exp(blast)

    def local(j, carry):
        r0 = pl.multiple_of(j * c, c)
        qi = qs[pl.ds(r0, c), :]
        ki = ks[pl.ds(r0, c), :]
        bi = bs[pl.ds(r0, c), :]
        vi = vs[pl.ds(r0, c), :]
        parts = []
        for s in range(c):
            dec = jnp.exp(jnp.minimum(bi - bi[s:s + 1, :], 0.0))
            parts.append(jnp.where(tt >= s, qi * ki[s:s + 1, :] * dec, 0.0))
        p_all = jnp.concatenate(parts, axis=0).astype(BF16)
        a_all = _dot(p_all, eb)
        o_diag = a_all[0:c, :] * vi[0:1, :]
        for s in range(1, c):
            o_diag = o_diag + a_all[s * c:(s + 1) * c, :] * vi[s:s + 1, :]
        os_[pl.ds(r0, c), :] = o_diag
        us[j] = _dot_tn(vi.astype(BF16), kts[pl.ds(r0, c), :]) * mk
        return carry

    def grouped(body, group):
        group = math.gcd(nsub, group)

        def trip(jj, carry):
            for u in range(group):
                body(jj * group + u, carry)
            return carry

        lax.fori_loop(0, nsub // group, trip, 0)

    grouped(local, 8)

    def recur(j, carry):
        s_t = st[...]
        ss[j] = s_t.astype(BF16)
        st[...] = s_t * dls[pl.ds(pl.multiple_of(j * c, c), 1), :] + us[j]
        return carry

    lax.fori_loop(0, nsub, recur, 0)

    def inter(j, carry):
        r0 = pl.multiple_of(j * c, c)
        os_[pl.ds(r0, c), :] += _dot_nt(qts[pl.ds(r0, c), :], ss[j])
        return carry

    grouped(inter, 8)

    o = os_[...]
    seg = seg_ref[...]
    og = og_ref[...]
    sil = r * (1.0 / (1.0 + jnp.exp(-r)))
    for h in range(2):
        sl = slice(h * LANES, (h + 1) * LANES)
        o_ref[:, sl] = (_head_rmsnorm(o[:, sl], og[:, sl], seg) * sil[:, sl]).astype(o_ref.dtype)
    sT_ref[0] = st[...]


def _gla(gla_rows, misc_rows, s0, wa2, ba, o_gain, b, t, base=None):
    base_specs, base_args, alias = _aliased_base(base, 9)
    t_valid = None
    if t % GLA_SUB:
        t_valid = t
        tp = -(-t // GLA_SUB) * GLA_SUB
        pad = lambda z: jnp.pad(z.reshape(b, t, -1), ((0, 0), (0, tp - t), (0, 0))).reshape(b * tp, -1)
        gla_rows, misc_rows = pad(gla_rows), pad(misc_rows)
    else:
        tp = t
    tg = _tile(tp, 256)
    assert tg % GLA_SUB == 0
    nt = tp // tg
    kk = GLA_HEADS * GLA_DK
    vv = GLA_HEADS * GLA_DV
    ki = jnp.arange(kk)
    vi = jnp.arange(vv)
    head_eq = (vi[:, None] // GLA_DV == ki[None, :] // GLA_DK)
    mk = head_eq.astype(F32)
    eb = head_eq.T.astype(BF16)
    s0t = jnp.einsum('bhkv,hg->bhvgk', s0.astype(F32), jnp.eye(GLA_HEADS, dtype=F32)).reshape(b, vv, kk)
    wa = jnp.zeros((LANES, kk), F32).at[24:24 + GLA_RANK].set(wa2).astype(BF16)
    full = lambda shape: pl.BlockSpec(shape, lambda i, j: (0,) * len(shape))
    rows = lambda cdim: pl.BlockSpec((tg, cdim), lambda i, j: (i * nt + j, 0))
    o, s_t = pl.pallas_call(
        functools.partial(_gla_kernel, tg=tg, t_valid=t_valid),
        grid=(b, nt),
        in_specs=[rows(768), rows(128), pl.BlockSpec((1, vv, kk), lambda i, j: (i, 0, 0)),
                  full((LANES, kk)), full((1, kk)), full((1, vv)), full((LANES, LANES)),
                  full((kk, vv)), full((vv, kk))] + base_specs,
        input_output_aliases=alias,
        out_specs=[rows(vv), pl.BlockSpec((1, vv, kk), lambda i, j: (i, 0, 0))],
        out_shape=[jax.ShapeDtypeStruct((gla_rows.shape[0], vv), BF16), jax.ShapeDtypeStruct((b, vv, kk), F32)],
        scratch_shapes=[pltpu.VMEM((vv, kk), F32), pltpu.VMEM((tg, kk), F32), pltpu.VMEM((tg, kk), F32),
                        pltpu.VMEM((tg, kk), F32), pltpu.VMEM((tg, kk), BF16), pltpu.VMEM((tg, kk), BF16),
                        pltpu.VMEM((tg, vv), F32), pltpu.VMEM((tg, vv), F32), pltpu.VMEM((tg, kk), F32),
                        pltpu.VMEM((tg // GLA_SUB, vv, kk), F32), pltpu.VMEM((tg // GLA_SUB, vv, kk), BF16)],
        compiler_params=_params("arbitrary", "arbitrary"),
        name="gla",
    )(gla_rows, misc_rows, s0t, wa, ba[None, :], jnp.tile(o_gain, GLA_HEADS)[None, :], _seg_ones(), eb, mk,
      *base_args)
    if tp != t:
        o = o.reshape(b, tp, vv)[:, :t].reshape(b * t, vv)
    s5 = s_t.reshape(b, GLA_HEADS, GLA_DV, GLA_HEADS, GLA_DK)
    s_new = jnp.einsum('bhvgk,hg->bhkv', s5, jnp.eye(GLA_HEADS, dtype=F32))
    return o, s_new


def _compress_rows(read, nb, pe_ref, wk_ref, wv_ref, kg_ref, seg_ref):
    acck = jnp.zeros((nb, LANES), F32)
    accv = jnp.zeros((nb, LANES), F32)
    for j in range(L_CMP):
        xk = (read(0, j) + pe_ref[0, j:j + 1, :]).astype(BF16)
        xv = (read(1, j) + pe_ref[1, j:j + 1, :]).astype(BF16)
        acck = acck + _dot(xk, wk_ref[j])
        accv = accv + _dot(xv, wv_ref[j])
    kc = _head_rmsnorm(acck, kg_ref[0:1, :], seg_ref[...])
    return kc, accv


def _compress_kernel(rk_ref, rv_ref, pe_ref, wk_ref, wv_ref, kg_ref, seg_ref, o_ref, *, nb):
    refs = (rk_ref, rv_ref)
    kc, vc = _compress_rows(lambda kind, j: refs[kind][pl.ds(j, nb, stride=L_CMP), :], nb,
                            pe_ref, wk_ref, wv_ref, kg_ref, seg_ref)
    o_ref[0, :, 0:LANES] = kc
    o_ref[0, :, LANES:2 * LANES] = vc


def _cmp_weights(cmp_w, cmp_pe, k_gain):
    def bd(w):
        z = jnp.zeros_like(w)
        return jnp.concatenate([jnp.concatenate([w, z], axis=2), jnp.concatenate([z, w], axis=2)], axis=1)
    pe = jnp.tile(cmp_pe, (1, 1, 2))
    kg = jnp.zeros((8, LANES), F32).at[0].set(jnp.tile(k_gain[0], 2))
    return pe, bd(cmp_w[0]).astype(BF16), bd(cmp_w[1]).astype(BF16), kg


def _compress(rk, rv, k_col, v_col, b, tp, cw):
    nb = tp // L_CMP
    pe, wk, wv, kg = cw
    full = lambda shape: pl.BlockSpec(shape, lambda i: (0,) * len(shape))
    return pl.pallas_call(
        functools.partial(_compress_kernel, nb=nb),
        grid=(b,),
        in_specs=[pl.BlockSpec((tp, LANES), lambda i: (i, k_col)), pl.BlockSpec((tp, LANES), lambda i: (i, v_col)),
                  full((2, L_CMP, LANES)), full((L_CMP, LANES, LANES)), full((L_CMP, LANES, LANES)),
                  full((8, LANES)), full((LANES, LANES))],
        out_specs=pl.BlockSpec((1, nb, 256), lambda i: (i, 0, 0)),
        out_shape=jax.ShapeDtypeStruct((b, nb, 256), F32),
        compiler_params=_params("arbitrary"),
        name="compress",
    )(rk, rv, pe, wk, wv, kg, _seg_ones())


SLAB_PITCH = 2 * NSA_KV * HEAD_DIM + 8


def _paged_compress_kernel(pt_ref, cache_ref, pe_ref, m_ref, kg_ref, seg_ref, o_ref,
                           slab, sem, *, pages, n_slab, layer):
    bi = pl.program_id(0)
    si = pl.program_id(1)
    step = bi * n_slab + si
    nsteps = pl.num_programs(0) * n_slab
    slot = step % 2
    rows_cmp = 2 * NSA_KV * HEAD_DIM

    def copy(page, slot_, p):
        return pltpu.make_async_copy(cache_ref.at[layer, page, pl.ds(0, rows_cmp), :],
                                     slab.at[slot_, pl.ds(p * SLAB_PITCH, rows_cmp), :], sem.at[slot_])

    def issue(bb, ss, slot_):
        for p in range(pages):
            copy(pt_ref[bb, ss * pages + p], slot_, p).start()

    @pl.when(step == 0)
    def _():
        issue(0, 0, 0)

    @pl.when(step + 1 < nsteps)
    def _():
        nxt = step + 1
        issue(nxt // n_slab, nxt % n_slab, 1 - slot)

    for p in range(pages):
        copy(0, slot, p).wait()

    def rows(r0):
        return slab[slot, pl.ds(r0, pages, stride=SLAB_PITCH), :]

    for c in range(2):
        acc = jnp.zeros((NSA_KV * pages, 2 * LANES), F32)
        for dp in range(HEAD_DIM // 2):
            parts = []
            for g in range(NSA_KV):
                r0 = (c * NSA_KV + g) * HEAD_DIM + 2 * dp
                parts.append(jnp.concatenate([rows(r0), rows(r0 + 1)], axis=1))
            a = jnp.concatenate(parts, axis=0) + pe_ref[c, dp:dp + 1, :]
            acc = acc + _dot(a.astype(BF16), m_ref[c, dp])
        for g in range(NSA_KV):
            blk = acc[g * pages:(g + 1) * pages, :]
            if c == 0:
                for h in range(2):
                    sl = slice(h * LANES, (h + 1) * LANES)
                    o_ref[0, c, g, :, sl] = _head_rmsnorm(blk[:, sl], kg_ref[0:1, :], seg_ref[...])
            else:
                o_ref[0, c, g] = blk


def _paged_cmp_weights(cmp_w, cmp_pe):
    nblk = PAGE_SIZE // L_CMP
    k6 = jnp.einsum('nm,cjde->cdnjme', jnp.eye(nblk, dtype=F32), cmp_w)
    m = k6.reshape(2, HEAD_DIM // 2, 2 * PAGE_SIZE, nblk * HEAD_DIM).astype(BF16)
    pe = jnp.tile(cmp_pe.transpose(0, 2, 1), (1, 1, nblk)).reshape(2, HEAD_DIM // 2, 2 * PAGE_SIZE)
    return pe, m


def _paged_compress(cache_v, page_table, layer, cw, pw):
    b, n_pages = page_table.shape
    pages = math.gcd(n_pages, 64)
    n_slab = n_pages // pages
    pe, m = pw
    kg = cw[3]
    nblk = PAGE_SIZE // L_CMP
    full = lambda shape: pl.BlockSpec(shape, lambda i, j, pt: (0,) * len(shape))
    grid_spec = pltpu.PrefetchScalarGridSpec(
        num_scalar_prefetch=1,
        grid=(b, n_slab),
        in_specs=[pl.BlockSpec(memory_space=pl.ANY), full(pe.shape), full(m.shape), full((8, LANES)),
                  full((LANES, LANES))],
        out_specs=pl.BlockSpec((1, 2, NSA_KV, pages, nblk * HEAD_DIM), lambda i, j, pt: (i, 0, 0, j, 0)),
        scratch_shapes=[pltpu.VMEM((2, pages * SLAB_PITCH, LANES), F32), pltpu.SemaphoreType.DMA((2,))],
    )
    return pl.pallas_call(
        functools.partial(_paged_compress_kernel, pages=pages, n_slab=n_slab, layer=layer),
        grid_spec=grid_spec,
        out_shape=jax.ShapeDtypeStruct((b, 2, NSA_KV, n_pages, nblk * HEAD_DIM), F32),
        compiler_params=_params("arbitrary", "arbitrary"),
        name="paged_compress",
    )(page_table, cache_v, pe, m, kg, _seg_ones())


def _nsa_prompt_kernel(q_ref, kc_ref, vc_ref, ks_ref, vs_ref, kw_ref, vw_ref, gate_ref, base_ref,
                       o_ref, ksb, vsb, kwb, vwb, *, tq, ck):
    i = pl.program_id(1)
    t0 = i * tq

    n_slc = LANES // 2
    t_keys = ks_ref.shape[0]

    @pl.when(i == 0)
    def _():
        ksb[:, 0:LANES] = ks_ref[...].astype(BF16)
        kblk = lax.broadcasted_iota(jnp.int32, (t_keys, LANES), 0) // L_SLC
        klane = lax.broadcasted_iota(jnp.int32, (t_keys, LANES), 1)
        ksb[:, LANES:2 * LANES] = jnp.where(kblk == klane, 1.0, 0.0).astype(BF16)
        vsb[...] = vs_ref[...].astype(BF16)
        kwb[...] = kw_ref[...].astype(BF16)
        vwb[...] = vw_ref[...].astype(BF16)

    row_t = t0 + lax.broadcasted_iota(jnp.int32, (tq, 1), 0)
    col_t = t0 + lax.broadcasted_iota(jnp.int32, (1, tq), 1)
    lane = lax.broadcasted_iota(jnp.int32, (1, LANES), 1)
    crow = lax.broadcasted_iota(jnp.int32, (LANES, 1), 0)
    nat = jnp.where(crow < n_slc, 2 * crow, 2 * (crow - n_slc) + 1)
    cmp_ok = ((nat + 1) * L_CMP - 1) <= col_t
    blk = lax.broadcasted_iota(jnp.int32, (n_slc, 1), 0)
    cur = col_t // L_SLC
    done = blk < cur
    gx = gate_ref[...]
    gates = 1.0 / (1.0 + jnp.exp(-gx))
    n_full = t0 // ck
    kstart = pl.multiple_of(jnp.maximum(t0 - WINDOW, 0), tq)
    wlen = WINDOW + tq
    wpos = kstart + lax.broadcasted_iota(jnp.int32, (1, wlen), 1)
    win_bias = jnp.where((wpos <= row_t) & (wpos >= row_t - WINDOW), 0.0, MASKED)

    for g in range(NSA_KV):
        in_g = (lane // HEAD_DIM) == g
        heads = []
        for j in range(NSA_HPG):
            h = g * NSA_HPG + j
            qh = q_ref[:, (h // 2) * LANES:(h // 2 + 1) * LANES].astype(F32)
            if h % 2 != g:
                qh = pltpu.roll(qh, HEAD_DIM, axis=1)
            heads.append(jnp.where(in_g, qh, 0.0).astype(BF16))
        qg = jnp.concatenate(heads, axis=0)
        s_c = _dot_nt(kc_ref[0], qg)
        imp = jnp.zeros((LANES, tq), F32)
        o_c = []
        for j in range(NSA_HPG):
            sj = jnp.where(cmp_ok, s_c[:, j * tq:(j + 1) * tq], MASKED)
            ej = jnp.where(cmp_ok, jnp.exp(sj - jnp.max(sj, axis=0, keepdims=True)), 0.0)
            den = jnp.sum(ej, axis=0, keepdims=True)
            pj = ej * (1.0 / jnp.where(den > 0.0, den, 1.0))
            imp = imp + pj
            o_c.append(_dot_tn(pj.astype(BF16), vc_ref[0]))
        imp = imp[0:n_slc] + imp[n_slc:LANES]
        key = jnp.where(done, lax.bitcast_convert_type(imp, jnp.int32), -1)
        key_m1 = key - 1
        rank = jnp.zeros((n_slc, tq), jnp.int32)
        for r in range(1, n_slc):
            vm = pltpu.roll(key, r, axis=0)
            rank = rank + jnp.where(vm > jnp.where(blk >= r, key_m1, key), 1, 0)
        sel = (done & (rank < N_SEL - 1)) | (blk == cur)
        sel_bias = jnp.concatenate([jnp.where(sel, 0.0, MASKED), jnp.zeros((n_slc, tq), F32)], axis=0)
        sel_bias = sel_bias.T.astype(BF16)
        qx = jnp.concatenate([qg, jnp.concatenate([sel_bias] * NSA_HPG, axis=0)], axis=1)

        def chunk(c, carry, diagonal):
            m, l, acc = carry
            k0 = pl.multiple_of(c * ck, ck)
            sm = _dot_nt(qx, ksb[pl.ds(k0, ck), :]).reshape(NSA_HPG, tq, ck)
            if diagonal:
                kpos = k0 + lax.broadcasted_iota(jnp.int32, (1, ck), 1)
                sm = sm + jnp.where(kpos <= row_t, 0.0, MASKED)[None]
            m_new = jnp.maximum(m, jnp.max(sm, axis=-1, keepdims=True))
            p = jnp.exp(sm - m_new)
            alpha = jnp.exp(m - m_new)
            l = alpha * l + jnp.sum(p, axis=-1, keepdims=True)
            pv = _dot(p.reshape(NSA_HPG * tq, ck).astype(BF16), vsb[pl.ds(k0, ck), :])
            acc = alpha * acc + pv.reshape(NSA_HPG, tq, LANES)
            return m_new, l, acc

        m0 = jnp.full((NSA_HPG, tq, 1), MASKED, F32)
        l0 = jnp.zeros((NSA_HPG, tq, 1), F32)
        a0 = jnp.zeros((NSA_HPG, tq, LANES), F32)
        carry = lax.fori_loop(0, n_full, functools.partial(chunk, diagonal=False), (m0, l0, a0))
        _, l_s, acc_s = chunk(n_full, carry, True)
        o_s = acc_s * (1.0 / jnp.where(l_s > 0.0, l_s, 1.0))
        s_w = _dot_nt(qg, kwb[pl.ds(kstart, wlen), :]).reshape(NSA_HPG, tq, wlen) + win_bias[None]
        e_w = jnp.exp(s_w - jnp.max(s_w, axis=-1, keepdims=True))
        p_w = e_w * (1.0 / jnp.sum(e_w, axis=-1, keepdims=True))
        o_w = _dot(p_w.reshape(NSA_HPG * tq, wlen).astype(BF16), vwb[pl.ds(kstart, wlen), :])
        o_w = o_w.reshape(NSA_HPG, tq, LANES)
        for pr in range(NSA_HPG // 2):
            pair = []
            for jj in range(2):
                j = 2 * pr + jj
                h = g * NSA_HPG + j
                o = (gates[:, 3 * h:3 * h + 1] * o_c[j] + gates[:, 3 * h + 1:3 * h + 2] * o_s[j]
                     + gates[:, 3 * h + 2:3 * h + 3] * o_w[j])
                pair.append(o if jj == g else pltpu.roll(o, HEAD_DIM, axis=1))
            col = (g * (NSA_HPG // 2) + pr) * LANES
            o_ref[:, col:col + LANES] = jnp.where(lane < HEAD_DIM, pair[0], pair[1]).astype(o_ref.dtype)


def _nsa_prompt(q_rows, rows, win, misc, cw, b, t, base):
    assert t // L_CMP == LANES and t % L_SLC == 0, "prompt kernel is laid out for 128 compressed blocks"
    tq = 256
    ck = 512
    assert ck % tq == 0 and t % ck == 0
    cmp = _compress(rows, rows, 0, 1, b, t, cw)
    order = jnp.concatenate([jnp.arange(0, LANES, 2), jnp.arange(1, LANES, 2)])
    cmp = cmp[:, order].astype(BF16)
    kc, vc = cmp[:, :, 0:LANES], cmp[:, :, LANES:2 * LANES]
    nt = t // tq
    col = lambda c: pl.BlockSpec((t, LANES), lambda i, j: (i, c))
    return pl.pallas_call(
        functools.partial(_nsa_prompt_kernel, tq=tq, ck=ck),
        grid=(b, nt),
        in_specs=[pl.BlockSpec((tq, NSA_WIDTH), lambda i, j: (i * nt + j, 0)),
                  pl.BlockSpec((1, LANES, LANES), lambda i, j: (i, 0, 0)),
                  pl.BlockSpec((1, LANES, LANES), lambda i, j: (i, 0, 0)),
                  col(2), col(3), col(0), col(1),
                  pl.BlockSpec((tq, LANES), lambda i, j: (i * nt + j, 0)),
                  pl.BlockSpec(memory_space=pl.ANY)],
        out_specs=pl.BlockSpec((tq, NSA_WIDTH), lambda i, j: (i * nt + j, 0)),
        out_shape=jax.ShapeDtypeStruct((q_rows.shape[0], NSA_WIDTH), BF16),
        scratch_shapes=[pltpu.VMEM((t, 2 * LANES), BF16)] + [pltpu.VMEM((t, LANES), BF16)] * 3,
        input_output_aliases={8: 0},
        compiler_params=_params("arbitrary", "arbitrary"),
        name="nsa_prompt",
    )(q_rows, kc, vc, rows, rows, win, win, misc, base)


def _split3_dot_r(a, ones):
    a1 = a.astype(BF16)
    r1 = a - a1.astype(F32)
    a2 = r1.astype(BF16)
    a3 = (r1 - a2.astype(F32)).astype(BF16)
    return _dot(a1, ones) + _dot(a2, ones) + _dot(a3, ones)


def _nsa_select_kernel(q_ref, cp_ref, ct_ref, pair_ref, oc_ref, sel_ref, *, t, past_len, n_past):
    nrow = NSA_HPG * t
    npad = n_past + LANES
    n_slc_pad = pair_ref.shape[1]
    row = lax.broadcasted_iota(jnp.int32, (nrow, 1), 0)
    pos = past_len + row % t
    lane_c = lax.broadcasted_iota(jnp.int32, (1, npad), 1)
    cmp_ok = ((lane_c + 1) * L_CMP - 1) <= pos
    cur = (past_len + lax.broadcasted_iota(jnp.int32, (t, 1), 0)) // L_SLC
    lane_s = lax.broadcasted_iota(jnp.int32, (1, n_slc_pad), 1)
    lane_o = lax.broadcasted_iota(jnp.int32, (1, LANES), 1)
    done = lane_s < cur
    keys = []
    for g in range(NSA_KV):
        q = q_ref[0, g]
        s = jnp.concatenate([_dot_nt(q, cp_ref[0, 0, g].astype(BF16)),
                             _dot_nt(q, ct_ref[0, 0, g].astype(BF16))], axis=1)
        p = _masked_softmax(s, cmp_ok)
        pb = p.astype(BF16)
        oc_ref[0, g] = (_dot(pb[:, :n_past], cp_ref[0, 1, g].astype(BF16))
                        + _dot(pb[:, n_past:], ct_ref[0, 1, g].astype(BF16)))
        imp_c = p[0:t]
        for j in range(1, NSA_HPG):
            imp_c = imp_c + p[j * t:(j + 1) * t]
        imp = _split3_dot_r(imp_c, pair_ref[...])
        keys.append(jnp.where(done, lax.bitcast_convert_type(imp, jnp.int32), -1))
    key = jnp.concatenate(keys, axis=0)
    key_col = key.T
    n_idx = lax.broadcasted_iota(jnp.int32, (n_slc_pad, n_slc_pad), 0)
    m_idx = lax.broadcasted_iota(jnp.int32, (n_slc_pad, n_slc_pad), 1)
    m_first = jnp.where(m_idx < n_idx, 1, 0)
    n_col = lax.broadcasted_iota(jnp.int32, (n_slc_pad, 1), 0).astype(F32)
    slot = lane_o.astype(F32)
    rows = []
    for r in range(NSA_KV * t):
        ahead = key[r:r + 1, :] > (key_col[:, r:r + 1] - m_first)
        rank = jnp.sum(jnp.where(ahead, 1.0, 0.0), axis=1, keepdims=True)
        rows.append(jnp.sum(jnp.where(rank == slot, n_col, 0.0), axis=0, keepdims=True))
    picked = jnp.concatenate(rows, axis=0).astype(jnp.int32)
    sel_ref[0] = jnp.where(lane_o == N_SEL - 1, jnp.concatenate([cur] * NSA_KV, axis=0), picked)


def _nsa_attend_kernel(pt_ref, sel_ref, cache_ref, q_ref, qp_ref, oc_ref, gate_ref, snew_ref, wnew_ref,
                       wst_ref, o_ref, buf, sc, sem, *, t, past_len, layer, w_buf):
    bi = pl.program_id(0)
    slot = bi % 2
    nsel = N_SEL - 1
    n_past_blk = past_len // L_SLC
    per_page = PAGE_SIZE // L_SLC
    nrow = t * NSA_HPG

    def block_of(bb, g, tok, k):
        return jnp.minimum(sel_ref[bb, (g * t + tok) * N_SEL + k], n_past_blk - 1)

    def copy(page, slot_, g, idx):
        return pltpu.make_async_copy(cache_ref.at[layer, page, pl.ds(2, 2), g], buf.at[slot_, idx], sem.at[slot_])

    def issue(bb, slot_):
        for g in range(NSA_KV):
            for tok in range(t):
                for k in range(nsel):
                    page = pt_ref[bb, block_of(bb, g, tok, k) // per_page]
                    copy(page, slot_, g, (g * t + tok) * nsel + k).start()

    @pl.when(bi == 0)
    def _():
        issue(0, 0)

    @pl.when(bi + 1 < pl.num_programs(0))
    def _():
        issue(bi + 1, 1 - slot)

    for g in range(NSA_KV):
        for i in range(t * nsel):
            copy(0, slot, g, g * t * nsel + i).wait()

    row = lax.broadcasted_iota(jnp.int32, (nrow, 1), 0)
    tok_r = row // NSA_HPG
    pos = past_len + tok_r
    lane = lax.broadcasted_iota(jnp.int32, (1, LANES), 1)
    new_bias = jnp.where(lane <= tok_r, 0.0, MASKED)
    wpos = past_len - w_buf + lax.broadcasted_iota(jnp.int32, (1, w_buf), 1)
    win_bias = jnp.where((wpos >= 0) & (wpos <= pos) & (wpos >= pos - WINDOW), 0.0, MASKED)
    for g in range(NSA_KV):
        q = q_ref[0, g]
        qp = qp_ref[0, g]
        glanes = slice(g * HEAD_DIM, (g + 1) * HEAD_DIM)
        m = jnp.full((nrow, 1), MASKED, F32)
        for tok in range(t):
            cur = (past_len + tok) // L_SLC
            for k in range(nsel):
                n = sel_ref[bi, (g * t + tok) * N_SEL + k]
                half = block_of(bi, g, tok, k) % per_page
                valid = (tok_r == tok) & (lane // L_SLC == half) & (n < cur)
                i = tok * nsel + k
                s = _dot(q, buf[slot, g * t * nsel + i, 0].astype(BF16)) + jnp.where(valid, 0.0, MASKED)
                sc[i] = s
                m = jnp.maximum(m, jnp.max(s, axis=1, keepdims=True))
        s_new = _dot_nt(qp, snew_ref[0, :, 0:LANES].astype(BF16)) + new_bias
        m = jnp.maximum(m, jnp.max(s_new, axis=1, keepdims=True))
        p_new = jnp.exp(s_new - m)
        l = jnp.sum(p_new, axis=1, keepdims=True)
        acc = _dot(p_new.astype(BF16), snew_ref[0, :, LANES:2 * LANES].astype(BF16))[:, glanes]
        for i in range(t * nsel):
            p = jnp.exp(sc[i] - m)
            l = l + jnp.sum(p, axis=1, keepdims=True)
            acc = acc + _dot_nt(p.astype(BF16), buf[slot, g * t * nsel + i, 1].astype(BF16))
        o_s = acc * (1.0 / l)
        s_w = _dot(q, wst_ref[0, 0, 0, g].astype(BF16)) + win_bias
        s_wn = _dot_nt(qp, wnew_ref[0, :, 0:LANES].astype(BF16)) + new_bias
        m_w = jnp.maximum(jnp.max(s_w, axis=1, keepdims=True), jnp.max(s_wn, axis=1, keepdims=True))
        e_w = jnp.exp(s_w - m_w)
        e_n = jnp.exp(s_wn - m_w)
        l_w = jnp.sum(e_w, axis=1, keepdims=True) + jnp.sum(e_n, axis=1, keepdims=True)
        acc_w = (_dot_nt(e_w.astype(BF16), wst_ref[0, 0, 1, g].astype(BF16))
                 + _dot(e_n.astype(BF16), wnew_ref[0, :, LANES:2 * LANES].astype(BF16))[:, glanes])
        o_w = acc_w * (1.0 / l_w)
        gate = 1.0 / (1.0 + jnp.exp(-gate_ref[0, g]))
        o_ref[0, g] = gate[:, 0:1] * oc_ref[0, g] + gate[:, 1:2] * o_s + gate[:, 2:3] * o_w


def _nsa_sample(q_rows, rows, win, misc, cw, pw, cache_v, page_table, layer, win_state_v, b, t, past_len):
    t_all = past_len + t
    t_pad = -(-t_all // L_SLC) * L_SLC
    n_cmp = t_pad // L_CMP
    n_tail = (t_pad - past_len) // L_CMP
    n_past = past_len // L_CMP
    w_buf = win_state_v.shape[-1]
    assert past_len % PAGE_SIZE == 0 and t <= L_SLC and n_tail <= LANES and NSA_KV * t * N_SEL <= LANES
    cmp_past = _paged_compress(cache_v, page_table, layer, cw, pw).reshape(b, 2, NSA_KV, n_past, HEAD_DIM)
    tail = jnp.concatenate([rows[:, 0:256].reshape(b, t, 256),
                            jnp.zeros((b, t_pad - t_all, 256), F32)], axis=1).reshape(b * (t_pad - past_len), 256)
    cmp_tail = _compress(tail, tail, 0, 1, 1, b * (t_pad - past_len), cw)
    cmp_tail = cmp_tail.reshape(b, n_tail, 2, NSA_KV, HEAD_DIM).transpose(0, 2, 3, 1, 4)
    cmp_tail = jnp.pad(cmp_tail, ((0, 0), (0, 0), (0, 0), (0, LANES - n_tail), (0, 0)))
    q5 = q_rows.reshape(b, t, NSA_KV, NSA_HPG, HEAD_DIM)
    nrow = NSA_HPG * t
    q_jt = q5.transpose(0, 2, 3, 1, 4).reshape(b, NSA_KV, nrow, HEAD_DIM)
    q_tj = q5.transpose(0, 2, 1, 3, 4).reshape(b, NSA_KV, nrow, HEAD_DIM)
    qp_tj = jnp.zeros((b, NSA_KV, nrow, LANES), BF16)
    for g in range(NSA_KV):
        qp_tj = qp_tj.at[:, g, :, g * HEAD_DIM:(g + 1) * HEAD_DIM].set(q_tj[:, g])
    npad = n_past + LANES
    n_slc_pad = -(-(t_pad // L_SLC) // LANES) * LANES
    nn = jnp.arange(npad)
    pair = ((nn[:, None] // (L_SLC // L_CMP) == jnp.arange(n_slc_pad)[None, :]) & (nn[:, None] < n_cmp)).astype(BF16)
    per_b = lambda shape: pl.BlockSpec(shape, lambda i: (i,) + (0,) * (len(shape) - 1))
    o_c, sel = pl.pallas_call(
        functools.partial(_nsa_select_kernel, t=t, past_len=past_len, n_past=n_past),
        grid=(b,),
        in_specs=[per_b((1, NSA_KV, nrow, HEAD_DIM)), per_b((1, 2, NSA_KV, n_past, HEAD_DIM)),
                  per_b((1, 2, NSA_KV, LANES, HEAD_DIM)), pl.BlockSpec((npad, n_slc_pad), lambda i: (0, 0))],
        out_specs=[per_b((1, NSA_KV, nrow, HEAD_DIM)), per_b((1, NSA_KV * t, LANES))],
        out_shape=[jax.ShapeDtypeStruct((b, NSA_KV, nrow, HEAD_DIM), F32),
                   jax.ShapeDtypeStruct((b, NSA_KV * t, LANES), jnp.int32)],
        compiler_params=_params("arbitrary"),
        name="nsa_select",
    )(q_jt, cmp_past, cmp_tail, pair)
    sel_c = sel[:, :, :N_SEL].reshape(b, NSA_KV * t * N_SEL)
    o_c = o_c.reshape(b, NSA_KV, NSA_HPG, t, HEAD_DIM).transpose(0, 1, 3, 2, 4).reshape(b, NSA_KV, nrow, HEAD_DIM)
    gates = misc[:, 0:3 * NSA_HEADS].reshape(b, t, NSA_KV, NSA_HPG, 3).transpose(0, 2, 1, 3, 4)
    gates = jnp.pad(gates.reshape(b, NSA_KV, nrow, 3), ((0, 0), (0, 0), (0, 0), (0, LANES - 3)))
    pad_rows = lambda z: jnp.pad(z.reshape(b, t, 256), ((0, 0), (0, LANES - t), (0, 0)))
    cache6 = cache_v.reshape(cache_v.shape[0], cache_v.shape[1], 4, NSA_KV, HEAD_DIM, PAGE_SIZE)
    nsel = N_SEL - 1
    pb = lambda shape: pl.BlockSpec(shape, lambda i, pt, sl: (i,) + (0,) * (len(shape) - 1))
    grid_spec = pltpu.PrefetchScalarGridSpec(
        num_scalar_prefetch=2,
        grid=(b,),
        in_specs=[pl.BlockSpec(memory_space=pl.ANY), pb((1, NSA_KV, nrow, HEAD_DIM)), pb((1, NSA_KV, nrow, LANES)),
                  pb((1, NSA_KV, nrow, HEAD_DIM)), pb((1, NSA_KV, nrow, LANES)), pb((1, LANES, 256)),
                  pb((1, LANES, 256)),
                  pl.BlockSpec((1, 1, 2, NSA_KV, HEAD_DIM, w_buf), lambda i, pt, sl: (layer, i, 0, 0, 0, 0))],
        out_specs=pb((1, NSA_KV, nrow, HEAD_DIM)),
        scratch_shapes=[pltpu.VMEM((2, NSA_KV * t * nsel, 2, HEAD_DIM, PAGE_SIZE), F32),
                        pltpu.VMEM((t * nsel, nrow, LANES), F32), pltpu.SemaphoreType.DMA((2,))],
    )
    y = pl.pallas_call(
        functools.partial(_nsa_attend_kernel, t=t, past_len=past_len, layer=layer, w_buf=w_buf),
        grid_spec=grid_spec,
        out_shape=jax.ShapeDtypeStruct((b, NSA_KV, nrow, HEAD_DIM), F32),
        compiler_params=_params("arbitrary"),
        name="nsa_attend",
    )(page_table, sel_c, cache6, q_tj, qp_tj, o_c, gates, pad_rows(rows[:, 256:512]), pad_rows(win), win_state_v)
    y = y.reshape(b, NSA_KV, t, NSA_HPG, HEAD_DIM).transpose(0, 2, 1, 3, 4).reshape(b * t, NSA_WIDTH)
    return y.astype(BF16)


def _outproj_kernel(yp_ref, yn_ref, yg_ref, x_ref, w_ref, g_ref, *rest, n_experts):
    with_router = n_experts > 0
    if with_router:
        r_ref, h_o, hn_o, lg_o = rest
    else:
        h_o, hn_o = rest
    h = (x_ref[...] + _dot(yp_ref[...], w_ref[0:256, :]) + _dot(yn_ref[...], w_ref[256:768, :])
         + _dot(yg_ref[...], w_ref[768:1024, :]))
    h_o[...] = h
    ms = jnp.mean(h * h, axis=-1, keepdims=True)
    hn = (h * lax.rsqrt(ms + EPS) * g_ref[...]).astype(BF16)
    hn_o[...] = hn
    if with_router:
        lane = lax.broadcasted_iota(jnp.int32, (1, LANES), 1)
        lg = jnp.where(lane < n_experts, _dot(hn, r_ref[...]), -jnp.inf)
        v1 = jnp.max(lg, axis=1, keepdims=True)
        i1 = jnp.min(jnp.where(lg == v1, lane, LANES), axis=1, keepdims=True)
        lg2 = jnp.where(lane == i1, -jnp.inf, lg)
        v2 = jnp.max(lg2, axis=1, keepdims=True)
        i2 = jnp.min(jnp.where(lg2 == v2, lane, LANES), axis=1, keepdims=True)
        e2 = jnp.exp(v2 - v1)
        den = 1.0 + e2
        lg_o[...] = jnp.where(lane == 0, 1.0 / den, jnp.where(lane == 1, e2 / den, jnp.where(
            lane == 2, i1.astype(F32), jnp.where(lane == 3, i2.astype(F32), 0.0))))


def _outproj(y_pool, y_nsa, y_gla, x, w_out_l, g_ffn, router):
    n, d = x.shape
    tm = _tile(n, 384)
    with_router = router is not None
    row = lambda c: pl.BlockSpec((tm, c), lambda i: (i, 0))
    full = lambda shape: pl.BlockSpec(shape, lambda i: (0, 0))
    in_specs = [row(256), row(512), row(256), row(d), full((d, d)), full((1, d))]
    args = [y_pool, y_nsa, y_gla, x, w_out_l.astype(BF16), g_ffn[None, :]]
    out_specs = [row(d), row(d)]
    out_shape = [jax.ShapeDtypeStruct((n, d), F32), jax.ShapeDtypeStruct((n, d), BF16)]
    if with_router:
        ne = router.shape[1]
        in_specs.append(full((d, LANES)))
        args.append(jnp.zeros((d, LANES), F32).at[:, :ne].set(router).astype(BF16))
        out_specs.append(row(LANES))
        out_shape.append(jax.ShapeDtypeStruct((n, LANES), F32))
    return pl.pallas_call(
        functools.partial(_outproj_kernel, n_experts=router.shape[1] if with_router else 0),
        grid=(n // tm,), in_specs=in_specs, out_specs=out_specs, out_shape=out_shape,
        compiler_params=_params("arbitrary"), name="outproj",
    )(*args)


def _swiglu_kernel(be_ref, bv_ref, x_ref, gate_ref, w1_ref, w3_ref, w2_ref, *rest, with_res):
    res_ref, o_ref = rest if with_res else (None,) + rest
    i = pl.program_id(0)
    j = pl.program_id(1)
    valid = bv_ref[i] > 0

    @pl.when(jnp.logical_not(valid) & (j == 0))
    def _():
        o_ref[...] = jnp.zeros_like(o_ref)

    @pl.when(valid)
    def _():
        x = x_ref[...]
        a = _dot(x, w1_ref[0])
        c = _dot(x, w3_ref[0])
        hmid = (a * (1.0 / (1.0 + jnp.exp(-a))) * c).astype(BF16)
        y = _dot(hmid, w2_ref[0])

        @pl.when(j == 0)
        def _():
            o_ref[...] = y

        @pl.when(j > 0)
        def _():
            o_ref[...] += y

        @pl.when(j == pl.num_programs(1) - 1)
        def _():
            y_all = o_ref[...] * gate_ref[...]
            o_ref[...] = y_all + res_ref[...] if with_res else y_all


def _swiglu(x, gate, blk_expert, blk_valid, w1, w3, w2, tm, res=None):
    r, d = x.shape
    f = w1.shape[2]
    tf = f // 2 if (f // 2) % LANES == 0 else f
    row = pl.BlockSpec((tm, d), lambda i, j, be, bv: (i, 0))
    in_specs = [row, pl.BlockSpec((tm, 1), lambda i, j, be, bv: (i, 0)),
                pl.BlockSpec((1, d, tf), lambda i, j, be, bv: (be[i], 0, j)),
                pl.BlockSpec((1, d, tf), lambda i, j, be, bv: (be[i], 0, j)),
                pl.BlockSpec((1, tf, d), lambda i, j, be, bv: (be[i], j, 0))]
    args = [blk_expert, blk_valid, x, gate, w1, w3, w2]
    if res is not None:
        in_specs.append(row)
        args.append(res)
    grid_spec = pltpu.PrefetchScalarGridSpec(
        num_scalar_prefetch=2, grid=(r // tm, f // tf), in_specs=in_specs, out_specs=row)
    return pl.pallas_call(
        functools.partial(_swiglu_kernel, with_res=res is not None), grid_spec=grid_spec,
        out_shape=jax.ShapeDtypeStruct((r, d), F32),
        compiler_params=_params("arbitrary", "arbitrary"), name="swiglu",
    )(*args)


def _dense_ffn(h, hn, w1, w3, w2):
    n = h.shape[0]
    tm = _tile(n, 704)
    nblk = n // tm
    return _swiglu(hn, jnp.ones((n, 1), F32), jnp.zeros((nblk,), jnp.int32), jnp.ones((nblk,), jnp.int32),
                   w1[None].astype(BF16), w3[None].astype(BF16), w2[None].astype(BF16), tm, res=h)


def _moe_ffn(h, hn, route, w1, w3, w2, split=None):
    n = h.shape[0]
    ne = w1.shape[0]
    tm = 512
    a = n * TOP_K
    e_a = route[:, TOP_K:2 * TOP_K].astype(jnp.int32).reshape(-1)
    g_a = route[:, 0:TOP_K].reshape(-1)
    tok_a = jnp.repeat(jnp.arange(n), TOP_K)
    onehot = (e_a[:, None] == jnp.arange(ne)[None, :]).astype(jnp.int32)
    csum = jnp.cumsum(onehot, axis=0)
    counts = csum[-1]
    padded = (counts + tm - 1) // tm * tm
    p_end = jnp.cumsum(padded)
    p_start = p_end - padded
    dest = jnp.sum(onehot * (csum - onehot + p_start[None, :]), axis=1)
    nblk = -(-a // tm) + ne
    r = nblk * tm
    row_info = jnp.zeros((r, 2), F32).at[dest].set(jnp.stack([tok_a.astype(F32), g_a], axis=1))
    row_tok = row_info[:, 0].astype(jnp.int32)
    row_gate = row_info[:, 1]
    blk0 = jnp.arange(nblk) * tm
    blk_valid = (blk0 < p_end[-1]).astype(jnp.int32)
    last = jnp.clip(jnp.searchsorted(p_end, p_end[-1] - 1, side='right'), 0, ne - 1)
    blk_expert = jnp.clip(jnp.searchsorted(p_end, blk0, side='right'), 0, ne - 1)
    blk_expert = jnp.where(blk_valid > 0, blk_expert, last).astype(jnp.int32)
    yb = _swiglu(hn[row_tok], row_gate[:, None], blk_expert, blk_valid,
                 w1.astype(BF16), w3.astype(BF16), w2.astype(BF16), tm)
    pos = dest.reshape(n, TOP_K)
    combine = lambda lo, hi: h[lo:hi] + (yb[pos[lo:hi, 0]] + yb[pos[lo:hi, 1]])
    if split is None:
        return combine(0, n)
    return combine(0, split), combine(split, n)


def kernel(x_prompt, x_sample, cache_nsa_kv, state_nsa_win, state_gla, state_pool, page_table,
           norm_mix, norm_ffn, w_in, w_out, pool_w, pool_scale, nsa_q_norm, nsa_k_norm,
           nsa_cmp_w, nsa_cmp_pe, gla_wa2, gla_ba, gla_norm, ffn_w1, ffn_w3, ffn_w2,
           moe_router, moe_w1, moe_w3, moe_w2):
    bp, tp, d = x_prompt.shape
    bs, ts, _ = x_sample.shape
    depth = w_in.shape[0]
    n_pool = cache_nsa_kv.shape[1]
    n_pages = page_table.shape[1]
    past_len = n_pages * PAGE_SIZE
    w_buf = state_nsa_win.shape[2]
    npr = bp * tp
    n_all = npr + bs * ts
    cache_v = cache_nsa_kv.transpose(0, 1, 3, 4, 5, 2).reshape(depth, n_pool, 4 * NSA_KV * HEAD_DIM, PAGE_SIZE)
    win_state_v = state_nsa_win.transpose(0, 1, 3, 4, 5, 2)
    x = jnp.concatenate([x_prompt.reshape(npr, d), x_sample.reshape(bs * ts, d)], axis=0)
    kv_p, kv_s, win_p, win_s, gla_p, gla_s, pool_p, pool_s = [], [], [], [], [], [], [], []
    n_win_p = min(WINDOW, tp)
    for l in range(depth):
        u_pool, q_rows, rows, win, misc, gla_rows = _inproj(x, norm_mix[l], w_in[l], nsa_q_norm[l], nsa_k_norm[l])
        cw = _cmp_weights(nsa_cmp_w[l], nsa_cmp_pe[l], nsa_k_norm[l])
        with_sample = lambda ys: jnp.zeros((n_all, ys.shape[-1]), BF16).at[npr:].set(ys.reshape(bs * ts, -1))
        up = u_pool[:npr].reshape(bp, tp, POOL_WIDTH)
        us = u_pool[npr:].reshape(bs, ts, POOL_WIDTH)
        ys_pool = _pool(us, state_pool[l], pool_w[l], pool_scale[l], past_len, bs, ts)
        y_pool = _pool(u_pool, jnp.zeros((bp, POOL_BUF, POOL_WIDTH), F32), pool_w[l], pool_scale[l], 0, bp, tp,
                       base=with_sample(ys_pool))
        pool_p.append(jnp.concatenate([jnp.zeros((bp, POOL_BUF, POOL_WIDTH), F32), up], axis=1)[:, -POOL_BUF:])
        pool_s.append(jnp.concatenate([state_pool[l], us], axis=1)[:, -POOL_BUF:])
        pw = _paged_cmp_weights(nsa_cmp_w[l], nsa_cmp_pe[l])
        ys_nsa = _nsa_sample(q_rows[npr:], rows[npr:], win[npr:], misc[npr:], cw, pw, cache_v,
                             page_table, l, win_state_v, bs, ts, past_len)
        y_nsa = _nsa_prompt(q_rows, rows, win, misc, cw, bp, tp, with_sample(ys_nsa))
        kv_p.append(rows[:npr].reshape(bp, tp, 4, NSA_KV, HEAD_DIM))
        kv_s.append(rows[npr:].reshape(bs, ts, 4, NSA_KV, HEAD_DIM))
        win_full_p = jnp.concatenate([jnp.zeros((bp, WINDOW, 256), F32), win[:npr].reshape(bp, tp, 256)], axis=1)
        win_p.append(win_full_p[:, -n_win_p:].reshape(bp, n_win_p, 2, NSA_KV, HEAD_DIM))
        win_ext_s = jnp.concatenate([state_nsa_win[l], win[npr:].reshape(bs, ts, 2, NSA_KV, HEAD_DIM)], axis=1)
        win_s.append(win_ext_s[:, -w_buf:])
        ys_gla, ss = _gla(gla_rows[npr:], misc[npr:], state_gla[l], gla_wa2[l], gla_ba[l], gla_norm[l], bs, ts)
        y_gla, sp = _gla(gla_rows, misc, jnp.zeros((bp, GLA_HEADS, GLA_DK, GLA_DV), F32),
                         gla_wa2[l], gla_ba[l], gla_norm[l], bp, tp, base=with_sample(ys_gla))
        gla_p.append(sp.astype(state_gla.dtype))
        gla_s.append(ss.astype(state_gla.dtype))
        i = l // 2
        router = moe_router[i] if l % 2 else None
        res = _outproj(y_pool, y_nsa, y_gla, x, w_out[l], norm_ffn[l], router)
        if l % 2 == 0:
            h, hn = res
            x = _dense_ffn(h, hn, ffn_w1[i], ffn_w3[i], ffn_w2[i])
        else:
            h, hn, route = res
            if l == depth - 1:
                x = _moe_ffn(h, hn, route, moe_w1[i], moe_w3[i], moe_w2[i], split=npr)
            else:
                x = _moe_ffn(h, hn, route, moe_w1[i], moe_w3[i], moe_w2[i])
    x_p, x_s = x if isinstance(x, tuple) else (x[:npr], x[npr:])
    return (x_p.reshape(bp, tp, d), x_s.reshape(bs, ts, d),
            jnp.stack(kv_p), jnp.stack(kv_s), jnp.stack(win_p), jnp.stack(win_s),
            jnp.stack(gla_p), jnp.stack(gla_s), jnp.stack(pool_p), jnp.stack(pool_s))
```

```python
import functools
import math

import jax
import jax.numpy as jnp
from jax import lax
from jax.experimental import pallas as pl
from jax.experimental.pallas import tpu as pltpu

F32 = jnp.float32
BF16 = jnp.bfloat16

EPS = 1e-6
LANES = 128
HEAD_DIM = 64
PAGE_SIZE = 128
POOL_WINDOWS = (2, 4, 8, 16)
POOL_BUF = 15
POOL_WIDTH = 256
NSA_WIDTH = 512
NSA_HEADS = 8
NSA_KV = 2
NSA_HPG = 4
L_CMP = 32
L_SLC = 64
N_SEL = 16
WINDOW = 512
GLA_HEADS = 4
GLA_DK = 32
GLA_DV = 64
GLA_WIDTH = 256
GLA_RANK = 16
GLA_TAU = 16.0
GLA_SUB = 16
TOP_K = 2
N_IN_PAD = 2432
MASKED = -1e30
VMEM_LIMIT = 56 * 1024 * 1024


def _params(*sem):
    return pltpu.CompilerParams(dimension_semantics=sem, vmem_limit_bytes=VMEM_LIMIT)


def _tile(n, target):
    best = None
    for t in range(8, min(n, target) + 1, 8):
        if n % t == 0:
            best = t
    assert best is not None, (n, target)
    return best


def _dot(a, b):
    return jnp.dot(a, b, preferred_element_type=F32)


def _dot_nt(a, b):
    return lax.dot_general(a, b, (((1,), (1,)), ((), ())), preferred_element_type=F32)


def _dot_tn(a, b):
    return lax.dot_general(a, b, (((0,), (0,)), ((), ())), preferred_element_type=F32)


def _split2_dot(a, ones):
    hi = a.astype(BF16)
    lo = (a - hi.astype(F32)).astype(BF16)
    return _dot(hi, ones) + _dot(lo, ones)


def _split3_dot(ones, a):
    a1 = a.astype(BF16)
    r1 = a - a1.astype(F32)
    a2 = r1.astype(BF16)
    a3 = (r1 - a2.astype(F32)).astype(BF16)
    return _dot(ones, a1) + _dot(ones, a2) + _dot(ones, a3)


def _head_rmsnorm(a, gain_row, seg_ones):
    ms = _split2_dot(a * a, seg_ones) * (1.0 / HEAD_DIM)
    return a * lax.rsqrt(ms + EPS) * gain_row


def _masked_softmax(s, mask):
    sm = jnp.where(mask, s, MASKED)
    m = jnp.max(sm, axis=-1, keepdims=True)
    e = jnp.where(mask, jnp.exp(sm - m), 0.0)
    den = jnp.sum(e, axis=-1, keepdims=True)
    return e * (1.0 / jnp.where(den > 0.0, den, 1.0))


def _seg_ones():
    i = jnp.arange(LANES)
    return (i[:, None] // HEAD_DIM == i[None, :] // HEAD_DIM).astype(BF16)


def _inproj_kernel(x_ref, g_ref, w_ref, qg_ref, kg_ref, seg_ref,
                   pool_o, q_o, rows_o, win_o, misc_o, gla_o):
    x = x_ref[...]
    ms = jnp.mean(x * x, axis=-1, keepdims=True)
    xn = (x * lax.rsqrt(ms + EPS) * g_ref[...]).astype(BF16)
    seg = seg_ref[...]

    def mm(c0, c1):
        return _dot(xn, w_ref[:, c0:c1])

    pool_o[...] = mm(0, 256)
    zq = mm(256, 768)
    for c in range(4):
        q_o[:, LANES * c:LANES * (c + 1)] = (
            _head_rmsnorm(zq[:, LANES * c:LANES * (c + 1)], qg_ref[...], seg) * (HEAD_DIM ** -0.5)).astype(BF16)
    zkv = mm(768, 1536)
    rows_o[:, 0:256] = zkv[:, 0:256]
    rows_o[:, 256:384] = _head_rmsnorm(zkv[:, 256:384], kg_ref[1:2, :], seg)
    rows_o[:, 384:512] = zkv[:, 384:512]
    win_o[:, 0:128] = _head_rmsnorm(zkv[:, 512:640], kg_ref[2:3, :], seg)
    win_o[:, 128:256] = zkv[:, 640:768]
    zg = mm(1536, 2432)
    misc_o[...] = zg[:, 0:128]
    gla_o[...] = zg[:, 128:896]


def _pad_w_in(w):
    d = w.shape[0]
    return jnp.concatenate([
        w[:, 0:1560], w[:, 2072:2088], jnp.zeros((d, 88), w.dtype),
        w[:, 1560:2072], w[:, 2088:2344]], axis=1)


def _inproj(x, g_mix, w_in_l, q_gain, k_gain):
    n, d = x.shape
    tm = _tile(n, 384)
    w = _pad_w_in(w_in_l).astype(BF16)
    qg = jnp.tile(q_gain, 2)[None, :]
    kg = jnp.zeros((8, LANES), F32).at[0:3].set(jnp.tile(k_gain, (1, 2)))
    full = lambda shape: pl.BlockSpec(shape, lambda i: (0, 0))
    row = lambda c: pl.BlockSpec((tm, c), lambda i: (i, 0))
    return pl.pallas_call(
        _inproj_kernel,
        grid=(n // tm,),
        in_specs=[row(d), full((1, d)), full((d, N_IN_PAD)), full((1, LANES)),
                  full((8, LANES)), full((LANES, LANES))],
        out_specs=[row(256), row(512), row(512), row(256), row(128), row(768)],
        out_shape=[jax.ShapeDtypeStruct((n, 256), F32), jax.ShapeDtypeStruct((n, 512), BF16),
                   jax.ShapeDtypeStruct((n, 512), F32), jax.ShapeDtypeStruct((n, 256), F32),
                   jax.ShapeDtypeStruct((n, 128), F32), jax.ShapeDtypeStruct((n, 768), F32)],
        compiler_params=_params("arbitrary"),
        name="inproj",
    )(x, g_mix[None, :], w, qg, kg, _seg_ones())


def _pool_kernel(buf_ref, u_ref, w_ref, sc_ref, *rest, pos0, tp):
    o_ref, ext = rest[-2:]
    i = pl.program_id(1)

    @pl.when(i == 0)
    def _():
        ext[0:16, :] = buf_ref[0]

    ext[16:16 + tp, :] = u_ref[...].reshape(tp, POOL_WIDTH)
    u0 = ext[16:16 + tp, :]
    acc = u0
    sums = {}
    for k in range(1, 16):
        acc = acc + ext[16 - k:16 - k + tp, :]
        if k + 1 in POOL_WINDOWS:
            sums[k + 1] = acc
    lane = lax.broadcasted_iota(jnp.int32, (tp, POOL_WIDTH), 1)
    pos = pos0 + i * tp + lax.broadcasted_iota(jnp.int32, (tp, POOL_WIDTH), 0)
    grp = lane // (POOL_WIDTH // len(POOL_WINDOWS))
    total = sums[16]
    wsize = jnp.full((tp, POOL_WIDTH), 16, jnp.int32)
    for gi, wz in enumerate(POOL_WINDOWS[:-1]):
        total = jnp.where(grp == gi, sums[wz], total)
        wsize = jnp.where(grp == gi, wz, wsize)
    cnt = jnp.minimum(wsize, pos + 1).astype(F32)
    dlt = total / cnt - u0
    o_ref[...] = (_dot(dlt.astype(BF16), w_ref[...]) * sc_ref[...]).astype(o_ref.dtype).reshape(o_ref.shape)
    if tp >= 16:
        ext[0:16, :] = ext[tp:tp + 16, :]


def _aliased_base(base, n_in):
    if base is None:
        return [], [], {}
    return [pl.BlockSpec(memory_space=pl.ANY)], [base], {n_in: 0}


def _pool(u, buf, pool_w_l, pool_scale_l, pos0, b, t):
    in_place = u.ndim == 2
    base_specs, base_args, alias = _aliased_base(u if in_place else None, 4)
    c = u.shape[-1]
    tp = _tile(t, 512) if t >= 8 else t
    nt = t // tp
    buf16 = jnp.concatenate([jnp.zeros((b, 1, c), F32), buf.astype(F32)], axis=1)
    gw = c // len(POOL_WINDOWS)
    wbd = jnp.zeros((c, c), F32)
    for gi in range(len(POOL_WINDOWS)):
        wbd = wbd.at[gi * gw:(gi + 1) * gw, gi * gw:(gi + 1) * gw].set(pool_w_l[gi])
    if in_place:
        rows = pl.BlockSpec((tp, c), lambda i, j: (i * nt + j, 0))
    else:
        rows = pl.BlockSpec((1, tp, c), lambda i, j: (i, j, 0))
    return pl.pallas_call(
        functools.partial(_pool_kernel, pos0=pos0, tp=tp),
        grid=(b, nt),
        in_specs=[pl.BlockSpec((1, 16, c), lambda i, j: (i, 0, 0)), rows,
                  pl.BlockSpec((c, c), lambda i, j: (0, 0)),
                  pl.BlockSpec((1, c), lambda i, j: (0, 0))] + base_specs,
        out_specs=rows,
        out_shape=jax.ShapeDtypeStruct(u.shape, F32 if in_place else BF16),
        scratch_shapes=[pltpu.VMEM((16 + tp, c), F32)],
        input_output_aliases=alias,
        compiler_params=_params("arbitrary", "arbitrary"),
        name="pool",
    )(buf16, u, wbd.astype(BF16), pool_scale_l[None, :], *base_args)


def _gla_kernel(gla_ref, misc_ref, s0_ref, wa_ref, ba_ref, og_ref, seg_ref, eb_ref, mk_ref, *rest, tg, t_valid):
    o_ref, sT_ref, st, qs, ks, bs, qts, kts, vs, os_, dls, us, ss = rest[-13:]
    i = pl.program_id(1)
    c = GLA_SUB
    nsub = tg // c

    @pl.when(i == 0)
    def _():
        st[...] = s0_ref[0]

    gl = gla_ref[...]
    q = gl[:, 0:128] * (GLA_DK ** -0.5)
    k = gl[:, 128:256]
    v = gl[:, 256:512]
    r = gl[:, 512:768]
    x = _dot(misc_ref[...].astype(BF16), wa_ref[...]) + ba_ref[...]
    la = (jnp.minimum(x, 0.0) - jnp.log1p(jnp.exp(-jnp.abs(x)))) * (1.0 / GLA_TAU)
    row = lax.broadcasted_iota(jnp.int32, (tg, LANES), 0)
    if t_valid is not None:
        la = jnp.where(i * tg + row < t_valid, la, 0.0)
    rr = lax.broadcasted_iota(jnp.int32, (tg, tg), 0)
    cc = lax.broadcasted_iota(jnp.int32, (tg, tg), 1)
    same = (rr // c) == (cc // c)
    tri = (same & (cc <= rr)).astype(BF16)
    allo = same.astype(BF16)
    b = _split3_dot(tri, la)
    blast = _split3_dot(allo, la)
    qs[...] = q
    ks[...] = k
    bs[...] = b
    qts[...] = (q * jnp.exp(b)).astype(BF16)
    kts[...] = (k * jnp.exp(blast - b)).astype(BF16)
    vs[...] = v
    eb = eb_ref[...]
    mk = mk_ref[...]
    tt = lax.broadcasted_iota(jnp.int32, (c, LANES), 0)

    dls[...] = jnp.exp(blast)

    def local(j, carry):
        r0 = pl.multiple_of(j * c, c)
        qi = qs[pl.ds(r0, c), :]
        ki = ks[pl.ds(r0, c), :]
        bi = bs[pl.ds(r0, c), :]
        vi = vs[pl.ds(r0, c), :]
        parts = []
        for s in range(c):
            dec = jnp.exp(jnp.minimum(bi - bi[s:s + 1, :], 0.0))
            parts.append(jnp.where(tt >= s, qi * ki[s:s + 1, :] * dec, 0.0))
        p_all = jnp.concatenate(parts, axis=0).astype(BF16)
        a_all = _dot(p_all, eb)
        o_diag = a_all[0:c, :] * vi[0:1, :]
        for s in range(1, c):
            o_diag = o_diag + a_all[s * c:(s + 1) * c, :] * vi[s:s + 1, :]
        os_[pl.ds(r0, c), :] = o_diag
        us[j] = _dot_tn(vi.astype(BF16), kts[pl.ds(r0, c), :]) * mk
        return carry

    def grouped(body, group):
        group = math.gcd(nsub, group)

        def trip(jj, carry):
            for u in range(group):
                body(jj * group + u, carry)
            return carry

        lax.fori_loop(0, nsub // group, trip, 0)

    grouped(local, 8)

    def recur(j, carry):
        s_t = st[...]
        ss[j] = s_t.astype(BF16)
        st[...] = s_t * dls[pl.ds(pl.multiple_of(j * c, c), 1), :] + us[j]
        return carry

    lax.fori_loop(0, nsub, recur, 0)

    def inter(j, carry):
        r0 = pl.multiple_of(j * c, c)
        os_[pl.ds(r0, c), :] += _dot_nt(qts[pl.ds(r0, c), :], ss[j])
        return carry

    grouped(inter, 8)

    o = os_[...]
    seg = seg_ref[...]
    og = og_ref[...]
    sil = r * (1.0 / (1.0 + jnp.exp(-r)))
    for h in range(2):
        sl = slice(h * LANES, (h + 1) * LANES)
        o_ref[:, sl] = (_head_rmsnorm(o[:, sl], og[:, sl], seg) * sil[:, sl]).astype(o_ref.dtype)
    sT_ref[0] = st[...]


def _gla(gla_rows, misc_rows, s0, wa2, ba, o_gain, b, t, base=None):
    base_specs, base_args, alias = _aliased_base(base, 9)
    t_valid = None
    if t % GLA_SUB:
        t_valid = t
        tp = -(-t // GLA_SUB) * GLA_SUB
        pad = lambda z: jnp.pad(z.reshape(b, t, -1), ((0, 0), (0, tp - t), (0, 0))).reshape(b * tp, -1)
        gla_rows, misc_rows = pad(gla_rows), pad(misc_rows)
    else:
        tp = t
    tg = _tile(tp, 256)
    assert tg % GLA_SUB == 0
    nt = tp // tg
    kk = GLA_HEADS * GLA_DK
    vv = GLA_HEADS * GLA_DV
    ki = jnp.arange(kk)
    vi = jnp.arange(vv)
    head_eq = (vi[:, None] // GLA_DV == ki[None, :] // GLA_DK)
    mk = head_eq.astype(F32)
    eb = head_eq.T.astype(BF16)
    s0t = jnp.einsum('bhkv,hg->bhvgk', s0.astype(F32), jnp.eye(GLA_HEADS, dtype=F32)).reshape(b, vv, kk)
    wa = jnp.zeros((LANES, kk), F32).at[24:24 + GLA_RANK].set(wa2).astype(BF16)
    full = lambda shape: pl.BlockSpec(shape, lambda i, j: (0,) * len(shape))
    rows = lambda cdim: pl.BlockSpec((tg, cdim), lambda i, j: (i * nt + j, 0))
    o, s_t = pl.pallas_call(
        functools.partial(_gla_kernel, tg=tg, t_valid=t_valid),
        grid=(b, nt),
        in_specs=[rows(768), rows(128), pl.BlockSpec((1, vv, kk), lambda i, j: (i, 0, 0)),
                  full((LANES, kk)), full((1, kk)), full((1, vv)), full((LANES, LANES)),
                  full((kk, vv)), full((vv, kk))] + base_specs,
        input_output_aliases=alias,
        out_specs=[rows(vv), pl.BlockSpec((1, vv, kk), lambda i, j: (i, 0, 0))],
        out_shape=[jax.ShapeDtypeStruct((gla_rows.shape[0], vv), BF16), jax.ShapeDtypeStruct((b, vv, kk), F32)],
        scratch_shapes=[pltpu.VMEM((vv, kk), F32), pltpu.VMEM((tg, kk), F32), pltpu.VMEM((tg, kk), F32),
                        pltpu.VMEM((tg, kk), F32), pltpu.VMEM((tg, kk), BF16), pltpu.VMEM((tg, kk), BF16),
                        pltpu.VMEM((tg, vv), F32), pltpu.VMEM((tg, vv), F32), pltpu.VMEM((tg, kk), F32),
                        pltpu.VMEM((tg // GLA_SUB, vv, kk), F32), pltpu.VMEM((tg // GLA_SUB, vv, kk), BF16)],
        compiler_params=_params("arbitrary", "arbitrary"),
        name="gla",
    )(gla_rows, misc_rows, s0t, wa, ba[None, :], jnp.tile(o_gain, GLA_HEADS)[None, :], _seg_ones(), eb, mk,
      *base_args)
    if tp != t:
        o = o.reshape(b, tp, vv)[:, :t].reshape(b * t, vv)
    s5 = s_t.reshape(b, GLA_HEADS, GLA_DV, GLA_HEADS, GLA_DK)
    s_new = jnp.einsum('bhvgk,hg->bhkv', s5, jnp.eye(GLA_HEADS, dtype=F32))
    return o, s_new


def _compress_rows(read, nb, pe_ref, wk_ref, wv_ref, kg_ref, seg_ref):
    acck = jnp.zeros((nb, LANES), F32)
    accv = jnp.zeros((nb, LANES), F32)
    for j in range(L_CMP):
        xk = (read(0, j) + pe_ref[0, j:j + 1, :]).astype(BF16)
        xv = (read(1, j) + pe_ref[1, j:j + 1, :]).astype(BF16)
        acck = acck + _dot(xk, wk_ref[j])
        accv = accv + _dot(xv, wv_ref[j])
    kc = _head_rmsnorm(acck, kg_ref[0:1, :], seg_ref[...])
    return kc, accv


def _compress_kernel(rk_ref, rv_ref, pe_ref, wk_ref, wv_ref, kg_ref, seg_ref, o_ref, *, nb):
    refs = (rk_ref, rv_ref)
    kc, vc = _compress_rows(lambda kind, j: refs[kind][pl.ds(j, nb, stride=L_CMP), :], nb,
                            pe_ref, wk_ref, wv_ref, kg_ref, seg_ref)
    o_ref[0, :, 0:LANES] = kc
    o_ref[0, :, LANES:2 * LANES] = vc


def _cmp_weights(cmp_w, cmp_pe, k_gain):
    def bd(w):
        z = jnp.zeros_like(w)
        return jnp.concatenate([jnp.concatenate([w, z], axis=2), jnp.concatenate([z, w], axis=2)], axis=1)
    pe = jnp.tile(cmp_pe, (1, 1, 2))
    kg = jnp.zeros((8, LANES), F32).at[0].set(jnp.tile(k_gain[0], 2))
    return pe, bd(cmp_w[0]).astype(BF16), bd(cmp_w[1]).astype(BF16), kg


def _compress(rk, rv, k_col, v_col, b, tp, cw):
    nb = tp // L_CMP
    pe, wk, wv, kg = cw
    full = lambda shape: pl.BlockSpec(shape, lambda i: (0,) * len(shape))
    return pl.pallas_call(
        functools.partial(_compress_kernel, nb=nb),
        grid=(b,),
        in_specs=[pl.BlockSpec((tp, LANES), lambda i: (i, k_col)), pl.BlockSpec((tp, LANES), lambda i: (i, v_col)),
                  full((2, L_CMP, LANES)), full((L_CMP, LANES, LANES)), full((L_CMP, LANES, LANES)),
                  full((8, LANES)), full((LANES, LANES))],
        out_specs=pl.BlockSpec((1, nb, 256), lambda i: (i, 0, 0)),
        out_shape=jax.ShapeDtypeStruct((b, nb, 256), F32),
        compiler_params=_params("arbitrary"),
        name="compress",
    )(rk, rv, pe, wk, wv, kg, _seg_ones())


SLAB_PITCH = 2 * NSA_KV * HEAD_DIM + 8


def _paged_compress_kernel(pt_ref, cache_ref, pe_ref, m_ref, kg_ref, seg_ref, o_ref,
                           slab, sem, *, pages, n_slab, layer):
    bi = pl.program_id(0)
    si = pl.program_id(1)
    step = bi * n_slab + si
    nsteps = pl.num_programs(0) * n_slab
    slot = step % 2
    rows_cmp = 2 * NSA_KV * HEAD_DIM

    def copy(page, slot_, p):
        return pltpu.make_async_copy(cache_ref.at[layer, page, pl.ds(0, rows_cmp), :],
                                     slab.at[slot_, pl.ds(p * SLAB_PITCH, rows_cmp), :], sem.at[slot_])

    def issue(bb, ss, slot_):
        for p in range(pages):
            copy(pt_ref[bb, ss * pages + p], slot_, p).start()

    @pl.when(step == 0)
    def _():
        issue(0, 0, 0)

    @pl.when(step + 1 < nsteps)
    def _():
        nxt = step + 1
        issue(nxt // n_slab, nxt % n_slab, 1 - slot)

    for p in range(pages):
        copy(0, slot, p).wait()

    def rows(r0):
        return slab[slot, pl.ds(r0, pages, stride=SLAB_PITCH), :]

    for c in range(2):
        acc = jnp.zeros((NSA_KV * pages, 2 * LANES), F32)
        for dp in range(HEAD_DIM // 2):
            parts = []
            for g in range(NSA_KV):
                r0 = (c * NSA_KV + g) * HEAD_DIM + 2 * dp
                parts.append(jnp.concatenate([rows(r0), rows(r0 + 1)], axis=1))
            a = jnp.concatenate(parts, axis=0) + pe_ref[c, dp:dp + 1, :]
            acc = acc + _dot(a.astype(BF16), m_ref[c, dp])
        for g in range(NSA_KV):
            blk = acc[g * pages:(g + 1) * pages, :]
            if c == 0:
                for h in range(2):
                    sl = slice(h * LANES, (h + 1) * LANES)
                    o_ref[0, c, g, :, sl] = _head_rmsnorm(blk[:, sl], kg_ref[0:1, :], seg_ref[...])
            else:
                o_ref[0, c, g] = blk


def _paged_cmp_weights(cmp_w, cmp_pe):
    nblk = PAGE_SIZE // L_CMP
    k6 = jnp.einsum('nm,cjde->cdnjme', jnp.eye(nblk, dtype=F32), cmp_w)
    m = k6.reshape(2, HEAD_DIM // 2, 2 * PAGE_SIZE, nblk * HEAD_DIM).astype(BF16)
    pe = jnp.tile(cmp_pe.transpose(0, 2, 1), (1, 1, nblk)).reshape(2, HEAD_DIM // 2, 2 * PAGE_SIZE)
    return pe, m


def _paged_compress(cache_v, page_table, layer, cw, pw):
    b, n_pages = page_table.shape
    pages = math.gcd(n_pages, 64)
    n_slab = n_pages // pages
    pe, m = pw
    kg = cw[3]
    nblk = PAGE_SIZE // L_CMP
    full = lambda shape: pl.BlockSpec(shape, lambda i, j, pt: (0,) * len(shape))
    grid_spec = pltpu.PrefetchScalarGridSpec(
        num_scalar_prefetch=1,
        grid=(b, n_slab),
        in_specs=[pl.BlockSpec(memory_space=pl.ANY), full(pe.shape), full(m.shape), full((8, LANES)),
                  full((LANES, LANES))],
        out_specs=pl.BlockSpec((1, 2, NSA_KV, pages, nblk * HEAD_DIM), lambda i, j, pt: (i, 0, 0, j, 0)),
        scratch_shapes=[pltpu.VMEM((2, pages * SLAB_PITCH, LANES), F32), pltpu.SemaphoreType.DMA((2,))],
    )
    return pl.pallas_call(
        functools.partial(_paged_compress_kernel, pages=pages, n_slab=n_slab, layer=layer),
        grid_spec=grid_spec,
        out_shape=jax.ShapeDtypeStruct((b, 2, NSA_KV, n_pages, nblk * HEAD_DIM), F32),
        compiler_params=_params("arbitrary", "arbitrary"),
        name="paged_compress",
    )(page_table, cache_v, pe, m, kg, _seg_ones())


def _nsa_prompt_kernel(q_ref, kc_ref, vc_ref, ks_ref, vs_ref, kw_ref, vw_ref, gate_ref, base_ref,
                       o_ref, ksb, vsb, kwb, vwb, *, tq, ck):
    i = pl.program_id(1)
    t0 = i * tq

    n_slc = LANES // 2
    t_keys = ks_ref.shape[0]

    @pl.when(i == 0)
    def _():
        ksb[:, 0:LANES] = ks_ref[...].astype(BF16)
        kblk = lax.broadcasted_iota(jnp.int32, (t_keys, LANES), 0) // L_SLC
        klane = lax.broadcasted_iota(jnp.int32, (t_keys, LANES), 1)
        ksb[:, LANES:2 * LANES] = jnp.where(kblk == klane, 1.0, 0.0).astype(BF16)
        vsb[...] = vs_ref[...].astype(BF16)
        kwb[...] = kw_ref[...].astype(BF16)
        vwb[...] = vw_ref[...].astype(BF16)

    row_t = t0 + lax.broadcasted_iota(jnp.int32, (tq, 1), 0)
    col_t = t0 + lax.broadcasted_iota(jnp.int32, (1, tq), 1)
    lane = lax.broadcasted_iota(jnp.int32, (1, LANES), 1)
    crow = lax.broadcasted_iota(jnp.int32, (LANES, 1), 0)
    nat = jnp.where(crow < n_slc, 2 * crow, 2 * (crow - n_slc) + 1)
    cmp_ok = ((nat + 1) * L_CMP - 1) <= col_t
    blk = lax.broadcasted_iota(jnp.int32, (n_slc, 1), 0)
    cur = col_t // L_SLC
    done = blk < cur
    gx = gate_ref[...]
    gates = 1.0 / (1.0 + jnp.exp(-gx))
    n_full = t0 // ck
    kstart = pl.multiple_of(jnp.maximum(t0 - WINDOW, 0), tq)
    wlen = WINDOW + tq
    wpos = kstart + lax.broadcasted_iota(jnp.int32, (1, wlen), 1)
    win_bias = jnp.where((wpos <= row_t) & (wpos >= row_t - WINDOW), 0.0, MASKED)

    for g in range(NSA_KV):
        in_g = (lane // HEAD_DIM) == g
        heads = []
        for j in range(NSA_HPG):
            h = g * NSA_HPG + j
            qh = q_ref[:, (h // 2) * LANES:(h // 2 + 1) * LANES].astype(F32)
            if h % 2 != g:
                qh = pltpu.roll(qh, HEAD_DIM, axis=1)
            heads.append(jnp.where(in_g, qh, 0.0).astype(BF16))
        qg = jnp.concatenate(heads, axis=0)
        s_c = _dot_nt(kc_ref[0], qg)
        imp = jnp.zeros((LANES, tq), F32)
        o_c = []
        for j in range(NSA_HPG):
            sj = jnp.where(cmp_ok, s_c[:, j * tq:(j + 1) * tq], MASKED)
            ej = jnp.where(cmp_ok, jnp.exp(sj - jnp.max(sj, axis=0, keepdims=True)), 0.0)
            den = jnp.sum(ej, axis=0, keepdims=True)
            pj = ej * (1.0 / jnp.where(den > 0.0, den, 1.0))
            imp = imp + pj
            o_c.append(_dot_tn(pj.astype(BF16), vc_ref[0]))
        imp = imp[0:n_slc] + imp[n_slc:LANES]
        key = jnp.where(done, lax.bitcast_convert_type(imp, jnp.int32), -1)
        key_m1 = key - 1
        rank = jnp.zeros((n_slc, tq), jnp.int32)
        for r in range(1, n_slc):
            vm = pltpu.roll(key, r, axis=0)
            rank = rank + jnp.where(vm > jnp.where(blk >= r, key_m1, key), 1, 0)
        sel = (done & (rank < N_SEL - 1)) | (blk == cur)
        sel_bias = jnp.concatenate([jnp.where(sel, 0.0, MASKED), jnp.zeros((n_slc, tq), F32)], axis=0)
        sel_bias = sel_bias.T.astype(BF16)
        qx = jnp.concatenate([qg, jnp.concatenate([sel_bias] * NSA_HPG, axis=0)], axis=1)

        def chunk(c, carry, diagonal):
            m, l, acc = carry
            k0 = pl.multiple_of(c * ck, ck)
            sm = _dot_nt(qx, ksb[pl.ds(k0, ck), :]).reshape(NSA_HPG, tq, ck)
            if diagonal:
                kpos = k0 + lax.broadcasted_iota(jnp.int32, (1, ck), 1)
                sm = sm + jnp.where(kpos <= row_t, 0.0, MASKED)[None]
            m_new = jnp.maximum(m, jnp.max(sm, axis=-1, keepdims=True))
            p = jnp.exp(sm - m_new)
            alpha = jnp.exp(m - m_new)
            l = alpha * l + jnp.sum(p, axis=-1, keepdims=True)
            pv = _dot(p.reshape(NSA_HPG * tq, ck).astype(BF16), vsb[pl.ds(k0, ck), :])
            acc = alpha * acc + pv.reshape(NSA_HPG, tq, LANES)
            return m_new, l, acc

        m0 = jnp.full((NSA_HPG, tq, 1), MASKED, F32)
        l0 = jnp.zeros((NSA_HPG, tq, 1), F32)
        a0 = jnp.zeros((NSA_HPG, tq, LANES), F32)
        carry = lax.fori_loop(0, n_full, functools.partial(chunk, diagonal=False), (m0, l0, a0))
        _, l_s, acc_s = chunk(n_full, carry, True)
        o_s = acc_s * (1.0 / jnp.where(l_s > 0.0, l_s, 1.0))
        s_w = _dot_nt(qg, kwb[pl.ds(kstart, wlen), :]).reshape(NSA_HPG, tq, wlen) + win_bias[None]
        e_w = jnp.exp(s_w - jnp.max(s_w, axis=-1, keepdims=True))
        o_w = _dot(e_w.reshape(NSA_HPG * tq, wlen).astype(BF16), vwb[pl.ds(kstart, wlen), :])
        o_w = o_w.reshape(NSA_HPG, tq, LANES) * (1.0 / jnp.sum(e_w, axis=-1, keepdims=True))
        for pr in range(NSA_HPG // 2):
            pair = []
            for jj in range(2):
                j = 2 * pr + jj
                h = g * NSA_HPG + j
                o = (gates[:, 3 * h:3 * h + 1] * o_c[j] + gates[:, 3 * h + 1:3 * h + 2] * o_s[j]
                     + gates[:, 3 * h + 2:3 * h + 3] * o_w[j])
                pair.append(o if jj == g else pltpu.roll(o, HEAD_DIM, axis=1))
            col = (g * (NSA_HPG // 2) + pr) * LANES
            o_ref[:, col:col + LANES] = jnp.where(lane < HEAD_DIM, pair[0], pair[1]).astype(o_ref.dtype)


def _nsa_prompt(q_rows, rows, win, misc, cw, b, t):
    assert t // L_CMP == LANES and t % L_SLC == 0, "prompt kernel is laid out for 128 compressed blocks"
    tq = 256
    ck = 512
    assert ck % tq == 0 and t % ck == 0
    cmp = _compress(rows, rows, 0, 1, b, t, cw)
    order = jnp.concatenate([jnp.arange(0, LANES, 2), jnp.arange(1, LANES, 2)])
    cmp = cmp[:, order].astype(BF16)
    kc, vc = cmp[:, :, 0:LANES], cmp[:, :, LANES:2 * LANES]
    nt = t // tq
    col = lambda c: pl.BlockSpec((t, LANES), lambda i, j: (i, c))
    return pl.pallas_call(
        functools.partial(_nsa_prompt_kernel, tq=tq, ck=ck),
        grid=(b, nt),
        in_specs=[pl.BlockSpec((tq, NSA_WIDTH), lambda i, j: (i * nt + j, 0)),
                  pl.BlockSpec((1, LANES, LANES), lambda i, j: (i, 0, 0)),
                  pl.BlockSpec((1, LANES, LANES), lambda i, j: (i, 0, 0)),
                  col(2), col(3), col(0), col(1),
                  pl.BlockSpec((tq, LANES), lambda i, j: (i * nt + j, 0)),
                  pl.BlockSpec(memory_space=pl.ANY)],
        out_specs=pl.BlockSpec((tq, NSA_WIDTH), lambda i, j: (i * nt + j, 0)),
        out_shape=jax.ShapeDtypeStruct((q_rows.shape[0], NSA_WIDTH), BF16),
        scratch_shapes=[pltpu.VMEM((t, 2 * LANES), BF16)] + [pltpu.VMEM((t, LANES), BF16)] * 3,
        input_output_aliases={8: 0},
        compiler_params=_params("arbitrary", "arbitrary"),
        name="nsa_prompt",
    )(q_rows, kc, vc, rows, rows, win, win, misc, q_rows)


def _split3_dot_r(a, ones):
    a1 = a.astype(BF16)
    r1 = a - a1.astype(F32)
    a2 = r1.astype(BF16)
    a3 = (r1 - a2.astype(F32)).astype(BF16)
    return _dot(a1, ones) + _dot(a2, ones) + _dot(a3, ones)


def _nsa_select_kernel(q_ref, cp_ref, ct_ref, pair_ref, oc_ref, sel_ref, *, t, past_len, n_past):
    nrow = NSA_HPG * t
    npad = n_past + LANES
    n_slc_pad = pair_ref.shape[1]
    row = lax.broadcasted_iota(jnp.int32, (nrow, 1), 0)
    pos = past_len + row % t
    lane_c = lax.broadcasted_iota(jnp.int32, (1, npad), 1)
    cmp_ok = ((lane_c + 1) * L_CMP - 1) <= pos
    cur = (past_len + lax.broadcasted_iota(jnp.int32, (t, 1), 0)) // L_SLC
    lane_s = lax.broadcasted_iota(jnp.int32, (1, n_slc_pad), 1)
    lane_o = lax.broadcasted_iota(jnp.int32, (1, LANES), 1)
    done = lane_s < cur
    keys = []
    for g in range(NSA_KV):
        q = q_ref[0, g]
        s = jnp.concatenate([_dot_nt(q, cp_ref[0, 0, g].astype(BF16)),
                             _dot_nt(q, ct_ref[0, 0, g].astype(BF16))], axis=1)
        p = _masked_softmax(s, cmp_ok)
        pb = p.astype(BF16)
        oc_ref[0, g] = (_dot(pb[:, :n_past], cp_ref[0, 1, g].astype(BF16))
                        + _dot(pb[:, n_past:], ct_ref[0, 1, g].astype(BF16)))
        imp_c = p[0:t]
        for j in range(1, NSA_HPG):
            imp_c = imp_c + p[j * t:(j + 1) * t]
        imp = _split3_dot_r(imp_c, pair_ref[...])
        keys.append(jnp.where(done, lax.bitcast_convert_type(imp, jnp.int32), -1))
    key = jnp.concatenate(keys, axis=0)
    key_col = key.T
    n_idx = lax.broadcasted_iota(jnp.int32, (n_slc_pad, n_slc_pad), 0)
    m_idx = lax.broadcasted_iota(jnp.int32, (n_slc_pad, n_slc_pad), 1)
    m_first = jnp.where(m_idx < n_idx, 1, 0)
    n_col = lax.broadcasted_iota(jnp.int32, (n_slc_pad, 1), 0).astype(F32)
    slot = lane_o.astype(F32)
    rows = []
    for r in range(NSA_KV * t):
        ahead = key[r:r + 1, :] > (key_col[:, r:r + 1] - m_first)
        rank = jnp.sum(jnp.where(ahead, 1.0, 0.0), axis=1, keepdims=True)
        rows.append(jnp.sum(jnp.where(rank == slot, n_col, 0.0), axis=0, keepdims=True))
    picked = jnp.concatenate(rows, axis=0).astype(jnp.int32)
    sel_ref[0] = jnp.where(lane_o == N_SEL - 1, jnp.concatenate([cur] * NSA_KV, axis=0), picked)


def _nsa_attend_kernel(pt_ref, sel_ref, cache_ref, q_ref, qp_ref, oc_ref, gate_ref, new_ref,
                       wst_ref, o_ref, buf, sc, sem, *, t, past_len, layer, w_buf):
    bi = pl.program_id(0)
    slot = bi % 2
    nsel = N_SEL - 1
    n_past_blk = past_len // L_SLC
    per_page = PAGE_SIZE // L_SLC
    nrow = t * NSA_HPG

    def block_of(bb, g, tok, k):
        return jnp.minimum(sel_ref[bb, (g * t + tok) * N_SEL + k], n_past_blk - 1)

    def copy(page, slot_, g, idx):
        return pltpu.make_async_copy(cache_ref.at[layer, page, pl.ds(2, 2), g], buf.at[slot_, idx], sem.at[slot_])

    def issue(bb, slot_):
        for g in range(NSA_KV):
            for tok in range(t):
                for k in range(nsel):
                    page = pt_ref[bb, block_of(bb, g, tok, k) // per_page]
                    copy(page, slot_, g, (g * t + tok) * nsel + k).start()

    @pl.when(bi == 0)
    def _():
        issue(0, 0)

    @pl.when(bi + 1 < pl.num_programs(0))
    def _():
        issue(bi + 1, 1 - slot)

    for g in range(NSA_KV):
        for i in range(t * nsel):
            copy(0, slot, g, g * t * nsel + i).wait()

    row = lax.broadcasted_iota(jnp.int32, (nrow, 1), 0)
    tok_r = row // NSA_HPG
    pos = past_len + tok_r
    lane = lax.broadcasted_iota(jnp.int32, (1, LANES), 1)
    new_bias = jnp.where(lane <= tok_r, 0.0, MASKED)
    wpos = past_len - w_buf + lax.broadcasted_iota(jnp.int32, (1, w_buf), 1)
    win_bias = jnp.where((wpos >= 0) & (wpos <= pos) & (wpos >= pos - WINDOW), 0.0, MASKED)
    for g in range(NSA_KV):
        q = q_ref[0, g]
        qp = qp_ref[0, g]
        glanes = slice(g * HEAD_DIM, (g + 1) * HEAD_DIM)
        m = jnp.full((nrow, 1), MASKED, F32)
        for tok in range(t):
            cur = (past_len + tok) // L_SLC
            for k in range(nsel):
                n = sel_ref[bi, (g * t + tok) * N_SEL + k]
                half = block_of(bi, g, tok, k) % per_page
                valid = (tok_r == tok) & (lane // L_SLC == half) & (n < cur)
                i = tok * nsel + k
                s = _dot(q, buf[slot, g * t * nsel + i, 0].astype(BF16)) + jnp.where(valid, 0.0, MASKED)
                sc[i] = s
                m = jnp.maximum(m, jnp.max(s, axis=1, keepdims=True))
        s_new = _dot_nt(qp, new_ref[0, :, 0:LANES].astype(BF16)) + new_bias
        m = jnp.maximum(m, jnp.max(s_new, axis=1, keepdims=True))
        p_new = jnp.exp(s_new - m)
        l = jnp.sum(p_new, axis=1, keepdims=True)
        acc = _dot(p_new.astype(BF16), new_ref[0, :, LANES:2 * LANES].astype(BF16))[:, glanes]
        for i in range(t * nsel):
            p = jnp.exp(sc[i] - m)
            l = l + jnp.sum(p, axis=1, keepdims=True)
            acc = acc + _dot_nt(p.astype(BF16), buf[slot, g * t * nsel + i, 1].astype(BF16))
        o_s = acc * (1.0 / l)
        s_w = _dot(q, wst_ref[0, 0, 0, g].astype(BF16)) + win_bias
        s_wn = _dot_nt(qp, new_ref[0, :, 2 * LANES:3 * LANES].astype(BF16)) + new_bias
        m_w = jnp.maximum(jnp.max(s_w, axis=1, keepdims=True), jnp.max(s_wn, axis=1, keepdims=True))
        e_w = jnp.exp(s_w - m_w)
        e_n = jnp.exp(s_wn - m_w)
        l_w = jnp.sum(e_w, axis=1, keepdims=True) + jnp.sum(e_n, axis=1, keepdims=True)
        acc_w = (_dot_nt(e_w.astype(BF16), wst_ref[0, 0, 1, g].astype(BF16))
                 + _dot(e_n.astype(BF16), new_ref[0, :, 3 * LANES:4 * LANES].astype(BF16))[:, glanes])
        o_w = acc_w * (1.0 / l_w)
        gate = 1.0 / (1.0 + jnp.exp(-gate_ref[0, g]))
        o_ref[0, g] = gate[:, 0:1] * oc_ref[0, g] + gate[:, 1:2] * o_s + gate[:, 2:3] * o_w


def _nsa_sample(q_rows, rows, win, misc, cw, pw, cache_v, page_table, layer, win_state_v, b, t, past_len):
    t_all = past_len + t
    t_pad = -(-t_all // L_SLC) * L_SLC
    n_cmp = t_pad // L_CMP
    n_tail = (t_pad - past_len) // L_CMP
    n_past = past_len // L_CMP
    w_buf = win_state_v.shape[-1]
    assert past_len % PAGE_SIZE == 0 and t <= L_SLC and n_tail <= LANES and NSA_KV * t * N_SEL <= LANES
    cmp_past = _paged_compress(cache_v, page_table, layer, cw, pw).reshape(b, 2, NSA_KV, n_past, HEAD_DIM)
    tail = jnp.concatenate([rows[:, 0:256].reshape(b, t, 256),
                            jnp.zeros((b, t_pad - t_all, 256), F32)], axis=1).reshape(b * (t_pad - past_len), 256)
    cmp_tail = _compress(tail, tail, 0, 1, 1, b * (t_pad - past_len), cw)
    cmp_tail = cmp_tail.reshape(b, n_tail, 2, NSA_KV, HEAD_DIM).transpose(0, 2, 3, 1, 4)
    cmp_tail = jnp.pad(cmp_tail, ((0, 0), (0, 0), (0, 0), (0, LANES - n_tail), (0, 0)))
    q5 = q_rows.reshape(b, t, NSA_KV, NSA_HPG, HEAD_DIM)
    nrow = NSA_HPG * t
    q_jt = q5.transpose(0, 2, 3, 1, 4).reshape(b, NSA_KV, nrow, HEAD_DIM)
    q_tj = q5.transpose(0, 2, 1, 3, 4).reshape(b, NSA_KV, nrow, HEAD_DIM)
    qp_tj = jnp.zeros((b, NSA_KV, nrow, LANES), BF16)
    for g in range(NSA_KV):
        qp_tj = qp_tj.at[:, g, :, g * HEAD_DIM:(g + 1) * HEAD_DIM].set(q_tj[:, g])
    npad = n_past + LANES
    n_slc_pad = -(-(t_pad // L_SLC) // LANES) * LANES
    nn = jnp.arange(npad)
    pair = ((nn[:, None] // (L_SLC // L_CMP) == jnp.arange(n_slc_pad)[None, :]) & (nn[:, None] < n_cmp)).astype(BF16)
    per_b = lambda shape: pl.BlockSpec(shape, lambda i: (i,) + (0,) * (len(shape) - 1))
    o_c, sel = pl.pallas_call(
        functools.partial(_nsa_select_kernel, t=t, past_len=past_len, n_past=n_past),
        grid=(b,),
        in_specs=[per_b((1, NSA_KV, nrow, HEAD_DIM)), per_b((1, 2, NSA_KV, n_past, HEAD_DIM)),
                  per_b((1, 2, NSA_KV, LANES, HEAD_DIM)), pl.BlockSpec((npad, n_slc_pad), lambda i: (0, 0))],
        out_specs=[per_b((1, NSA_KV, nrow, HEAD_DIM)), per_b((1, NSA_KV * t, LANES))],
        out_shape=[jax.ShapeDtypeStruct((b, NSA_KV, nrow, HEAD_DIM), F32),
                   jax.ShapeDtypeStruct((b, NSA_KV * t, LANES), jnp.int32)],
        compiler_params=_params("arbitrary"),
        name="nsa_select",
    )(q_jt, cmp_past, cmp_tail, pair)
    sel_c = sel[:, :, :N_SEL].reshape(b, NSA_KV * t * N_SEL)
    o_c = o_c.reshape(b, NSA_KV, NSA_HPG, t, HEAD_DIM).transpose(0, 1, 3, 2, 4).reshape(b, NSA_KV, nrow, HEAD_DIM)
    gates = misc[:, 0:3 * NSA_HEADS].reshape(b, t, NSA_KV, NSA_HPG, 3).transpose(0, 2, 1, 3, 4)
    gates = jnp.pad(gates.reshape(b, NSA_KV, nrow, 3), ((0, 0), (0, 0), (0, 0), (0, LANES - 3)))
    new_rows = jnp.concatenate([rows[:, 256:512], win], axis=1).reshape(b, t, 512)
    new_rows = jnp.pad(new_rows, ((0, 0), (0, LANES - t), (0, 0)))
    cache6 = cache_v.reshape(cache_v.shape[0], cache_v.shape[1], 4, NSA_KV, HEAD_DIM, PAGE_SIZE)
    nsel = N_SEL - 1
    pb = lambda shape: pl.BlockSpec(shape, lambda i, pt, sl: (i,) + (0,) * (len(shape) - 1))
    grid_spec = pltpu.PrefetchScalarGridSpec(
        num_scalar_prefetch=2,
        grid=(b,),
        in_specs=[pl.BlockSpec(memory_space=pl.ANY), pb((1, NSA_KV, nrow, HEAD_DIM)), pb((1, NSA_KV, nrow, LANES)),
                  pb((1, NSA_KV, nrow, HEAD_DIM)), pb((1, NSA_KV, nrow, LANES)), pb((1, LANES, 512)),
                  pl.BlockSpec((1, 1, 2, NSA_KV, HEAD_DIM, w_buf), lambda i, pt, sl: (layer, i, 0, 0, 0, 0))],
        out_specs=pb((1, NSA_KV, nrow, HEAD_DIM)),
        scratch_shapes=[pltpu.VMEM((2, NSA_KV * t * nsel, 2, HEAD_DIM, PAGE_SIZE), F32),
                        pltpu.VMEM((t * nsel, nrow, LANES), F32), pltpu.SemaphoreType.DMA((2,))],
    )
    y = pl.pallas_call(
        functools.partial(_nsa_attend_kernel, t=t, past_len=past_len, layer=layer, w_buf=w_buf),
        grid_spec=grid_spec,
        out_shape=jax.ShapeDtypeStruct((b, NSA_KV, nrow, HEAD_DIM), F32),
        compiler_params=_params("arbitrary"),
        name="nsa_attend",
    )(page_table, sel_c, cache6, q_tj, qp_tj, o_c, gates, new_rows, win_state_v)
    y = y.reshape(b, NSA_KV, t, NSA_HPG, HEAD_DIM).transpose(0, 2, 1, 3, 4).reshape(b * t, NSA_WIDTH)
    return y.astype(BF16)


def _outproj_kernel(yp_ref, yn_ref, yg_ref, x_ref, w_ref, g_ref, *rest, n_experts):
    with_router = n_experts > 0
    if with_router:
        r_ref, h_o, hn_o, lg_o = rest
    else:
        h_o, hn_o = rest
    h = (x_ref[...] + _dot(yp_ref[...].astype(BF16), w_ref[0:256, :]) + _dot(yn_ref[...], w_ref[256:768, :])
         + _dot(yg_ref[...], w_ref[768:1024, :]))
    h_o[...] = h
    ms = jnp.mean(h * h, axis=-1, keepdims=True)
    hn = (h * lax.rsqrt(ms + EPS) * g_ref[...]).astype(BF16)
    hn_o[...] = hn
    if with_router:
        lane = lax.broadcasted_iota(jnp.int32, (1, LANES), 1)
        lg = jnp.where(lane < n_experts, _dot(hn, r_ref[...]), -jnp.inf)
        v1 = jnp.max(lg, axis=1, keepdims=True)
        i1 = jnp.min(jnp.where(lg == v1, lane, LANES), axis=1, keepdims=True)
        lg2 = jnp.where(lane == i1, -jnp.inf, lg)
        v2 = jnp.max(lg2, axis=1, keepdims=True)
        i2 = jnp.min(jnp.where(lg2 == v2, lane, LANES), axis=1, keepdims=True)
        e2 = jnp.exp(v2 - v1)
        den = 1.0 + e2
        lg_o[...] = jnp.where(lane == 0, 1.0 / den, jnp.where(lane == 1, e2 / den, jnp.where(
            lane == 2, i1.astype(F32), jnp.where(lane == 3, i2.astype(F32), 0.0))))


def _outproj(y_pool, y_nsa, y_gla, x, w_out_l, g_ffn, router):
    n, d = x.shape
    tm = _tile(n, 384)
    with_router = router is not None
    row = lambda c: pl.BlockSpec((tm, c), lambda i: (i, 0))
    full = lambda shape: pl.BlockSpec(shape, lambda i: (0, 0))
    in_specs = [row(256), row(512), row(256), row(d), full((d, d)), full((1, d))]
    args = [y_pool, y_nsa, y_gla, x, w_out_l.astype(BF16), g_ffn[None, :]]
    out_specs = [row(d), row(d)]
    out_shape = [jax.ShapeDtypeStruct((n, d), F32), jax.ShapeDtypeStruct((n, d), BF16)]
    if with_router:
        ne = router.shape[1]
        in_specs.append(full((d, LANES)))
        args.append(jnp.zeros((d, LANES), F32).at[:, :ne].set(router).astype(BF16))
        out_specs.append(row(LANES))
        out_shape.append(jax.ShapeDtypeStruct((n, LANES), F32))
    return pl.pallas_call(
        functools.partial(_outproj_kernel, n_experts=router.shape[1] if with_router else 0),
        grid=(n // tm,), in_specs=in_specs, out_specs=out_specs, out_shape=out_shape,
        compiler_params=_params("arbitrary"), name="outproj",
    )(*args)


def _swiglu_kernel(be_ref, bv_ref, x_ref, gate_ref, w1_ref, w3_ref, w2_ref, *rest, with_res):
    res_ref, o_ref = rest if with_res else (None,) + rest
    i = pl.program_id(0)
    j = pl.program_id(1)
    valid = bv_ref[i] > 0

    @pl.when(jnp.logical_not(valid) & (j == 0))
    def _():
        o_ref[...] = jnp.zeros_like(o_ref)

    @pl.when(valid)
    def _():
        x = x_ref[...]
        a = _dot(x, w1_ref[0])
        c = _dot(x, w3_ref[0])
        hmid = (a * (1.0 / (1.0 + jnp.exp(-a))) * c).astype(BF16)
        y = _dot(hmid, w2_ref[0])

        @pl.when(j == 0)
        def _():
            o_ref[...] = y

        @pl.when(j > 0)
        def _():
            o_ref[...] += y

        @pl.when(j == pl.num_programs(1) - 1)
        def _():
            y_all = o_ref[...] * gate_ref[...]
            o_ref[...] = y_all + res_ref[...] if with_res else y_all


def _swiglu(x, gate, blk_expert, blk_valid, w1, w3, w2, tm, res=None):
    r, d = x.shape
    f = w1.shape[2]
    tf = f // 2 if (f // 2) % LANES == 0 else f
    row = pl.BlockSpec((tm, d), lambda i, j, be, bv: (i, 0))
    in_specs = [row, pl.BlockSpec((tm, 1), lambda i, j, be, bv: (i, 0)),
                pl.BlockSpec((1, d, tf), lambda i, j, be, bv: (be[i], 0, j)),
                pl.BlockSpec((1, d, tf), lambda i, j, be, bv: (be[i], 0, j)),
                pl.BlockSpec((1, tf, d), lambda i, j, be, bv: (be[i], j, 0))]
    args = [blk_expert, blk_valid, x, gate, w1, w3, w2]
    if res is not None:
        in_specs.append(row)
        args.append(res)
    grid_spec = pltpu.PrefetchScalarGridSpec(
        num_scalar_prefetch=2, grid=(r // tm, f // tf), in_specs=in_specs, out_specs=row)
    return pl.pallas_call(
        functools.partial(_swiglu_kernel, with_res=res is not None), grid_spec=grid_spec,
        out_shape=jax.ShapeDtypeStruct((r, d), F32),
        compiler_params=_params("arbitrary", "arbitrary"), name="swiglu",
    )(*args)


def _dense_ffn(h, hn, w1, w3, w2):
    n = h.shape[0]
    tm = _tile(n, 704)
    nblk = n // tm
    return _swiglu(hn, jnp.ones((n, 1), F32), jnp.zeros((nblk,), jnp.int32), jnp.ones((nblk,), jnp.int32),
                   w1[None].astype(BF16), w3[None].astype(BF16), w2[None].astype(BF16), tm, res=h)


def _moe_ffn(h, hn, route, w1, w3, w2, split=None):
    n = h.shape[0]
    ne = w1.shape[0]
    tm = 512
    a = n * TOP_K
    e_a = route[:, TOP_K:2 * TOP_K].astype(jnp.int32).reshape(-1)
    g_a = route[:, 0:TOP_K].reshape(-1)
    tok_a = jnp.repeat(jnp.arange(n), TOP_K)
    onehot = (e_a[:, None] == jnp.arange(ne)[None, :]).astype(jnp.int32)
    csum = jnp.cumsum(onehot, axis=0)
    counts = csum[-1]
    padded = (counts + tm - 1) // tm * tm
    p_end = jnp.cumsum(padded)
    p_start = p_end - padded
    dest = jnp.sum(onehot * (csum - onehot + p_start[None, :]), axis=1)
    nblk = -(-a // tm) + ne
    r = nblk * tm
    row_info = jnp.zeros((r, 2), F32).at[dest].set(jnp.stack([tok_a.astype(F32), g_a], axis=1))
    row_tok = row_info[:, 0].astype(jnp.int32)
    row_gate = row_info[:, 1]
    blk0 = jnp.arange(nblk) * tm
    blk_valid = (blk0 < p_end[-1]).astype(jnp.int32)
    last = jnp.clip(jnp.searchsorted(p_end, p_end[-1] - 1, side='right'), 0, ne - 1)
    blk_expert = jnp.clip(jnp.searchsorted(p_end, blk0, side='right'), 0, ne - 1)
    blk_expert = jnp.where(blk_valid > 0, blk_expert, last).astype(jnp.int32)
    yb = _swiglu(hn[row_tok], row_gate[:, None], blk_expert, blk_valid,
                 w1.astype(BF16), w3.astype(BF16), w2.astype(BF16), tm)
    pos = dest.reshape(n, TOP_K)
    combine = lambda lo, hi: h[lo:hi] + (yb[pos[lo:hi, 0]] + yb[pos[lo:hi, 1]])
    if split is None:
        return combine(0, n)
    return combine(0, split), combine(split, n)


def kernel(x_prompt, x_sample, cache_nsa_kv, state_nsa_win, state_gla, state_pool, page_table,
           norm_mix, norm_ffn, w_in, w_out, pool_w, pool_scale, nsa_q_norm, nsa_k_norm,
           nsa_cmp_w, nsa_cmp_pe, gla_wa2, gla_ba, gla_norm, ffn_w1, ffn_w3, ffn_w2,
           moe_router, moe_w1, moe_w3, moe_w2):
    bp, tp, d = x_prompt.shape
    bs, ts, _ = x_sample.shape
    depth = w_in.shape[0]
    n_pool = cache_nsa_kv.shape[1]
    n_pages = page_table.shape[1]
    past_len = n_pages * PAGE_SIZE
    w_buf = state_nsa_win.shape[2]
    npr = bp * tp
    n_all = npr + bs * ts
    cache_v = cache_nsa_kv.transpose(0, 1, 3, 4, 5, 2).reshape(depth, n_pool, 4 * NSA_KV * HEAD_DIM, PAGE_SIZE)
    win_state_v = state_nsa_win.transpose(0, 1, 3, 4, 5, 2)
    x = jnp.concatenate([x_prompt.reshape(npr, d), x_sample.reshape(bs * ts, d)], axis=0)
    kv_p, kv_s, win_p, win_s, gla_p, gla_s, pool_p, pool_s = [], [], [], [], [], [], [], []
    n_win_p = min(WINDOW, tp)
    for l in range(depth):
        u_pool, q_rows, rows, win, misc, gla_rows = _inproj(x, norm_mix[l], w_in[l], nsa_q_norm[l], nsa_k_norm[l])
        cw = _cmp_weights(nsa_cmp_w[l], nsa_cmp_pe[l], nsa_k_norm[l])
        up = u_pool[:npr].reshape(bp, tp, POOL_WIDTH)
        us = u_pool[npr:].reshape(bs, ts, POOL_WIDTH)
        ys_pool = _pool(us, state_pool[l], pool_w[l], pool_scale[l], past_len, bs, ts)
        y_pool = _pool(u_pool, jnp.zeros((bp, POOL_BUF, POOL_WIDTH), F32), pool_w[l], pool_scale[l], 0, bp, tp)
        y_pool = y_pool.at[npr:].set(ys_pool.reshape(bs * ts, POOL_WIDTH).astype(F32))
        pool_p.append(jnp.concatenate([jnp.zeros((bp, POOL_BUF, POOL_WIDTH), F32), up], axis=1)[:, -POOL_BUF:])
        pool_s.append(jnp.concatenate([state_pool[l], us], axis=1)[:, -POOL_BUF:])
        pw = _paged_cmp_weights(nsa_cmp_w[l], nsa_cmp_pe[l])
        ys_nsa = _nsa_sample(q_rows[npr:], rows[npr:], win[npr:], misc[npr:], cw, pw, cache_v,
                             page_table, l, win_state_v, bs, ts, past_len)
        y_nsa = _nsa_prompt(q_rows, rows, win, misc, cw, bp, tp).at[npr:].set(ys_nsa)
        kv_p.append(rows[:npr].reshape(bp, tp, 4, NSA_KV, HEAD_DIM))
        kv_s.append(rows[npr:].reshape(bs, ts, 4, NSA_KV, HEAD_DIM))
        win_full_p = jnp.concatenate([jnp.zeros((bp, WINDOW, 256), F32), win[:npr].reshape(bp, tp, 256)], axis=1)
        win_p.append(win_full_p[:, -n_win_p:].reshape(bp, n_win_p, 2, NSA_KV, HEAD_DIM))
        win_ext_s = jnp.concatenate([state_nsa_win[l], win[npr:].reshape(bs, ts, 2, NSA_KV, HEAD_DIM)], axis=1)
        win_s.append(win_ext_s[:, -w_buf:])
        ys_gla, ss = _gla(gla_rows[npr:], misc[npr:], state_gla[l], gla_wa2[l], gla_ba[l], gla_norm[l], bs, ts)
        y_gla, sp = _gla(gla_rows, misc, jnp.zeros((bp, GLA_HEADS, GLA_DK, GLA_DV), F32),
                         gla_wa2[l], gla_ba[l], gla_norm[l], bp, tp,
                         base=jnp.zeros((n_all, GLA_WIDTH), BF16).at[npr:].set(ys_gla))
        gla_p.append(sp.astype(state_gla.dtype))
        gla_s.append(ss.astype(state_gla.dtype))
        i = l // 2
        router = moe_router[i] if l % 2 else None
        res = _outproj(y_pool, y_nsa, y_gla, x, w_out[l], norm_ffn[l], router)
        if l % 2 == 0:
            h, hn = res
            x = _dense_ffn(h, hn, ffn_w1[i], ffn_w3[i], ffn_w2[i])
        else:
            h, hn, route = res
            if l == depth - 1:
                x = _moe_ffn(h, hn, route, moe_w1[i], moe_w3[i], moe_w2[i], split=npr)
            else:
                x = _moe_ffn(h, hn, route, moe_w1[i], moe_w3[i], moe_w2[i])
    x_p, x_s = x if isinstance(x, tuple) else (x[:npr], x[npr:])
    return (x_p.reshape(bp, tp, d), x_s.reshape(bs, ts, d),
            jnp.stack(kv_p), jnp.stack(kv_s), jnp.stack(win_p), jnp.stack(win_s),
            jnp.stack(gla_p), jnp.stack(gla_s), jnp.stack(pool_p), jnp.stack(pool_s))
```

```python
import functools
import math

import jax
import jax.numpy as jnp
from jax import lax
from jax.experimental import pallas as pl
from jax.experimental.pallas import tpu as pltpu

F32 = jnp.float32
BF16 = jnp.bfloat16

EPS = 1e-6
LANES = 128
HEAD_DIM = 64
PAGE_SIZE = 128
POOL_WINDOWS = (2, 4, 8, 16)
POOL_BUF = 15
POOL_WIDTH = 256
NSA_WIDTH = 512
NSA_HEADS = 8
NSA_KV = 2
NSA_HPG = 4
L_CMP = 32
L_SLC = 64
N_SEL = 16
WINDOW = 512
GLA_HEADS = 4
GLA_DK = 32
GLA_DV = 64
GLA_WIDTH = 256
GLA_RANK = 16
GLA_TAU = 16.0
GLA_SUB = 16
TOP_K = 2
N_IN_PAD = 2432
MASKED = -1e30
VMEM_LIMIT = 56 * 1024 * 1024


def _params(*sem):
    return pltpu.CompilerParams(dimension_semantics=sem, vmem_limit_bytes=VMEM_LIMIT)


def _tile(n, target):
    best = None
    for t in range(8, min(n, target) + 1, 8):
        if n % t == 0:
            best = t
    assert best is not None, (n, target)
    return best


def _dot(a, b):
    return jnp.dot(a, b, preferred_element_type=F32)


def _dot_nt(a, b):
    return lax.dot_general(a, b, (((1,), (1,)), ((), ())), preferred_element_type=F32)


def _dot_tn(a, b):
    return lax.dot_general(a, b, (((0,), (0,)), ((), ())), preferred_element_type=F32)


def _split2_dot(a, ones):
    hi = a.astype(BF16)
    lo = (a - hi.astype(F32)).astype(BF16)
    return _dot(hi, ones) + _dot(lo, ones)


def _split3_dot(ones, a):
    a1 = a.astype(BF16)
    r1 = a - a1.astype(F32)
    a2 = r1.astype(BF16)
    a3 = (r1 - a2.astype(F32)).astype(BF16)
    return _dot(ones, a1) + _dot(ones, a2) + _dot(ones, a3)


def _head_rmsnorm(a, gain_row, seg_ones):
    ms = _split2_dot(a * a, seg_ones) * (1.0 / HEAD_DIM)
    return a * lax.rsqrt(ms + EPS) * gain_row


def _masked_softmax(s, mask):
    sm = jnp.where(mask, s, MASKED)
    m = jnp.max(sm, axis=-1, keepdims=True)
    e = jnp.where(mask, jnp.exp(sm - m), 0.0)
    den = jnp.sum(e, axis=-1, keepdims=True)
    return e * (1.0 / jnp.where(den > 0.0, den, 1.0))


def _seg_ones():
    i = jnp.arange(LANES)
    return (i[:, None] // HEAD_DIM == i[None, :] // HEAD_DIM).astype(BF16)


def _inproj_kernel(x_ref, g_ref, w_ref, qg_ref, kg_ref, seg_ref,
                   pool_o, q_o, rows_o, win_o, misc_o, gla_o):
    x = x_ref[...]
    ms = jnp.mean(x * x, axis=-1, keepdims=True)
    xn = (x * lax.rsqrt(ms + EPS) * g_ref[...]).astype(BF16)
    seg = seg_ref[...]

    def mm(c0, c1):
        return _dot(xn, w_ref[:, c0:c1])

    pool_o[...] = mm(0, 256)
    zq = mm(256, 768)
    for c in range(4):
        q_o[:, LANES * c:LANES * (c + 1)] = (
            _head_rmsnorm(zq[:, LANES * c:LANES * (c + 1)], qg_ref[...], seg) * (HEAD_DIM ** -0.5)).astype(BF16)
    zkv = mm(768, 1536)
    rows_o[:, 0:256] = zkv[:, 0:256]
    rows_o[:, 256:384] = _head_rmsnorm(zkv[:, 256:384], kg_ref[1:2, :], seg)
    rows_o[:, 384:512] = zkv[:, 384:512]
    win_o[:, 0:128] = _head_rmsnorm(zkv[:, 512:640], kg_ref[2:3, :], seg)
    win_o[:, 128:256] = zkv[:, 640:768]
    zg = mm(1536, 2432)
    misc_o[...] = zg[:, 0:128]
    gla_o[...] = zg[:, 128:896]


def _pad_w_in(w):
    d = w.shape[0]
    return jnp.concatenate([
        w[:, 0:1560], w[:, 2072:2088], jnp.zeros((d, 88), w.dtype),
        w[:, 1560:2072], w[:, 2088:2344]], axis=1)


def _inproj(x, g_mix, w_in_l, q_gain, k_gain):
    n, d = x.shape
    tm = _tile(n, 384)
    w = _pad_w_in(w_in_l).astype(BF16)
    qg = jnp.tile(q_gain, 2)[None, :]
    kg = jnp.zeros((8, LANES), F32).at[0:3].set(jnp.tile(k_gain, (1, 2)))
    full = lambda shape: pl.BlockSpec(shape, lambda i: (0, 0))
    row = lambda c: pl.BlockSpec((tm, c), lambda i: (i, 0))
    return pl.pallas_call(
        _inproj_kernel,
        grid=(n // tm,),
        in_specs=[row(d), full((1, d)), full((d, N_IN_PAD)), full((1, LANES)),
                  full((8, LANES)), full((LANES, LANES))],
        out_specs=[row(256), row(512), row(512), row(256), row(128), row(768)],
        out_shape=[jax.ShapeDtypeStruct((n, 256), F32), jax.ShapeDtypeStruct((n, 512), BF16),
                   jax.ShapeDtypeStruct((n, 512), F32), jax.ShapeDtypeStruct((n, 256), F32),
                   jax.ShapeDtypeStruct((n, 128), F32), jax.ShapeDtypeStruct((n, 768), F32)],
        compiler_params=_params("arbitrary"),
        name="inproj",
    )(x, g_mix[None, :], w, qg, kg, _seg_ones())


def _pool_kernel(buf_ref, u_ref, w_ref, sc_ref, *rest, pos0, tp):
    o_ref, ext = rest[-2:]
    i = pl.program_id(1)

    @pl.when(i == 0)
    def _():
        ext[0:16, :] = buf_ref[0]

    ext[16:16 + tp, :] = u_ref[...].reshape(tp, POOL_WIDTH)
    u0 = ext[16:16 + tp, :]
    acc = u0
    sums = {}
    for k in range(1, 16):
        acc = acc + ext[16 - k:16 - k + tp, :]
        if k + 1 in POOL_WINDOWS:
            sums[k + 1] = acc
    lane = lax.broadcasted_iota(jnp.int32, (tp, POOL_WIDTH), 1)
    pos = pos0 + i * tp + lax.broadcasted_iota(jnp.int32, (tp, POOL_WIDTH), 0)
    grp = lane // (POOL_WIDTH // len(POOL_WINDOWS))
    total = sums[16]
    wsize = jnp.full((tp, POOL_WIDTH), 16, jnp.int32)
    for gi, wz in enumerate(POOL_WINDOWS[:-1]):
        total = jnp.where(grp == gi, sums[wz], total)
        wsize = jnp.where(grp == gi, wz, wsize)
    cnt = jnp.minimum(wsize, pos + 1).astype(F32)
    dlt = total / cnt - u0
    o_ref[...] = (_dot(dlt.astype(BF16), w_ref[...]) * sc_ref[...]).astype(o_ref.dtype).reshape(o_ref.shape)
    if tp >= 16:
        ext[0:16, :] = ext[tp:tp + 16, :]


def _aliased_base(base, n_in):
    if base is None:
        return [], [], {}
    return [pl.BlockSpec(memory_space=pl.ANY)], [base], {n_in: 0}


def _pool(u, buf, pool_w_l, pool_scale_l, pos0, b, t):
    in_place = u.ndim == 2
    base_specs, base_args, alias = _aliased_base(u if in_place else None, 4)
    c = u.shape[-1]
    tp = _tile(t, 512) if t >= 8 else t
    nt = t // tp
    buf16 = jnp.concatenate([jnp.zeros((b, 1, c), F32), buf.astype(F32)], axis=1)
    gw = c // len(POOL_WINDOWS)
    wbd = jnp.zeros((c, c), F32)
    for gi in range(len(POOL_WINDOWS)):
        wbd = wbd.at[gi * gw:(gi + 1) * gw, gi * gw:(gi + 1) * gw].set(pool_w_l[gi])
    if in_place:
        rows = pl.BlockSpec((tp, c), lambda i, j: (i * nt + j, 0))
    else:
        rows = pl.BlockSpec((1, tp, c), lambda i, j: (i, j, 0))
    return pl.pallas_call(
        functools.partial(_pool_kernel, pos0=pos0, tp=tp),
        grid=(b, nt),
        in_specs=[pl.BlockSpec((1, 16, c), lambda i, j: (i, 0, 0)), rows,
                  pl.BlockSpec((c, c), lambda i, j: (0, 0)),
                  pl.BlockSpec((1, c), lambda i, j: (0, 0))] + base_specs,
        out_specs=rows,
        out_shape=jax.ShapeDtypeStruct(u.shape, F32 if in_place else BF16),
        scratch_shapes=[pltpu.VMEM((16 + tp, c), F32)],
        input_output_aliases=alias,
        compiler_params=_params("arbitrary", "arbitrary"),
        name="pool",
    )(buf16, u, wbd.astype(BF16), pool_scale_l[None, :], *base_args)


def _gla_kernel(gla_ref, misc_ref, s0_ref, wa_ref, ba_ref, og_ref, seg_ref, eb_ref, mk_ref, *rest, tg, t_valid):
    o_ref, sT_ref, st, qs, ks, bs, qts, kts, vs, os_, dls, us, ss = rest[-13:]
    i = pl.program_id(1)
    c = GLA_SUB
    nsub = tg // c

    @pl.when(i == 0)
    def _():
        st[...] = s0_ref[0]

    gl = gla_ref[...]
    q = gl[:, 0:128] * (GLA_DK ** -0.5)
    k = gl[:, 128:256]
    v = gl[:, 256:512]
    r = gl[:, 512:768]
    x = _dot(misc_ref[...].astype(BF16), wa_ref[...]) + ba_ref[...]
    la = (jnp.minimum(x, 0.0) - jnp.log1p(jnp.exp(-jnp.abs(x)))) * (1.0 / GLA_TAU)
    row = lax.broadcasted_iota(jnp.int32, (tg, LANES), 0)
    if t_valid is not None:
        la = jnp.where(i * tg + row < t_valid, la, 0.0)
    rr = lax.broadcasted_iota(jnp.int32, (tg, tg), 0)
    cc = lax.broadcasted_iota(jnp.int32, (tg, tg), 1)
    same = (rr // c) == (cc // c)
    tri = (same & (cc <= rr)).astype(BF16)
    allo = same.astype(BF16)
    b = _split3_dot(tri, la)
    blast = _split3_dot(allo, la)
    qs[...] = q
    ks[...] = k
    bs[...] = b
    qts[...] = (q * jnp.exp(b)).astype(BF16)
    kts[...] = (k * jnp.exp(blast - b)).astype(BF16)
    vs[...] = v
    eb = eb_ref[...]
    mk = mk_ref[...]
    tt = lax.broadcasted_iota(jnp.int32, (c, LANES), 0)

    dls[...] = jnp.exp(blast)

    def local(j, carry):
        r0 = pl.multiple_of(j * c, c)
        qi = qs[pl.ds(r0, c), :]
        ki = ks[pl.ds(r0, c), :]
        bi = bs[pl.ds(r0, c), :]
        vi = vs[pl.ds(r0, c), :]
        parts = []
        for s in range(c):
            dec = jnp.exp(jnp.minimum(bi - bi[s:s + 1, :], 0.0))
            parts.append(jnp.where(tt >= s, qi * ki[s:s + 1, :] * dec, 0.0))
        p_all = jnp.concatenate(parts, axis=0).astype(BF16)
        a_all = _dot(p_all, eb)
        o_diag = a_all[0:c, :] * vi[0:1, :]
        for s in range(1, c):
            o_diag = o_diag + a_all[s * c:(s + 1) * c, :] * vi[s:s + 1, :]
        os_[pl.ds(r0, c), :] = o_diag
        us[j] = _dot_tn(vi.astype(BF16), kts[pl.ds(r0, c), :]) * mk
        return carry

    def grouped(body, group):
        group = math.gcd(nsub, group)

        def trip(jj, carry):
            for u in range(group):
                body(jj * group + u, carry)
            return carry

        lax.fori_loop(0, nsub // group, trip, 0)

    grouped(local, 8)

    def recur(j, carry):
        s_t = st[...]
        ss[j] = s_t.astype(BF16)
        st[...] = s_t * dls[pl.ds(pl.multiple_of(j * c, c), 1), :] + us[j]
        return carry

    lax.fori_loop(0, nsub, recur, 0)

    def inter(j, carry):
        r0 = pl.multiple_of(j * c, c)
        os_[pl.ds(r0, c), :] += _dot_nt(qts[pl.ds(r0, c), :], ss[j])
        return carry

    grouped(inter, 8)

    o = os_[...]
    seg = seg_ref[...]
    og = og_ref[...]
    sil = r * (1.0 / (1.0 + jnp.exp(-r)))
    for h in range(2):
        sl = slice(h * LANES, (h + 1) * LANES)
        o_ref[:, sl] = (_head_rmsnorm(o[:, sl], og[:, sl], seg) * sil[:, sl]).astype(o_ref.dtype)
    sT_ref[0] = st[...]


def _gla(gla_rows, misc_rows, s0, wa2, ba, o_gain, b, t, base=None):
    base_specs, base_args, alias = _aliased_base(base, 9)
    t_valid = None
    if t % GLA_SUB:
        t_valid = t
        tp = -(-t // GLA_SUB) * GLA_SUB
        pad = lambda z: jnp.pad(z.reshape(b, t, -1), ((0, 0), (0, tp - t), (0, 0))).reshape(b * tp, -1)
        gla_rows, misc_rows = pad(gla_rows), pad(misc_rows)
    else:
        tp = t
    tg = _tile(tp, 256)
    assert tg % GLA_SUB == 0
    nt = tp // tg
    kk = GLA_HEADS * GLA_DK
    vv = GLA_HEADS * GLA_DV
    ki = jnp.arange(kk)
    vi = jnp.arange(vv)
    head_eq = (vi[:, None] // GLA_DV == ki[None, :] // GLA_DK)
    mk = head_eq.astype(F32)
    eb = head_eq.T.astype(BF16)
    s0t = jnp.einsum('bhkv,hg->bhvgk', s0.astype(F32), jnp.eye(GLA_HEADS, dtype=F32)).reshape(b, vv, kk)
    wa = jnp.zeros((LANES, kk), F32).at[24:24 + GLA_RANK].set(wa2).astype(BF16)
    full = lambda shape: pl.BlockSpec(shape, lambda i, j: (0,) * len(shape))
    rows = lambda cdim: pl.BlockSpec((tg, cdim), lambda i, j: (i * nt + j, 0))
    o, s_t = pl.pallas_call(
        functools.partial(_gla_kernel, tg=tg, t_valid=t_valid),
        grid=(b, nt),
        in_specs=[rows(768), rows(128), pl.BlockSpec((1, vv, kk), lambda i, j: (i, 0, 0)),
                  full((LANES, kk)), full((1, kk)), full((1, vv)), full((LANES, LANES)),
                  full((kk, vv)), full((vv, kk))] + base_specs,
        input_output_aliases=alias,
        out_specs=[rows(vv), pl.BlockSpec((1, vv, kk), lambda i, j: (i, 0, 0))],
        out_shape=[jax.ShapeDtypeStruct((gla_rows.shape[0], vv), BF16), jax.ShapeDtypeStruct((b, vv, kk), F32)],
        scratch_shapes=[pltpu.VMEM((vv, kk), F32), pltpu.VMEM((tg, kk), F32), pltpu.VMEM((tg, kk), F32),
                        pltpu.VMEM((tg, kk), F32), pltpu.VMEM((tg, kk), BF16), pltpu.VMEM((tg, kk), BF16),
                        pltpu.VMEM((tg, vv), F32), pltpu.VMEM((tg, vv), F32), pltpu.VMEM((tg, kk), F32),
                        pltpu.VMEM((tg // GLA_SUB, vv, kk), F32), pltpu.VMEM((tg // GLA_SUB, vv, kk), BF16)],
        compiler_params=_params("arbitrary", "arbitrary"),
        name="gla",
    )(gla_rows, misc_rows, s0t, wa, ba[None, :], jnp.tile(o_gain, GLA_HEADS)[None, :], _seg_ones(), eb, mk,
      *base_args)
    if tp != t:
        o = o.reshape(b, tp, vv)[:, :t].reshape(b * t, vv)
    s5 = s_t.reshape(b, GLA_HEADS, GLA_DV, GLA_HEADS, GLA_DK)
    s_new = jnp.einsum('bhvgk,hg->bhkv', s5, jnp.eye(GLA_HEADS, dtype=F32))
    return o, s_new


def _compress_rows(read, nb, pe_ref, wk_ref, wv_ref, kg_ref, seg_ref):
    acck = jnp.zeros((nb, LANES), F32)
    accv = jnp.zeros((nb, LANES), F32)
    for j in range(L_CMP):
        xk = (read(0, j) + pe_ref[0, j:j + 1, :]).astype(BF16)
        xv = (read(1, j) + pe_ref[1, j:j + 1, :]).astype(BF16)
        acck = acck + _dot(xk, wk_ref[j])
        accv = accv + _dot(xv, wv_ref[j])
    kc = _head_rmsnorm(acck, kg_ref[0:1, :], seg_ref[...])
    return kc, accv


def _compress_kernel(rk_ref, rv_ref, pe_ref, wk_ref, wv_ref, kg_ref, seg_ref, o_ref, *, nb):
    refs = (rk_ref, rv_ref)
    kc, vc = _compress_rows(lambda kind, j: refs[kind][pl.ds(j, nb, stride=L_CMP), :], nb,
                            pe_ref, wk_ref, wv_ref, kg_ref, seg_ref)
    o_ref[0, :, 0:LANES] = kc
    o_ref[0, :, LANES:2 * LANES] = vc


def _cmp_weights(cmp_w, cmp_pe, k_gain):
    def bd(w):
        z = jnp.zeros_like(w)
        return jnp.concatenate([jnp.concatenate([w, z], axis=2), jnp.concatenate([z, w], axis=2)], axis=1)
    pe = jnp.tile(cmp_pe, (1, 1, 2))
    kg = jnp.zeros((8, LANES), F32).at[0].set(jnp.tile(k_gain[0], 2))
    return pe, bd(cmp_w[0]).astype(BF16), bd(cmp_w[1]).astype(BF16), kg


def _compress(rk, rv, k_col, v_col, b, tp, cw):
    nb = tp // L_CMP
    pe, wk, wv, kg = cw
    full = lambda shape: pl.BlockSpec(shape, lambda i: (0,) * len(shape))
    return pl.pallas_call(
        functools.partial(_compress_kernel, nb=nb),
        grid=(b,),
        in_specs=[pl.BlockSpec((tp, LANES), lambda i: (i, k_col)), pl.BlockSpec((tp, LANES), lambda i: (i, v_col)),
                  full((2, L_CMP, LANES)), full((L_CMP, LANES, LANES)), full((L_CMP, LANES, LANES)),
                  full((8, LANES)), full((LANES, LANES))],
        out_specs=pl.BlockSpec((1, nb, 256), lambda i: (i, 0, 0)),
        out_shape=jax.ShapeDtypeStruct((b, nb, 256), F32),
        compiler_params=_params("arbitrary"),
        name="compress",
    )(rk, rv, pe, wk, wv, kg, _seg_ones())


SLAB_PITCH = 2 * NSA_KV * HEAD_DIM + 8


def _paged_compress_kernel(pt_ref, cache_ref, pe_ref, m_ref, kg_ref, seg_ref, o_ref,
                           slab, sem, *, pages, n_slab, layer):
    bi = pl.program_id(0)
    si = pl.program_id(1)
    step = bi * n_slab + si
    nsteps = pl.num_programs(0) * n_slab
    slot = step % 2
    rows_cmp = 2 * NSA_KV * HEAD_DIM

    def copy(page, slot_, p):
        return pltpu.make_async_copy(cache_ref.at[layer, page, pl.ds(0, rows_cmp), :],
                                     slab.at[slot_, pl.ds(p * SLAB_PITCH, rows_cmp), :], sem.at[slot_])

    def issue(bb, ss, slot_):
        for p in range(pages):
            copy(pt_ref[bb, ss * pages + p], slot_, p).start()

    @pl.when(step == 0)
    def _():
        issue(0, 0, 0)

    @pl.when(step + 1 < nsteps)
    def _():
        nxt = step + 1
        issue(nxt // n_slab, nxt % n_slab, 1 - slot)

    for p in range(pages):
        copy(0, slot, p).wait()

    def rows(r0):
        return slab[slot, pl.ds(r0, pages, stride=SLAB_PITCH), :]

    for c in range(2):
        acc = jnp.zeros((NSA_KV * pages, 2 * LANES), F32)
        for dp in range(HEAD_DIM // 2):
            parts = []
            for g in range(NSA_KV):
                r0 = (c * NSA_KV + g) * HEAD_DIM + 2 * dp
                parts.append(jnp.concatenate([rows(r0), rows(r0 + 1)], axis=1))
            a = jnp.concatenate(parts, axis=0) + pe_ref[c, dp:dp + 1, :]
            acc = acc + _dot(a.astype(BF16), m_ref[c, dp])
        for g in range(NSA_KV):
            blk = acc[g * pages:(g + 1) * pages, :]
            if c == 0:
                for h in range(2):
                    sl = slice(h * LANES, (h + 1) * LANES)
                    o_ref[0, c, g, :, sl] = _head_rmsnorm(blk[:, sl], kg_ref[0:1, :], seg_ref[...])
            else:
                o_ref[0, c, g] = blk


def _paged_cmp_weights(cmp_w, cmp_pe):
    nblk = PAGE_SIZE // L_CMP
    k6 = jnp.einsum('nm,cjde->cdnjme', jnp.eye(nblk, dtype=F32), cmp_w)
    m = k6.reshape(2, HEAD_DIM // 2, 2 * PAGE_SIZE, nblk * HEAD_DIM).astype(BF16)
    pe = jnp.tile(cmp_pe.transpose(0, 2, 1), (1, 1, nblk)).reshape(2, HEAD_DIM // 2, 2 * PAGE_SIZE)
    return pe, m


def _paged_compress(cache_v, page_table, layer, cw, pw):
    b, n_pages = page_table.shape
    pages = math.gcd(n_pages, 64)
    n_slab = n_pages // pages
    pe, m = pw
    kg = cw[3]
    nblk = PAGE_SIZE // L_CMP
    full = lambda shape: pl.BlockSpec(shape, lambda i, j, pt: (0,) * len(shape))
    grid_spec = pltpu.PrefetchScalarGridSpec(
        num_scalar_prefetch=1,
        grid=(b, n_slab),
        in_specs=[pl.BlockSpec(memory_space=pl.ANY), full(pe.shape), full(m.shape), full((8, LANES)),
                  full((LANES, LANES))],
        out_specs=pl.BlockSpec((1, 2, NSA_KV, pages, nblk * HEAD_DIM), lambda i, j, pt: (i, 0, 0, j, 0)),
        scratch_shapes=[pltpu.VMEM((2, pages * SLAB_PITCH, LANES), F32), pltpu.SemaphoreType.DMA((2,))],
    )
    return pl.pallas_call(
        functools.partial(_paged_compress_kernel, pages=pages, n_slab=n_slab, layer=layer),
        grid_spec=grid_spec,
        out_shape=jax.ShapeDtypeStruct((b, 2, NSA_KV, n_pages, nblk * HEAD_DIM), F32),
        compiler_params=_params("arbitrary", "arbitrary"),
        name="paged_compress",
    )(page_table, cache_v, pe, m, kg, _seg_ones())


def _nsa_prompt_kernel(q_ref, kc_ref, vc_ref, ks_ref, vs_ref, kw_ref, vw_ref, gate_ref, base_ref,
                       o_ref, ksb, vsb, kwb, vwb, *, tq, ck):
    i = pl.program_id(1)
    t0 = i * tq

    n_slc = LANES // 2
    t_keys = ks_ref.shape[0]

    @pl.when(i == 0)
    def _():
        ksb[:, 0:LANES] = ks_ref[...].astype(BF16)
        kblk = lax.broadcasted_iota(jnp.int32, (t_keys, LANES), 0) // L_SLC
        klane = lax.broadcasted_iota(jnp.int32, (t_keys, LANES), 1)
        ksb[:, LANES:2 * LANES] = jnp.where(kblk == klane, 1.0, 0.0).astype(BF16)
        vsb[...] = vs_ref[...].astype(BF16)
        kwb[...] = kw_ref[...].astype(BF16)
        vwb[...] = vw_ref[...].astype(BF16)

    row_t = t0 + lax.broadcasted_iota(jnp.int32, (tq, 1), 0)
    col_t = t0 + lax.broadcasted_iota(jnp.int32, (1, tq), 1)
    lane = lax.broadcasted_iota(jnp.int32, (1, LANES), 1)
    crow = lax.broadcasted_iota(jnp.int32, (LANES, 1), 0)
    nat = jnp.where(crow < n_slc, 2 * crow, 2 * (crow - n_slc) + 1)
    cmp_ok = ((nat + 1) * L_CMP - 1) <= col_t
    blk = lax.broadcasted_iota(jnp.int32, (n_slc, 1), 0)
    cur = col_t // L_SLC
    done = blk < cur
    gx = gate_ref[...]
    gates = 1.0 / (1.0 + jnp.exp(-gx))
    n_full = t0 // ck
    kstart = pl.multiple_of(jnp.maximum(t0 - WINDOW, 0), tq)
    wlen = WINDOW + tq
    wpos = kstart + lax.broadcasted_iota(jnp.int32, (1, wlen), 1)
    win_bias = jnp.where((wpos <= row_t) & (wpos >= row_t - WINDOW), 0.0, MASKED)

    for g in range(NSA_KV):
        in_g = (lane // HEAD_DIM) == g
        heads = []
        for j in range(NSA_HPG):
            h = g * NSA_HPG + j
            qh = q_ref[:, (h // 2) * LANES:(h // 2 + 1) * LANES].astype(F32)
            if h % 2 != g:
                qh = pltpu.roll(qh, HEAD_DIM, axis=1)
            heads.append(jnp.where(in_g, qh, 0.0).astype(BF16))
        qg = jnp.concatenate(heads, axis=0)
        s_c = _dot_nt(kc_ref[0], qg)
        imp = jnp.zeros((LANES, tq), F32)
        o_c = []
        for j in range(NSA_HPG):
            sj = jnp.where(cmp_ok, s_c[:, j * tq:(j + 1) * tq], MASKED)
            ej = jnp.where(cmp_ok, jnp.exp(sj - jnp.max(sj, axis=0, keepdims=True)), 0.0)
            den = jnp.sum(ej, axis=0, keepdims=True)
            pj = ej * (1.0 / jnp.where(den > 0.0, den, 1.0))
            imp = imp + pj
            o_c.append(_dot_tn(pj.astype(BF16), vc_ref[0]))
        imp = imp[0:n_slc] + imp[n_slc:LANES]
        key = jnp.where(done, lax.bitcast_convert_type(imp, jnp.int32), -1)
        key_m1 = key - 1
        rank = jnp.zeros((n_slc, tq), jnp.int32)
        for r in range(1, n_slc):
            vm = pltpu.roll(key, r, axis=0)
            rank = rank + jnp.where(vm > jnp.where(blk >= r, key_m1, key), 1, 0)
        sel = (done & (rank < N_SEL - 1)) | (blk == cur)
        sel_bias = jnp.concatenate([jnp.where(sel, 0.0, MASKED), jnp.zeros((n_slc, tq), F32)], axis=0)
        sel_bias = sel_bias.T.astype(BF16)
        qx = jnp.concatenate([qg, jnp.concatenate([sel_bias] * NSA_HPG, axis=0)], axis=1)

        def chunk(c, carry, diagonal):
            m, l, acc = carry
            k0 = pl.multiple_of(c * ck, ck)
            sm = _dot_nt(qx, ksb[pl.ds(k0, ck), :]).reshape(NSA_HPG, tq, ck)
            if diagonal:
                kpos = k0 + lax.broadcasted_iota(jnp.int32, (1, ck), 1)
                sm = sm + jnp.where(kpos <= row_t, 0.0, MASKED)[None]
            m_new = jnp.maximum(m, jnp.max(sm, axis=-1, keepdims=True))
            p = jnp.exp(sm - m_new)
            alpha = jnp.exp(m - m_new)
            l = alpha * l + jnp.sum(p, axis=-1, keepdims=True)
            pv = _dot(p.reshape(NSA_HPG * tq, ck).astype(BF16), vsb[pl.ds(k0, ck), :])
            acc = alpha * acc + pv.reshape(NSA_HPG, tq, LANES)
            return m_new, l, acc

        m0 = jnp.full((NSA_HPG, tq, 1), MASKED, F32)
        l0 = jnp.zeros((NSA_HPG, tq, 1), F32)
        a0 = jnp.zeros((NSA_HPG, tq, LANES), F32)
        carry = lax.fori_loop(0, n_full, functools.partial(chunk, diagonal=False), (m0, l0, a0))
        _, l_s, acc_s = chunk(n_full, carry, True)
        o_s = acc_s * (1.0 / jnp.where(l_s > 0.0, l_s, 1.0))
        s_w = _dot_nt(qg, kwb[pl.ds(kstart, wlen), :]).reshape(NSA_HPG, tq, wlen) + win_bias[None]
        e_w = jnp.exp(s_w - jnp.max(s_w, axis=-1, keepdims=True))
        o_w = _dot(e_w.reshape(NSA_HPG * tq, wlen).astype(BF16), vwb[pl.ds(kstart, wlen), :])
        o_w = o_w.reshape(NSA_HPG, tq, LANES) * (1.0 / jnp.sum(e_w, axis=-1, keepdims=True))
        for pr in range(NSA_HPG // 2):
            pair = []
            for jj in range(2):
                j = 2 * pr + jj
                h = g * NSA_HPG + j
                o = (gates[:, 3 * h:3 * h + 1] * o_c[j] + gates[:, 3 * h + 1:3 * h + 2] * o_s[j]
                     + gates[:, 3 * h + 2:3 * h + 3] * o_w[j])
                pair.append(o if jj == g else pltpu.roll(o, HEAD_DIM, axis=1))
            col = (g * (NSA_HPG // 2) + pr) * LANES
            o_ref[:, col:col + LANES] = jnp.where(lane < HEAD_DIM, pair[0], pair[1]).astype(o_ref.dtype)


def _nsa_prompt(q_rows, rows, win, misc, cw, b, t):
    assert t // L_CMP == LANES and t % L_SLC == 0, "prompt kernel is laid out for 128 compressed blocks"
    tq = 256
    ck = 512
    assert ck % tq == 0 and t % ck == 0
    cmp = _compress(rows, rows, 0, 1, b, t, cw)
    order = jnp.concatenate([jnp.arange(0, LANES, 2), jnp.arange(1, LANES, 2)])
    cmp = cmp[:, order].astype(BF16)
    kc, vc = cmp[:, :, 0:LANES], cmp[:, :, LANES:2 * LANES]
    nt = t // tq
    col = lambda c: pl.BlockSpec((t, LANES), lambda i, j: (i, c))
    return pl.pallas_call(
        functools.partial(_nsa_prompt_kernel, tq=tq, ck=ck),
        grid=(b, nt),
        in_specs=[pl.BlockSpec((tq, NSA_WIDTH), lambda i, j: (i * nt + j, 0)),
                  pl.BlockSpec((1, LANES, LANES), lambda i, j: (i, 0, 0)),
                  pl.BlockSpec((1, LANES, LANES), lambda i, j: (i, 0, 0)),
                  col(2), col(3), col(0), col(1),
                  pl.BlockSpec((tq, LANES), lambda i, j: (i * nt + j, 0)),
                  pl.BlockSpec(memory_space=pl.ANY)],
        out_specs=pl.BlockSpec((tq, NSA_WIDTH), lambda i, j: (i * nt + j, 0)),
        out_shape=jax.ShapeDtypeStruct((q_rows.shape[0], NSA_WIDTH), BF16),
        scratch_shapes=[pltpu.VMEM((t, 2 * LANES), BF16)] + [pltpu.VMEM((t, LANES), BF16)] * 3,
        input_output_aliases={8: 0},
        compiler_params=_params("arbitrary", "arbitrary"),
        name="nsa_prompt",
    )(q_rows, kc, vc, rows, rows, win, win, misc, q_rows)


def _split3_dot_r(a, ones):
    a1 = a.astype(BF16)
    r1 = a - a1.astype(F32)
    a2 = r1.astype(BF16)
    a3 = (r1 - a2.astype(F32)).astype(BF16)
    return _dot(a1, ones) + _dot(a2, ones) + _dot(a3, ones)


def _nsa_select_kernel(q_ref, cp_ref, ct_ref, pair_ref, oc_ref, sel_ref, *, t, past_len, n_past):
    nrow = NSA_HPG * t
    npad = n_past + LANES
    n_slc_pad = pair_ref.shape[1]
    row = lax.broadcasted_iota(jnp.int32, (nrow, 1), 0)
    pos = past_len + row % t
    lane_c = lax.broadcasted_iota(jnp.int32, (1, npad), 1)
    cmp_ok = ((lane_c + 1) * L_CMP - 1) <= pos
    cur = (past_len + lax.broadcasted_iota(jnp.int32, (t, 1), 0)) // L_SLC
    lane_s = lax.broadcasted_iota(jnp.int32, (1, n_slc_pad), 1)
    lane_o = lax.broadcasted_iota(jnp.int32, (1, LANES), 1)
    done = lane_s < cur
    keys = []
    for g in range(NSA_KV):
        q = q_ref[0, g]
        s = jnp.concatenate([_dot_nt(q, cp_ref[0, 0, g].astype(BF16)),
                             _dot_nt(q, ct_ref[0, 0, g].astype(BF16))], axis=1)
        p = _masked_softmax(s, cmp_ok)
        pb = p.astype(BF16)
        oc_ref[0, g] = (_dot(pb[:, :n_past], cp_ref[0, 1, g].astype(BF16))
                        + _dot(pb[:, n_past:], ct_ref[0, 1, g].astype(BF16)))
        imp_c = p[0:t]
        for j in range(1, NSA_HPG):
            imp_c = imp_c + p[j * t:(j + 1) * t]
        imp = _split3_dot_r(imp_c, pair_ref[...])
        keys.append(jnp.where(done, lax.bitcast_convert_type(imp, jnp.int32), -1))
    key = jnp.concatenate(keys, axis=0)
    key_col = key.T
    n_idx = lax.broadcasted_iota(jnp.int32, (n_slc_pad, n_slc_pad), 0)
    m_idx = lax.broadcasted_iota(jnp.int32, (n_slc_pad, n_slc_pad), 1)
    m_first = jnp.where(m_idx < n_idx, 1, 0)
    n_col = lax.broadcasted_iota(jnp.int32, (n_slc_pad, 1), 0).astype(F32)
    slot = lane_o.astype(F32)
    rows = []
    for r in range(NSA_KV * t):
        ahead = key[r:r + 1, :] > (key_col[:, r:r + 1] - m_first)
        rank = jnp.sum(jnp.where(ahead, 1.0, 0.0), axis=1, keepdims=True)
        rows.append(jnp.sum(jnp.where(rank == slot, n_col, 0.0), axis=0, keepdims=True))
    picked = jnp.concatenate(rows, axis=0).astype(jnp.int32)
    sel_ref[0] = jnp.where(lane_o == N_SEL - 1, jnp.concatenate([cur] * NSA_KV, axis=0), picked)


def _nsa_attend_kernel(pt_ref, sel_ref, cache_ref, q_ref, qp_ref, oc_ref, gate_ref, new_ref,
                       wst_ref, o_ref, buf, sc, sem, *, t, past_len, layer, w_buf):
    bi = pl.program_id(0)
    slot = bi % 2
    nsel = N_SEL - 1
    n_past_blk = past_len // L_SLC
    per_page = PAGE_SIZE // L_SLC
    nrow = t * NSA_HPG

    def block_of(bb, g, tok, k):
        return jnp.minimum(sel_ref[bb, (g * t + tok) * N_SEL + k], n_past_blk - 1)

    def copy(page, slot_, g, idx):
        return pltpu.make_async_copy(cache_ref.at[layer, page, pl.ds(2, 2), g], buf.at[slot_, idx], sem.at[slot_])

    def issue(bb, slot_):
        for g in range(NSA_KV):
            for tok in range(t):
                for k in range(nsel):
                    page = pt_ref[bb, block_of(bb, g, tok, k) // per_page]
                    copy(page, slot_, g, (g * t + tok) * nsel + k).start()

    @pl.when(bi == 0)
    def _():
        issue(0, 0)

    @pl.when(bi + 1 < pl.num_programs(0))
    def _():
        issue(bi + 1, 1 - slot)

    for g in range(NSA_KV):
        for i in range(t * nsel):
            copy(0, slot, g, g * t * nsel + i).wait()

    row = lax.broadcasted_iota(jnp.int32, (nrow, 1), 0)
    tok_r = row // NSA_HPG
    pos = past_len + tok_r
    lane = lax.broadcasted_iota(jnp.int32, (1, LANES), 1)
    new_bias = jnp.where(lane <= tok_r, 0.0, MASKED)
    wpos = past_len - w_buf + lax.broadcasted_iota(jnp.int32, (1, w_buf), 1)
    win_bias = jnp.where((wpos >= 0) & (wpos <= pos) & (wpos >= pos - WINDOW), 0.0, MASKED)
    for g in range(NSA_KV):
        q = q_ref[0, g]
        qp = qp_ref[0, g]
        glanes = slice(g * HEAD_DIM, (g + 1) * HEAD_DIM)
        m = jnp.full((nrow, 1), MASKED, F32)
        for tok in range(t):
            cur = (past_len + tok) // L_SLC
            for k in range(nsel):
                n = sel_ref[bi, (g * t + tok) * N_SEL + k]
                half = block_of(bi, g, tok, k) % per_page
                valid = (tok_r == tok) & (lane // L_SLC == half) & (n < cur)
                i = tok * nsel + k
                s = _dot(q, buf[slot, g * t * nsel + i, 0].astype(BF16)) + jnp.where(valid, 0.0, MASKED)
                sc[i] = s
                m = jnp.maximum(m, jnp.max(s, axis=1, keepdims=True))
        s_new = _dot_nt(qp, new_ref[0, :, 0:LANES].astype(BF16)) + new_bias
        m = jnp.maximum(m, jnp.max(s_new, axis=1, keepdims=True))
        p_new = jnp.exp(s_new - m)
        l = jnp.sum(p_new, axis=1, keepdims=True)
        acc = _dot(p_new.astype(BF16), new_ref[0, :, LANES:2 * LANES].astype(BF16))[:, glanes]
        for i in range(t * nsel):
            p = jnp.exp(sc[i] - m)
            l = l + jnp.sum(p, axis=1, keepdims=True)
            acc = acc + _dot_nt(p.astype(BF16), buf[slot, g * t * nsel + i, 1].astype(BF16))
        o_s = acc * (1.0 / l)
        s_w = _dot(q, wst_ref[0, 0, 0, g].astype(BF16)) + win_bias
        s_wn = _dot_nt(qp, new_ref[0, :, 2 * LANES:3 * LANES].astype(BF16)) + new_bias
        m_w = jnp.maximum(jnp.max(s_w, axis=1, keepdims=True), jnp.max(s_wn, axis=1, keepdims=True))
        e_w = jnp.exp(s_w - m_w)
        e_n = jnp.exp(s_wn - m_w)
        l_w = jnp.sum(e_w, axis=1, keepdims=True) + jnp.sum(e_n, axis=1, keepdims=True)
        acc_w = (_dot_nt(e_w.astype(BF16), wst_ref[0, 0, 1, g].astype(BF16))
                 + _dot(e_n.astype(BF16), new_ref[0, :, 3 * LANES:4 * LANES].astype(BF16))[:, glanes])
        o_w = acc_w * (1.0 / l_w)
        gate = 1.0 / (1.0 + jnp.exp(-gate_ref[0, g]))
        o_ref[0, g] = gate[:, 0:1] * oc_ref[0, g] + gate[:, 1:2] * o_s + gate[:, 2:3] * o_w


def _nsa_sample(q_rows, rows, win, misc, cw, pw, cache_v, page_table, layer, win_state_v, b, t, past_len):
    t_all = past_len + t
    t_pad = -(-t_all // L_SLC) * L_SLC
    n_cmp = t_pad // L_CMP
    n_tail = (t_pad - past_len) // L_CMP
    n_past = past_len // L_CMP
    w_buf = win_state_v.shape[-1]
    assert past_len % PAGE_SIZE == 0 and t <= L_SLC and n_tail <= LANES and NSA_KV * t * N_SEL <= LANES
    cmp_past = _paged_compress(cache_v, page_table, layer, cw, pw).reshape(b, 2, NSA_KV, n_past, HEAD_DIM)
    tail = jnp.concatenate([rows[:, 0:256].reshape(b, t, 256),
                            jnp.zeros((b, t_pad - t_all, 256), F32)], axis=1).reshape(b * (t_pad - past_len), 256)
    cmp_tail = _compress(tail, tail, 0, 1, 1, b * (t_pad - past_len), cw)
    cmp_tail = cmp_tail.reshape(b, n_tail, 2, NSA_KV, HEAD_DIM).transpose(0, 2, 3, 1, 4)
    cmp_tail = jnp.pad(cmp_tail, ((0, 0), (0, 0), (0, 0), (0, LANES - n_tail), (0, 0)))
    q5 = q_rows.reshape(b, t, NSA_KV, NSA_HPG, HEAD_DIM)
    nrow = NSA_HPG * t
    q_jt = q5.transpose(0, 2, 3, 1, 4).reshape(b, NSA_KV, nrow, HEAD_DIM)
    q_tj = q5.transpose(0, 2, 1, 3, 4).reshape(b, NSA_KV, nrow, HEAD_DIM)
    qp_tj = jnp.zeros((b, NSA_KV, nrow, LANES), BF16)
    for g in range(NSA_KV):
        qp_tj = qp_tj.at[:, g, :, g * HEAD_DIM:(g + 1) * HEAD_DIM].set(q_tj[:, g])
    npad = n_past + LANES
    n_slc_pad = -(-(t_pad // L_SLC) // LANES) * LANES
    nn = jnp.arange(npad)
    pair = ((nn[:, None] // (L_SLC // L_CMP) == jnp.arange(n_slc_pad)[None, :]) & (nn[:, None] < n_cmp)).astype(BF16)
    per_b = lambda shape: pl.BlockSpec(shape, lambda i: (i,) + (0,) * (len(shape) - 1))
    o_c, sel = pl.pallas_call(
        functools.partial(_nsa_select_kernel, t=t, past_len=past_len, n_past=n_past),
        grid=(b,),
        in_specs=[per_b((1, NSA_KV, nrow, HEAD_DIM)), per_b((1, 2, NSA_KV, n_past, HEAD_DIM)),
                  per_b((1, 2, NSA_KV, LANES, HEAD_DIM)), pl.BlockSpec((npad, n_slc_pad), lambda i: (0, 0))],
        out_specs=[per_b((1, NSA_KV, nrow, HEAD_DIM)), per_b((1, NSA_KV * t, LANES))],
        out_shape=[jax.ShapeDtypeStruct((b, NSA_KV, nrow, HEAD_DIM), F32),
                   jax.ShapeDtypeStruct((b, NSA_KV * t, LANES), jnp.int32)],
        compiler_params=_params("arbitrary"),
        name="nsa_select",
    )(q_jt, cmp_past, cmp_tail, pair)
    sel_c = sel[:, :, :N_SEL].reshape(b, NSA_KV * t * N_SEL)
    o_c = o_c.reshape(b, NSA_KV, NSA_HPG, t, HEAD_DIM).transpose(0, 1, 3, 2, 4).reshape(b, NSA_KV, nrow, HEAD_DIM)
    gates = misc[:, 0:3 * NSA_HEADS].reshape(b, t, NSA_KV, NSA_HPG, 3).transpose(0, 2, 1, 3, 4)
    gates = jnp.pad(gates.reshape(b, NSA_KV, nrow, 3), ((0, 0), (0, 0), (0, 0), (0, LANES - 3)))
    new_rows = jnp.concatenate([rows[:, 256:512], win], axis=1).reshape(b, t, 512)
    new_rows = jnp.pad(new_rows, ((0, 0), (0, LANES - t), (0, 0)))
    cache6 = cache_v.reshape(cache_v.shape[0], cache_v.shape[1], 4, NSA_KV, HEAD_DIM, PAGE_SIZE)
    nsel = N_SEL - 1
    pb = lambda shape: pl.BlockSpec(shape, lambda i, pt, sl: (i,) + (0,) * (len(shape) - 1))
    grid_spec = pltpu.PrefetchScalarGridSpec(
        num_scalar_prefetch=2,
        grid=(b,),
        in_specs=[pl.BlockSpec(memory_space=pl.ANY), pb((1, NSA_KV, nrow, HEAD_DIM)), pb((1, NSA_KV, nrow, LANES)),
                  pb((1, NSA_KV, nrow, HEAD_DIM)), pb((1, NSA_KV, nrow, LANES)), pb((1, LANES, 512)),
                  pl.BlockSpec((1, 1, 2, NSA_KV, HEAD_DIM, w_buf), lambda i, pt, sl: (layer, i, 0, 0, 0, 0))],
        out_specs=pb((1, NSA_KV, nrow, HEAD_DIM)),
        scratch_shapes=[pltpu.VMEM((2, NSA_KV * t * nsel, 2, HEAD_DIM, PAGE_SIZE), F32),
                        pltpu.VMEM((t * nsel, nrow, LANES), F32), pltpu.SemaphoreType.DMA((2,))],
    )
    y = pl.pallas_call(
        functools.partial(_nsa_attend_kernel, t=t, past_len=past_len, layer=layer, w_buf=w_buf),
        grid_spec=grid_spec,
        out_shape=jax.ShapeDtypeStruct((b, NSA_KV, nrow, HEAD_DIM), F32),
        compiler_params=_params("arbitrary"),
        name="nsa_attend",
    )(page_table, sel_c, cache6, q_tj, qp_tj, o_c, gates, new_rows, win_state_v)
    y = y.reshape(b, NSA_KV, t, NSA_HPG, HEAD_DIM).transpose(0, 2, 1, 3, 4).reshape(b * t, NSA_WIDTH)
    return y.astype(BF16)


def _outproj_kernel(yp_ref, yn_ref, yg_ref, x_ref, w_ref, g_ref, *rest, n_experts):
    with_router = n_experts > 0
    if with_router:
        r_ref, h_o, hn_o, lg_o = rest
    else:
        h_o, hn_o = rest
    h = (x_ref[...] + _dot(yp_ref[...].astype(BF16), w_ref[0:256, :]) + _dot(yn_ref[...], w_ref[256:768, :])
         + _dot(yg_ref[...], w_ref[768:1024, :]))
    h_o[...] = h
    ms = jnp.mean(h * h, axis=-1, keepdims=True)
    hn = (h * lax.rsqrt(ms + EPS) * g_ref[...]).astype(BF16)
    hn_o[...] = hn.astype(hn_o.dtype)
    if with_router:
        lane = lax.broadcasted_iota(jnp.int32, (1, LANES), 1)
        lg = jnp.where(lane < n_experts, _dot(hn, r_ref[...]), -jnp.inf)
        v1 = jnp.max(lg, axis=1, keepdims=True)
        i1 = jnp.min(jnp.where(lg == v1, lane, LANES), axis=1, keepdims=True)
        lg2 = jnp.where(lane == i1, -jnp.inf, lg)
        v2 = jnp.max(lg2, axis=1, keepdims=True)
        i2 = jnp.min(jnp.where(lg2 == v2, lane, LANES), axis=1, keepdims=True)
        e2 = jnp.exp(v2 - v1)
        den = 1.0 + e2
        lg_o[...] = jnp.where(lane == 0, 1.0 / den, jnp.where(lane == 1, e2 / den, jnp.where(
            lane == 2, i1.astype(F32), jnp.where(lane == 3, i2.astype(F32), 0.0))))


def _outproj(y_pool, y_nsa, y_gla, x, w_out_l, g_ffn, router):
    n, d = x.shape
    tm = _tile(n, 384)
    with_router = router is not None
    row = lambda c: pl.BlockSpec((tm, c), lambda i: (i, 0))
    full = lambda shape: pl.BlockSpec(shape, lambda i: (0, 0))
    in_specs = [row(256), row(512), row(256), row(d), full((d, d)), full((1, d))]
    args = [y_pool, y_nsa, y_gla, x, w_out_l.astype(BF16), g_ffn[None, :]]
    out_specs = [row(d), row(d)]
    out_shape = [jax.ShapeDtypeStruct((n, d), F32), jax.ShapeDtypeStruct((n, d), F32 if with_router else BF16)]
    if with_router:
        ne = router.shape[1]
        in_specs.append(full((d, LANES)))
        args.append(jnp.zeros((d, LANES), F32).at[:, :ne].set(router).astype(BF16))
        out_specs.append(row(LANES))
        out_shape.append(jax.ShapeDtypeStruct((n, LANES), F32))
    return pl.pallas_call(
        functools.partial(_outproj_kernel, n_experts=router.shape[1] if with_router else 0),
        grid=(n // tm,), in_specs=in_specs, out_specs=out_specs, out_shape=out_shape,
        compiler_params=_params("arbitrary"), name="outproj",
    )(*args)


def _swiglu_kernel(be_ref, bv_ref, *rest, with_res, gather_rows):
    i = pl.program_id(0)
    j = pl.program_id(1)
    if gather_rows:
        tok_ref, x_ref, gate_ref, w1_ref, w3_ref, w2_ref, o_ref, xbuf, sem = rest
        res_ref = None
        tm = gather_rows
        slot = i % 2

        def issue(blk, slot_):
            for r in range(tm):
                pltpu.make_async_copy(x_ref.at[pl.ds(tok_ref[blk * tm + r], 1)], xbuf.at[slot_, pl.ds(r, 1)],
                                      sem.at[slot_]).start(priority=r % 2)

        @pl.when(j == 0)
        def _():
            @pl.when(i == 0)
            def _():
                issue(0, 0)

            @pl.when(i + 1 < pl.num_programs(0))
            def _():
                issue(i + 1, 1 - slot)

            pltpu.make_async_copy(x_ref.at[pl.ds(0, tm)], xbuf.at[slot], sem.at[slot]).wait()
    else:
        x_ref, gate_ref, w1_ref, w3_ref, w2_ref = rest[:5]
        res_ref, o_ref = rest[5:] if with_res else (None,) + rest[5:]
    valid = bv_ref[i] > 0

    @pl.when(jnp.logical_not(valid) & (j == 0))
    def _():
        o_ref[...] = jnp.zeros_like(o_ref)

    @pl.when(valid)
    def _():
        x = xbuf[slot].astype(BF16) if gather_rows else x_ref[...]
        a = _dot(x, w1_ref[0])
        c = _dot(x, w3_ref[0])
        hmid = (a * (1.0 / (1.0 + jnp.exp(-a))) * c).astype(BF16)
        y = _dot(hmid, w2_ref[0])

        @pl.when(j == 0)
        def _():
            o_ref[...] = y

        @pl.when(j > 0)
        def _():
            o_ref[...] += y

        @pl.when(j == pl.num_programs(1) - 1)
        def _():
            y_all = o_ref[...] * gate_ref[...]
            o_ref[...] = y_all + res_ref[...] if with_res else y_all


def _swiglu(x, gate, blk_expert, blk_valid, w1, w3, w2, tm, res=None, row_tok=None):
    r, d = gate.shape[0], x.shape[1]
    f = w1.shape[2]
    tf = f // 2 if (f // 2) % LANES == 0 else f
    gather = row_tok is not None
    assert not (gather and res is not None)
    row = pl.BlockSpec((tm, d), lambda i, j, be, bv, *_: (i, 0))
    in_specs = [pl.BlockSpec(memory_space=pl.ANY) if gather else row,
                pl.BlockSpec((tm, 1), lambda i, j, be, bv, *_: (i, 0)),
                pl.BlockSpec((1, d, tf), lambda i, j, be, bv, *_: (be[i], 0, j)),
                pl.BlockSpec((1, d, tf), lambda i, j, be, bv, *_: (be[i], 0, j)),
                pl.BlockSpec((1, tf, d), lambda i, j, be, bv, *_: (be[i], j, 0))]
    prefetch = [blk_expert, blk_valid] + ([row_tok] if gather else [])
    args = prefetch + [x, gate, w1, w3, w2]
    if res is not None:
        in_specs.append(row)
        args.append(res)
    scratch = [pltpu.VMEM((2, tm, d), F32), pltpu.SemaphoreType.DMA((2,))] if gather else []
    grid_spec = pltpu.PrefetchScalarGridSpec(
        num_scalar_prefetch=len(prefetch), grid=(r // tm, f // tf), in_specs=in_specs, out_specs=row,
        scratch_shapes=scratch)
    return pl.pallas_call(
        functools.partial(_swiglu_kernel, with_res=res is not None, gather_rows=tm if gather else 0),
        grid_spec=grid_spec,
        out_shape=jax.ShapeDtypeStruct((r, d), F32),
        compiler_params=_params("arbitrary", "arbitrary"), name="swiglu",
    )(*args)


def _dense_ffn(h, hn, w1, w3, w2):
    n = h.shape[0]
    tm = _tile(n, 704)
    nblk = n // tm
    return _swiglu(hn, jnp.ones((n, 1), F32), jnp.zeros((nblk,), jnp.int32), jnp.ones((nblk,), jnp.int32),
                   w1[None].astype(BF16), w3[None].astype(BF16), w2[None].astype(BF16), tm, res=h)


def _moe_ffn(h, hn, route, w1, w3, w2, split=None):
    n = h.shape[0]
    ne = w1.shape[0]
    tm = 512
    a = n * TOP_K
    e_a = route[:, TOP_K:2 * TOP_K].astype(jnp.int32).reshape(-1)
    g_a = route[:, 0:TOP_K].reshape(-1)
    tok_a = jnp.repeat(jnp.arange(n), TOP_K)
    onehot = (e_a[:, None] == jnp.arange(ne)[None, :]).astype(jnp.int32)
    csum = jnp.cumsum(onehot, axis=0)
    counts = csum[-1]
    padded = (counts + tm - 1) // tm * tm
    p_end = jnp.cumsum(padded)
    p_start = p_end - padded
    dest = jnp.sum(onehot * (csum - onehot + p_start[None, :]), axis=1)
    nblk = -(-a // tm) + ne
    r = nblk * tm
    row_info = jnp.zeros((r, 2), F32).at[dest].set(jnp.stack([tok_a.astype(F32), g_a], axis=1))
    row_tok = row_info[:, 0].astype(jnp.int32)
    row_gate = row_info[:, 1]
    blk0 = jnp.arange(nblk) * tm
    blk_valid = (blk0 < p_end[-1]).astype(jnp.int32)
    last = jnp.clip(jnp.searchsorted(p_end, p_end[-1] - 1, side='right'), 0, ne - 1)
    blk_expert = jnp.clip(jnp.searchsorted(p_end, blk0, side='right'), 0, ne - 1)
    blk_expert = jnp.where(blk_valid > 0, blk_expert, last).astype(jnp.int32)
    yb = _swiglu(hn, row_gate[:, None], blk_expert, blk_valid,
                 w1.astype(BF16), w3.astype(BF16), w2.astype(BF16), tm, row_tok=row_tok)
    pos = dest.reshape(n, TOP_K)
    combine = lambda lo, hi: h[lo:hi] + (yb[pos[lo:hi, 0]] + yb[pos[lo:hi, 1]])
    if split is None:
        return combine(0, n)
    return combine(0, split), combine(split, n)


def _kv_rows_out_kernel(*refs):
    o_ref = refs[-1]
    layer = pl.program_id(0)
    for l, r_ref in enumerate(refs[:-1]):
        @pl.when(layer == l)
        def _():
            o_ref[0, 0] = r_ref[...].T


def _kv_rows_out(rows_per_layer, b, t):
    depth = len(rows_per_layer)
    width = rows_per_layer[0].shape[1]
    tq = _tile(t, 512)
    nt = t // tq
    last = b * nt - 1

    def rows_spec(l):
        return pl.BlockSpec((tq, width), lambda ll, i, j: (jnp.where(ll == l, i * nt + j, jnp.where(ll > l, last, 0)), 0))

    out = pl.pallas_call(
        _kv_rows_out_kernel,
        grid=(depth, b, nt),
        in_specs=[rows_spec(l) for l in range(depth)],
        out_specs=pl.BlockSpec((1, 1, width, tq), lambda ll, i, j: (ll, i, 0, j)),
        out_shape=jax.ShapeDtypeStruct((depth, b, width, t), F32),
        compiler_params=_params("arbitrary", "arbitrary", "arbitrary"),
        name="kv_rows_out",
    )(*rows_per_layer)
    return out.reshape(depth, b, 4, NSA_KV, HEAD_DIM, t).transpose(0, 1, 5, 2, 3, 4)


def kernel(x_prompt, x_sample, cache_nsa_kv, state_nsa_win, state_gla, state_pool, page_table,
           norm_mix, norm_ffn, w_in, w_out, pool_w, pool_scale, nsa_q_norm, nsa_k_norm,
           nsa_cmp_w, nsa_cmp_pe, gla_wa2, gla_ba, gla_norm, ffn_w1, ffn_w3, ffn_w2,
           moe_router, moe_w1, moe_w3, moe_w2):
    bp, tp, d = x_prompt.shape
    bs, ts, _ = x_sample.shape
    depth = w_in.shape[0]
    n_pool = cache_nsa_kv.shape[1]
    n_pages = page_table.shape[1]
    past_len = n_pages * PAGE_SIZE
    w_buf = state_nsa_win.shape[2]
    npr = bp * tp
    n_all = npr + bs * ts
    cache_v = cache_nsa_kv.transpose(0, 1, 3, 4, 5, 2).reshape(depth, n_pool, 4 * NSA_KV * HEAD_DIM, PAGE_SIZE)
    win_state_v = state_nsa_win.transpose(0, 1, 3, 4, 5, 2)
    x = jnp.concatenate([x_prompt.reshape(npr, d), x_sample.reshape(bs * ts, d)], axis=0)
    kv_p, kv_s, win_p, win_s, gla_p, gla_s, pool_p, pool_s = [], [], [], [], [], [], [], []
    n_win_p = min(WINDOW, tp)
    for l in range(depth):
        u_pool, q_rows, rows, win, misc, gla_rows = _inproj(x, norm_mix[l], w_in[l], nsa_q_norm[l], nsa_k_norm[l])
        cw = _cmp_weights(nsa_cmp_w[l], nsa_cmp_pe[l], nsa_k_norm[l])
        up = u_pool[:npr].reshape(bp, tp, POOL_WIDTH)
        us = u_pool[npr:].reshape(bs, ts, POOL_WIDTH)
        ys_pool = _pool(us, state_pool[l], pool_w[l], pool_scale[l], past_len, bs, ts)
        y_pool = _pool(u_pool, jnp.zeros((bp, POOL_BUF, POOL_WIDTH), F32), pool_w[l], pool_scale[l], 0, bp, tp)
        y_pool = y_pool.at[npr:].set(ys_pool.reshape(bs * ts, POOL_WIDTH).astype(F32))
        pool_p.append(jnp.concatenate([jnp.zeros((bp, POOL_BUF, POOL_WIDTH), F32), up], axis=1)[:, -POOL_BUF:])
        pool_s.append(jnp.concatenate([state_pool[l], us], axis=1)[:, -POOL_BUF:])
        pw = _paged_cmp_weights(nsa_cmp_w[l], nsa_cmp_pe[l])
        ys_nsa = _nsa_sample(q_rows[npr:], rows[npr:], win[npr:], misc[npr:], cw, pw, cache_v,
                             page_table, l, win_state_v, bs, ts, past_len)
        y_nsa = _nsa_prompt(q_rows, rows, win, misc, cw, bp, tp).at[npr:].set(ys_nsa)
        kv_p.append(rows)
        kv_s.append(rows[npr:].reshape(bs, ts, 4, NSA_KV, HEAD_DIM))
        win_full_p = jnp.concatenate([jnp.zeros((bp, WINDOW, 256), F32), win[:npr].reshape(bp, tp, 256)], axis=1)
        win_p.append(win_full_p[:, -n_win_p:].reshape(bp, n_win_p, 2, NSA_KV, HEAD_DIM))
        win_ext_s = jnp.concatenate([state_nsa_win[l], win[npr:].reshape(bs, ts, 2, NSA_KV, HEAD_DIM)], axis=1)
        win_s.append(win_ext_s[:, -w_buf:])
        ys_gla, ss = _gla(gla_rows[npr:], misc[npr:], state_gla[l], gla_wa2[l], gla_ba[l], gla_norm[l], bs, ts)
        y_gla, sp = _gla(gla_rows, misc, jnp.zeros((bp, GLA_HEADS, GLA_DK, GLA_DV), F32),
                         gla_wa2[l], gla_ba[l], gla_norm[l], bp, tp,
                         base=jnp.zeros((n_all, GLA_WIDTH), BF16).at[npr:].set(ys_gla))
        gla_p.append(sp.astype(state_gla.dtype))
        gla_s.append(ss.astype(state_gla.dtype))
        i = l // 2
        router = moe_router[i] if l % 2 else None
        res = _outproj(y_pool, y_nsa, y_gla, x, w_out[l], norm_ffn[l], router)
        if l % 2 == 0:
            h, hn = res
            x = _dense_ffn(h, hn, ffn_w1[i], ffn_w3[i], ffn_w2[i])
        else:
            h, hn, route = res
            if l == depth - 1:
                x = _moe_ffn(h, hn, route, moe_w1[i], moe_w3[i], moe_w2[i], split=npr)
            else:
                x = _moe_ffn(h, hn, route, moe_w1[i], moe_w3[i], moe_w2[i])
    x_p, x_s = x if isinstance(x, tuple) else (x[:npr], x[npr:])
    return (x_p.reshape(bp, tp, d), x_s.reshape(bs, ts, d),
            _kv_rows_out(kv_p, bp, tp), jnp.stack(kv_s), jnp.stack(win_p), jnp.stack(win_s),
            jnp.stack(gla_p), jnp.stack(gla_s), jnp.stack(pool_p), jnp.stack(pool_s))
```

```python
import functools
import math

import jax
import jax.numpy as jnp
from jax import lax
from jax.experimental import pallas as pl
from jax.experimental.pallas import tpu as pltpu

F32 = jnp.float32
BF16 = jnp.bfloat16

EPS = 1e-6
LANES = 128
HEAD_DIM = 64
PAGE_SIZE = 128
POOL_WINDOWS = (2, 4, 8, 16)
POOL_BUF = 15
POOL_WIDTH = 256
NSA_WIDTH = 512
NSA_HEADS = 8
NSA_KV = 2
NSA_HPG = 4
L_CMP = 32
L_SLC = 64
N_SEL = 16
WINDOW = 512
GLA_HEADS = 4
GLA_DK = 32
GLA_DV = 64
GLA_WIDTH = 256
GLA_RANK = 16
GLA_TAU = 16.0
GLA_SUB = 16
TOP_K = 2
N_IN_PAD = 2432
MASKED = -1e30
VMEM_LIMIT = 56 * 1024 * 1024


def _params(*sem):
    return pltpu.CompilerParams(dimension_semantics=sem, vmem_limit_bytes=VMEM_LIMIT)


def _tile(n, target):
    best = None
    for t in range(8, min(n, target) + 1, 8):
        if n % t == 0:
            best = t
    assert best is not None, (n, target)
    return best


def _dot(a, b):
    return jnp.dot(a, b, preferred_element_type=F32)


def _dot_nt(a, b):
    return lax.dot_general(a, b, (((1,), (1,)), ((), ())), preferred_element_type=F32)


def _dot_tn(a, b):
    return lax.dot_general(a, b, (((0,), (0,)), ((), ())), preferred_element_type=F32)


def _split2_dot(a, ones):
    hi = a.astype(BF16)
    lo = (a - hi.astype(F32)).astype(BF16)
    return _dot(hi, ones) + _dot(lo, ones)


def _split3_dot(ones, a):
    a1 = a.astype(BF16)
    r1 = a - a1.astype(F32)
    a2 = r1.astype(BF16)
    a3 = (r1 - a2.astype(F32)).astype(BF16)
    return _dot(ones, a1) + _dot(ones, a2) + _dot(ones, a3)


def _head_rmsnorm(a, gain_row, seg_ones):
    ms = _split2_dot(a * a, seg_ones) * (1.0 / HEAD_DIM)
    return a * lax.rsqrt(ms + EPS) * gain_row


def _masked_softmax(s, mask):
    sm = jnp.where(mask, s, MASKED)
    m = jnp.max(sm, axis=-1, keepdims=True)
    e = jnp.where(mask, jnp.exp(sm - m), 0.0)
    den = jnp.sum(e, axis=-1, keepdims=True)
    return e * (1.0 / jnp.where(den > 0.0, den, 1.0))


def _seg_ones():
    i = jnp.arange(LANES)
    return (i[:, None] // HEAD_DIM == i[None, :] // HEAD_DIM).astype(BF16)


def _inproj_kernel(x_ref, g_ref, w_ref, qg_ref, kg_ref, seg_ref,
                   pool_o, q_o, rows_o, win_o, misc_o, gla_o):
    x = x_ref[...]
    ms = jnp.mean(x * x, axis=-1, keepdims=True)
    xn = (x * lax.rsqrt(ms + EPS) * g_ref[...]).astype(BF16)
    seg = seg_ref[...]

    def mm(c0, c1):
        return _dot(xn, w_ref[:, c0:c1])

    pool_o[...] = mm(0, 256)
    zq = mm(256, 768)
    for c in range(4):
        q_o[:, LANES * c:LANES * (c + 1)] = (
            _head_rmsnorm(zq[:, LANES * c:LANES * (c + 1)], qg_ref[...], seg) * (HEAD_DIM ** -0.5)).astype(BF16)
    zkv = mm(768, 1536)
    rows_o[:, 0:256] = zkv[:, 0:256]
    rows_o[:, 256:384] = _head_rmsnorm(zkv[:, 256:384], kg_ref[1:2, :], seg)
    rows_o[:, 384:512] = zkv[:, 384:512]
    win_o[:, 0:128] = _head_rmsnorm(zkv[:, 512:640], kg_ref[2:3, :], seg)
    win_o[:, 128:256] = zkv[:, 640:768]
    zg = mm(1536, 2432)
    misc_o[...] = zg[:, 0:128]
    gla_o[...] = zg[:, 128:896]


def _pad_w_in(w):
    d = w.shape[0]
    return jnp.concatenate([
        w[:, 0:1560], w[:, 2072:2088], jnp.zeros((d, 88), w.dtype),
        w[:, 1560:2072], w[:, 2088:2344]], axis=1)


def _inproj(x, g_mix, w_in_l, q_gain, k_gain):
    n, d = x.shape
    tm = _tile(n, 384)
    w = _pad_w_in(w_in_l).astype(BF16)
    qg = jnp.tile(q_gain, 2)[None, :]
    kg = jnp.zeros((8, LANES), F32).at[0:3].set(jnp.tile(k_gain, (1, 2)))
    full = lambda shape: pl.BlockSpec(shape, lambda i: (0, 0))
    row = lambda c: pl.BlockSpec((tm, c), lambda i: (i, 0))
    return pl.pallas_call(
        _inproj_kernel,
        grid=(n // tm,),
        in_specs=[row(d), full((1, d)), full((d, N_IN_PAD)), full((1, LANES)),
                  full((8, LANES)), full((LANES, LANES))],
        out_specs=[row(256), row(512), row(512), row(256), row(128), row(768)],
        out_shape=[jax.ShapeDtypeStruct((n, 256), F32), jax.ShapeDtypeStruct((n, 512), BF16),
                   jax.ShapeDtypeStruct((n, 512), F32), jax.ShapeDtypeStruct((n, 256), F32),
                   jax.ShapeDtypeStruct((n, 128), F32), jax.ShapeDtypeStruct((n, 768), F32)],
        compiler_params=_params("arbitrary"),
        name="inproj",
    )(x, g_mix[None, :], w, qg, kg, _seg_ones())


def _pool_kernel(buf_ref, u_ref, w_ref, sc_ref, *rest, pos0, tp):
    o_ref, ext = rest[-2:]
    i = pl.program_id(1)

    @pl.when(i == 0)
    def _():
        ext[0:16, :] = buf_ref[0]

    ext[16:16 + tp, :] = u_ref[...].reshape(tp, POOL_WIDTH)
    u0 = ext[16:16 + tp, :]
    acc = u0
    sums = {}
    for k in range(1, 16):
        acc = acc + ext[16 - k:16 - k + tp, :]
        if k + 1 in POOL_WINDOWS:
            sums[k + 1] = acc
    lane = lax.broadcasted_iota(jnp.int32, (tp, POOL_WIDTH), 1)
    pos = pos0 + i * tp + lax.broadcasted_iota(jnp.int32, (tp, POOL_WIDTH), 0)
    grp = lane // (POOL_WIDTH // len(POOL_WINDOWS))
    total = sums[16]
    wsize = jnp.full((tp, POOL_WIDTH), 16, jnp.int32)
    for gi, wz in enumerate(POOL_WINDOWS[:-1]):
        total = jnp.where(grp == gi, sums[wz], total)
        wsize = jnp.where(grp == gi, wz, wsize)
    cnt = jnp.minimum(wsize, pos + 1).astype(F32)
    dlt = total / cnt - u0
    o_ref[...] = (_dot(dlt.astype(BF16), w_ref[...]) * sc_ref[...]).astype(o_ref.dtype).reshape(o_ref.shape)
    if tp >= 16:
        ext[0:16, :] = ext[tp:tp + 16, :]


def _aliased_base(base, n_in):
    if base is None:
        return [], [], {}
    return [pl.BlockSpec(memory_space=pl.ANY)], [base], {n_in: 0}


def _pool(u, buf, pool_w_l, pool_scale_l, pos0, b, t):
    in_place = u.ndim == 2
    base_specs, base_args, alias = _aliased_base(u if in_place else None, 4)
    c = u.shape[-1]
    tp = _tile(t, 512) if t >= 8 else t
    nt = t // tp
    buf16 = jnp.concatenate([jnp.zeros((b, 1, c), F32), buf.astype(F32)], axis=1)
    gw = c // len(POOL_WINDOWS)
    wbd = jnp.zeros((c, c), F32)
    for gi in range(len(POOL_WINDOWS)):
        wbd = wbd.at[gi * gw:(gi + 1) * gw, gi * gw:(gi + 1) * gw].set(pool_w_l[gi])
    if in_place:
        rows = pl.BlockSpec((tp, c), lambda i, j: (i * nt + j, 0))
    else:
        rows = pl.BlockSpec((1, tp, c), lambda i, j: (i, j, 0))
    return pl.pallas_call(
        functools.partial(_pool_kernel, pos0=pos0, tp=tp),
        grid=(b, nt),
        in_specs=[pl.BlockSpec((1, 16, c), lambda i, j: (i, 0, 0)), rows,
                  pl.BlockSpec((c, c), lambda i, j: (0, 0)),
                  pl.BlockSpec((1, c), lambda i, j: (0, 0))] + base_specs,
        out_specs=rows,
        out_shape=jax.ShapeDtypeStruct(u.shape, F32 if in_place else BF16),
        scratch_shapes=[pltpu.VMEM((16 + tp, c), F32)],
        input_output_aliases=alias,
        compiler_params=_params("arbitrary", "arbitrary"),
        name="pool",
    )(buf16, u, wbd.astype(BF16), pool_scale_l[None, :], *base_args)


def _gla_kernel(gla_ref, misc_ref, s0_ref, wa_ref, ba_ref, og_ref, seg_ref, eb_ref, mk_ref, *rest, tg, t_valid):
    o_ref, sT_ref, st, qs, ks, bs, qts, kts, vs, os_, dls, us, ss = rest[-13:]
    i = pl.program_id(1)
    c = GLA_SUB
    nsub = tg // c

    @pl.when(i == 0)
    def _():
        st[...] = s0_ref[0]

    gl = gla_ref[...]
    q = gl[:, 0:128] * (GLA_DK ** -0.5)
    k = gl[:, 128:256]
    v = gl[:, 256:512]
    r = gl[:, 512:768]
    x = _dot(misc_ref[...].astype(BF16), wa_ref[...]) + ba_ref[...]
    la = (jnp.minimum(x, 0.0) - jnp.log1p(jnp.exp(-jnp.abs(x)))) * (1.0 / GLA_TAU)
    row = lax.broadcasted_iota(jnp.int32, (tg, LANES), 0)
    if t_valid is not None:
        la = jnp.where(i * tg + row < t_valid, la, 0.0)
    rr = lax.broadcasted_iota(jnp.int32, (tg, tg), 0)
    cc = lax.broadcasted_iota(jnp.int32, (tg, tg), 1)
    same = (rr // c) == (cc // c)
    tri = (same & (cc <= rr)).astype(BF16)
    allo = same.astype(BF16)
    b = _split3_dot(tri, la)
    blast = _split3_dot(allo, la)
    qs[...] = q
    ks[...] = k
    bs[...] = b
    qts[...] = (q * jnp.exp(b)).astype(BF16)
    kts[...] = (k * jnp.exp(blast - b)).astype(BF16)
    vs[...] = v
    eb = eb_ref[...]
    mk = mk_ref[...]
    tt = lax.broadcasted_iota(jnp.int32, (c, LANES), 0)

    dls[...] = jnp.exp(blast)

    def local(j, carry):
        r0 = pl.multiple_of(j * c, c)
        qi = qs[pl.ds(r0, c), :]
        ki = ks[pl.ds(r0, c), :]
        bi = bs[pl.ds(r0, c), :]
        vi = vs[pl.ds(r0, c), :]
        parts = []
        for s in range(c):
            dec = jnp.exp(jnp.minimum(bi - bi[s:s + 1, :], 0.0))
            parts.append(jnp.where(tt >= s, qi * ki[s:s + 1, :] * dec, 0.0))
        p_all = jnp.concatenate(parts, axis=0).astype(BF16)
        a_all = _dot(p_all, eb)
        o_diag = a_all[0:c, :] * vi[0:1, :]
        for s in range(1, c):
            o_diag = o_diag + a_all[s * c:(s + 1) * c, :] * vi[s:s + 1, :]
        os_[pl.ds(r0, c), :] = o_diag
        us[j] = _dot_tn(vi.astype(BF16), kts[pl.ds(r0, c), :]) * mk
        return carry

    def grouped(body, group):
        group = math.gcd(nsub, group)

        def trip(jj, carry):
            for u in range(group):
                body(jj * group + u, carry)
            return carry

        lax.fori_loop(0, nsub // group, trip, 0)

    grouped(local, 8)

    def recur(j, carry):
        s_t = st[...]
        ss[j] = s_t.astype(BF16)
        st[...] = s_t * dls[pl.ds(pl.multiple_of(j * c, c), 1), :] + us[j]
        return carry

    lax.fori_loop(0, nsub, recur, 0)

    def inter(j, carry):
        r0 = pl.multiple_of(j * c, c)
        os_[pl.ds(r0, c), :] += _dot_nt(qts[pl.ds(r0, c), :], ss[j])
        return carry

    grouped(inter, 8)

    o = os_[...]
    seg = seg_ref[...]
    og = og_ref[...]
    sil = r * (1.0 / (1.0 + jnp.exp(-r)))
    for h in range(2):
        sl = slice(h * LANES, (h + 1) * LANES)
        o_ref[:, sl] = (_head_rmsnorm(o[:, sl], og[:, sl], seg) * sil[:, sl]).astype(o_ref.dtype)
    sT_ref[0] = st[...]


def _gla(gla_rows, misc_rows, s0, wa2, ba, o_gain, b, t, base=None):
    base_specs, base_args, alias = _aliased_base(base, 9)
    t_valid = None
    if t % GLA_SUB:
        t_valid = t
        tp = -(-t // GLA_SUB) * GLA_SUB
        pad = lambda z: jnp.pad(z.reshape(b, t, -1), ((0, 0), (0, tp - t), (0, 0))).reshape(b * tp, -1)
        gla_rows, misc_rows = pad(gla_rows), pad(misc_rows)
    else:
        tp = t
    tg = _tile(tp, 256)
    assert tg % GLA_SUB == 0
    nt = tp // tg
    kk = GLA_HEADS * GLA_DK
    vv = GLA_HEADS * GLA_DV
    ki = jnp.arange(kk)
    vi = jnp.arange(vv)
    head_eq = (vi[:, None] // GLA_DV == ki[None, :] // GLA_DK)
    mk = head_eq.astype(F32)
    eb = head_eq.T.astype(BF16)
    s0t = jnp.einsum('bhkv,hg->bhvgk', s0.astype(F32), jnp.eye(GLA_HEADS, dtype=F32)).reshape(b, vv, kk)
    wa = jnp.zeros((LANES, kk), F32).at[24:24 + GLA_RANK].set(wa2).astype(BF16)
    full = lambda shape: pl.BlockSpec(shape, lambda i, j: (0,) * len(shape))
    rows = lambda cdim: pl.BlockSpec((tg, cdim), lambda i, j: (i * nt + j, 0))
    o, s_t = pl.pallas_call(
        functools.partial(_gla_kernel, tg=tg, t_valid=t_valid),
        grid=(b, nt),
        in_specs=[rows(768), rows(128), pl.BlockSpec((1, vv, kk), lambda i, j: (i, 0, 0)),
                  full((LANES, kk)), full((1, kk)), full((1, vv)), full((LANES, LANES)),
                  full((kk, vv)), full((vv, kk))] + base_specs,
        input_output_aliases=alias,
        out_specs=[rows(vv), pl.BlockSpec((1, vv, kk), lambda i, j: (i, 0, 0))],
        out_shape=[jax.ShapeDtypeStruct((gla_rows.shape[0], vv), BF16), jax.ShapeDtypeStruct((b, vv, kk), F32)],
        scratch_shapes=[pltpu.VMEM((vv, kk), F32), pltpu.VMEM((tg, kk), F32), pltpu.VMEM((tg, kk), F32),
                        pltpu.VMEM((tg, kk), F32), pltpu.VMEM((tg, kk), BF16), pltpu.VMEM((tg, kk), BF16),
                        pltpu.VMEM((tg, vv), F32), pltpu.VMEM((tg, vv), F32), pltpu.VMEM((tg, kk), F32),
                        pltpu.VMEM((tg // GLA_SUB, vv, kk), F32), pltpu.VMEM((tg // GLA_SUB, vv, kk), BF16)],
        compiler_params=_params("arbitrary", "arbitrary"),
        name="gla",
    )(gla_rows, misc_rows, s0t, wa, ba[None, :], jnp.tile(o_gain, GLA_HEADS)[None, :], _seg_ones(), eb, mk,
      *base_args)
    if tp != t:
        o = o.reshape(b, tp, vv)[:, :t].reshape(b * t, vv)
    s5 = s_t.reshape(b, GLA_HEADS, GLA_DV, GLA_HEADS, GLA_DK)
    s_new = jnp.einsum('bhvgk,hg->bhkv', s5, jnp.eye(GLA_HEADS, dtype=F32))
    return o, s_new


def _compress_rows(read, nb, pe_ref, wk_ref, wv_ref, kg_ref, seg_ref):
    acck = jnp.zeros((nb, LANES), F32)
    accv = jnp.zeros((nb, LANES), F32)
    for j in range(L_CMP):
        xk = (read(0, j) + pe_ref[0, j:j + 1, :]).astype(BF16)
        xv = (read(1, j) + pe_ref[1, j:j + 1, :]).astype(BF16)
        acck = acck + _dot(xk, wk_ref[j])
        accv = accv + _dot(xv, wv_ref[j])
    kc = _head_rmsnorm(acck, kg_ref[0:1, :], seg_ref[...])
    return kc, accv


def _compress_kernel(rk_ref, rv_ref, pe_ref, wk_ref, wv_ref, kg_ref, seg_ref, o_ref, *, nb):
    refs = (rk_ref, rv_ref)
    kc, vc = _compress_rows(lambda kind, j: refs[kind][pl.ds(j, nb, stride=L_CMP), :], nb,
                            pe_ref, wk_ref, wv_ref, kg_ref, seg_ref)
    o_ref[0, :, 0:LANES] = kc
    o_ref[0, :, LANES:2 * LANES] = vc


def _cmp_weights(cmp_w, cmp_pe, k_gain):
    def bd(w):
        z = jnp.zeros_like(w)
        return jnp.concatenate([jnp.concatenate([w, z], axis=2), jnp.concatenate([z, w], axis=2)], axis=1)
    pe = jnp.tile(cmp_pe, (1, 1, 2))
    kg = jnp.zeros((8, LANES), F32).at[0].set(jnp.tile(k_gain[0], 2))
    return pe, bd(cmp_w[0]).astype(BF16), bd(cmp_w[1]).astype(BF16), kg


def _compress(rk, rv, k_col, v_col, b, tp, cw):
    nb = tp // L_CMP
    pe, wk, wv, kg = cw
    full = lambda shape: pl.BlockSpec(shape, lambda i: (0,) * len(shape))
    return pl.pallas_call(
        functools.partial(_compress_kernel, nb=nb),
        grid=(b,),
        in_specs=[pl.BlockSpec((tp, LANES), lambda i: (i, k_col)), pl.BlockSpec((tp, LANES), lambda i: (i, v_col)),
                  full((2, L_CMP, LANES)), full((L_CMP, LANES, LANES)), full((L_CMP, LANES, LANES)),
                  full((8, LANES)), full((LANES, LANES))],
        out_specs=pl.BlockSpec((1, nb, 256), lambda i: (i, 0, 0)),
        out_shape=jax.ShapeDtypeStruct((b, nb, 256), F32),
        compiler_params=_params("arbitrary"),
        name="compress",
    )(rk, rv, pe, wk, wv, kg, _seg_ones())


SLAB_PITCH = 2 * NSA_KV * HEAD_DIM + 8


def _paged_compress_kernel(pt_ref, cache_ref, pe_ref, m_ref, kg_ref, seg_ref, o_ref,
                           slab, sem, *, pages, n_slab, layer):
    bi = pl.program_id(0)
    si = pl.program_id(1)
    step = bi * n_slab + si
    nsteps = pl.num_programs(0) * n_slab
    slot = step % 2
    rows_cmp = 2 * NSA_KV * HEAD_DIM

    def copy(page, slot_, p):
        return pltpu.make_async_copy(cache_ref.at[layer, page, pl.ds(0, rows_cmp), :],
                                     slab.at[slot_, pl.ds(p * SLAB_PITCH, rows_cmp), :], sem.at[slot_])

    def issue(bb, ss, slot_):
        for p in range(pages):
            copy(pt_ref[bb, ss * pages + p], slot_, p).start()

    @pl.when(step == 0)
    def _():
        issue(0, 0, 0)

    @pl.when(step + 1 < nsteps)
    def _():
        nxt = step + 1
        issue(nxt // n_slab, nxt % n_slab, 1 - slot)

    for p in range(pages):
        copy(0, slot, p).wait()

    def rows(r0):
        return slab[slot, pl.ds(r0, pages, stride=SLAB_PITCH), :]

    for c in range(2):
        acc = jnp.zeros((NSA_KV * pages, 2 * LANES), F32)
        for dp in range(HEAD_DIM // 2):
            parts = []
            for g in range(NSA_KV):
                r0 = (c * NSA_KV + g) * HEAD_DIM + 2 * dp
                parts.append(jnp.concatenate([rows(r0), rows(r0 + 1)], axis=1))
            a = jnp.concatenate(parts, axis=0) + pe_ref[c, dp:dp + 1, :]
            acc = acc + _dot(a.astype(BF16), m_ref[c, dp])
        for g in range(NSA_KV):
            blk = acc[g * pages:(g + 1) * pages, :]
            if c == 0:
                for h in range(2):
                    sl = slice(h * LANES, (h + 1) * LANES)
                    o_ref[0, c, g, :, sl] = _head_rmsnorm(blk[:, sl], kg_ref[0:1, :], seg_ref[...])
            else:
                o_ref[0, c, g] = blk


def _paged_cmp_weights(cmp_w, cmp_pe):
    nblk = PAGE_SIZE // L_CMP
    k6 = jnp.einsum('nm,cjde->cdnjme', jnp.eye(nblk, dtype=F32), cmp_w)
    m = k6.reshape(2, HEAD_DIM // 2, 2 * PAGE_SIZE, nblk * HEAD_DIM).astype(BF16)
    pe = jnp.tile(cmp_pe.transpose(0, 2, 1), (1, 1, nblk)).reshape(2, HEAD_DIM // 2, 2 * PAGE_SIZE)
    return pe, m


def _paged_compress(cache_v, page_table, layer, cw, pw):
    b, n_pages = page_table.shape
    pages = math.gcd(n_pages, 64)
    n_slab = n_pages // pages
    pe, m = pw
    kg = cw[3]
    nblk = PAGE_SIZE // L_CMP
    full = lambda shape: pl.BlockSpec(shape, lambda i, j, pt: (0,) * len(shape))
    grid_spec = pltpu.PrefetchScalarGridSpec(
        num_scalar_prefetch=1,
        grid=(b, n_slab),
        in_specs=[pl.BlockSpec(memory_space=pl.ANY), full(pe.shape), full(m.shape), full((8, LANES)),
                  full((LANES, LANES))],
        out_specs=pl.BlockSpec((1, 2, NSA_KV, pages, nblk * HEAD_DIM), lambda i, j, pt: (i, 0, 0, j, 0)),
        scratch_shapes=[pltpu.VMEM((2, pages * SLAB_PITCH, LANES), F32), pltpu.SemaphoreType.DMA((2,))],
    )
    return pl.pallas_call(
        functools.partial(_paged_compress_kernel, pages=pages, n_slab=n_slab, layer=layer),
        grid_spec=grid_spec,
        out_shape=jax.ShapeDtypeStruct((b, 2, NSA_KV, n_pages, nblk * HEAD_DIM), F32),
        compiler_params=_params("arbitrary", "arbitrary"),
        name="paged_compress",
    )(page_table, cache_v, pe, m, kg, _seg_ones())


def _nsa_prompt_kernel(q_ref, kc_ref, vc_ref, ks_ref, vs_ref, kw_ref, vw_ref, gate_ref, base_ref,
                       o_ref, ksb, vsb, kwb, vwb, *, tq, ck):
    i = pl.program_id(1)
    t0 = i * tq

    n_slc = LANES // 2
    t_keys = ks_ref.shape[0]

    @pl.when(i == 0)
    def _():
        ksb[:, 0:LANES] = ks_ref[...].astype(BF16)
        kblk = lax.broadcasted_iota(jnp.int32, (t_keys, LANES), 0) // L_SLC
        klane = lax.broadcasted_iota(jnp.int32, (t_keys, LANES), 1)
        ksb[:, LANES:2 * LANES] = jnp.where(kblk == klane, 1.0, 0.0).astype(BF16)
        vsb[...] = vs_ref[...].astype(BF16)
        kwb[...] = kw_ref[...].astype(BF16)
        vwb[...] = vw_ref[...].astype(BF16)

    row_t = t0 + lax.broadcasted_iota(jnp.int32, (tq, 1), 0)
    col_t = t0 + lax.broadcasted_iota(jnp.int32, (1, tq), 1)
    lane = lax.broadcasted_iota(jnp.int32, (1, LANES), 1)
    crow = lax.broadcasted_iota(jnp.int32, (LANES, 1), 0)
    nat = jnp.where(crow < n_slc, 2 * crow, 2 * (crow - n_slc) + 1)
    cmp_ok = ((nat + 1) * L_CMP - 1) <= col_t
    blk = lax.broadcasted_iota(jnp.int32, (n_slc, 1), 0)
    cur = col_t // L_SLC
    done = blk < cur
    gx = gate_ref[...]
    gates = 1.0 / (1.0 + jnp.exp(-gx))
    n_full = t0 // ck
    kstart = pl.multiple_of(jnp.maximum(t0 - WINDOW, 0), tq)
    wlen = WINDOW + tq
    wpos = kstart + lax.broadcasted_iota(jnp.int32, (1, wlen), 1)
    win_bias = jnp.where((wpos <= row_t) & (wpos >= row_t - WINDOW), 0.0, MASKED)

    for g in range(NSA_KV):
        in_g = (lane // HEAD_DIM) == g
        heads = []
        for j in range(NSA_HPG):
            h = g * NSA_HPG + j
            qh = q_ref[:, (h // 2) * LANES:(h // 2 + 1) * LANES].astype(F32)
            if h % 2 != g:
                qh = pltpu.roll(qh, HEAD_DIM, axis=1)
            heads.append(jnp.where(in_g, qh, 0.0).astype(BF16))
        qg = jnp.concatenate(heads, axis=0)
        s_c = _dot_nt(kc_ref[0], qg)
        imp = jnp.zeros((LANES, tq), F32)
        o_c = []
        for j in range(NSA_HPG):
            sj = jnp.where(cmp_ok, s_c[:, j * tq:(j + 1) * tq], MASKED)
            ej = jnp.where(cmp_ok, jnp.exp(sj - jnp.max(sj, axis=0, keepdims=True)), 0.0)
            den = jnp.sum(ej, axis=0, keepdims=True)
            pj = ej * (1.0 / jnp.where(den > 0.0, den, 1.0))
            imp = imp + pj
            o_c.append(_dot_tn(pj.astype(BF16), vc_ref[0]))
        imp = imp[0:n_slc] + imp[n_slc:LANES]
        key = jnp.where(done, lax.bitcast_convert_type(imp, jnp.int32), -1)
        key_m1 = key - 1
        rank = jnp.zeros((n_slc, tq), jnp.int32)
        for r in range(1, n_slc):
            vm = pltpu.roll(key, r, axis=0)
            rank = rank + jnp.where(vm > jnp.where(blk >= r, key_m1, key), 1, 0)
        sel = (done & (rank < N_SEL - 1)) | (blk == cur)
        sel_bias = jnp.concatenate([jnp.where(sel, 0.0, MASKED), jnp.zeros((n_slc, tq), F32)], axis=0)
        sel_bias = sel_bias.T.astype(BF16)
        qx = jnp.concatenate([qg, jnp.concatenate([sel_bias] * NSA_HPG, axis=0)], axis=1)

        def chunk(c, carry, diagonal):
            m, l, acc = carry
            k0 = pl.multiple_of(c * ck, ck)
            sm = _dot_nt(qx, ksb[pl.ds(k0, ck), :]).reshape(NSA_HPG, tq, ck)
            if diagonal:
                kpos = k0 + lax.broadcasted_iota(jnp.int32, (1, ck), 1)
                sm = sm + jnp.where(kpos <= row_t, 0.0, MASKED)[None]
            m_new = jnp.maximum(m, jnp.max(sm, axis=-1, keepdims=True))
            p = jnp.exp(sm - m_new)
            alpha = jnp.exp(m - m_new)
            l = alpha * l + jnp.sum(p, axis=-1, keepdims=True)
            pv = _dot(p.reshape(NSA_HPG * tq, ck).astype(BF16), vsb[pl.ds(k0, ck), :])
            acc = alpha * acc + pv.reshape(NSA_HPG, tq, LANES)
            return m_new, l, acc

        m0 = jnp.full((NSA_HPG, tq, 1), MASKED, F32)
        l0 = jnp.zeros((NSA_HPG, tq, 1), F32)
        a0 = jnp.zeros((NSA_HPG, tq, LANES), F32)
        carry = lax.fori_loop(0, n_full, functools.partial(chunk, diagonal=False), (m0, l0, a0))
        _, l_s, acc_s = chunk(n_full, carry, True)
        o_s = acc_s * (1.0 / jnp.where(l_s > 0.0, l_s, 1.0))
        s_w = _dot_nt(qg, kwb[pl.ds(kstart, wlen), :]).reshape(NSA_HPG, tq, wlen) + win_bias[None]
        e_w = jnp.exp(s_w - jnp.max(s_w, axis=-1, keepdims=True))
        o_w = _dot(e_w.reshape(NSA_HPG * tq, wlen).astype(BF16), vwb[pl.ds(kstart, wlen), :])
        o_w = o_w.reshape(NSA_HPG, tq, LANES) * (1.0 / jnp.sum(e_w, axis=-1, keepdims=True))
        for pr in range(NSA_HPG // 2):
            pair = []
            for jj in range(2):
                j = 2 * pr + jj
                h = g * NSA_HPG + j
                o = (gates[:, 3 * h:3 * h + 1] * o_c[j] + gates[:, 3 * h + 1:3 * h + 2] * o_s[j]
                     + gates[:, 3 * h + 2:3 * h + 3] * o_w[j])
                pair.append(o if jj == g else pltpu.roll(o, HEAD_DIM, axis=1))
            col = (g * (NSA_HPG // 2) + pr) * LANES
            o_ref[:, col:col + LANES] = jnp.where(lane < HEAD_DIM, pair[0], pair[1]).astype(o_ref.dtype)


def _nsa_prompt(q_rows, rows, win, misc, cw, b, t):
    assert t // L_CMP == LANES and t % L_SLC == 0, "prompt kernel is laid out for 128 compressed blocks"
    tq = 256
    ck = 512
    assert ck % tq == 0 and t % ck == 0
    cmp = _compress(rows, rows, 0, 1, b, t, cw)
    order = jnp.concatenate([jnp.arange(0, LANES, 2), jnp.arange(1, LANES, 2)])
    cmp = cmp[:, order].astype(BF16)
    kc, vc = cmp[:, :, 0:LANES], cmp[:, :, LANES:2 * LANES]
    nt = t // tq
    col = lambda c: pl.BlockSpec((t, LANES), lambda i, j: (i, c))
    return pl.pallas_call(
        functools.partial(_nsa_prompt_kernel, tq=tq, ck=ck),
        grid=(b, nt),
        in_specs=[pl.BlockSpec((tq, NSA_WIDTH), lambda i, j: (i * nt + j, 0)),
                  pl.BlockSpec((1, LANES, LANES), lambda i, j: (i, 0, 0)),
                  pl.BlockSpec((1, LANES, LANES), lambda i, j: (i, 0, 0)),
                  col(2), col(3), col(0), col(1),
                  pl.BlockSpec((tq, LANES), lambda i, j: (i * nt + j, 0)),
                  pl.BlockSpec(memory_space=pl.ANY)],
        out_specs=pl.BlockSpec((tq, NSA_WIDTH), lambda i, j: (i * nt + j, 0)),
        out_shape=jax.ShapeDtypeStruct((q_rows.shape[0], NSA_WIDTH), BF16),
        scratch_shapes=[pltpu.VMEM((t, 2 * LANES), BF16)] + [pltpu.VMEM((t, LANES), BF16)] * 3,
        input_output_aliases={8: 0},
        compiler_params=_params("arbitrary", "arbitrary"),
        name="nsa_prompt",
    )(q_rows, kc, vc, rows, rows, win, win, misc, q_rows)


def _split3_dot_r(a, ones):
    a1 = a.astype(BF16)
    r1 = a - a1.astype(F32)
    a2 = r1.astype(BF16)
    a3 = (r1 - a2.astype(F32)).astype(BF16)
    return _dot(a1, ones) + _dot(a2, ones) + _dot(a3, ones)


def _nsa_select_kernel(q_ref, cp_ref, ct_ref, pair_ref, oc_ref, sel_ref, *, t, past_len, n_past):
    nrow = NSA_HPG * t
    npad = n_past + LANES
    n_slc_pad = pair_ref.shape[1]
    row = lax.broadcasted_iota(jnp.int32, (nrow, 1), 0)
    pos = past_len + row % t
    lane_c = lax.broadcasted_iota(jnp.int32, (1, npad), 1)
    cmp_ok = ((lane_c + 1) * L_CMP - 1) <= pos
    cur = (past_len + lax.broadcasted_iota(jnp.int32, (t, 1), 0)) // L_SLC
    lane_s = lax.broadcasted_iota(jnp.int32, (1, n_slc_pad), 1)
    lane_o = lax.broadcasted_iota(jnp.int32, (1, LANES), 1)
    done = lane_s < cur
    keys = []
    for g in range(NSA_KV):
        q = q_ref[0, g]
        s = jnp.concatenate([_dot_nt(q, cp_ref[0, 0, g].astype(BF16)),
                             _dot_nt(q, ct_ref[0, 0, g].astype(BF16))], axis=1)
        p = _masked_softmax(s, cmp_ok)
        pb = p.astype(BF16)
        oc_ref[0, g] = (_dot(pb[:, :n_past], cp_ref[0, 1, g].astype(BF16))
                        + _dot(pb[:, n_past:], ct_ref[0, 1, g].astype(BF16)))
        imp_c = p[0:t]
        for j in range(1, NSA_HPG):
            imp_c = imp_c + p[j * t:(j + 1) * t]
        imp = _split3_dot_r(imp_c, pair_ref[...])
        keys.append(jnp.where(done, lax.bitcast_convert_type(imp, jnp.int32), -1))
    key = jnp.concatenate(keys, axis=0)
    key_col = key.T
    n_idx = lax.broadcasted_iota(jnp.int32, (n_slc_pad, n_slc_pad), 0)
    m_idx = lax.broadcasted_iota(jnp.int32, (n_slc_pad, n_slc_pad), 1)
    m_first = jnp.where(m_idx < n_idx, 1, 0)
    n_col = lax.broadcasted_iota(jnp.int32, (n_slc_pad, 1), 0).astype(F32)
    slot = lane_o.astype(F32)
    rows = []
    for r in range(NSA_KV * t):
        ahead = key[r:r + 1, :] > (key_col[:, r:r + 1] - m_first)
        rank = jnp.sum(jnp.where(ahead, 1.0, 0.0), axis=1, keepdims=True)
        rows.append(jnp.sum(jnp.where(rank == slot, n_col, 0.0), axis=0, keepdims=True))
    picked = jnp.concatenate(rows, axis=0).astype(jnp.int32)
    sel_ref[0] = jnp.where(lane_o == N_SEL - 1, jnp.concatenate([cur] * NSA_KV, axis=0), picked)


def _nsa_attend_kernel(pt_ref, sel_ref, cache_ref, q_ref, qp_ref, oc_ref, gate_ref, new_ref,
                       wst_ref, o_ref, buf, sc, sem, *, t, past_len, layer, w_buf):
    bi = pl.program_id(0)
    slot = bi % 2
    nsel = N_SEL - 1
    n_past_blk = past_len // L_SLC
    per_page = PAGE_SIZE // L_SLC
    nrow = t * NSA_HPG

    def block_of(bb, g, tok, k):
        return jnp.minimum(sel_ref[bb, (g * t + tok) * N_SEL + k], n_past_blk - 1)

    def copy(page, slot_, g, idx):
        return pltpu.make_async_copy(cache_ref.at[layer, page, pl.ds(2, 2), g], buf.at[slot_, idx], sem.at[slot_])

    def issue(bb, slot_):
        for g in range(NSA_KV):
            for tok in range(t):
                for k in range(nsel):
                    page = pt_ref[bb, block_of(bb, g, tok, k) // per_page]
                    copy(page, slot_, g, (g * t + tok) * nsel + k).start()

    @pl.when(bi == 0)
    def _():
        issue(0, 0)

    @pl.when(bi + 1 < pl.num_programs(0))
    def _():
        issue(bi + 1, 1 - slot)

    for g in range(NSA_KV):
        for i in range(t * nsel):
            copy(0, slot, g, g * t * nsel + i).wait()

    row = lax.broadcasted_iota(jnp.int32, (nrow, 1), 0)
    tok_r = row // NSA_HPG
    pos = past_len + tok_r
    lane = lax.broadcasted_iota(jnp.int32, (1, LANES), 1)
    new_bias = jnp.where(lane <= tok_r, 0.0, MASKED)
    wpos = past_len - w_buf + lax.broadcasted_iota(jnp.int32, (1, w_buf), 1)
    win_bias = jnp.where((wpos >= 0) & (wpos <= pos) & (wpos >= pos - WINDOW), 0.0, MASKED)
    for g in range(NSA_KV):
        q = q_ref[0, g]
        qp = qp_ref[0, g]
        glanes = slice(g * HEAD_DIM, (g + 1) * HEAD_DIM)
        m = jnp.full((nrow, 1), MASKED, F32)
        for tok in range(t):
            cur = (past_len + tok) // L_SLC
            for k in range(nsel):
                n = sel_ref[bi, (g * t + tok) * N_SEL + k]
                half = block_of(bi, g, tok, k) % per_page
                valid = (tok_r == tok) & (lane // L_SLC == half) & (n < cur)
                i = tok * nsel + k
                s = _dot(q, buf[slot, g * t * nsel + i, 0].astype(BF16)) + jnp.where(valid, 0.0, MASKED)
                sc[i] = s
                m = jnp.maximum(m, jnp.max(s, axis=1, keepdims=True))
        s_new = _dot_nt(qp, new_ref[0, :, 0:LANES].astype(BF16)) + new_bias
        m = jnp.maximum(m, jnp.max(s_new, axis=1, keepdims=True))
        p_new = jnp.exp(s_new - m)
        l = jnp.sum(p_new, axis=1, keepdims=True)
        acc = _dot(p_new.astype(BF16), new_ref[0, :, LANES:2 * LANES].astype(BF16))[:, glanes]
        for i in range(t * nsel):
            p = jnp.exp(sc[i] - m)
            l = l + jnp.sum(p, axis=1, keepdims=True)
            acc = acc + _dot_nt(p.astype(BF16), buf[slot, g * t * nsel + i, 1].astype(BF16))
        o_s = acc * (1.0 / l)
        s_w = _dot(q, wst_ref[0, 0, 0, g].astype(BF16)) + win_bias
        s_wn = _dot_nt(qp, new_ref[0, :, 2 * LANES:3 * LANES].astype(BF16)) + new_bias
        m_w = jnp.maximum(jnp.max(s_w, axis=1, keepdims=True), jnp.max(s_wn, axis=1, keepdims=True))
        e_w = jnp.exp(s_w - m_w)
        e_n = jnp.exp(s_wn - m_w)
        l_w = jnp.sum(e_w, axis=1, keepdims=True) + jnp.sum(e_n, axis=1, keepdims=True)
        acc_w = (_dot_nt(e_w.astype(BF16), wst_ref[0, 0, 1, g].astype(BF16))
                 + _dot(e_n.astype(BF16), new_ref[0, :, 3 * LANES:4 * LANES].astype(BF16))[:, glanes])
        o_w = acc_w * (1.0 / l_w)
        gate = 1.0 / (1.0 + jnp.exp(-gate_ref[0, g]))
        o_ref[0, g] = gate[:, 0:1] * oc_ref[0, g] + gate[:, 1:2] * o_s + gate[:, 2:3] * o_w


def _nsa_sample(q_rows, rows, win, misc, cw, pw, cache_v, page_table, layer, win_state_v, b, t, past_len):
    t_all = past_len + t
    t_pad = -(-t_all // L_SLC) * L_SLC
    n_cmp = t_pad // L_CMP
    n_tail = (t_pad - past_len) // L_CMP
    n_past = past_len // L_CMP
    w_buf = win_state_v.shape[-1]
    assert past_len % PAGE_SIZE == 0 and t <= L_SLC and n_tail <= LANES and NSA_KV * t * N_SEL <= LANES
    cmp_past = _paged_compress(cache_v, page_table, layer, cw, pw).reshape(b, 2, NSA_KV, n_past, HEAD_DIM)
    tail = jnp.concatenate([rows[:, 0:256].reshape(b, t, 256),
                            jnp.zeros((b, t_pad - t_all, 256), F32)], axis=1).reshape(b * (t_pad - past_len), 256)
    cmp_tail = _compress(tail, tail, 0, 1, 1, b * (t_pad - past_len), cw)
    cmp_tail = cmp_tail.reshape(b, n_tail, 2, NSA_KV, HEAD_DIM).transpose(0, 2, 3, 1, 4)
    cmp_tail = jnp.pad(cmp_tail, ((0, 0), (0, 0), (0, 0), (0, LANES - n_tail), (0, 0)))
    q5 = q_rows.reshape(b, t, NSA_KV, NSA_HPG, HEAD_DIM)
    nrow = NSA_HPG * t
    q_jt = q5.transpose(0, 2, 3, 1, 4).reshape(b, NSA_KV, nrow, HEAD_DIM)
    q_tj = q5.transpose(0, 2, 1, 3, 4).reshape(b, NSA_KV, nrow, HEAD_DIM)
    qp_tj = jnp.zeros((b, NSA_KV, nrow, LANES), BF16)
    for g in range(NSA_KV):
        qp_tj = qp_tj.at[:, g, :, g * HEAD_DIM:(g + 1) * HEAD_DIM].set(q_tj[:, g])
    npad = n_past + LANES
    n_slc_pad = -(-(t_pad // L_SLC) // LANES) * LANES
    nn = jnp.arange(npad)
    pair = ((nn[:, None] // (L_SLC // L_CMP) == jnp.arange(n_slc_pad)[None, :]) & (nn[:, None] < n_cmp)).astype(BF16)
    per_b = lambda shape: pl.BlockSpec(shape, lambda i: (i,) + (0,) * (len(shape) - 1))
    o_c, sel = pl.pallas_call(
        functools.partial(_nsa_select_kernel, t=t, past_len=past_len, n_past=n_past),
        grid=(b,),
        in_specs=[per_b((1, NSA_KV, nrow, HEAD_DIM)), per_b((1, 2, NSA_KV, n_past, HEAD_DIM)),
                  per_b((1, 2, NSA_KV, LANES, HEAD_DIM)), pl.BlockSpec((npad, n_slc_pad), lambda i: (0, 0))],
        out_specs=[per_b((1, NSA_KV, nrow, HEAD_DIM)), per_b((1, NSA_KV * t, LANES))],
        out_shape=[jax.ShapeDtypeStruct((b, NSA_KV, nrow, HEAD_DIM), F32),
                   jax.ShapeDtypeStruct((b, NSA_KV * t, LANES), jnp.int32)],
        compiler_params=_params("arbitrary"),
        name="nsa_select",
    )(q_jt, cmp_past, cmp_tail, pair)
    sel_c = sel[:, :, :N_SEL].reshape(b, NSA_KV * t * N_SEL)
    o_c = o_c.reshape(b, NSA_KV, NSA_HPG, t, HEAD_DIM).transpose(0, 1, 3, 2, 4).reshape(b, NSA_KV, nrow, HEAD_DIM)
    gates = misc[:, 0:3 * NSA_HEADS].reshape(b, t, NSA_KV, NSA_HPG, 3).transpose(0, 2, 1, 3, 4)
    gates = jnp.pad(gates.reshape(b, NSA_KV, nrow, 3), ((0, 0), (0, 0), (0, 0), (0, LANES - 3)))
    new_rows = jnp.concatenate([rows[:, 256:512], win], axis=1).reshape(b, t, 512)
    new_rows = jnp.pad(new_rows, ((0, 0), (0, LANES - t), (0, 0)))
    cache6 = cache_v.reshape(cache_v.shape[0], cache_v.shape[1], 4, NSA_KV, HEAD_DIM, PAGE_SIZE)
    nsel = N_SEL - 1
    pb = lambda shape: pl.BlockSpec(shape, lambda i, pt, sl: (i,) + (0,) * (len(shape) - 1))
    grid_spec = pltpu.PrefetchScalarGridSpec(
        num_scalar_prefetch=2,
        grid=(b,),
        in_specs=[pl.BlockSpec(memory_space=pl.ANY), pb((1, NSA_KV, nrow, HEAD_DIM)), pb((1, NSA_KV, nrow, LANES)),
                  pb((1, NSA_KV, nrow, HEAD_DIM)), pb((1, NSA_KV, nrow, LANES)), pb((1, LANES, 512)),
                  pl.BlockSpec((1, 1, 2, NSA_KV, HEAD_DIM, w_buf), lambda i, pt, sl: (layer, i, 0, 0, 0, 0))],
        out_specs=pb((1, NSA_KV, nrow, HEAD_DIM)),
        scratch_shapes=[pltpu.VMEM((2, NSA_KV * t * nsel, 2, HEAD_DIM, PAGE_SIZE), F32),
                        pltpu.VMEM((t * nsel, nrow, LANES), F32), pltpu.SemaphoreType.DMA((2,))],
    )
    y = pl.pallas_call(
        functools.partial(_nsa_attend_kernel, t=t, past_len=past_len, layer=layer, w_buf=w_buf),
        grid_spec=grid_spec,
        out_shape=jax.ShapeDtypeStruct((b, NSA_KV, nrow, HEAD_DIM), F32),
        compiler_params=_params("arbitrary"),
        name="nsa_attend",
    )(page_table, sel_c, cache6, q_tj, qp_tj, o_c, gates, new_rows, win_state_v)
    y = y.reshape(b, NSA_KV, t, NSA_HPG, HEAD_DIM).transpose(0, 2, 1, 3, 4).reshape(b * t, NSA_WIDTH)
    return y.astype(BF16)


def _outproj_kernel(yp_ref, yn_ref, yg_ref, x_ref, w_ref, g_ref, *rest, n_experts):
    with_router = n_experts > 0
    if with_router:
        r_ref, h_o, hn_o, lg_o = rest
    else:
        h_o, hn_o = rest
    h = (x_ref[...] + _dot(yp_ref[...].astype(BF16), w_ref[0:256, :]) + _dot(yn_ref[...], w_ref[256:768, :])
         + _dot(yg_ref[...], w_ref[768:1024, :]))
    h_o[...] = h
    ms = jnp.mean(h * h, axis=-1, keepdims=True)
    hn = (h * lax.rsqrt(ms + EPS) * g_ref[...]).astype(BF16)
    hn_o[...] = hn.astype(hn_o.dtype)
    if with_router:
        lane = lax.broadcasted_iota(jnp.int32, (1, LANES), 1)
        lg = jnp.where(lane < n_experts, _dot(hn, r_ref[...]), -jnp.inf)
        v1 = jnp.max(lg, axis=1, keepdims=True)
        i1 = jnp.min(jnp.where(lg == v1, lane, LANES), axis=1, keepdims=True)
        lg2 = jnp.where(lane == i1, -jnp.inf, lg)
        v2 = jnp.max(lg2, axis=1, keepdims=True)
        i2 = jnp.min(jnp.where(lg2 == v2, lane, LANES), axis=1, keepdims=True)
        e2 = jnp.exp(v2 - v1)
        den = 1.0 + e2
        lg_o[...] = jnp.where(lane == 0, 1.0 / den, jnp.where(lane == 1, e2 / den, jnp.where(
            lane == 2, i1.astype(F32), jnp.where(lane == 3, i2.astype(F32), 0.0))))


def _outproj(y_pool, y_nsa, y_gla, x, w_out_l, g_ffn, router):
    n, d = x.shape
    tm = _tile(n, 384)
    with_router = router is not None
    row = lambda c: pl.BlockSpec((tm, c), lambda i: (i, 0))
    full = lambda shape: pl.BlockSpec(shape, lambda i: (0, 0))
    in_specs = [row(256), row(512), row(256), row(d), full((d, d)), full((1, d))]
    args = [y_pool, y_nsa, y_gla, x, w_out_l.astype(BF16), g_ffn[None, :]]
    out_specs = [row(d), row(d)]
    out_shape = [jax.ShapeDtypeStruct((n, d), F32), jax.ShapeDtypeStruct((n, d), F32 if with_router else BF16)]
    if with_router:
        ne = router.shape[1]
        in_specs.append(full((d, LANES)))
        args.append(jnp.zeros((d, LANES), F32).at[:, :ne].set(router).astype(BF16))
        out_specs.append(row(LANES))
        out_shape.append(jax.ShapeDtypeStruct((n, LANES), F32))
    return pl.pallas_call(
        functools.partial(_outproj_kernel, n_experts=router.shape[1] if with_router else 0),
        grid=(n // tm,), in_specs=in_specs, out_specs=out_specs, out_shape=out_shape,
        compiler_params=_params("arbitrary"), name="outproj",
    )(*args)


def _gather_rows_kernel(tok_ref, x_ref, o_ref, sem, *, tm):
    i = pl.program_id(0)
    slot = i % 2

    def wait(blk, slot_):
        pltpu.make_async_copy(x_ref.at[pl.ds(0, tm)], o_ref.at[pl.ds(blk * tm, tm)], sem.at[slot_]).wait()

    for r in range(tm):
        pltpu.make_async_copy(x_ref.at[pl.ds(tok_ref[i * tm + r], 1)], o_ref.at[pl.ds(i * tm + r, 1)],
                              sem.at[slot]).start()

    @pl.when(i > 0)
    def _():
        wait(i - 1, 1 - slot)

    @pl.when(i == pl.num_programs(0) - 1)
    def _():
        wait(i, slot)


def _gather_rows(x, row_tok, tm):
    r = row_tok.shape[0]
    grid_spec = pltpu.PrefetchScalarGridSpec(
        num_scalar_prefetch=1, grid=(r // tm,),
        in_specs=[pl.BlockSpec(memory_space=pl.ANY)], out_specs=pl.BlockSpec(memory_space=pl.ANY),
        scratch_shapes=[pltpu.SemaphoreType.DMA((2,))])
    return pl.pallas_call(
        functools.partial(_gather_rows_kernel, tm=tm), grid_spec=grid_spec,
        out_shape=jax.ShapeDtypeStruct((r, x.shape[1]), x.dtype),
        compiler_params=_params("arbitrary"), name="gather_rows",
    )(row_tok, x)


def _swiglu_kernel(be_ref, bv_ref, x_ref, gate_ref, w1_ref, w3_ref, w2_ref, *rest, with_res):
    res_ref, o_ref = rest if with_res else (None,) + rest
    i = pl.program_id(0)
    j = pl.program_id(1)
    valid = bv_ref[i] > 0

    @pl.when(jnp.logical_not(valid) & (j == 0))
    def _():
        o_ref[...] = jnp.zeros_like(o_ref)

    @pl.when(valid)
    def _():
        x = x_ref[...].astype(BF16)
        a = _dot(x, w1_ref[0])
        c = _dot(x, w3_ref[0])
        hmid = (a * (1.0 / (1.0 + jnp.exp(-a))) * c).astype(BF16)
        y = _dot(hmid, w2_ref[0])

        @pl.when(j == 0)
        def _():
            o_ref[...] = y

        @pl.when(j > 0)
        def _():
            o_ref[...] += y

        @pl.when(j == pl.num_programs(1) - 1)
        def _():
            y_all = o_ref[...] * gate_ref[...]
            o_ref[...] = y_all + res_ref[...] if with_res else y_all


def _swiglu(x, gate, blk_expert, blk_valid, w1, w3, w2, tm, res=None):
    r, d = x.shape
    f = w1.shape[2]
    tf = f // 2 if (f // 2) % LANES == 0 else f
    row = pl.BlockSpec((tm, d), lambda i, j, be, bv: (i, 0))
    in_specs = [row, pl.BlockSpec((tm, 1), lambda i, j, be, bv: (i, 0)),
                pl.BlockSpec((1, d, tf), lambda i, j, be, bv: (be[i], 0, j)),
                pl.BlockSpec((1, d, tf), lambda i, j, be, bv: (be[i], 0, j)),
                pl.BlockSpec((1, tf, d), lambda i, j, be, bv: (be[i], j, 0))]
    args = [blk_expert, blk_valid, x, gate, w1, w3, w2]
    if res is not None:
        in_specs.append(row)
        args.append(res)
    grid_spec = pltpu.PrefetchScalarGridSpec(
        num_scalar_prefetch=2, grid=(r // tm, f // tf), in_specs=in_specs, out_specs=row)
    return pl.pallas_call(
        functools.partial(_swiglu_kernel, with_res=res is not None), grid_spec=grid_spec,
        out_shape=jax.ShapeDtypeStruct((r, d), F32),
        compiler_params=_params("arbitrary", "arbitrary"), name="swiglu",
    )(*args)


def _dense_ffn(h, hn, w1, w3, w2):
    n = h.shape[0]
    tm = _tile(n, 704)
    nblk = n // tm
    return _swiglu(hn, jnp.ones((n, 1), F32), jnp.zeros((nblk,), jnp.int32), jnp.ones((nblk,), jnp.int32),
                   w1[None].astype(BF16), w3[None].astype(BF16), w2[None].astype(BF16), tm, res=h)


def _moe_ffn(h, hn, route, w1, w3, w2, split=None):
    n = h.shape[0]
    ne = w1.shape[0]
    tm = 512
    a = n * TOP_K
    e_a = route[:, TOP_K:2 * TOP_K].astype(jnp.int32).reshape(-1)
    g_a = route[:, 0:TOP_K].reshape(-1)
    tok_a = jnp.repeat(jnp.arange(n), TOP_K)
    onehot = (e_a[:, None] == jnp.arange(ne)[None, :]).astype(jnp.int32)
    csum = jnp.cumsum(onehot, axis=0)
    counts = csum[-1]
    padded = (counts + tm - 1) // tm * tm
    p_end = jnp.cumsum(padded)
    p_start = p_end - padded
    dest = jnp.sum(onehot * (csum - onehot + p_start[None, :]), axis=1)
    nblk = -(-a // tm) + ne
    r = nblk * tm
    row_info = jnp.zeros((r, 2), F32).at[dest].set(jnp.stack([tok_a.astype(F32), g_a], axis=1))
    row_tok = row_info[:, 0].astype(jnp.int32)
    row_gate = row_info[:, 1]
    blk0 = jnp.arange(nblk) * tm
    blk_valid = (blk0 < p_end[-1]).astype(jnp.int32)
    last = jnp.clip(jnp.searchsorted(p_end, p_end[-1] - 1, side='right'), 0, ne - 1)
    blk_expert = jnp.clip(jnp.searchsorted(p_end, blk0, side='right'), 0, ne - 1)
    blk_expert = jnp.where(blk_valid > 0, blk_expert, last).astype(jnp.int32)
    yb = _swiglu(_gather_rows(hn, row_tok, tm), row_gate[:, None], blk_expert, blk_valid,
                 w1.astype(BF16), w3.astype(BF16), w2.astype(BF16), tm)
    pos = dest.reshape(n, TOP_K)
    combine = lambda lo, hi: h[lo:hi] + (yb[pos[lo:hi, 0]] + yb[pos[lo:hi, 1]])
    if split is None:
        return combine(0, n)
    return combine(0, split), combine(split, n)


def _kv_rows_out_kernel(*refs):
    o_ref = refs[-1]
    layer = pl.program_id(0)
    for l, r_ref in enumerate(refs[:-1]):
        @pl.when(layer == l)
        def _():
            o_ref[0, 0] = r_ref[...].T


def _kv_rows_out(rows_per_layer, b, t):
    depth = len(rows_per_layer)
    width = rows_per_layer[0].shape[1]
    tq = _tile(t, 512)
    nt = t // tq
    last = b * nt - 1

    def rows_spec(l):
        return pl.BlockSpec((tq, width), lambda ll, i, j: (jnp.where(ll == l, i * nt + j, jnp.where(ll > l, last, 0)), 0))

    out = pl.pallas_call(
        _kv_rows_out_kernel,
        grid=(depth, b, nt),
        in_specs=[rows_spec(l) for l in range(depth)],
        out_specs=pl.BlockSpec((1, 1, width, tq), lambda ll, i, j: (ll, i, 0, j)),
        out_shape=jax.ShapeDtypeStruct((depth, b, width, t), F32),
        compiler_params=_params("arbitrary", "arbitrary", "arbitrary"),
        name="kv_rows_out",
    )(*rows_per_layer)
    return out.reshape(depth, b, 4, NSA_KV, HEAD_DIM, t).transpose(0, 1, 5, 2, 3, 4)


def kernel(x_prompt, x_sample, cache_nsa_kv, state_nsa_win, state_gla, state_pool, page_table,
           norm_mix, norm_ffn, w_in, w_out, pool_w, pool_scale, nsa_q_norm, nsa_k_norm,
           nsa_cmp_w, nsa_cmp_pe, gla_wa2, gla_ba, gla_norm, ffn_w1, ffn_w3, ffn_w2,
           moe_router, moe_w1, moe_w3, moe_w2):
    bp, tp, d = x_prompt.shape
    bs, ts, _ = x_sample.shape
    depth = w_in.shape[0]
    n_pool = cache_nsa_kv.shape[1]
    n_pages = page_table.shape[1]
    past_len = n_pages * PAGE_SIZE
    w_buf = state_nsa_win.shape[2]
    npr = bp * tp
    n_all = npr + bs * ts
    cache_v = cache_nsa_kv.transpose(0, 1, 3, 4, 5, 2).reshape(depth, n_pool, 4 * NSA_KV * HEAD_DIM, PAGE_SIZE)
    win_state_v = state_nsa_win.transpose(0, 1, 3, 4, 5, 2)
    x = jnp.concatenate([x_prompt.reshape(npr, d), x_sample.reshape(bs * ts, d)], axis=0)
    kv_p, kv_s, win_p, win_s, gla_p, gla_s, pool_p, pool_s = [], [], [], [], [], [], [], []
    n_win_p = min(WINDOW, tp)
    for l in range(depth):
        u_pool, q_rows, rows, win, misc, gla_rows = _inproj(x, norm_mix[l], w_in[l], nsa_q_norm[l], nsa_k_norm[l])
        cw = _cmp_weights(nsa_cmp_w[l], nsa_cmp_pe[l], nsa_k_norm[l])
        up = u_pool[:npr].reshape(bp, tp, POOL_WIDTH)
        us = u_pool[npr:].reshape(bs, ts, POOL_WIDTH)
        ys_pool = _pool(us, state_pool[l], pool_w[l], pool_scale[l], past_len, bs, ts)
        y_pool = _pool(u_pool, jnp.zeros((bp, POOL_BUF, POOL_WIDTH), F32), pool_w[l], pool_scale[l], 0, bp, tp)
        y_pool = y_pool.at[npr:].set(ys_pool.reshape(bs * ts, POOL_WIDTH).astype(F32))
        pool_p.append(jnp.concatenate([jnp.zeros((bp, POOL_BUF, POOL_WIDTH), F32), up], axis=1)[:, -POOL_BUF:])
        pool_s.append(jnp.concatenate([state_pool[l], us], axis=1)[:, -POOL_BUF:])
        pw = _paged_cmp_weights(nsa_cmp_w[l], nsa_cmp_pe[l])
        ys_nsa = _nsa_sample(q_rows[npr:], rows[npr:], win[npr:], misc[npr:], cw, pw, cache_v,
                             page_table, l, win_state_v, bs, ts, past_len)
        y_nsa = _nsa_prompt(q_rows, rows, win, misc, cw, bp, tp).at[npr:].set(ys_nsa)
        kv_p.append(rows)
        kv_s.append(rows[npr:].reshape(bs, ts, 4, NSA_KV, HEAD_DIM))
        win_full_p = jnp.concatenate([jnp.zeros((bp, WINDOW, 256), F32), win[:npr].reshape(bp, tp, 256)], axis=1)
        win_p.append(win_full_p[:, -n_win_p:].reshape(bp, n_win_p, 2, NSA_KV, HEAD_DIM))
        win_ext_s = jnp.concatenate([state_nsa_win[l], win[npr:].reshape(bs, ts, 2, NSA_KV, HEAD_DIM)], axis=1)
        win_s.append(win_ext_s[:, -w_buf:])
        ys_gla, ss = _gla(gla_rows[npr:], misc[npr:], state_gla[l], gla_wa2[l], gla_ba[l], gla_norm[l], bs, ts)
        y_gla, sp = _gla(gla_rows, misc, jnp.zeros((bp, GLA_HEADS, GLA_DK, GLA_DV), F32),
                         gla_wa2[l], gla_ba[l], gla_norm[l], bp, tp,
                         base=jnp.zeros((n_all, GLA_WIDTH), BF16).at[npr:].set(ys_gla))
        gla_p.append(sp.astype(state_gla.dtype))
        gla_s.append(ss.astype(state_gla.dtype))
        i = l // 2
        router = moe_router[i] if l % 2 else None
        res = _outproj(y_pool, y_nsa, y_gla, x, w_out[l], norm_ffn[l], router)
        if l % 2 == 0:
            h, hn = res
            x = _dense_ffn(h, hn, ffn_w1[i], ffn_w3[i], ffn_w2[i])
        else:
            h, hn, route = res
            if l == depth - 1:
                x = _moe_ffn(h, hn, route, moe_w1[i], moe_w3[i], moe_w2[i], split=npr)
            else:
                x = _moe_ffn(h, hn, route, moe_w1[i], moe_w3[i], moe_w2[i])
    x_p, x_s = x if isinstance(x, tuple) else (x[:npr], x[npr:])
    return (x_p.reshape(bp, tp, d), x_s.reshape(bs, ts, d),
            _kv_rows_out(kv_p, bp, tp), jnp.stack(kv_s), jnp.stack(win_p), jnp.stack(win_s),
            jnp.stack(gla_p), jnp.stack(gla_s), jnp.stack(pool_p), jnp.stack(pool_s))
```

```python
import functools
import math

import jax
import jax.numpy as jnp
from jax import lax
from jax.experimental import pallas as pl
from jax.experimental.pallas import tpu as pltpu

F32 = jnp.float32
BF16 = jnp.bfloat16

EPS = 1e-6
LANES = 128
HEAD_DIM = 64
PAGE_SIZE = 128
POOL_WINDOWS = (2, 4, 8, 16)
POOL_BUF = 15
POOL_WIDTH = 256
NSA_WIDTH = 512
NSA_HEADS = 8
NSA_KV = 2
NSA_HPG = 4
L_CMP = 32
L_SLC = 64
N_SEL = 16
WINDOW = 512
GLA_HEADS = 4
GLA_DK = 32
GLA_DV = 64
GLA_WIDTH = 256
GLA_RANK = 16
GLA_TAU = 16.0
GLA_SUB = 16
TOP_K = 2
N_IN_PAD = 2432
MASKED = -1e30
VMEM_LIMIT = 56 * 1024 * 1024


def _params(*sem):
    return pltpu.CompilerParams(dimension_semantics=sem, vmem_limit_bytes=VMEM_LIMIT)


def _tile(n, target):
    best = None
    for t in range(8, min(n, target) + 1, 8):
        if n % t == 0:
            best = t
    assert best is not None, (n, target)
    return best


def _dot(a, b):
    return jnp.dot(a, b, preferred_element_type=F32)


def _dot_nt(a, b):
    return lax.dot_general(a, b, (((1,), (1,)), ((), ())), preferred_element_type=F32)


def _dot_tn(a, b):
    return lax.dot_general(a, b, (((0,), (0,)), ((), ())), preferred_element_type=F32)


def _split2_dot(a, ones):
    hi = a.astype(BF16)
    lo = (a - hi.astype(F32)).astype(BF16)
    return _dot(hi, ones) + _dot(lo, ones)


def _split3_dot(ones, a):
    a1 = a.astype(BF16)
    r1 = a - a1.astype(F32)
    a2 = r1.astype(BF16)
    a3 = (r1 - a2.astype(F32)).astype(BF16)
    return _dot(ones, a1) + _dot(ones, a2) + _dot(ones, a3)


def _head_rmsnorm(a, gain_row, seg_ones):
    ms = _split2_dot(a * a, seg_ones) * (1.0 / HEAD_DIM)
    return a * lax.rsqrt(ms + EPS) * gain_row


def _masked_softmax(s, mask):
    sm = jnp.where(mask, s, MASKED)
    m = jnp.max(sm, axis=-1, keepdims=True)
    e = jnp.where(mask, jnp.exp(sm - m), 0.0)
    den = jnp.sum(e, axis=-1, keepdims=True)
    return e * (1.0 / jnp.where(den > 0.0, den, 1.0))


def _seg_ones():
    i = jnp.arange(LANES)
    return (i[:, None] // HEAD_DIM == i[None, :] // HEAD_DIM).astype(BF16)


def _inproj_kernel(x_ref, g_ref, w_ref, qg_ref, kg_ref, seg_ref,
                   pool_o, q_o, rows_o, win_o, misc_o, gla_o):
    x = x_ref[...]
    ms = jnp.mean(x * x, axis=-1, keepdims=True)
    xn = (x * lax.rsqrt(ms + EPS) * g_ref[...]).astype(BF16)
    seg = seg_ref[...]

    def mm(c0, c1):
        return _dot(xn, w_ref[:, c0:c1])

    pool_o[...] = mm(0, 256)
    zq = mm(256, 768)
    for c in range(4):
        q_o[:, LANES * c:LANES * (c + 1)] = (
            _head_rmsnorm(zq[:, LANES * c:LANES * (c + 1)], qg_ref[...], seg) * (HEAD_DIM ** -0.5)).astype(BF16)
    zkv = mm(768, 1536)
    rows_o[:, 0:256] = zkv[:, 0:256]
    rows_o[:, 256:384] = _head_rmsnorm(zkv[:, 256:384], kg_ref[1:2, :], seg)
    rows_o[:, 384:512] = zkv[:, 384:512]
    win_o[:, 0:128] = _head_rmsnorm(zkv[:, 512:640], kg_ref[2:3, :], seg)
    win_o[:, 128:256] = zkv[:, 640:768]
    zg = mm(1536, 2432)
    misc_o[...] = zg[:, 0:128]
    gla_o[...] = zg[:, 128:896]


def _pad_w_in(w):
    d = w.shape[0]
    return jnp.concatenate([
        w[:, 0:1560], w[:, 2072:2088], jnp.zeros((d, 88), w.dtype),
        w[:, 1560:2072], w[:, 2088:2344]], axis=1)


def _inproj(x, g_mix, w_in_l, q_gain, k_gain):
    n, d = x.shape
    tm = _tile(n, 384)
    w = _pad_w_in(w_in_l).astype(BF16)
    qg = jnp.tile(q_gain, 2)[None, :]
    kg = jnp.zeros((8, LANES), F32).at[0:3].set(jnp.tile(k_gain, (1, 2)))
    full = lambda shape: pl.BlockSpec(shape, lambda i: (0, 0))
    row = lambda c: pl.BlockSpec((tm, c), lambda i: (i, 0))
    return pl.pallas_call(
        _inproj_kernel,
        grid=(n // tm,),
        in_specs=[row(d), full((1, d)), full((d, N_IN_PAD)), full((1, LANES)),
                  full((8, LANES)), full((LANES, LANES))],
        out_specs=[row(256), row(512), row(512), row(256), row(128), row(768)],
        out_shape=[jax.ShapeDtypeStruct((n, 256), F32), jax.ShapeDtypeStruct((n, 512), BF16),
                   jax.ShapeDtypeStruct((n, 512), F32), jax.ShapeDtypeStruct((n, 256), F32),
                   jax.ShapeDtypeStruct((n, 128), F32), jax.ShapeDtypeStruct((n, 768), F32)],
        compiler_params=_params("arbitrary"),
        name="inproj",
    )(x, g_mix[None, :], w, qg, kg, _seg_ones())


def _pool_kernel(buf_ref, u_ref, w_ref, sc_ref, *rest, pos0, tp):
    o_ref, ext = rest[-2:]
    i = pl.program_id(1)

    @pl.when(i == 0)
    def _():
        ext[0:16, :] = buf_ref[0]

    ext[16:16 + tp, :] = u_ref[...].reshape(tp, POOL_WIDTH)
    u0 = ext[16:16 + tp, :]
    acc = u0
    sums = {}
    for k in range(1, 16):
        acc = acc + ext[16 - k:16 - k + tp, :]
        if k + 1 in POOL_WINDOWS:
            sums[k + 1] = acc
    lane = lax.broadcasted_iota(jnp.int32, (tp, POOL_WIDTH), 1)
    pos = pos0 + i * tp + lax.broadcasted_iota(jnp.int32, (tp, POOL_WIDTH), 0)
    grp = lane // (POOL_WIDTH // len(POOL_WINDOWS))
    total = sums[16]
    wsize = jnp.full((tp, POOL_WIDTH), 16, jnp.int32)
    for gi, wz in enumerate(POOL_WINDOWS[:-1]):
        total = jnp.where(grp == gi, sums[wz], total)
        wsize = jnp.where(grp == gi, wz, wsize)
    cnt = jnp.minimum(wsize, pos + 1).astype(F32)
    dlt = total / cnt - u0
    o_ref[...] = (_dot(dlt.astype(BF16), w_ref[...]) * sc_ref[...]).astype(o_ref.dtype).reshape(o_ref.shape)
    if tp >= 16:
        ext[0:16, :] = ext[tp:tp + 16, :]


def _aliased_base(base, n_in):
    if base is None:
        return [], [], {}
    return [pl.BlockSpec(memory_space=pl.ANY)], [base], {n_in: 0}


def _pool(u, buf, pool_w_l, pool_scale_l, pos0, b, t):
    in_place = u.ndim == 2
    base_specs, base_args, alias = _aliased_base(u if in_place else None, 4)
    c = u.shape[-1]
    tp = _tile(t, 512) if t >= 8 else t
    nt = t // tp
    buf16 = jnp.concatenate([jnp.zeros((b, 1, c), F32), buf.astype(F32)], axis=1)
    gw = c // len(POOL_WINDOWS)
    wbd = jnp.zeros((c, c), F32)
    for gi in range(len(POOL_WINDOWS)):
        wbd = wbd.at[gi * gw:(gi + 1) * gw, gi * gw:(gi + 1) * gw].set(pool_w_l[gi])
    if in_place:
        rows = pl.BlockSpec((tp, c), lambda i, j: (i * nt + j, 0))
    else:
        rows = pl.BlockSpec((1, tp, c), lambda i, j: (i, j, 0))
    return pl.pallas_call(
        functools.partial(_pool_kernel, pos0=pos0, tp=tp),
        grid=(b, nt),
        in_specs=[pl.BlockSpec((1, 16, c), lambda i, j: (i, 0, 0)), rows,
                  pl.BlockSpec((c, c), lambda i, j: (0, 0)),
                  pl.BlockSpec((1, c), lambda i, j: (0, 0))] + base_specs,
        out_specs=rows,
        out_shape=jax.ShapeDtypeStruct(u.shape, F32 if in_place else BF16),
        scratch_shapes=[pltpu.VMEM((16 + tp, c), F32)],
        input_output_aliases=alias,
        compiler_params=_params("arbitrary", "arbitrary"),
        name="pool",
    )(buf16, u, wbd.astype(BF16), pool_scale_l[None, :], *base_args)


def _gla_kernel(gla_ref, misc_ref, s0_ref, wa_ref, ba_ref, og_ref, seg_ref, eb_ref, mk_ref, *rest, tg, t_valid):
    o_ref, sT_ref, st, qs, ks, bs, qts, kts, vs, os_, dls, us, ss = rest[-13:]
    i = pl.program_id(1)
    c = GLA_SUB
    nsub = tg // c

    @pl.when(i == 0)
    def _():
        st[...] = s0_ref[0]

    gl = gla_ref[...]
    q = gl[:, 0:128] * (GLA_DK ** -0.5)
    k = gl[:, 128:256]
    v = gl[:, 256:512]
    r = gl[:, 512:768]
    x = _dot(misc_ref[...].astype(BF16), wa_ref[...]) + ba_ref[...]
    la = (jnp.minimum(x, 0.0) - jnp.log1p(jnp.exp(-jnp.abs(x)))) * (1.0 / GLA_TAU)
    row = lax.broadcasted_iota(jnp.int32, (tg, LANES), 0)
    if t_valid is not None:
        la = jnp.where(i * tg + row < t_valid, la, 0.0)
    rr = lax.broadcasted_iota(jnp.int32, (tg, tg), 0)
    cc = lax.broadcasted_iota(jnp.int32, (tg, tg), 1)
    same = (rr // c) == (cc // c)
    tri = (same & (cc <= rr)).astype(BF16)
    allo = same.astype(BF16)
    b = _split3_dot(tri, la)
    blast = _split3_dot(allo, la)
    qs[...] = q
    ks[...] = k
    bs[...] = b
    qts[...] = (q * jnp.exp(b)).astype(BF16)
    kts[...] = (k * jnp.exp(blast - b)).astype(BF16)
    vs[...] = v
    eb = eb_ref[...]
    mk = mk_ref[...]
    tt = lax.broadcasted_iota(jnp.int32, (c, LANES), 0)

    dls[...] = jnp.exp(blast)

    def local(j, carry):
        r0 = pl.multiple_of(j * c, c)
        qi = qs[pl.ds(r0, c), :]
        ki = ks[pl.ds(r0, c), :]
        bi = bs[pl.ds(r0, c), :]
        vi = vs[pl.ds(r0, c), :]
        parts = []
        for s in range(c):
            dec = jnp.exp(jnp.minimum(bi - bi[s:s + 1, :], 0.0))
            parts.append(jnp.where(tt >= s, qi * ki[s:s + 1, :] * dec, 0.0))
        p_all = jnp.concatenate(parts, axis=0).astype(BF16)
        a_all = _dot(p_all, eb)
        o_diag = a_all[0:c, :] * vi[0:1, :]
        for s in range(1, c):
            o_diag = o_diag + a_all[s * c:(s + 1) * c, :] * vi[s:s + 1, :]
        os_[pl.ds(r0, c), :] = o_diag
        us[j] = _dot_tn(vi.astype(BF16), kts[pl.ds(r0, c), :]) * mk
        return carry

    def grouped(body, group):
        group = math.gcd(nsub, group)

        def trip(jj, carry):
            for u in range(group):
                body(jj * group + u, carry)
            return carry

        lax.fori_loop(0, nsub // group, trip, 0)

    grouped(local, 8)

    def recur(j, carry):
        s_t = st[...]
        ss[j] = s_t.astype(BF16)
        st[...] = s_t * dls[pl.ds(pl.multiple_of(j * c, c), 1), :] + us[j]
        return carry

    lax.fori_loop(0, nsub, recur, 0)

    def inter(j, carry):
        r0 = pl.multiple_of(j * c, c)
        os_[pl.ds(r0, c), :] += _dot_nt(qts[pl.ds(r0, c), :], ss[j])
        return carry

    grouped(inter, 8)

    o = os_[...]
    seg = seg_ref[...]
    og = og_ref[...]
    sil = r * (1.0 / (1.0 + jnp.exp(-r)))
    for h in range(2):
        sl = slice(h * LANES, (h + 1) * LANES)
        o_ref[:, sl] = (_head_rmsnorm(o[:, sl], og[:, sl], seg) * sil[:, sl]).astype(o_ref.dtype)
    sT_ref[0] = st[...]


def _gla(gla_rows, misc_rows, s0, wa2, ba, o_gain, b, t, base=None):
    base_specs, base_args, alias = _aliased_base(base, 9)
    t_valid = None
    if t % GLA_SUB:
        t_valid = t
        tp = -(-t // GLA_SUB) * GLA_SUB
        pad = lambda z: jnp.pad(z.reshape(b, t, -1), ((0, 0), (0, tp - t), (0, 0))).reshape(b * tp, -1)
        gla_rows, misc_rows = pad(gla_rows), pad(misc_rows)
    else:
        tp = t
    tg = _tile(tp, 256)
    assert tg % GLA_SUB == 0
    nt = tp // tg
    kk = GLA_HEADS * GLA_DK
    vv = GLA_HEADS * GLA_DV
    ki = jnp.arange(kk)
    vi = jnp.arange(vv)
    head_eq = (vi[:, None] // GLA_DV == ki[None, :] // GLA_DK)
    mk = head_eq.astype(F32)
    eb = head_eq.T.astype(BF16)
    s0t = jnp.einsum('bhkv,hg->bhvgk', s0.astype(F32), jnp.eye(GLA_HEADS, dtype=F32)).reshape(b, vv, kk)
    wa = jnp.zeros((LANES, kk), F32).at[24:24 + GLA_RANK].set(wa2).astype(BF16)
    full = lambda shape: pl.BlockSpec(shape, lambda i, j: (0,) * len(shape))
    rows = lambda cdim: pl.BlockSpec((tg, cdim), lambda i, j: (i * nt + j, 0))
    o, s_t = pl.pallas_call(
        functools.partial(_gla_kernel, tg=tg, t_valid=t_valid),
        grid=(b, nt),
        in_specs=[rows(768), rows(128), pl.BlockSpec((1, vv, kk), lambda i, j: (i, 0, 0)),
                  full((LANES, kk)), full((1, kk)), full((1, vv)), full((LANES, LANES)),
                  full((kk, vv)), full((vv, kk))] + base_specs,
        input_output_aliases=alias,
        out_specs=[rows(vv), pl.BlockSpec((1, vv, kk), lambda i, j: (i, 0, 0))],
        out_shape=[jax.ShapeDtypeStruct((gla_rows.shape[0], vv), BF16), jax.ShapeDtypeStruct((b, vv, kk), F32)],
        scratch_shapes=[pltpu.VMEM((vv, kk), F32), pltpu.VMEM((tg, kk), F32), pltpu.VMEM((tg, kk), F32),
                        pltpu.VMEM((tg, kk), F32), pltpu.VMEM((tg, kk), BF16), pltpu.VMEM((tg, kk), BF16),
                        pltpu.VMEM((tg, vv), F32), pltpu.VMEM((tg, vv), F32), pltpu.VMEM((tg, kk), F32),
                        pltpu.VMEM((tg // GLA_SUB, vv, kk), F32), pltpu.VMEM((tg // GLA_SUB, vv, kk), BF16)],
        compiler_params=_params("arbitrary", "arbitrary"),
        name="gla",
    )(gla_rows, misc_rows, s0t, wa, ba[None, :], jnp.tile(o_gain, GLA_HEADS)[None, :], _seg_ones(), eb, mk,
      *base_args)
    if tp != t:
        o = o.reshape(b, tp, vv)[:, :t].reshape(b * t, vv)
    s5 = s_t.reshape(b, GLA_HEADS, GLA_DV, GLA_HEADS, GLA_DK)
    s_new = jnp.einsum('bhvgk,hg->bhkv', s5, jnp.eye(GLA_HEADS, dtype=F32))
    return o, s_new


def _compress_rows(read, nb, pe_ref, wk_ref, wv_ref, kg_ref, seg_ref):
    acck = jnp.zeros((nb, LANES), F32)
    accv = jnp.zeros((nb, LANES), F32)
    for j in range(L_CMP):
        xk = (read(0, j) + pe_ref[0, j:j + 1, :]).astype(BF16)
        xv = (read(1, j) + pe_ref[1, j:j + 1, :]).astype(BF16)
        acck = acck + _dot(xk, wk_ref[j])
        accv = accv + _dot(xv, wv_ref[j])
    kc = _head_rmsnorm(acck, kg_ref[0:1, :], seg_ref[...])
    return kc, accv


def _compress_kernel(rk_ref, rv_ref, pe_ref, wk_ref, wv_ref, kg_ref, seg_ref, o_ref, *, nb):
    refs = (rk_ref, rv_ref)
    kc, vc = _compress_rows(lambda kind, j: refs[kind][pl.ds(j, nb, stride=L_CMP), :], nb,
                            pe_ref, wk_ref, wv_ref, kg_ref, seg_ref)
    o_ref[0, :, 0:LANES] = kc
    o_ref[0, :, LANES:2 * LANES] = vc


def _cmp_weights(cmp_w, cmp_pe, k_gain):
    def bd(w):
        z = jnp.zeros_like(w)
        return jnp.concatenate([jnp.concatenate([w, z], axis=2), jnp.concatenate([z, w], axis=2)], axis=1)
    pe = jnp.tile(cmp_pe, (1, 1, 2))
    kg = jnp.zeros((8, LANES), F32).at[0].set(jnp.tile(k_gain[0], 2))
    return pe, bd(cmp_w[0]).astype(BF16), bd(cmp_w[1]).astype(BF16), kg


def _compress(rk, rv, k_col, v_col, b, tp, cw):
    nb = tp // L_CMP
    pe, wk, wv, kg = cw
    full = lambda shape: pl.BlockSpec(shape, lambda i: (0,) * len(shape))
    return pl.pallas_call(
        functools.partial(_compress_kernel, nb=nb),
        grid=(b,),
        in_specs=[pl.BlockSpec((tp, LANES), lambda i: (i, k_col)), pl.BlockSpec((tp, LANES), lambda i: (i, v_col)),
                  full((2, L_CMP, LANES)), full((L_CMP, LANES, LANES)), full((L_CMP, LANES, LANES)),
                  full((8, LANES)), full((LANES, LANES))],
        out_specs=pl.BlockSpec((1, nb, 256), lambda i: (i, 0, 0)),
        out_shape=jax.ShapeDtypeStruct((b, nb, 256), F32),
        compiler_params=_params("arbitrary"),
        name="compress",
    )(rk, rv, pe, wk, wv, kg, _seg_ones())


SLAB_PITCH = 2 * NSA_KV * HEAD_DIM + 8


def _paged_compress_kernel(pt_ref, cache_ref, pe_ref, m_ref, kg_ref, seg_ref, o_ref,
                           slab, sem, *, pages, n_slab, layer):
    bi = pl.program_id(0)
    si = pl.program_id(1)
    step = bi * n_slab + si
    nsteps = pl.num_programs(0) * n_slab
    slot = step % 2
    rows_cmp = 2 * NSA_KV * HEAD_DIM

    def copy(page, slot_, p):
        return pltpu.make_async_copy(cache_ref.at[layer, page, pl.ds(0, rows_cmp), :],
                                     slab.at[slot_, pl.ds(p * SLAB_PITCH, rows_cmp), :], sem.at[slot_])

    def issue(bb, ss, slot_):
        for p in range(pages):
            copy(pt_ref[bb, ss * pages + p], slot_, p).start()

    @pl.when(step == 0)
    def _():
        issue(0, 0, 0)

    @pl.when(step + 1 < nsteps)
    def _():
        nxt = step + 1
        issue(nxt // n_slab, nxt % n_slab, 1 - slot)

    for p in range(pages):
        copy(0, slot, p).wait()

    def rows(r0):
        return slab[slot, pl.ds(r0, pages, stride=SLAB_PITCH), :]

    for c in range(2):
        acc = jnp.zeros((NSA_KV * pages, 2 * LANES), F32)
        for dp in range(HEAD_DIM // 2):
            parts = []
            for g in range(NSA_KV):
                r0 = (c * NSA_KV + g) * HEAD_DIM + 2 * dp
                parts.append(jnp.concatenate([rows(r0), rows(r0 + 1)], axis=1))
            a = jnp.concatenate(parts, axis=0) + pe_ref[c, dp:dp + 1, :]
            acc = acc + _dot(a.astype(BF16), m_ref[c, dp])
        for g in range(NSA_KV):
            blk = acc[g * pages:(g + 1) * pages, :]
            if c == 0:
                for h in range(2):
                    sl = slice(h * LANES, (h + 1) * LANES)
                    o_ref[0, c, g, :, sl] = _head_rmsnorm(blk[:, sl], kg_ref[0:1, :], seg_ref[...])
            else:
                o_ref[0, c, g] = blk


def _paged_cmp_weights(cmp_w, cmp_pe):
    nblk = PAGE_SIZE // L_CMP
    k6 = jnp.einsum('nm,cjde->cdnjme', jnp.eye(nblk, dtype=F32), cmp_w)
    m = k6.reshape(2, HEAD_DIM // 2, 2 * PAGE_SIZE, nblk * HEAD_DIM).astype(BF16)
    pe = jnp.tile(cmp_pe.transpose(0, 2, 1), (1, 1, nblk)).reshape(2, HEAD_DIM // 2, 2 * PAGE_SIZE)
    return pe, m


def _paged_compress(cache_v, page_table, layer, cw, pw):
    b, n_pages = page_table.shape
    pages = math.gcd(n_pages, 64)
    n_slab = n_pages // pages
    pe, m = pw
    kg = cw[3]
    nblk = PAGE_SIZE // L_CMP
    full = lambda shape: pl.BlockSpec(shape, lambda i, j, pt: (0,) * len(shape))
    grid_spec = pltpu.PrefetchScalarGridSpec(
        num_scalar_prefetch=1,
        grid=(b, n_slab),
        in_specs=[pl.BlockSpec(memory_space=pl.ANY), full(pe.shape), full(m.shape), full((8, LANES)),
                  full((LANES, LANES))],
        out_specs=pl.BlockSpec((1, 2, NSA_KV, pages, nblk * HEAD_DIM), lambda i, j, pt: (i, 0, 0, j, 0)),
        scratch_shapes=[pltpu.VMEM((2, pages * SLAB_PITCH, LANES), F32), pltpu.SemaphoreType.DMA((2,))],
    )
    return pl.pallas_call(
        functools.partial(_paged_compress_kernel, pages=pages, n_slab=n_slab, layer=layer),
        grid_spec=grid_spec,
        out_shape=jax.ShapeDtypeStruct((b, 2, NSA_KV, n_pages, nblk * HEAD_DIM), F32),
        compiler_params=_params("arbitrary", "arbitrary"),
        name="paged_compress",
    )(page_table, cache_v, pe, m, kg, _seg_ones())


def _nsa_prompt_kernel(q_ref, kc_ref, vc_ref, ks_ref, vs_ref, kw_ref, vw_ref, gate_ref, base_ref,
                       o_ref, ksb, vsb, kwb, vwb, *, tq, ck):
    i = pl.program_id(1)
    t0 = i * tq

    n_slc = LANES // 2
    t_keys = ks_ref.shape[0]

    @pl.when(i == 0)
    def _():
        ksb[:, 0:LANES] = ks_ref[...].astype(BF16)
        kblk = lax.broadcasted_iota(jnp.int32, (t_keys, LANES), 0) // L_SLC
        klane = lax.broadcasted_iota(jnp.int32, (t_keys, LANES), 1)
        ksb[:, LANES:2 * LANES] = jnp.where(kblk == klane, 1.0, 0.0).astype(BF16)
        vsb[...] = vs_ref[...].astype(BF16)
        kwb[...] = kw_ref[...].astype(BF16)
        vwb[...] = vw_ref[...].astype(BF16)

    row_t = t0 + lax.broadcasted_iota(jnp.int32, (tq, 1), 0)
    col_t = t0 + lax.broadcasted_iota(jnp.int32, (1, tq), 1)
    lane = lax.broadcasted_iota(jnp.int32, (1, LANES), 1)
    crow = lax.broadcasted_iota(jnp.int32, (LANES, 1), 0)
    nat = jnp.where(crow < n_slc, 2 * crow, 2 * (crow - n_slc) + 1)
    cmp_ok = ((nat + 1) * L_CMP - 1) <= col_t
    blk = lax.broadcasted_iota(jnp.int32, (n_slc, 1), 0)
    cur = col_t // L_SLC
    done = blk < cur
    gx = gate_ref[...]
    gates = 1.0 / (1.0 + jnp.exp(-gx))
    n_full = t0 // ck
    kstart = pl.multiple_of(jnp.maximum(t0 - WINDOW, 0), tq)
    wlen = WINDOW + tq
    wpos = kstart + lax.broadcasted_iota(jnp.int32, (1, wlen), 1)
    win_bias = jnp.where((wpos <= row_t) & (wpos >= row_t - WINDOW), 0.0, MASKED)

    for g in range(NSA_KV):
        in_g = (lane // HEAD_DIM) == g
        heads = []
        for j in range(NSA_HPG):
            h = g * NSA_HPG + j
            qh = q_ref[:, (h // 2) * LANES:(h // 2 + 1) * LANES].astype(F32)
            if h % 2 != g:
                qh = pltpu.roll(qh, HEAD_DIM, axis=1)
            heads.append(jnp.where(in_g, qh, 0.0).astype(BF16))
        qg = jnp.concatenate(heads, axis=0)
        s_c = _dot_nt(kc_ref[0], qg)
        imp = jnp.zeros((LANES, tq), F32)
        o_c = []
        for j in range(NSA_HPG):
            sj = jnp.where(cmp_ok, s_c[:, j * tq:(j + 1) * tq], MASKED)
            ej = jnp.where(cmp_ok, jnp.exp(sj - jnp.max(sj, axis=0, keepdims=True)), 0.0)
            den = jnp.sum(ej, axis=0, keepdims=True)
            pj = ej * (1.0 / jnp.where(den > 0.0, den, 1.0))
            imp = imp + pj
            o_c.append(_dot_tn(pj.astype(BF16), vc_ref[0]))
        imp = imp[0:n_slc] + imp[n_slc:LANES]
        key = jnp.where(done, lax.bitcast_convert_type(imp, jnp.int32), -1)
        key_m1 = key - 1
        rank = jnp.zeros((n_slc, tq), jnp.int32)
        for r in range(1, n_slc):
            vm = pltpu.roll(key, r, axis=0)
            rank = rank + jnp.where(vm > jnp.where(blk >= r, key_m1, key), 1, 0)
        sel = (done & (rank < N_SEL - 1)) | (blk == cur)
        sel_bias = jnp.concatenate([jnp.where(sel, 0.0, MASKED), jnp.zeros((n_slc, tq), F32)], axis=0)
        sel_bias = sel_bias.T.astype(BF16)
        qx = jnp.concatenate([qg, jnp.concatenate([sel_bias] * NSA_HPG, axis=0)], axis=1)

        def chunk(c, carry, diagonal):
            m, l, acc = carry
            k0 = pl.multiple_of(c * ck, ck)
            sm = _dot_nt(qx, ksb[pl.ds(k0, ck), :]).reshape(NSA_HPG, tq, ck)
            if diagonal:
                kpos = k0 + lax.broadcasted_iota(jnp.int32, (1, ck), 1)
                sm = sm + jnp.where(kpos <= row_t, 0.0, MASKED)[None]
            m_new = jnp.maximum(m, jnp.max(sm, axis=-1, keepdims=True))
            p = jnp.exp(sm - m_new)
            alpha = jnp.exp(m - m_new)
            l = alpha * l + jnp.sum(p, axis=-1, keepdims=True)
            pv = _dot(p.reshape(NSA_HPG * tq, ck).astype(BF16), vsb[pl.ds(k0, ck), :])
            acc = alpha * acc + pv.reshape(NSA_HPG, tq, LANES)
            return m_new, l, acc

        m0 = jnp.full((NSA_HPG, tq, 1), MASKED, F32)
        l0 = jnp.zeros((NSA_HPG, tq, 1), F32)
        a0 = jnp.zeros((NSA_HPG, tq, LANES), F32)
        carry = lax.fori_loop(0, n_full, functools.partial(chunk, diagonal=False), (m0, l0, a0))
        _, l_s, acc_s = chunk(n_full, carry, True)
        o_s = acc_s * (1.0 / jnp.where(l_s > 0.0, l_s, 1.0))
        s_w = _dot_nt(qg, kwb[pl.ds(kstart, wlen), :]).reshape(NSA_HPG, tq, wlen) + win_bias[None]
        e_w = jnp.exp(s_w - jnp.max(s_w, axis=-1, keepdims=True))
        o_w = _dot(e_w.reshape(NSA_HPG * tq, wlen).astype(BF16), vwb[pl.ds(kstart, wlen), :])
        o_w = o_w.reshape(NSA_HPG, tq, LANES) * (1.0 / jnp.sum(e_w, axis=-1, keepdims=True))
        for pr in range(NSA_HPG // 2):
            pair = []
            for jj in range(2):
                j = 2 * pr + jj
                h = g * NSA_HPG + j
                o = (gates[:, 3 * h:3 * h + 1] * o_c[j] + gates[:, 3 * h + 1:3 * h + 2] * o_s[j]
                     + gates[:, 3 * h + 2:3 * h + 3] * o_w[j])
                pair.append(o if jj == g else pltpu.roll(o, HEAD_DIM, axis=1))
            col = (g * (NSA_HPG // 2) + pr) * LANES
            o_ref[:, col:col + LANES] = jnp.where(lane < HEAD_DIM, pair[0], pair[1]).astype(o_ref.dtype)


def _nsa_prompt(q_rows, rows, win, misc, cw, b, t):
    assert t // L_CMP == LANES and t % L_SLC == 0, "prompt kernel is laid out for 128 compressed blocks"
    tq = 256
    ck = 1024
    assert ck % tq == 0 and t % ck == 0
    cmp = _compress(rows, rows, 0, 1, b, t, cw)
    order = jnp.concatenate([jnp.arange(0, LANES, 2), jnp.arange(1, LANES, 2)])
    cmp = cmp[:, order].astype(BF16)
    kc, vc = cmp[:, :, 0:LANES], cmp[:, :, LANES:2 * LANES]
    nt = t // tq
    col = lambda c: pl.BlockSpec((t, LANES), lambda i, j: (i, c))
    return pl.pallas_call(
        functools.partial(_nsa_prompt_kernel, tq=tq, ck=ck),
        grid=(b, nt),
        in_specs=[pl.BlockSpec((tq, NSA_WIDTH), lambda i, j: (i * nt + j, 0)),
                  pl.BlockSpec((1, LANES, LANES), lambda i, j: (i, 0, 0)),
                  pl.BlockSpec((1, LANES, LANES), lambda i, j: (i, 0, 0)),
                  col(2), col(3), col(0), col(1),
                  pl.BlockSpec((tq, LANES), lambda i, j: (i * nt + j, 0)),
                  pl.BlockSpec(memory_space=pl.ANY)],
        out_specs=pl.BlockSpec((tq, NSA_WIDTH), lambda i, j: (i * nt + j, 0)),
        out_shape=jax.ShapeDtypeStruct((q_rows.shape[0], NSA_WIDTH), BF16),
        scratch_shapes=[pltpu.VMEM((t, 2 * LANES), BF16)] + [pltpu.VMEM((t, LANES), BF16)] * 3,
        input_output_aliases={8: 0},
        compiler_params=_params("arbitrary", "arbitrary"),
        name="nsa_prompt",
    )(q_rows, kc, vc, rows, rows, win, win, misc, q_rows)


def _split3_dot_r(a, ones):
    a1 = a.astype(BF16)
    r1 = a - a1.astype(F32)
    a2 = r1.astype(BF16)
    a3 = (r1 - a2.astype(F32)).astype(BF16)
    return _dot(a1, ones) + _dot(a2, ones) + _dot(a3, ones)


def _nsa_select_kernel(q_ref, cp_ref, ct_ref, pair_ref, oc_ref, sel_ref, *, t, past_len, n_past):
    nrow = NSA_HPG * t
    npad = n_past + LANES
    n_slc_pad = pair_ref.shape[1]
    row = lax.broadcasted_iota(jnp.int32, (nrow, 1), 0)
    pos = past_len + row % t
    lane_c = lax.broadcasted_iota(jnp.int32, (1, npad), 1)
    cmp_ok = ((lane_c + 1) * L_CMP - 1) <= pos
    cur = (past_len + lax.broadcasted_iota(jnp.int32, (t, 1), 0)) // L_SLC
    lane_s = lax.broadcasted_iota(jnp.int32, (1, n_slc_pad), 1)
    lane_o = lax.broadcasted_iota(jnp.int32, (1, LANES), 1)
    done = lane_s < cur
    keys = []
    for g in range(NSA_KV):
        q = q_ref[0, g]
        s = jnp.concatenate([_dot_nt(q, cp_ref[0, 0, g].astype(BF16)),
                             _dot_nt(q, ct_ref[0, 0, g].astype(BF16))], axis=1)
        p = _masked_softmax(s, cmp_ok)
        pb = p.astype(BF16)
        oc_ref[0, g] = (_dot(pb[:, :n_past], cp_ref[0, 1, g].astype(BF16))
                        + _dot(pb[:, n_past:], ct_ref[0, 1, g].astype(BF16)))
        imp_c = p[0:t]
        for j in range(1, NSA_HPG):
            imp_c = imp_c + p[j * t:(j + 1) * t]
        imp = _split3_dot_r(imp_c, pair_ref[...])
        keys.append(jnp.where(done, lax.bitcast_convert_type(imp, jnp.int32), -1))
    key = jnp.concatenate(keys, axis=0)
    key_col = key.T
    n_idx = lax.broadcasted_iota(jnp.int32, (n_slc_pad, n_slc_pad), 0)
    m_idx = lax.broadcasted_iota(jnp.int32, (n_slc_pad, n_slc_pad), 1)
    m_first = jnp.where(m_idx < n_idx, 1, 0)
    n_col = lax.broadcasted_iota(jnp.int32, (n_slc_pad, 1), 0).astype(F32)
    slot = lane_o.astype(F32)
    rows = []
    for r in range(NSA_KV * t):
        ahead = key[r:r + 1, :] > (key_col[:, r:r + 1] - m_first)
        rank = jnp.sum(jnp.where(ahead, 1.0, 0.0), axis=1, keepdims=True)
        rows.append(jnp.sum(jnp.where(rank == slot, n_col, 0.0), axis=0, keepdims=True))
    picked = jnp.concatenate(rows, axis=0).astype(jnp.int32)
    sel_ref[0] = jnp.where(lane_o == N_SEL - 1, jnp.concatenate([cur] * NSA_KV, axis=0), picked)


def _nsa_attend_kernel(pt_ref, sel_ref, cache_ref, q_ref, qp_ref, oc_ref, gate_ref, new_ref,
                       wst_ref, o_ref, buf, sc, sem, *, t, past_len, layer, w_buf):
    bi = pl.program_id(0)
    slot = bi % 2
    nsel = N_SEL - 1
    n_past_blk = past_len // L_SLC
    per_page = PAGE_SIZE // L_SLC
    nrow = t * NSA_HPG

    def block_of(bb, g, tok, k):
        return jnp.minimum(sel_ref[bb, (g * t + tok) * N_SEL + k], n_past_blk - 1)

    def copy(page, slot_, g, idx):
        return pltpu.make_async_copy(cache_ref.at[layer, page, pl.ds(2, 2), g], buf.at[slot_, idx], sem.at[slot_])

    def issue(bb, slot_):
        for g in range(NSA_KV):
            for tok in range(t):
                for k in range(nsel):
                    page = pt_ref[bb, block_of(bb, g, tok, k) // per_page]
                    copy(page, slot_, g, (g * t + tok) * nsel + k).start()

    @pl.when(bi == 0)
    def _():
        issue(0, 0)

    @pl.when(bi + 1 < pl.num_programs(0))
    def _():
        issue(bi + 1, 1 - slot)

    for g in range(NSA_KV):
        for i in range(t * nsel):
            copy(0, slot, g, g * t * nsel + i).wait()

    row = lax.broadcasted_iota(jnp.int32, (nrow, 1), 0)
    tok_r = row // NSA_HPG
    pos = past_len + tok_r
    lane = lax.broadcasted_iota(jnp.int32, (1, LANES), 1)
    new_bias = jnp.where(lane <= tok_r, 0.0, MASKED)
    wpos = past_len - w_buf + lax.broadcasted_iota(jnp.int32, (1, w_buf), 1)
    win_bias = jnp.where((wpos >= 0) & (wpos <= pos) & (wpos >= pos - WINDOW), 0.0, MASKED)
    for g in range(NSA_KV):
        q = q_ref[0, g]
        qp = qp_ref[0, g]
        glanes = slice(g * HEAD_DIM, (g + 1) * HEAD_DIM)
        m = jnp.full((nrow, 1), MASKED, F32)
        for tok in range(t):
            cur = (past_len + tok) // L_SLC
            for k in range(nsel):
                n = sel_ref[bi, (g * t + tok) * N_SEL + k]
                half = block_of(bi, g, tok, k) % per_page
                valid = (tok_r == tok) & (lane // L_SLC == half) & (n < cur)
                i = tok * nsel + k
                s = _dot(q, buf[slot, g * t * nsel + i, 0].astype(BF16)) + jnp.where(valid, 0.0, MASKED)
                sc[i] = s
                m = jnp.maximum(m, jnp.max(s, axis=1, keepdims=True))
        s_new = _dot_nt(qp, new_ref[0, :, 0:LANES].astype(BF16)) + new_bias
        m = jnp.maximum(m, jnp.max(s_new, axis=1, keepdims=True))
        p_new = jnp.exp(s_new - m)
        l = jnp.sum(p_new, axis=1, keepdims=True)
        acc = _dot(p_new.astype(BF16), new_ref[0, :, LANES:2 * LANES].astype(BF16))[:, glanes]
        for i in range(t * nsel):
            p = jnp.exp(sc[i] - m)
            l = l + jnp.sum(p, axis=1, keepdims=True)
            acc = acc + _dot_nt(p.astype(BF16), buf[slot, g * t * nsel + i, 1].astype(BF16))
        o_s = acc * (1.0 / l)
        s_w = _dot(q, wst_ref[0, 0, 0, g].astype(BF16)) + win_bias
        s_wn = _dot_nt(qp, new_ref[0, :, 2 * LANES:3 * LANES].astype(BF16)) + new_bias
        m_w = jnp.maximum(jnp.max(s_w, axis=1, keepdims=True), jnp.max(s_wn, axis=1, keepdims=True))
        e_w = jnp.exp(s_w - m_w)
        e_n = jnp.exp(s_wn - m_w)
        l_w = jnp.sum(e_w, axis=1, keepdims=True) + jnp.sum(e_n, axis=1, keepdims=True)
        acc_w = (_dot_nt(e_w.astype(BF16), wst_ref[0, 0, 1, g].astype(BF16))
                 + _dot(e_n.astype(BF16), new_ref[0, :, 3 * LANES:4 * LANES].astype(BF16))[:, glanes])
        o_w = acc_w * (1.0 / l_w)
        gate = 1.0 / (1.0 + jnp.exp(-gate_ref[0, g]))
        o_ref[0, g] = gate[:, 0:1] * oc_ref[0, g] + gate[:, 1:2] * o_s + gate[:, 2:3] * o_w


def _nsa_sample(q_rows, rows, win, misc, cw, pw, cache_v, page_table, layer, win_state_v, b, t, past_len):
    t_all = past_len + t
    t_pad = -(-t_all // L_SLC) * L_SLC
    n_cmp = t_pad // L_CMP
    n_tail = (t_pad - past_len) // L_CMP
    n_past = past_len // L_CMP
    w_buf = win_state_v.shape[-1]
    assert past_len % PAGE_SIZE == 0 and t <= L_SLC and n_tail <= LANES and NSA_KV * t * N_SEL <= LANES
    cmp_past = _paged_compress(cache_v, page_table, layer, cw, pw).reshape(b, 2, NSA_KV, n_past, HEAD_DIM)
    tail = jnp.concatenate([rows[:, 0:256].reshape(b, t, 256),
                            jnp.zeros((b, t_pad - t_all, 256), F32)], axis=1).reshape(b * (t_pad - past_len), 256)
    cmp_tail = _compress(tail, tail, 0, 1, 1, b * (t_pad - past_len), cw)
    cmp_tail = cmp_tail.reshape(b, n_tail, 2, NSA_KV, HEAD_DIM).transpose(0, 2, 3, 1, 4)
    cmp_tail = jnp.pad(cmp_tail, ((0, 0), (0, 0), (0, 0), (0, LANES - n_tail), (0, 0)))
    q5 = q_rows.reshape(b, t, NSA_KV, NSA_HPG, HEAD_DIM)
    nrow = NSA_HPG * t
    q_jt = q5.transpose(0, 2, 3, 1, 4).reshape(b, NSA_KV, nrow, HEAD_DIM)
    q_tj = q5.transpose(0, 2, 1, 3, 4).reshape(b, NSA_KV, nrow, HEAD_DIM)
    qp_tj = jnp.zeros((b, NSA_KV, nrow, LANES), BF16)
    for g in range(NSA_KV):
        qp_tj = qp_tj.at[:, g, :, g * HEAD_DIM:(g + 1) * HEAD_DIM].set(q_tj[:, g])
    npad = n_past + LANES
    n_slc_pad = -(-(t_pad // L_SLC) // LANES) * LANES
    nn = jnp.arange(npad)
    pair = ((nn[:, None] // (L_SLC // L_CMP) == jnp.arange(n_slc_pad)[None, :]) & (nn[:, None] < n_cmp)).astype(BF16)
    per_b = lambda shape: pl.BlockSpec(shape, lambda i: (i,) + (0,) * (len(shape) - 1))
    o_c, sel = pl.pallas_call(
        functools.partial(_nsa_select_kernel, t=t, past_len=past_len, n_past=n_past),
        grid=(b,),
        in_specs=[per_b((1, NSA_KV, nrow, HEAD_DIM)), per_b((1, 2, NSA_KV, n_past, HEAD_DIM)),
                  per_b((1, 2, NSA_KV, LANES, HEAD_DIM)), pl.BlockSpec((npad, n_slc_pad), lambda i: (0, 0))],
        out_specs=[per_b((1, NSA_KV, nrow, HEAD_DIM)), per_b((1, NSA_KV * t, LANES))],
        out_shape=[jax.ShapeDtypeStruct((b, NSA_KV, nrow, HEAD_DIM), F32),
                   jax.ShapeDtypeStruct((b, NSA_KV * t, LANES), jnp.int32)],
        compiler_params=_params("arbitrary"),
        name="nsa_select",
    )(q_jt, cmp_past, cmp_tail, pair)
    sel_c = sel[:, :, :N_SEL].reshape(b, NSA_KV * t * N_SEL)
    o_c = o_c.reshape(b, NSA_KV, NSA_HPG, t, HEAD_DIM).transpose(0, 1, 3, 2, 4).reshape(b, NSA_KV, nrow, HEAD_DIM)
    gates = misc[:, 0:3 * NSA_HEADS].reshape(b, t, NSA_KV, NSA_HPG, 3).transpose(0, 2, 1, 3, 4)
    gates = jnp.pad(gates.reshape(b, NSA_KV, nrow, 3), ((0, 0), (0, 0), (0, 0), (0, LANES - 3)))
    new_rows = jnp.concatenate([rows[:, 256:512], win], axis=1).reshape(b, t, 512)
    new_rows = jnp.pad(new_rows, ((0, 0), (0, LANES - t), (0, 0)))
    cache6 = cache_v.reshape(cache_v.shape[0], cache_v.shape[1], 4, NSA_KV, HEAD_DIM, PAGE_SIZE)
    nsel = N_SEL - 1
    pb = lambda shape: pl.BlockSpec(shape, lambda i, pt, sl: (i,) + (0,) * (len(shape) - 1))
    grid_spec = pltpu.PrefetchScalarGridSpec(
        num_scalar_prefetch=2,
        grid=(b,),
        in_specs=[pl.BlockSpec(memory_space=pl.ANY), pb((1, NSA_KV, nrow, HEAD_DIM)), pb((1, NSA_KV, nrow, LANES)),
                  pb((1, NSA_KV, nrow, HEAD_DIM)), pb((1, NSA_KV, nrow, LANES)), pb((1, LANES, 512)),
                  pl.BlockSpec((1, 1, 2, NSA_KV, HEAD_DIM, w_buf), lambda i, pt, sl: (layer, i, 0, 0, 0, 0))],
        out_specs=pb((1, NSA_KV, nrow, HEAD_DIM)),
        scratch_shapes=[pltpu.VMEM((2, NSA_KV * t * nsel, 2, HEAD_DIM, PAGE_SIZE), F32),
                        pltpu.VMEM((t * nsel, nrow, LANES), F32), pltpu.SemaphoreType.DMA((2,))],
    )
    y = pl.pallas_call(
        functools.partial(_nsa_attend_kernel, t=t, past_len=past_len, layer=layer, w_buf=w_buf),
        grid_spec=grid_spec,
        out_shape=jax.ShapeDtypeStruct((b, NSA_KV, nrow, HEAD_DIM), F32),
        compiler_params=_params("arbitrary"),
        name="nsa_attend",
    )(page_table, sel_c, cache6, q_tj, qp_tj, o_c, gates, new_rows, win_state_v)
    y = y.reshape(b, NSA_KV, t, NSA_HPG, HEAD_DIM).transpose(0, 2, 1, 3, 4).reshape(b * t, NSA_WIDTH)
    return y.astype(BF16)


def _outproj_kernel(yp_ref, yn_ref, yg_ref, x_ref, w_ref, g_ref, *rest, n_experts):
    with_router = n_experts > 0
    if with_router:
        r_ref, h_o, hn_o, lg_o = rest
    else:
        h_o, hn_o = rest
    h = (x_ref[...] + _dot(yp_ref[...].astype(BF16), w_ref[0:256, :]) + _dot(yn_ref[...], w_ref[256:768, :])
         + _dot(yg_ref[...], w_ref[768:1024, :]))
    h_o[...] = h
    ms = jnp.mean(h * h, axis=-1, keepdims=True)
    hn = (h * lax.rsqrt(ms + EPS) * g_ref[...]).astype(BF16)
    hn_o[...] = hn.astype(hn_o.dtype)
    if with_router:
        lane = lax.broadcasted_iota(jnp.int32, (1, LANES), 1)
        lg = jnp.where(lane < n_experts, _dot(hn, r_ref[...]), -jnp.inf)
        v1 = jnp.max(lg, axis=1, keepdims=True)
        i1 = jnp.min(jnp.where(lg == v1, lane, LANES), axis=1, keepdims=True)
        lg2 = jnp.where(lane == i1, -jnp.inf, lg)
        v2 = jnp.max(lg2, axis=1, keepdims=True)
        i2 = jnp.min(jnp.where(lg2 == v2, lane, LANES), axis=1, keepdims=True)
        e2 = jnp.exp(v2 - v1)
        den = 1.0 + e2
        lg_o[...] = jnp.where(lane == 0, 1.0 / den, jnp.where(lane == 1, e2 / den, jnp.where(
            lane == 2, i1.astype(F32), jnp.where(lane == 3, i2.astype(F32), 0.0))))


def _outproj(y_pool, y_nsa, y_gla, x, w_out_l, g_ffn, router):
    n, d = x.shape
    tm = _tile(n, 384)
    with_router = router is not None
    row = lambda c: pl.BlockSpec((tm, c), lambda i: (i, 0))
    full = lambda shape: pl.BlockSpec(shape, lambda i: (0, 0))
    in_specs = [row(256), row(512), row(256), row(d), full((d, d)), full((1, d))]
    args = [y_pool, y_nsa, y_gla, x, w_out_l.astype(BF16), g_ffn[None, :]]
    out_specs = [row(d), row(d)]
    out_shape = [jax.ShapeDtypeStruct((n, d), F32), jax.ShapeDtypeStruct((n, d), F32 if with_router else BF16)]
    if with_router:
        ne = router.shape[1]
        in_specs.append(full((d, LANES)))
        args.append(jnp.zeros((d, LANES), F32).at[:, :ne].set(router).astype(BF16))
        out_specs.append(row(LANES))
        out_shape.append(jax.ShapeDtypeStruct((n, LANES), F32))
    return pl.pallas_call(
        functools.partial(_outproj_kernel, n_experts=router.shape[1] if with_router else 0),
        grid=(n // tm,), in_specs=in_specs, out_specs=out_specs, out_shape=out_shape,
        compiler_params=_params("arbitrary"), name="outproj",
    )(*args)


def _swiglu_kernel(be_ref, bv_ref, *rest, with_res, gather_rows):
    i = pl.program_id(0)
    j = pl.program_id(1)
    if gather_rows:
        tok_ref, x_ref, gate_ref, w1_ref, w3_ref, w2_ref, o_ref, xbuf, sem = rest
        res_ref = None
        tm = gather_rows
        slot = i % 2

        def issue(blk, slot_):
            for r in range(tm):
                pltpu.make_async_copy(x_ref.at[pl.ds(tok_ref[blk * tm + r], 1)], xbuf.at[slot_, pl.ds(r, 1)],
                                      sem.at[slot_]).start(priority=r % 2)

        @pl.when(j == 0)
        def _():
            @pl.when(i == 0)
            def _():
                issue(0, 0)

            @pl.when(i + 1 < pl.num_programs(0))
            def _():
                issue(i + 1, 1 - slot)

            pltpu.make_async_copy(x_ref.at[pl.ds(0, tm)], xbuf.at[slot], sem.at[slot]).wait()
    else:
        x_ref, gate_ref, w1_ref, w3_ref, w2_ref = rest[:5]
        res_ref, o_ref = rest[5:] if with_res else (None,) + rest[5:]
    valid = bv_ref[i] > 0

    @pl.when(jnp.logical_not(valid) & (j == 0))
    def _():
        o_ref[...] = jnp.zeros_like(o_ref)

    @pl.when(valid)
    def _():
        x = xbuf[slot].astype(BF16) if gather_rows else x_ref[...]
        a = _dot(x, w1_ref[0])
        c = _dot(x, w3_ref[0])
        hmid = (a * (1.0 / (1.0 + jnp.exp(-a))) * c).astype(BF16)
        y = _dot(hmid, w2_ref[0])

        @pl.when(j == 0)
        def _():
            o_ref[...] = y

        @pl.when(j > 0)
        def _():
            o_ref[...] += y

        @pl.when(j == pl.num_programs(1) - 1)
        def _():
            y_all = o_ref[...] * gate_ref[...]
            o_ref[...] = y_all + res_ref[...] if with_res else y_all


def _swiglu(x, gate, blk_expert, blk_valid, w1, w3, w2, tm, res=None, row_tok=None):
    r, d = gate.shape[0], x.shape[1]
    f = w1.shape[2]
    tf = f // 2 if (f // 2) % LANES == 0 else f
    gather = row_tok is not None
    assert not (gather and res is not None)
    row = pl.BlockSpec((tm, d), lambda i, j, be, bv, *_: (i, 0))
    in_specs = [pl.BlockSpec(memory_space=pl.ANY) if gather else row,
                pl.BlockSpec((tm, 1), lambda i, j, be, bv, *_: (i, 0)),
                pl.BlockSpec((1, d, tf), lambda i, j, be, bv, *_: (be[i], 0, j)),
                pl.BlockSpec((1, d, tf), lambda i, j, be, bv, *_: (be[i], 0, j)),
                pl.BlockSpec((1, tf, d), lambda i, j, be, bv, *_: (be[i], j, 0))]
    prefetch = [blk_expert, blk_valid] + ([row_tok] if gather else [])
    args = prefetch + [x, gate, w1, w3, w2]
    if res is not None:
        in_specs.append(row)
        args.append(res)
    scratch = [pltpu.VMEM((2, tm, d), F32), pltpu.SemaphoreType.DMA((2,))] if gather else []
    grid_spec = pltpu.PrefetchScalarGridSpec(
        num_scalar_prefetch=len(prefetch), grid=(r // tm, f // tf), in_specs=in_specs, out_specs=row,
        scratch_shapes=scratch)
    return pl.pallas_call(
        functools.partial(_swiglu_kernel, with_res=res is not None, gather_rows=tm if gather else 0),
        grid_spec=grid_spec,
        out_shape=jax.ShapeDtypeStruct((r, d), F32),
        compiler_params=_params("arbitrary", "arbitrary"), name="swiglu",
    )(*args)


def _dense_ffn(h, hn, w1, w3, w2):
    n = h.shape[0]
    tm = _tile(n, 704)
    nblk = n // tm
    return _swiglu(hn, jnp.ones((n, 1), F32), jnp.zeros((nblk,), jnp.int32), jnp.ones((nblk,), jnp.int32),
                   w1[None].astype(BF16), w3[None].astype(BF16), w2[None].astype(BF16), tm, res=h)


def _moe_ffn(h, hn, route, w1, w3, w2, split=None):
    n = h.shape[0]
    ne = w1.shape[0]
    tm = 512
    a = n * TOP_K
    e_a = route[:, TOP_K:2 * TOP_K].astype(jnp.int32).reshape(-1)
    g_a = route[:, 0:TOP_K].reshape(-1)
    tok_a = jnp.repeat(jnp.arange(n), TOP_K)
    onehot = (e_a[:, None] == jnp.arange(ne)[None, :]).astype(jnp.int32)
    csum = jnp.cumsum(onehot, axis=0)
    counts = csum[-1]
    padded = (counts + tm - 1) // tm * tm
    p_end = jnp.cumsum(padded)
    p_start = p_end - padded
    dest = jnp.sum(onehot * (csum - onehot + p_start[None, :]), axis=1)
    nblk = -(-a // tm) + ne
    r = nblk * tm
    row_info = jnp.zeros((r, 2), F32).at[dest].set(jnp.stack([tok_a.astype(F32), g_a], axis=1))
    row_tok = row_info[:, 0].astype(jnp.int32)
    row_gate = row_info[:, 1]
    blk0 = jnp.arange(nblk) * tm
    blk_valid = (blk0 < p_end[-1]).astype(jnp.int32)
    last = jnp.clip(jnp.searchsorted(p_end, p_end[-1] - 1, side='right'), 0, ne - 1)
    blk_expert = jnp.clip(jnp.searchsorted(p_end, blk0, side='right'), 0, ne - 1)
    blk_expert = jnp.where(blk_valid > 0, blk_expert, last).astype(jnp.int32)
    yb = _swiglu(hn, row_gate[:, None], blk_expert, blk_valid,
                 w1.astype(BF16), w3.astype(BF16), w2.astype(BF16), tm, row_tok=row_tok)
    pos = dest.reshape(n, TOP_K)
    combine = lambda lo, hi: h[lo:hi] + (yb[pos[lo:hi, 0]] + yb[pos[lo:hi, 1]])
    if split is None:
        return combine(0, n)
    return combine(0, split), combine(split, n)


def _kv_rows_out_kernel(*refs):
    o_ref = refs[-1]
    layer = pl.program_id(0)
    for l, r_ref in enumerate(refs[:-1]):
        @pl.when(layer == l)
        def _():
            o_ref[0, 0] = r_ref[...].T


def _kv_rows_out(rows_per_layer, b, t):
    depth = len(rows_per_layer)
    width = rows_per_layer[0].shape[1]
    tq = _tile(t, 512)
    nt = t // tq
    last = b * nt - 1

    def rows_spec(l):
        return pl.BlockSpec((tq, width), lambda ll, i, j: (jnp.where(ll == l, i * nt + j, jnp.where(ll > l, last, 0)), 0))

    out = pl.pallas_call(
        _kv_rows_out_kernel,
        grid=(depth, b, nt),
        in_specs=[rows_spec(l) for l in range(depth)],
        out_specs=pl.BlockSpec((1, 1, width, tq), lambda ll, i, j: (ll, i, 0, j)),
        out_shape=jax.ShapeDtypeStruct((depth, b, width, t), F32),
        compiler_params=_params("arbitrary", "arbitrary", "arbitrary"),
        name="kv_rows_out",
    )(*rows_per_layer)
    return out.reshape(depth, b, 4, NSA_KV, HEAD_DIM, t).transpose(0, 1, 5, 2, 3, 4)


def kernel(x_prompt, x_sample, cache_nsa_kv, state_nsa_win, state_gla, state_pool, page_table,
           norm_mix, norm_ffn, w_in, w_out, pool_w, pool_scale, nsa_q_norm, nsa_k_norm,
           nsa_cmp_w, nsa_cmp_pe, gla_wa2, gla_ba, gla_norm, ffn_w1, ffn_w3, ffn_w2,
           moe_router, moe_w1, moe_w3, moe_w2):
    bp, tp, d = x_prompt.shape
    bs, ts, _ = x_sample.shape
    depth = w_in.shape[0]
    n_pool = cache_nsa_kv.shape[1]
    n_pages = page_table.shape[1]
    past_len = n_pages * PAGE_SIZE
    w_buf = state_nsa_win.shape[2]
    npr = bp * tp
    n_all = npr + bs * ts
    cache_v = cache_nsa_kv.transpose(0, 1, 3, 4, 5, 2).reshape(depth, n_pool, 4 * NSA_KV * HEAD_DIM, PAGE_SIZE)
    win_state_v = state_nsa_win.transpose(0, 1, 3, 4, 5, 2)
    x = jnp.concatenate([x_prompt.reshape(npr, d), x_sample.reshape(bs * ts, d)], axis=0)
    kv_p, kv_s, win_p, win_s, gla_p, gla_s, pool_p, pool_s = [], [], [], [], [], [], [], []
    n_win_p = min(WINDOW, tp)
    for l in range(depth):
        u_pool, q_rows, rows, win, misc, gla_rows = _inproj(x, norm_mix[l], w_in[l], nsa_q_norm[l], nsa_k_norm[l])
        cw = _cmp_weights(nsa_cmp_w[l], nsa_cmp_pe[l], nsa_k_norm[l])
        up = u_pool[:npr].reshape(bp, tp, POOL_WIDTH)
        us = u_pool[npr:].reshape(bs, ts, POOL_WIDTH)
        ys_pool = _pool(us, state_pool[l], pool_w[l], pool_scale[l], past_len, bs, ts)
        y_pool = _pool(u_pool, jnp.zeros((bp, POOL_BUF, POOL_WIDTH), F32), pool_w[l], pool_scale[l], 0, bp, tp)
        y_pool = y_pool.at[npr:].set(ys_pool.reshape(bs * ts, POOL_WIDTH).astype(F32))
        pool_p.append(jnp.concatenate([jnp.zeros((bp, POOL_BUF, POOL_WIDTH), F32), up], axis=1)[:, -POOL_BUF:])
        pool_s.append(jnp.concatenate([state_pool[l], us], axis=1)[:, -POOL_BUF:])
        pw = _paged_cmp_weights(nsa_cmp_w[l], nsa_cmp_pe[l])
        ys_nsa = _nsa_sample(q_rows[npr:], rows[npr:], win[npr:], misc[npr:], cw, pw, cache_v,
                             page_table, l, win_state_v, bs, ts, past_len)
        y_nsa = _nsa_prompt(q_rows, rows, win, misc, cw, bp, tp).at[npr:].set(ys_nsa)
        kv_p.append(rows)
        kv_s.append(rows[npr:].reshape(bs, ts, 4, NSA_KV, HEAD_DIM))
        win_full_p = jnp.concatenate([jnp.zeros((bp, WINDOW, 256), F32), win[:npr].reshape(bp, tp, 256)], axis=1)
        win_p.append(win_full_p[:, -n_win_p:].reshape(bp, n_win_p, 2, NSA_KV, HEAD_DIM))
        win_ext_s = jnp.concatenate([state_nsa_win[l], win[npr:].reshape(bs, ts, 2, NSA_KV, HEAD_DIM)], axis=1)
        win_s.append(win_ext_s[:, -w_buf:])
        ys_gla, ss = _gla(gla_rows[npr:], misc[npr:], state_gla[l], gla_wa2[l], gla_ba[l], gla_norm[l], bs, ts)
        y_gla, sp = _gla(gla_rows, misc, jnp.zeros((bp, GLA_HEADS, GLA_DK, GLA_DV), F32),
                         gla_wa2[l], gla_ba[l], gla_norm[l], bp, tp,
                         base=jnp.zeros((n_all, GLA_WIDTH), BF16).at[npr:].set(ys_gla))
        gla_p.append(sp.astype(state_gla.dtype))
        gla_s.append(ss.astype(state_gla.dtype))
        i = l // 2
        router = moe_router[i] if l % 2 else None
        res = _outproj(y_pool, y_nsa, y_gla, x, w_out[l], norm_ffn[l], router)
        if l % 2 == 0:
            h, hn = res
            x = _dense_ffn(h, hn, ffn_w1[i], ffn_w3[i], ffn_w2[i])
        else:
            h, hn, route = res
            if l == depth - 1:
                x = _moe_ffn(h, hn, route, moe_w1[i], moe_w3[i], moe_w2[i], split=npr)
            else:
                x = _moe_ffn(h, hn, route, moe_w1[i], moe_w3[i], moe_w2[i])
    x_p, x_s = x if isinstance(x, tuple) else (x[:npr], x[npr:])
    return (x_p.reshape(bp, tp, d), x_s.reshape(bs, ts, d),
            _kv_rows_out(kv_p, bp, tp), jnp.stack(kv_s), jnp.stack(win_p), jnp.stack(win_s),
            jnp.stack(gla_p), jnp.stack(gla_s), jnp.stack(pool_p), jnp.stack(pool_s))
```

```python
import functools
import math

import jax
import jax.numpy as jnp
from jax import lax
from jax.experimental import pallas as pl
from jax.experimental.pallas import tpu as pltpu

F32 = jnp.float32
BF16 = jnp.bfloat16

EPS = 1e-6
LANES = 128
HEAD_DIM = 64
PAGE_SIZE = 128
POOL_WINDOWS = (2, 4, 8, 16)
POOL_BUF = 15
POOL_WIDTH = 256
NSA_WIDTH = 512
NSA_HEADS = 8
NSA_KV = 2
NSA_HPG = 4
L_CMP = 32
L_SLC = 64
N_SEL = 16
WINDOW = 512
GLA_HEADS = 4
GLA_DK = 32
GLA_DV = 64
GLA_WIDTH = 256
GLA_RANK = 16
GLA_TAU = 16.0
GLA_SUB = 16
TOP_K = 2
N_IN_PAD = 2432
MASKED = -1e30
VMEM_LIMIT = 56 * 1024 * 1024


def _params(*sem):
    return pltpu.CompilerParams(dimension_semantics=sem, vmem_limit_bytes=VMEM_LIMIT)


def _tile(n, target):
    best = None
    for t in range(8, min(n, target) + 1, 8):
        if n % t == 0:
            best = t
    assert best is not None, (n, target)
    return best


def _dot(a, b):
    return jnp.dot(a, b, preferred_element_type=F32)


def _dot_nt(a, b):
    return lax.dot_general(a, b, (((1,), (1,)), ((), ())), preferred_element_type=F32)


def _dot_tn(a, b):
    return lax.dot_general(a, b, (((0,), (0,)), ((), ())), preferred_element_type=F32)


def _split2_dot(a, ones):
    hi = a.astype(BF16)
    lo = (a - hi.astype(F32)).astype(BF16)
    return _dot(hi, ones) + _dot(lo, ones)


def _split3_dot(ones, a):
    a1 = a.astype(BF16)
    r1 = a - a1.astype(F32)
    a2 = r1.astype(BF16)
    a3 = (r1 - a2.astype(F32)).astype(BF16)
    return _dot(ones, a1) + _dot(ones, a2) + _dot(ones, a3)


def _head_rmsnorm(a, gain_row, seg_ones):
    ms = _split2_dot(a * a, seg_ones) * (1.0 / HEAD_DIM)
    return a * lax.rsqrt(ms + EPS) * gain_row


def _masked_softmax(s, mask):
    sm = jnp.where(mask, s, MASKED)
    m = jnp.max(sm, axis=-1, keepdims=True)
    e = jnp.where(mask, jnp.exp(sm - m), 0.0)
    den = jnp.sum(e, axis=-1, keepdims=True)
    return e * (1.0 / jnp.where(den > 0.0, den, 1.0))


def _seg_ones():
    i = jnp.arange(LANES)
    return (i[:, None] // HEAD_DIM == i[None, :] // HEAD_DIM).astype(BF16)


def _inproj_kernel(x_ref, g_ref, w_ref, qg_ref, kg_ref, seg_ref,
                   pool_o, q_o, rows_o, win_o, misc_o, gla_o):
    x = x_ref[...]
    ms = jnp.mean(x * x, axis=-1, keepdims=True)
    xn = (x * lax.rsqrt(ms + EPS) * g_ref[...]).astype(BF16)
    seg = seg_ref[...]

    def mm(c0, c1):
        return _dot(xn, w_ref[:, c0:c1])

    pool_o[...] = mm(0, 256)
    zq = mm(256, 768)
    for c in range(4):
        q_o[:, LANES * c:LANES * (c + 1)] = (
            _head_rmsnorm(zq[:, LANES * c:LANES * (c + 1)], qg_ref[...], seg) * (HEAD_DIM ** -0.5)).astype(BF16)
    zkv = mm(768, 1536)
    rows_o[:, 0:256] = zkv[:, 0:256]
    rows_o[:, 256:384] = _head_rmsnorm(zkv[:, 256:384], kg_ref[1:2, :], seg)
    rows_o[:, 384:512] = zkv[:, 384:512]
    win_o[:, 0:128] = _head_rmsnorm(zkv[:, 512:640], kg_ref[2:3, :], seg)
    win_o[:, 128:256] = zkv[:, 640:768]
    zg = mm(1536, 2432)
    misc_o[...] = zg[:, 0:128]
    gla_o[...] = zg[:, 128:896]


def _pad_w_in(w):
    d = w.shape[0]
    return jnp.concatenate([
        w[:, 0:1560], w[:, 2072:2088], jnp.zeros((d, 88), w.dtype),
        w[:, 1560:2072], w[:, 2088:2344]], axis=1)


def _inproj(x, g_mix, w_in_l, q_gain, k_gain):
    n, d = x.shape
    tm = _tile(n, 384)
    w = _pad_w_in(w_in_l).astype(BF16)
    qg = jnp.tile(q_gain, 2)[None, :]
    kg = jnp.zeros((8, LANES), F32).at[0:3].set(jnp.tile(k_gain, (1, 2)))
    full = lambda shape: pl.BlockSpec(shape, lambda i: (0, 0))
    row = lambda c: pl.BlockSpec((tm, c), lambda i: (i, 0))
    return pl.pallas_call(
        _inproj_kernel,
        grid=(n // tm,),
        in_specs=[row(d), full((1, d)), full((d, N_IN_PAD)), full((1, LANES)),
                  full((8, LANES)), full((LANES, LANES))],
        out_specs=[row(256), row(512), row(512), row(256), row(128), row(768)],
        out_shape=[jax.ShapeDtypeStruct((n, 256), F32), jax.ShapeDtypeStruct((n, 512), BF16),
                   jax.ShapeDtypeStruct((n, 512), F32), jax.ShapeDtypeStruct((n, 256), F32),
                   jax.ShapeDtypeStruct((n, 128), F32), jax.ShapeDtypeStruct((n, 768), F32)],
        compiler_params=_params("arbitrary"),
        name="inproj",
    )(x, g_mix[None, :], w, qg, kg, _seg_ones())


def _pool_kernel(buf_ref, u_ref, w_ref, sc_ref, *rest, pos0, tp):
    o_ref, ext = rest[-2:]
    i = pl.program_id(1)

    @pl.when(i == 0)
    def _():
        ext[0:16, :] = buf_ref[0]

    ext[16:16 + tp, :] = u_ref[...].reshape(tp, POOL_WIDTH)
    u0 = ext[16:16 + tp, :]
    acc = u0
    sums = {}
    for k in range(1, 16):
        acc = acc + ext[16 - k:16 - k + tp, :]
        if k + 1 in POOL_WINDOWS:
            sums[k + 1] = acc
    lane = lax.broadcasted_iota(jnp.int32, (tp, POOL_WIDTH), 1)
    pos = pos0 + i * tp + lax.broadcasted_iota(jnp.int32, (tp, POOL_WIDTH), 0)
    grp = lane // (POOL_WIDTH // len(POOL_WINDOWS))
    total = sums[16]
    wsize = jnp.full((tp, POOL_WIDTH), 16, jnp.int32)
    for gi, wz in enumerate(POOL_WINDOWS[:-1]):
        total = jnp.where(grp == gi, sums[wz], total)
        wsize = jnp.where(grp == gi, wz, wsize)
    cnt = jnp.minimum(wsize, pos + 1).astype(F32)
    dlt = total / cnt - u0
    o_ref[...] = (_dot(dlt.astype(BF16), w_ref[...]) * sc_ref[...]).astype(o_ref.dtype).reshape(o_ref.shape)
    if tp >= 16:
        ext[0:16, :] = ext[tp:tp + 16, :]


def _aliased_base(base, n_in):
    if base is None:
        return [], [], {}
    return [pl.BlockSpec(memory_space=pl.ANY)], [base], {n_in: 0}


def _pool(u, buf, pool_w_l, pool_scale_l, pos0, b, t):
    in_place = u.ndim == 2
    base_specs, base_args, alias = _aliased_base(u if in_place else None, 4)
    c = u.shape[-1]
    tp = _tile(t, 512) if t >= 8 else t
    nt = t // tp
    buf16 = jnp.concatenate([jnp.zeros((b, 1, c), F32), buf.astype(F32)], axis=1)
    gw = c // len(POOL_WINDOWS)
    wbd = jnp.zeros((c, c), F32)
    for gi in range(len(POOL_WINDOWS)):
        wbd = wbd.at[gi * gw:(gi + 1) * gw, gi * gw:(gi + 1) * gw].set(pool_w_l[gi])
    if in_place:
        rows = pl.BlockSpec((tp, c), lambda i, j: (i * nt + j, 0))
    else:
        rows = pl.BlockSpec((1, tp, c), lambda i, j: (i, j, 0))
    return pl.pallas_call(
        functools.partial(_pool_kernel, pos0=pos0, tp=tp),
        grid=(b, nt),
        in_specs=[pl.BlockSpec((1, 16, c), lambda i, j: (i, 0, 0)), rows,
                  pl.BlockSpec((c, c), lambda i, j: (0, 0)),
                  pl.BlockSpec((1, c), lambda i, j: (0, 0))] + base_specs,
        out_specs=rows,
        out_shape=jax.ShapeDtypeStruct(u.shape, F32 if in_place else BF16),
        scratch_shapes=[pltpu.VMEM((16 + tp, c), F32)],
        input_output_aliases=alias,
        compiler_params=_params("arbitrary", "arbitrary"),
        name="pool",
    )(buf16, u, wbd.astype(BF16), pool_scale_l[None, :], *base_args)


def _gla_kernel(gla_ref, misc_ref, s0_ref, wa_ref, ba_ref, og_ref, seg_ref, eb_ref, mk_ref, *rest, tg, t_valid):
    o_ref, sT_ref, st, qs, ks, bs, qts, kts, vs, os_, dls, us, ss = rest[-13:]
    i = pl.program_id(1)
    c = GLA_SUB
    nsub = tg // c

    @pl.when(i == 0)
    def _():
        st[...] = s0_ref[0]

    gl = gla_ref[...]
    q = gl[:, 0:128] * (GLA_DK ** -0.5)
    k = gl[:, 128:256]
    v = gl[:, 256:512]
    r = gl[:, 512:768]
    x = _dot(misc_ref[...].astype(BF16), wa_ref[...]) + ba_ref[...]
    la = (jnp.minimum(x, 0.0) - jnp.log1p(jnp.exp(-jnp.abs(x)))) * (1.0 / GLA_TAU)
    row = lax.broadcasted_iota(jnp.int32, (tg, LANES), 0)
    if t_valid is not None:
        la = jnp.where(i * tg + row < t_valid, la, 0.0)
    rr = lax.broadcasted_iota(jnp.int32, (tg, tg), 0)
    cc = lax.broadcasted_iota(jnp.int32, (tg, tg), 1)
    same = (rr // c) == (cc // c)
    tri = (same & (cc <= rr)).astype(BF16)
    allo = same.astype(BF16)
    b = _split3_dot(tri, la)
    blast = _split3_dot(allo, la)
    qs[...] = q
    ks[...] = k
    bs[...] = b
    qts[...] = (q * jnp.exp(b)).astype(BF16)
    kts[...] = (k * jnp.exp(blast - b)).astype(BF16)
    vs[...] = v
    eb = eb_ref[...]
    mk = mk_ref[...]
    tt = lax.broadcasted_iota(jnp.int32, (c, LANES), 0)

    dls[...] = jnp.exp(blast)

    def local(j, carry):
        r0 = pl.multiple_of(j * c, c)
        qi = qs[pl.ds(r0, c), :]
        ki = ks[pl.ds(r0, c), :]
        bi = bs[pl.ds(r0, c), :]
        vi = vs[pl.ds(r0, c), :]
        parts = []
        for s in range(c):
            dec = jnp.exp(jnp.minimum(bi - bi[s:s + 1, :], 0.0))
            parts.append(jnp.where(tt >= s, qi * ki[s:s + 1, :] * dec, 0.0))
        p_all = jnp.concatenate(parts, axis=0).astype(BF16)
        a_all = _dot(p_all, eb)
        o_diag = a_all[0:c, :] * vi[0:1, :]
        for s in range(1, c):
            o_diag = o_diag + a_all[s * c:(s + 1) * c, :] * vi[s:s + 1, :]
        os_[pl.ds(r0, c), :] = o_diag
        us[j] = _dot_tn(vi.astype(BF16), kts[pl.ds(r0, c), :]) * mk
        return carry

    def grouped(body, group):
        group = math.gcd(nsub, group)

        def trip(jj, carry):
            for u in range(group):
                body(jj * group + u, carry)
            return carry

        lax.fori_loop(0, nsub // group, trip, 0)

    grouped(local, 16)

    def recur(j, carry):
        s_t = st[...]
        ss[j] = s_t.astype(BF16)
        st[...] = s_t * dls[pl.ds(pl.multiple_of(j * c, c), 1), :] + us[j]
        return carry

    lax.fori_loop(0, nsub, recur, 0)

    def inter(j, carry):
        r0 = pl.multiple_of(j * c, c)
        os_[pl.ds(r0, c), :] += _dot_nt(qts[pl.ds(r0, c), :], ss[j])
        return carry

    grouped(inter, 8)

    o = os_[...]
    seg = seg_ref[...]
    og = og_ref[...]
    sil = r * (1.0 / (1.0 + jnp.exp(-r)))
    for h in range(2):
        sl = slice(h * LANES, (h + 1) * LANES)
        o_ref[:, sl] = (_head_rmsnorm(o[:, sl], og[:, sl], seg) * sil[:, sl]).astype(o_ref.dtype)
    sT_ref[0] = st[...]


def _gla(gla_rows, misc_rows, s0, wa2, ba, o_gain, b, t, base=None):
    base_specs, base_args, alias = _aliased_base(base, 9)
    t_valid = None
    if t % GLA_SUB:
        t_valid = t
        tp = -(-t // GLA_SUB) * GLA_SUB
        pad = lambda z: jnp.pad(z.reshape(b, t, -1), ((0, 0), (0, tp - t), (0, 0))).reshape(b * tp, -1)
        gla_rows, misc_rows = pad(gla_rows), pad(misc_rows)
    else:
        tp = t
    tg = _tile(tp, 256)
    assert tg % GLA_SUB == 0
    nt = tp // tg
    kk = GLA_HEADS * GLA_DK
    vv = GLA_HEADS * GLA_DV
    ki = jnp.arange(kk)
    vi = jnp.arange(vv)
    head_eq = (vi[:, None] // GLA_DV == ki[None, :] // GLA_DK)
    mk = head_eq.astype(F32)
    eb = head_eq.T.astype(BF16)
    s0t = jnp.einsum('bhkv,hg->bhvgk', s0.astype(F32), jnp.eye(GLA_HEADS, dtype=F32)).reshape(b, vv, kk)
    wa = jnp.zeros((LANES, kk), F32).at[24:24 + GLA_RANK].set(wa2).astype(BF16)
    full = lambda shape: pl.BlockSpec(shape, lambda i, j: (0,) * len(shape))
    rows = lambda cdim: pl.BlockSpec((tg, cdim), lambda i, j: (i * nt + j, 0))
    o, s_t = pl.pallas_call(
        functools.partial(_gla_kernel, tg=tg, t_valid=t_valid),
        grid=(b, nt),
        in_specs=[rows(768), rows(128), pl.BlockSpec((1, vv, kk), lambda i, j: (i, 0, 0)),
                  full((LANES, kk)), full((1, kk)), full((1, vv)), full((LANES, LANES)),
                  full((kk, vv)), full((vv, kk))] + base_specs,
        input_output_aliases=alias,
        out_specs=[rows(vv), pl.BlockSpec((1, vv, kk), lambda i, j: (i, 0, 0))],
        out_shape=[jax.ShapeDtypeStruct((gla_rows.shape[0], vv), BF16), jax.ShapeDtypeStruct((b, vv, kk), F32)],
        scratch_shapes=[pltpu.VMEM((vv, kk), F32), pltpu.VMEM((tg, kk), F32), pltpu.VMEM((tg, kk), F32),
                        pltpu.VMEM((tg, kk), F32), pltpu.VMEM((tg, kk), BF16), pltpu.VMEM((tg, kk), BF16),
                        pltpu.VMEM((tg, vv), F32), pltpu.VMEM((tg, vv), F32), pltpu.VMEM((tg, kk), F32),
                        pltpu.VMEM((tg // GLA_SUB, vv, kk), F32), pltpu.VMEM((tg // GLA_SUB, vv, kk), BF16)],
        compiler_params=_params("arbitrary", "arbitrary"),
        name="gla",
    )(gla_rows, misc_rows, s0t, wa, ba[None, :], jnp.tile(o_gain, GLA_HEADS)[None, :], _seg_ones(), eb, mk,
      *base_args)
    if tp != t:
        o = o.reshape(b, tp, vv)[:, :t].reshape(b * t, vv)
    s5 = s_t.reshape(b, GLA_HEADS, GLA_DV, GLA_HEADS, GLA_DK)
    s_new = jnp.einsum('bhvgk,hg->bhkv', s5, jnp.eye(GLA_HEADS, dtype=F32))
    return o, s_new


def _compress_rows(read, nb, pe_ref, wk_ref, wv_ref, kg_ref, seg_ref):
    acck = jnp.zeros((nb, LANES), F32)
    accv = jnp.zeros((nb, LANES), F32)
    for j in range(L_CMP):
        xk = (read(0, j) + pe_ref[0, j:j + 1, :]).astype(BF16)
        xv = (read(1, j) + pe_ref[1, j:j + 1, :]).astype(BF16)
        acck = acck + _dot(xk, wk_ref[j])
        accv = accv + _dot(xv, wv_ref[j])
    kc = _head_rmsnorm(acck, kg_ref[0:1, :], seg_ref[...])
    return kc, accv


def _compress_kernel(rk_ref, rv_ref, pe_ref, wk_ref, wv_ref, kg_ref, seg_ref, o_ref, *, nb):
    refs = (rk_ref, rv_ref)
    kc, vc = _compress_rows(lambda kind, j: refs[kind][pl.ds(j, nb, stride=L_CMP), :], nb,
                            pe_ref, wk_ref, wv_ref, kg_ref, seg_ref)
    o_ref[0, :, 0:LANES] = kc
    o_ref[0, :, LANES:2 * LANES] = vc


def _cmp_weights(cmp_w, cmp_pe, k_gain):
    def bd(w):
        z = jnp.zeros_like(w)
        return jnp.concatenate([jnp.concatenate([w, z], axis=2), jnp.concatenate([z, w], axis=2)], axis=1)
    pe = jnp.tile(cmp_pe, (1, 1, 2))
    kg = jnp.zeros((8, LANES), F32).at[0].set(jnp.tile(k_gain[0], 2))
    return pe, bd(cmp_w[0]).astype(BF16), bd(cmp_w[1]).astype(BF16), kg


def _compress(rk, rv, k_col, v_col, b, tp, cw):
    nb = tp // L_CMP
    pe, wk, wv, kg = cw
    full = lambda shape: pl.BlockSpec(shape, lambda i: (0,) * len(shape))
    return pl.pallas_call(
        functools.partial(_compress_kernel, nb=nb),
        grid=(b,),
        in_specs=[pl.BlockSpec((tp, LANES), lambda i: (i, k_col)), pl.BlockSpec((tp, LANES), lambda i: (i, v_col)),
                  full((2, L_CMP, LANES)), full((L_CMP, LANES, LANES)), full((L_CMP, LANES, LANES)),
                  full((8, LANES)), full((LANES, LANES))],
        out_specs=pl.BlockSpec((1, nb, 256), lambda i: (i, 0, 0)),
        out_shape=jax.ShapeDtypeStruct((b, nb, 256), F32),
        compiler_params=_params("arbitrary"),
        name="compress",
    )(rk, rv, pe, wk, wv, kg, _seg_ones())


SLAB_PITCH = 2 * NSA_KV * HEAD_DIM + 8


def _paged_compress_kernel(pt_ref, cache_ref, pe_ref, m_ref, kg_ref, seg_ref, o_ref,
                           slab, sem, *, pages, n_slab, layer):
    bi = pl.program_id(0)
    si = pl.program_id(1)
    step = bi * n_slab + si
    nsteps = pl.num_programs(0) * n_slab
    slot = step % 2
    rows_cmp = 2 * NSA_KV * HEAD_DIM

    def copy(page, slot_, p):
        return pltpu.make_async_copy(cache_ref.at[layer, page, pl.ds(0, rows_cmp), :],
                                     slab.at[slot_, pl.ds(p * SLAB_PITCH, rows_cmp), :], sem.at[slot_])

    def issue(bb, ss, slot_):
        for p in range(pages):
            copy(pt_ref[bb, ss * pages + p], slot_, p).start()

    @pl.when(step == 0)
    def _():
        issue(0, 0, 0)

    @pl.when(step + 1 < nsteps)
    def _():
        nxt = step + 1
        issue(nxt // n_slab, nxt % n_slab, 1 - slot)

    for p in range(pages):
        copy(0, slot, p).wait()

    def rows(r0):
        return slab[slot, pl.ds(r0, pages, stride=SLAB_PITCH), :]

    for c in range(2):
        acc = jnp.zeros((NSA_KV * pages, 2 * LANES), F32)
        for dp in range(HEAD_DIM // 2):
            parts = []
            for g in range(NSA_KV):
                r0 = (c * NSA_KV + g) * HEAD_DIM + 2 * dp
                parts.append(jnp.concatenate([rows(r0), rows(r0 + 1)], axis=1))
            a = jnp.concatenate(parts, axis=0) + pe_ref[c, dp:dp + 1, :]
            acc = acc + _dot(a.astype(BF16), m_ref[c, dp])
        for g in range(NSA_KV):
            blk = acc[g * pages:(g + 1) * pages, :]
            if c == 0:
                for h in range(2):
                    sl = slice(h * LANES, (h + 1) * LANES)
                    o_ref[0, c, g, :, sl] = _head_rmsnorm(blk[:, sl], kg_ref[0:1, :], seg_ref[...])
            else:
                o_ref[0, c, g] = blk


def _paged_cmp_weights(cmp_w, cmp_pe):
    nblk = PAGE_SIZE // L_CMP
    k6 = jnp.einsum('nm,cjde->cdnjme', jnp.eye(nblk, dtype=F32), cmp_w)
    m = k6.reshape(2, HEAD_DIM // 2, 2 * PAGE_SIZE, nblk * HEAD_DIM).astype(BF16)
    pe = jnp.tile(cmp_pe.transpose(0, 2, 1), (1, 1, nblk)).reshape(2, HEAD_DIM // 2, 2 * PAGE_SIZE)
    return pe, m


def _paged_compress(cache_v, page_table, layer, cw, pw):
    b, n_pages = page_table.shape
    pages = math.gcd(n_pages, 64)
    n_slab = n_pages // pages
    pe, m = pw
    kg = cw[3]
    nblk = PAGE_SIZE // L_CMP
    full = lambda shape: pl.BlockSpec(shape, lambda i, j, pt: (0,) * len(shape))
    grid_spec = pltpu.PrefetchScalarGridSpec(
        num_scalar_prefetch=1,
        grid=(b, n_slab),
        in_specs=[pl.BlockSpec(memory_space=pl.ANY), full(pe.shape), full(m.shape), full((8, LANES)),
                  full((LANES, LANES))],
        out_specs=pl.BlockSpec((1, 2, NSA_KV, pages, nblk * HEAD_DIM), lambda i, j, pt: (i, 0, 0, j, 0)),
        scratch_shapes=[pltpu.VMEM((2, pages * SLAB_PITCH, LANES), F32), pltpu.SemaphoreType.DMA((2,))],
    )
    return pl.pallas_call(
        functools.partial(_paged_compress_kernel, pages=pages, n_slab=n_slab, layer=layer),
        grid_spec=grid_spec,
        out_shape=jax.ShapeDtypeStruct((b, 2, NSA_KV, n_pages, nblk * HEAD_DIM), F32),
        compiler_params=_params("arbitrary", "arbitrary"),
        name="paged_compress",
    )(page_table, cache_v, pe, m, kg, _seg_ones())


def _nsa_prompt_kernel(q_ref, kc_ref, vc_ref, ks_ref, vs_ref, kw_ref, vw_ref, gate_ref, base_ref,
                       o_ref, ksb, vsb, kwb, vwb, *, tq, ck):
    i = pl.program_id(1)
    t0 = i * tq

    n_slc = LANES // 2
    t_keys = ks_ref.shape[0]

    @pl.when(i == 0)
    def _():
        ksb[:, 0:LANES] = ks_ref[...].astype(BF16)
        kblk = lax.broadcasted_iota(jnp.int32, (t_keys, LANES), 0) // L_SLC
        klane = lax.broadcasted_iota(jnp.int32, (t_keys, LANES), 1)
        ksb[:, LANES:2 * LANES] = jnp.where(kblk == klane, 1.0, 0.0).astype(BF16)
        vsb[...] = vs_ref[...].astype(BF16)
        kwb[...] = kw_ref[...].astype(BF16)
        vwb[...] = vw_ref[...].astype(BF16)

    row_t = t0 + lax.broadcasted_iota(jnp.int32, (tq, 1), 0)
    col_t = t0 + lax.broadcasted_iota(jnp.int32, (1, tq), 1)
    lane = lax.broadcasted_iota(jnp.int32, (1, LANES), 1)
    crow = lax.broadcasted_iota(jnp.int32, (LANES, 1), 0)
    nat = jnp.where(crow < n_slc, 2 * crow, 2 * (crow - n_slc) + 1)
    cmp_ok = ((nat + 1) * L_CMP - 1) <= col_t
    blk = lax.broadcasted_iota(jnp.int32, (n_slc, 1), 0)
    cur = col_t // L_SLC
    done = blk < cur
    gx = gate_ref[...]
    gates = 1.0 / (1.0 + jnp.exp(-gx))
    n_full = t0 // ck
    kstart = pl.multiple_of(jnp.maximum(t0 - WINDOW, 0), tq)
    wlen = WINDOW + tq
    wpos = kstart + lax.broadcasted_iota(jnp.int32, (1, wlen), 1)
    win_bias = jnp.where((wpos <= row_t) & (wpos >= row_t - WINDOW), 0.0, MASKED)

    for g in range(NSA_KV):
        in_g = (lane // HEAD_DIM) == g
        heads = []
        for j in range(NSA_HPG):
            h = g * NSA_HPG + j
            qh = q_ref[:, (h // 2) * LANES:(h // 2 + 1) * LANES].astype(F32)
            if h % 2 != g:
                qh = pltpu.roll(qh, HEAD_DIM, axis=1)
            heads.append(jnp.where(in_g, qh, 0.0).astype(BF16))
        qg = jnp.concatenate(heads, axis=0)
        s_c = _dot_nt(kc_ref[0], qg)
        imp = jnp.zeros((LANES, tq), F32)
        o_c = []
        for j in range(NSA_HPG):
            sj = jnp.where(cmp_ok, s_c[:, j * tq:(j + 1) * tq], MASKED)
            ej = jnp.where(cmp_ok, jnp.exp(sj - jnp.max(sj, axis=0, keepdims=True)), 0.0)
            den = jnp.sum(ej, axis=0, keepdims=True)
            pj = ej * (1.0 / jnp.where(den > 0.0, den, 1.0))
            imp = imp + pj
            o_c.append(_dot_tn(pj.astype(BF16), vc_ref[0]))
        imp = imp[0:n_slc] + imp[n_slc:LANES]
        key = jnp.where(done, lax.bitcast_convert_type(imp, jnp.int32), -1)
        key_m1 = key - 1
        rank = jnp.zeros((n_slc, tq), jnp.int32)
        for r in range(1, n_slc):
            vm = pltpu.roll(key, r, axis=0)
            rank = rank + jnp.where(vm > jnp.where(blk >= r, key_m1, key), 1, 0)
        sel = (done & (rank < N_SEL - 1)) | (blk == cur)
        sel_bias = jnp.concatenate([jnp.where(sel, 0.0, MASKED), jnp.zeros((n_slc, tq), F32)], axis=0)
        sel_bias = sel_bias.T.astype(BF16)
        qx = jnp.concatenate([qg, jnp.concatenate([sel_bias] * NSA_HPG, axis=0)], axis=1)

        def chunk(c, carry, diagonal):
            m, l, acc = carry
            k0 = pl.multiple_of(c * ck, ck)
            sm = _dot_nt(qx, ksb[pl.ds(k0, ck), :]).reshape(NSA_HPG, tq, ck)
            if diagonal:
                kpos = k0 + lax.broadcasted_iota(jnp.int32, (1, ck), 1)
                sm = sm + jnp.where(kpos <= row_t, 0.0, MASKED)[None]
            m_new = jnp.maximum(m, jnp.max(sm, axis=-1, keepdims=True))
            p = jnp.exp(sm - m_new)
            alpha = jnp.exp(m - m_new)
            l = alpha * l + jnp.sum(p, axis=-1, keepdims=True)
            pv = _dot(p.reshape(NSA_HPG * tq, ck).astype(BF16), vsb[pl.ds(k0, ck), :])
            acc = alpha * acc + pv.reshape(NSA_HPG, tq, LANES)
            return m_new, l, acc

        m0 = jnp.full((NSA_HPG, tq, 1), MASKED, F32)
        l0 = jnp.zeros((NSA_HPG, tq, 1), F32)
        a0 = jnp.zeros((NSA_HPG, tq, LANES), F32)
        carry = lax.fori_loop(0, n_full, functools.partial(chunk, diagonal=False), (m0, l0, a0))
        _, l_s, acc_s = chunk(n_full, carry, True)
        o_s = acc_s * (1.0 / jnp.where(l_s > 0.0, l_s, 1.0))
        s_w = _dot_nt(qg, kwb[pl.ds(kstart, wlen), :]).reshape(NSA_HPG, tq, wlen) + win_bias[None]
        e_w = jnp.exp(s_w - jnp.max(s_w, axis=-1, keepdims=True))
        o_w = _dot(e_w.reshape(NSA_HPG * tq, wlen).astype(BF16), vwb[pl.ds(kstart, wlen), :])
        o_w = o_w.reshape(NSA_HPG, tq, LANES) * (1.0 / jnp.sum(e_w, axis=-1, keepdims=True))
        for pr in range(NSA_HPG // 2):
            pair = []
            for jj in range(2):
                j = 2 * pr + jj
                h = g * NSA_HPG + j
                o = (gates[:, 3 * h:3 * h + 1] * o_c[j] + gates[:, 3 * h + 1:3 * h + 2] * o_s[j]
                     + gates[:, 3 * h + 2:3 * h + 3] * o_w[j])
                pair.append(o if jj == g else pltpu.roll(o, HEAD_DIM, axis=1))
            col = (g * (NSA_HPG // 2) + pr) * LANES
            o_ref[:, col:col + LANES] = jnp.where(lane < HEAD_DIM, pair[0], pair[1]).astype(o_ref.dtype)


def _nsa_prompt(q_rows, rows, win, misc, cw, b, t):
    assert t // L_CMP == LANES and t % L_SLC == 0, "prompt kernel is laid out for 128 compressed blocks"
    tq = 256
    ck = 1024
    assert ck % tq == 0 and t % ck == 0
    cmp = _compress(rows, rows, 0, 1, b, t, cw)
    order = jnp.concatenate([jnp.arange(0, LANES, 2), jnp.arange(1, LANES, 2)])
    cmp = cmp[:, order].astype(BF16)
    kc, vc = cmp[:, :, 0:LANES], cmp[:, :, LANES:2 * LANES]
    nt = t // tq
    col = lambda c: pl.BlockSpec((t, LANES), lambda i, j: (i, c))
    return pl.pallas_call(
        functools.partial(_nsa_prompt_kernel, tq=tq, ck=ck),
        grid=(b, nt),
        in_specs=[pl.BlockSpec((tq, NSA_WIDTH), lambda i, j: (i * nt + j, 0)),
                  pl.BlockSpec((1, LANES, LANES), lambda i, j: (i, 0, 0)),
                  pl.BlockSpec((1, LANES, LANES), lambda i, j: (i, 0, 0)),
                  col(2), col(3), col(0), col(1),
                  pl.BlockSpec((tq, LANES), lambda i, j: (i * nt + j, 0)),
                  pl.BlockSpec(memory_space=pl.ANY)],
        out_specs=pl.BlockSpec((tq, NSA_WIDTH), lambda i, j: (i * nt + j, 0)),
        out_shape=jax.ShapeDtypeStruct((q_rows.shape[0], NSA_WIDTH), BF16),
        scratch_shapes=[pltpu.VMEM((t, 2 * LANES), BF16)] + [pltpu.VMEM((t, LANES), BF16)] * 3,
        input_output_aliases={8: 0},
        compiler_params=_params("arbitrary", "arbitrary"),
        name="nsa_prompt",
    )(q_rows, kc, vc, rows, rows, win, win, misc, q_rows)


def _split3_dot_r(a, ones):
    a1 = a.astype(BF16)
    r1 = a - a1.astype(F32)
    a2 = r1.astype(BF16)
    a3 = (r1 - a2.astype(F32)).astype(BF16)
    return _dot(a1, ones) + _dot(a2, ones) + _dot(a3, ones)


def _nsa_select_kernel(q_ref, cp_ref, ct_ref, pair_ref, oc_ref, sel_ref, *, t, past_len, n_past):
    nrow = NSA_HPG * t
    npad = n_past + LANES
    n_slc_pad = pair_ref.shape[1]
    row = lax.broadcasted_iota(jnp.int32, (nrow, 1), 0)
    pos = past_len + row % t
    lane_c = lax.broadcasted_iota(jnp.int32, (1, npad), 1)
    cmp_ok = ((lane_c + 1) * L_CMP - 1) <= pos
    cur = (past_len + lax.broadcasted_iota(jnp.int32, (t, 1), 0)) // L_SLC
    lane_s = lax.broadcasted_iota(jnp.int32, (1, n_slc_pad), 1)
    lane_o = lax.broadcasted_iota(jnp.int32, (1, LANES), 1)
    done = lane_s < cur
    keys = []
    for g in range(NSA_KV):
        q = q_ref[0, g]
        s = jnp.concatenate([_dot_nt(q, cp_ref[0, 0, g].astype(BF16)),
                             _dot_nt(q, ct_ref[0, 0, g].astype(BF16))], axis=1)
        p = _masked_softmax(s, cmp_ok)
        pb = p.astype(BF16)
        oc_ref[0, g] = (_dot(pb[:, :n_past], cp_ref[0, 1, g].astype(BF16))
                        + _dot(pb[:, n_past:], ct_ref[0, 1, g].astype(BF16)))
        imp_c = p[0:t]
        for j in range(1, NSA_HPG):
            imp_c = imp_c + p[j * t:(j + 1) * t]
        imp = _split3_dot_r(imp_c, pair_ref[...])
        keys.append(jnp.where(done, lax.bitcast_convert_type(imp, jnp.int32), -1))
    key = jnp.concatenate(keys, axis=0)
    key_col = key.T
    n_idx = lax.broadcasted_iota(jnp.int32, (n_slc_pad, n_slc_pad), 0)
    m_idx = lax.broadcasted_iota(jnp.int32, (n_slc_pad, n_slc_pad), 1)
    m_first = jnp.where(m_idx < n_idx, 1, 0)
    n_col = lax.broadcasted_iota(jnp.int32, (n_slc_pad, 1), 0).astype(F32)
    slot = lane_o.astype(F32)
    rows = []
    for r in range(NSA_KV * t):
        ahead = key[r:r + 1, :] > (key_col[:, r:r + 1] - m_first)
        rank = jnp.sum(jnp.where(ahead, 1.0, 0.0), axis=1, keepdims=True)
        rows.append(jnp.sum(jnp.where(rank == slot, n_col, 0.0), axis=0, keepdims=True))
    picked = jnp.concatenate(rows, axis=0).astype(jnp.int32)
    sel_ref[0] = jnp.where(lane_o == N_SEL - 1, jnp.concatenate([cur] * NSA_KV, axis=0), picked)


def _nsa_attend_kernel(pt_ref, sel_ref, cache_ref, q_ref, qp_ref, oc_ref, gate_ref, new_ref,
                       wst_ref, o_ref, buf, sc, sem, *, t, past_len, layer, w_buf):
    bi = pl.program_id(0)
    slot = bi % 2
    nsel = N_SEL - 1
    n_past_blk = past_len // L_SLC
    per_page = PAGE_SIZE // L_SLC
    nrow = t * NSA_HPG

    def block_of(bb, g, tok, k):
        return jnp.minimum(sel_ref[bb, (g * t + tok) * N_SEL + k], n_past_blk - 1)

    def copy(page, slot_, g, idx):
        return pltpu.make_async_copy(cache_ref.at[layer, page, pl.ds(2, 2), g], buf.at[slot_, idx], sem.at[slot_])

    def issue(bb, slot_):
        for g in range(NSA_KV):
            for tok in range(t):
                for k in range(nsel):
                    page = pt_ref[bb, block_of(bb, g, tok, k) // per_page]
                    copy(page, slot_, g, (g * t + tok) * nsel + k).start()

    @pl.when(bi == 0)
    def _():
        issue(0, 0)

    @pl.when(bi + 1 < pl.num_programs(0))
    def _():
        issue(bi + 1, 1 - slot)

    for g in range(NSA_KV):
        for i in range(t * nsel):
            copy(0, slot, g, g * t * nsel + i).wait()

    row = lax.broadcasted_iota(jnp.int32, (nrow, 1), 0)
    tok_r = row // NSA_HPG
    pos = past_len + tok_r
    lane = lax.broadcasted_iota(jnp.int32, (1, LANES), 1)
    new_bias = jnp.where(lane <= tok_r, 0.0, MASKED)
    wpos = past_len - w_buf + lax.broadcasted_iota(jnp.int32, (1, w_buf), 1)
    win_bias = jnp.where((wpos >= 0) & (wpos <= pos) & (wpos >= pos - WINDOW), 0.0, MASKED)
    for g in range(NSA_KV):
        q = q_ref[0, g]
        qp = qp_ref[0, g]
        glanes = slice(g * HEAD_DIM, (g + 1) * HEAD_DIM)
        m = jnp.full((nrow, 1), MASKED, F32)
        for tok in range(t):
            cur = (past_len + tok) // L_SLC
            for k in range(nsel):
                n = sel_ref[bi, (g * t + tok) * N_SEL + k]
                half = block_of(bi, g, tok, k) % per_page
                valid = (tok_r == tok) & (lane // L_SLC == half) & (n < cur)
                i = tok * nsel + k
                s = _dot(q, buf[slot, g * t * nsel + i, 0].astype(BF16)) + jnp.where(valid, 0.0, MASKED)
                sc[i] = s
                m = jnp.maximum(m, jnp.max(s, axis=1, keepdims=True))
        s_new = _dot_nt(qp, new_ref[0, :, 0:LANES].astype(BF16)) + new_bias
        m = jnp.maximum(m, jnp.max(s_new, axis=1, keepdims=True))
        p_new = jnp.exp(s_new - m)
        l = jnp.sum(p_new, axis=1, keepdims=True)
        acc = _dot(p_new.astype(BF16), new_ref[0, :, LANES:2 * LANES].astype(BF16))[:, glanes]
        for i in range(t * nsel):
            p = jnp.exp(sc[i] - m)
            l = l + jnp.sum(p, axis=1, keepdims=True)
            acc = acc + _dot_nt(p.astype(BF16), buf[slot, g * t * nsel + i, 1].astype(BF16))
        o_s = acc * (1.0 / l)
        s_w = _dot(q, wst_ref[0, 0, 0, g].astype(BF16)) + win_bias
        s_wn = _dot_nt(qp, new_ref[0, :, 2 * LANES:3 * LANES].astype(BF16)) + new_bias
        m_w = jnp.maximum(jnp.max(s_w, axis=1, keepdims=True), jnp.max(s_wn, axis=1, keepdims=True))
        e_w = jnp.exp(s_w - m_w)
        e_n = jnp.exp(s_wn - m_w)
        l_w = jnp.sum(e_w, axis=1, keepdims=True) + jnp.sum(e_n, axis=1, keepdims=True)
        acc_w = (_dot_nt(e_w.astype(BF16), wst_ref[0, 0, 1, g].astype(BF16))
                 + _dot(e_n.astype(BF16), new_ref[0, :, 3 * LANES:4 * LANES].astype(BF16))[:, glanes])
        o_w = acc_w * (1.0 / l_w)
        gate = 1.0 / (1.0 + jnp.exp(-gate_ref[0, g]))
        o_ref[0, g] = gate[:, 0:1] * oc_ref[0, g] + gate[:, 1:2] * o_s + gate[:, 2:3] * o_w


def _nsa_sample(q_rows, rows, win, misc, cw, pw, cache_v, page_table, layer, win_state_v, b, t, past_len):
    t_all = past_len + t
    t_pad = -(-t_all // L_SLC) * L_SLC
    n_cmp = t_pad // L_CMP
    n_tail = (t_pad - past_len) // L_CMP
    n_past = past_len // L_CMP
    w_buf = win_state_v.shape[-1]
    assert past_len % PAGE_SIZE == 0 and t <= L_SLC and n_tail <= LANES and NSA_KV * t * N_SEL <= LANES
    cmp_past = _paged_compress(cache_v, page_table, layer, cw, pw).reshape(b, 2, NSA_KV, n_past, HEAD_DIM)
    tail = jnp.concatenate([rows[:, 0:256].reshape(b, t, 256),
                            jnp.zeros((b, t_pad - t_all, 256), F32)], axis=1).reshape(b * (t_pad - past_len), 256)
    cmp_tail = _compress(tail, tail, 0, 1, 1, b * (t_pad - past_len), cw)
    cmp_tail = cmp_tail.reshape(b, n_tail, 2, NSA_KV, HEAD_DIM).transpose(0, 2, 3, 1, 4)
    cmp_tail = jnp.pad(cmp_tail, ((0, 0), (0, 0), (0, 0), (0, LANES - n_tail), (0, 0)))
    q5 = q_rows.reshape(b, t, NSA_KV, NSA_HPG, HEAD_DIM)
    nrow = NSA_HPG * t
    q_jt = q5.transpose(0, 2, 3, 1, 4).reshape(b, NSA_KV, nrow, HEAD_DIM)
    q_tj = q5.transpose(0, 2, 1, 3, 4).reshape(b, NSA_KV, nrow, HEAD_DIM)
    qp_tj = jnp.zeros((b, NSA_KV, nrow, LANES), BF16)
    for g in range(NSA_KV):
        qp_tj = qp_tj.at[:, g, :, g * HEAD_DIM:(g + 1) * HEAD_DIM].set(q_tj[:, g])
    npad = n_past + LANES
    n_slc_pad = -(-(t_pad // L_SLC) // LANES) * LANES
    nn = jnp.arange(npad)
    pair = ((nn[:, None] // (L_SLC // L_CMP) == jnp.arange(n_slc_pad)[None, :]) & (nn[:, None] < n_cmp)).astype(BF16)
    per_b = lambda shape: pl.BlockSpec(shape, lambda i: (i,) + (0,) * (len(shape) - 1))
    o_c, sel = pl.pallas_call(
        functools.partial(_nsa_select_kernel, t=t, past_len=past_len, n_past=n_past),
        grid=(b,),
        in_specs=[per_b((1, NSA_KV, nrow, HEAD_DIM)), per_b((1, 2, NSA_KV, n_past, HEAD_DIM)),
                  per_b((1, 2, NSA_KV, LANES, HEAD_DIM)), pl.BlockSpec((npad, n_slc_pad), lambda i: (0, 0))],
        out_specs=[per_b((1, NSA_KV, nrow, HEAD_DIM)), per_b((1, NSA_KV * t, LANES))],
        out_shape=[jax.ShapeDtypeStruct((b, NSA_KV, nrow, HEAD_DIM), F32),
                   jax.ShapeDtypeStruct((b, NSA_KV * t, LANES), jnp.int32)],
        compiler_params=_params("arbitrary"),
        name="nsa_select",
    )(q_jt, cmp_past, cmp_tail, pair)
    sel_c = sel[:, :, :N_SEL].reshape(b, NSA_KV * t * N_SEL)
    o_c = o_c.reshape(b, NSA_KV, NSA_HPG, t, HEAD_DIM).transpose(0, 1, 3, 2, 4).reshape(b, NSA_KV, nrow, HEAD_DIM)
    gates = misc[:, 0:3 * NSA_HEADS].reshape(b, t, NSA_KV, NSA_HPG, 3).transpose(0, 2, 1, 3, 4)
    gates = jnp.pad(gates.reshape(b, NSA_KV, nrow, 3), ((0, 0), (0, 0), (0, 0), (0, LANES - 3)))
    new_rows = jnp.concatenate([rows[:, 256:512], win], axis=1).reshape(b, t, 512)
    new_rows = jnp.pad(new_rows, ((0, 0), (0, LANES - t), (0, 0)))
    cache6 = cache_v.reshape(cache_v.shape[0], cache_v.shape[1], 4, NSA_KV, HEAD_DIM, PAGE_SIZE)
    nsel = N_SEL - 1
    pb = lambda shape: pl.BlockSpec(shape, lambda i, pt, sl: (i,) + (0,) * (len(shape) - 1))
    grid_spec = pltpu.PrefetchScalarGridSpec(
        num_scalar_prefetch=2,
        grid=(b,),
        in_specs=[pl.BlockSpec(memory_space=pl.ANY), pb((1, NSA_KV, nrow, HEAD_DIM)), pb((1, NSA_KV, nrow, LANES)),
                  pb((1, NSA_KV, nrow, HEAD_DIM)), pb((1, NSA_KV, nrow, LANES)), pb((1, LANES, 512)),
                  pl.BlockSpec((1, 1, 2, NSA_KV, HEAD_DIM, w_buf), lambda i, pt, sl: (layer, i, 0, 0, 0, 0))],
        out_specs=pb((1, NSA_KV, nrow, HEAD_DIM)),
        scratch_shapes=[pltpu.VMEM((2, NSA_KV * t * nsel, 2, HEAD_DIM, PAGE_SIZE), F32),
                        pltpu.VMEM((t * nsel, nrow, LANES), F32), pltpu.SemaphoreType.DMA((2,))],
    )
    y = pl.pallas_call(
        functools.partial(_nsa_attend_kernel, t=t, past_len=past_len, layer=layer, w_buf=w_buf),
        grid_spec=grid_spec,
        out_shape=jax.ShapeDtypeStruct((b, NSA_KV, nrow, HEAD_DIM), F32),
        compiler_params=_params("arbitrary"),
        name="nsa_attend",
    )(page_table, sel_c, cache6, q_tj, qp_tj, o_c, gates, new_rows, win_state_v)
    y = y.reshape(b, NSA_KV, t, NSA_HPG, HEAD_DIM).transpose(0, 2, 1, 3, 4).reshape(b * t, NSA_WIDTH)
    return y.astype(BF16)


def _outproj_kernel(yp_ref, yn_ref, yg_ref, x_ref, w_ref, g_ref, *rest, n_experts):
    with_router = n_experts > 0
    if with_router:
        r_ref, h_o, hn_o, lg_o = rest
    else:
        h_o, hn_o = rest
    h = (x_ref[...] + _dot(yp_ref[...].astype(BF16), w_ref[0:256, :]) + _dot(yn_ref[...], w_ref[256:768, :])
         + _dot(yg_ref[...], w_ref[768:1024, :]))
    h_o[...] = h
    ms = jnp.mean(h * h, axis=-1, keepdims=True)
    hn = (h * lax.rsqrt(ms + EPS) * g_ref[...]).astype(BF16)
    hn_o[...] = hn.astype(hn_o.dtype)
    if with_router:
        lane = lax.broadcasted_iota(jnp.int32, (1, LANES), 1)
        lg = jnp.where(lane < n_experts, _dot(hn, r_ref[...]), -jnp.inf)
        v1 = jnp.max(lg, axis=1, keepdims=True)
        i1 = jnp.min(jnp.where(lg == v1, lane, LANES), axis=1, keepdims=True)
        lg2 = jnp.where(lane == i1, -jnp.inf, lg)
        v2 = jnp.max(lg2, axis=1, keepdims=True)
        i2 = jnp.min(jnp.where(lg2 == v2, lane, LANES), axis=1, keepdims=True)
        e2 = jnp.exp(v2 - v1)
        den = 1.0 + e2
        lg_o[...] = jnp.where(lane == 0, 1.0 / den, jnp.where(lane == 1, e2 / den, jnp.where(
            lane == 2, i1.astype(F32), jnp.where(lane == 3, i2.astype(F32), 0.0))))


def _outproj(y_pool, y_nsa, y_gla, x, w_out_l, g_ffn, router):
    n, d = x.shape
    tm = _tile(n, 384)
    with_router = router is not None
    row = lambda c: pl.BlockSpec((tm, c), lambda i: (i, 0))
    full = lambda shape: pl.BlockSpec(shape, lambda i: (0, 0))
    in_specs = [row(256), row(512), row(256), row(d), full((d, d)), full((1, d))]
    args = [y_pool, y_nsa, y_gla, x, w_out_l.astype(BF16), g_ffn[None, :]]
    out_specs = [row(d), row(d)]
    out_shape = [jax.ShapeDtypeStruct((n, d), F32), jax.ShapeDtypeStruct((n, d), F32 if with_router else BF16)]
    if with_router:
        ne = router.shape[1]
        in_specs.append(full((d, LANES)))
        args.append(jnp.zeros((d, LANES), F32).at[:, :ne].set(router).astype(BF16))
        out_specs.append(row(LANES))
        out_shape.append(jax.ShapeDtypeStruct((n, LANES), F32))
    return pl.pallas_call(
        functools.partial(_outproj_kernel, n_experts=router.shape[1] if with_router else 0),
        grid=(n // tm,), in_specs=in_specs, out_specs=out_specs, out_shape=out_shape,
        compiler_params=_params("arbitrary"), name="outproj",
    )(*args)


def _swiglu_kernel(be_ref, bv_ref, *rest, with_res, gather_rows):
    i = pl.program_id(0)
    j = pl.program_id(1)
    if gather_rows:
        tok_ref, x_ref, gate_ref, w1_ref, w3_ref, w2_ref, o_ref, xbuf, sem = rest
        res_ref = None
        tm = gather_rows
        slot = i % 2

        def issue(blk, slot_):
            for r in range(tm):
                pltpu.make_async_copy(x_ref.at[pl.ds(tok_ref[blk * tm + r], 1)], xbuf.at[slot_, pl.ds(r, 1)],
                                      sem.at[slot_]).start(priority=r % 2)

        @pl.when(j == 0)
        def _():
            @pl.when(i == 0)
            def _():
                issue(0, 0)

            @pl.when(i + 1 < pl.num_programs(0))
            def _():
                issue(i + 1, 1 - slot)

            pltpu.make_async_copy(x_ref.at[pl.ds(0, tm)], xbuf.at[slot], sem.at[slot]).wait()
    else:
        x_ref, gate_ref, w1_ref, w3_ref, w2_ref = rest[:5]
        res_ref, o_ref = rest[5:] if with_res else (None,) + rest[5:]
    valid = bv_ref[i] > 0

    @pl.when(jnp.logical_not(valid) & (j == 0))
    def _():
        o_ref[...] = jnp.zeros_like(o_ref)

    @pl.when(valid)
    def _():
        x = xbuf[slot].astype(BF16) if gather_rows else x_ref[...]
        a = _dot(x, w1_ref[0])
        c = _dot(x, w3_ref[0])
        hmid = (a * (1.0 / (1.0 + jnp.exp(-a))) * c).astype(BF16)
        y = _dot(hmid, w2_ref[0])

        @pl.when(j == 0)
        def _():
            o_ref[...] = y

        @pl.when(j > 0)
        def _():
            o_ref[...] += y

        @pl.when(j == pl.num_programs(1) - 1)
        def _():
            y_all = o_ref[...] * gate_ref[...]
            o_ref[...] = y_all + res_ref[...] if with_res else y_all


def _swiglu(x, gate, blk_expert, blk_valid, w1, w3, w2, tm, res=None, row_tok=None):
    r, d = gate.shape[0], x.shape[1]
    f = w1.shape[2]
    tf = f // 2 if (f // 2) % LANES == 0 else f
    gather = row_tok is not None
    assert not (gather and res is not None)
    row = pl.BlockSpec((tm, d), lambda i, j, be, bv, *_: (i, 0))
    in_specs = [pl.BlockSpec(memory_space=pl.ANY) if gather else row,
                pl.BlockSpec((tm, 1), lambda i, j, be, bv, *_: (i, 0)),
                pl.BlockSpec((1, d, tf), lambda i, j, be, bv, *_: (be[i], 0, j)),
                pl.BlockSpec((1, d, tf), lambda i, j, be, bv, *_: (be[i], 0, j)),
                pl.BlockSpec((1, tf, d), lambda i, j, be, bv, *_: (be[i], j, 0))]
    prefetch = [blk_expert, blk_valid] + ([row_tok] if gather else [])
    args = prefetch + [x, gate, w1, w3, w2]
    if res is not None:
        in_specs.append(row)
        args.append(res)
    scratch = [pltpu.VMEM((2, tm, d), F32), pltpu.SemaphoreType.DMA((2,))] if gather else []
    grid_spec = pltpu.PrefetchScalarGridSpec(
        num_scalar_prefetch=len(prefetch), grid=(r // tm, f // tf), in_specs=in_specs, out_specs=row,
        scratch_shapes=scratch)
    return pl.pallas_call(
        functools.partial(_swiglu_kernel, with_res=res is not None, gather_rows=tm if gather else 0),
        grid_spec=grid_spec,
        out_shape=jax.ShapeDtypeStruct((r, d), F32),
        compiler_params=_params("arbitrary", "arbitrary"), name="swiglu",
    )(*args)


def _dense_ffn(h, hn, w1, w3, w2):
    n = h.shape[0]
    tm = _tile(n, 704)
    nblk = n // tm
    return _swiglu(hn, jnp.ones((n, 1), F32), jnp.zeros((nblk,), jnp.int32), jnp.ones((nblk,), jnp.int32),
                   w1[None].astype(BF16), w3[None].astype(BF16), w2[None].astype(BF16), tm, res=h)


def _moe_ffn(h, hn, route, w1, w3, w2, split=None):
    n = h.shape[0]
    ne = w1.shape[0]
    tm = 512
    a = n * TOP_K
    e_a = route[:, TOP_K:2 * TOP_K].astype(jnp.int32).reshape(-1)
    g_a = route[:, 0:TOP_K].reshape(-1)
    tok_a = jnp.repeat(jnp.arange(n), TOP_K)
    onehot = (e_a[:, None] == jnp.arange(ne)[None, :]).astype(jnp.int32)
    csum = jnp.cumsum(onehot, axis=0)
    counts = csum[-1]
    padded = (counts + tm - 1) // tm * tm
    p_end = jnp.cumsum(padded)
    p_start = p_end - padded
    dest = jnp.sum(onehot * (csum - onehot + p_start[None, :]), axis=1)
    nblk = -(-a // tm) + ne
    r = nblk * tm
    row_info = jnp.zeros((r, 2), F32).at[dest].set(jnp.stack([tok_a.astype(F32), g_a], axis=1))
    row_tok = row_info[:, 0].astype(jnp.int32)
    row_gate = row_info[:, 1]
    blk0 = jnp.arange(nblk) * tm
    blk_valid = (blk0 < p_end[-1]).astype(jnp.int32)
    last = jnp.clip(jnp.searchsorted(p_end, p_end[-1] - 1, side='right'), 0, ne - 1)
    blk_expert = jnp.clip(jnp.searchsorted(p_end, blk0, side='right'), 0, ne - 1)
    blk_expert = jnp.where(blk_valid > 0, blk_expert, last).astype(jnp.int32)
    yb = _swiglu(hn, row_gate[:, None], blk_expert, blk_valid,
                 w1.astype(BF16), w3.astype(BF16), w2.astype(BF16), tm, row_tok=row_tok)
    pos = dest.reshape(n, TOP_K)
    combine = lambda lo, hi: h[lo:hi] + (yb[pos[lo:hi, 0]] + yb[pos[lo:hi, 1]])
    if split is None:
        return combine(0, n)
    return combine(0, split), combine(split, n)


def _kv_rows_out_kernel(*refs):
    o_ref = refs[-1]
    layer = pl.program_id(0)
    for l, r_ref in enumerate(refs[:-1]):
        @pl.when(layer == l)
        def _():
            o_ref[0, 0] = r_ref[...].T


def _kv_rows_out(rows_per_layer, b, t):
    depth = len(rows_per_layer)
    width = rows_per_layer[0].shape[1]
    tq = _tile(t, 512)
    nt = t // tq
    last = b * nt - 1

    def rows_spec(l):
        return pl.BlockSpec((tq, width), lambda ll, i, j: (jnp.where(ll == l, i * nt + j, jnp.where(ll > l, last, 0)), 0))

    out = pl.pallas_call(
        _kv_rows_out_kernel,
        grid=(depth, b, nt),
        in_specs=[rows_spec(l) for l in range(depth)],
        out_specs=pl.BlockSpec((1, 1, width, tq), lambda ll, i, j: (ll, i, 0, j)),
        out_shape=jax.ShapeDtypeStruct((depth, b, width, t), F32),
        compiler_params=_params("arbitrary", "arbitrary", "arbitrary"),
        name="kv_rows_out",
    )(*rows_per_layer)
    return out.reshape(depth, b, 4, NSA_KV, HEAD_DIM, t).transpose(0, 1, 5, 2, 3, 4)


def kernel(x_prompt, x_sample, cache_nsa_kv, state_nsa_win, state_gla, state_pool, page_table,
           norm_mix, norm_ffn, w_in, w_out, pool_w, pool_scale, nsa_q_norm, nsa_k_norm,
           nsa_cmp_w, nsa_cmp_pe, gla_wa2, gla_ba, gla_norm, ffn_w1, ffn_w3, ffn_w2,
           moe_router, moe_w1, moe_w3, moe_w2):
    bp, tp, d = x_prompt.shape
    bs, ts, _ = x_sample.shape
    depth = w_in.shape[0]
    n_pool = cache_nsa_kv.shape[1]
    n_pages = page_table.shape[1]
    past_len = n_pages * PAGE_SIZE
    w_buf = state_nsa_win.shape[2]
    npr = bp * tp
    n_all = npr + bs * ts
    cache_v = cache_nsa_kv.transpose(0, 1, 3, 4, 5, 2).reshape(depth, n_pool, 4 * NSA_KV * HEAD_DIM, PAGE_SIZE)
    win_state_v = state_nsa_win.transpose(0, 1, 3, 4, 5, 2)
    x = jnp.concatenate([x_prompt.reshape(npr, d), x_sample.reshape(bs * ts, d)], axis=0)
    kv_p, kv_s, win_p, win_s, gla_p, gla_s, pool_p, pool_s = [], [], [], [], [], [], [], []
    n_win_p = min(WINDOW, tp)
    for l in range(depth):
        u_pool, q_rows, rows, win, misc, gla_rows = _inproj(x, norm_mix[l], w_in[l], nsa_q_norm[l], nsa_k_norm[l])
        cw = _cmp_weights(nsa_cmp_w[l], nsa_cmp_pe[l], nsa_k_norm[l])
        up = u_pool[:npr].reshape(bp, tp, POOL_WIDTH)
        us = u_pool[npr:].reshape(bs, ts, POOL_WIDTH)
        ys_pool = _pool(us, state_pool[l], pool_w[l], pool_scale[l], past_len, bs, ts)
        y_pool = _pool(u_pool, jnp.zeros((bp, POOL_BUF, POOL_WIDTH), F32), pool_w[l], pool_scale[l], 0, bp, tp)
        y_pool = y_pool.at[npr:].set(ys_pool.reshape(bs * ts, POOL_WIDTH).astype(F32))
        pool_p.append(jnp.concatenate([jnp.zeros((bp, POOL_BUF, POOL_WIDTH), F32), up], axis=1)[:, -POOL_BUF:])
        pool_s.append(jnp.concatenate([state_pool[l], us], axis=1)[:, -POOL_BUF:])
        pw = _paged_cmp_weights(nsa_cmp_w[l], nsa_cmp_pe[l])
        ys_nsa = _nsa_sample(q_rows[npr:], rows[npr:], win[npr:], misc[npr:], cw, pw, cache_v,
                             page_table, l, win_state_v, bs, ts, past_len)
        y_nsa = _nsa_prompt(q_rows, rows, win, misc, cw, bp, tp).at[npr:].set(ys_nsa)
        kv_p.append(rows)
        kv_s.append(rows[npr:].reshape(bs, ts, 4, NSA_KV, HEAD_DIM))
        win_full_p = jnp.concatenate([jnp.zeros((bp, WINDOW, 256), F32), win[:npr].reshape(bp, tp, 256)], axis=1)
        win_p.append(win_full_p[:, -n_win_p:].reshape(bp, n_win_p, 2, NSA_KV, HEAD_DIM))
        win_ext_s = jnp.concatenate([state_nsa_win[l], win[npr:].reshape(bs, ts, 2, NSA_KV, HEAD_DIM)], axis=1)
        win_s.append(win_ext_s[:, -w_buf:])
        ys_gla, ss = _gla(gla_rows[npr:], misc[npr:], state_gla[l], gla_wa2[l], gla_ba[l], gla_norm[l], bs, ts)
        y_gla, sp = _gla(gla_rows, misc, jnp.zeros((bp, GLA_HEADS, GLA_DK, GLA_DV), F32),
                         gla_wa2[l], gla_ba[l], gla_norm[l], bp, tp,
                         base=jnp.zeros((n_all, GLA_WIDTH), BF16).at[npr:].set(ys_gla))
        gla_p.append(sp.astype(state_gla.dtype))
        gla_s.append(ss.astype(state_gla.dtype))
        i = l // 2
        router = moe_router[i] if l % 2 else None
        res = _outproj(y_pool, y_nsa, y_gla, x, w_out[l], norm_ffn[l], router)
        if l % 2 == 0:
            h, hn = res
            x = _dense_ffn(h, hn, ffn_w1[i], ffn_w3[i], ffn_w2[i])
        else:
            h, hn, route = res
            if l == depth - 1:
                x = _moe_ffn(h, hn, route, moe_w1[i], moe_w3[i], moe_w2[i], split=npr)
            else:
                x = _moe_ffn(h, hn, route, moe_w1[i], moe_w3[i], moe_w2[i])
    x_p, x_s = x if isinstance(x, tuple) else (x[:npr], x[npr:])
    return (x_p.reshape(bp, tp, d), x_s.reshape(bs, ts, d),
            _kv_rows_out(kv_p, bp, tp), jnp.stack(kv_s), jnp.stack(win_p), jnp.stack(win_s),
            jnp.stack(gla_p), jnp.stack(gla_s), jnp.stack(pool_p), jnp.stack(pool_s))
```
